```python
import math
import jax, jax.numpy as jnp
from jax import lax
import numpy as np

D_MODEL = 1024
BATCH = 4
SEQ = 4096
DEPTH = 1

GRID_W = 64
CTX_LEN = 256
SSD_HEADS = 8
SSD_HEAD_DIM = 64
SSD_INNER = SSD_HEADS * SSD_HEAD_DIM
SSD_STATE = 64
SSD_GROUPS = 2
SSD_CONV = 5
SSD_XBC = SSD_INNER + 2 * SSD_GROUPS * SSD_STATE
SSD_COLS = SSD_INNER + SSD_XBC + 2 * SSD_HEADS
HG_HEADS = 4
HG_DK = 128
HG_DV = 128
HG_INNER = HG_HEADS * HG_DV
HG_COLS = 5 * HG_INNER
D_MIX = SSD_INNER + HG_INNER
IN_COLS = SSD_COLS + HG_COLS
CHUNK = 64
N_EXPERTS = 32
TOP_K = 4
D_FF = D_MODEL
SWIGLU_ALPHA = 1.702
SWIGLU_LIMIT = 7.0
MOE_BLOCK = 128
EPS = 1e-6

kernel_name = 'hybrid_ssd_hgrn2_moe_diffusion_block'


def rmsnorm(x, w):
    xf = x.astype(jnp.float32)
    y = xf * lax.rsqrt(jnp.mean(xf * xf, axis=-1, keepdims=True) + EPS)
    return (y * w.astype(jnp.float32)).astype(x.dtype)


def modulate(h, shift, scale):
    return h * (1 + scale) + shift


def centred_dwconv(x, w, b):
    pad = SSD_CONV // 2
    y = lax.conv_general_dilated(x, w[:, None, :].astype(x.dtype), window_strides=(1,),
                                 padding=[(pad, pad)], dimension_numbers=('NWC', 'WIO', 'NWC'),
                                 feature_group_count=x.shape[-1])
    return y + b


def chunked_scan(q, k, v, log_a, s0):
    bsz, t_len, n_h, _ = q.shape
    n_chunks = t_len // CHUNK
    scalar_decay = log_a.shape[-1] == 1
    causal = jnp.tril(jnp.ones((CHUNK, CHUNK), dtype=bool))[None, None, :, :, None]

    def to_chunks(a):
        a = a.astype(jnp.float32).reshape(bsz, n_chunks, CHUNK, n_h, a.shape[-1])
        return jnp.moveaxis(a, 1, 0)

    def step(s, blk):
        qc, kc, vc, lac = blk
        g = jnp.cumsum(lac, axis=1)
        gh = jnp.swapaxes(g, 1, 2)
        decay = jnp.exp(jnp.where(causal, gh[:, :, :, None, :] - gh[:, :, None, :, :], -jnp.inf))
        if scalar_decay:
            scores = jnp.einsum('bihk,bjhk->bhij', qc, kc) * decay[..., 0]
        else:
            scores = jnp.einsum('bihk,bjhk,bhijk->bhij', qc, kc, decay)
        o = jnp.einsum('bhij,bjhv->bihv', scores, vc) + jnp.einsum('bihk,bhkv->bihv', qc * jnp.exp(g), s)
        g_last = g[:, -1]
        s = jnp.exp(g_last)[..., None] * s + jnp.einsum('bjhk,bjhv->bhkv', kc * jnp.exp(g_last[:, None] - g), vc)
        return s, o

    s_fin, o = lax.scan(step, s0.astype(jnp.float32),
                        (to_chunks(q), to_chunks(k), to_chunks(v), to_chunks(log_a)))
    o = jnp.moveaxis(o, 0, 1).reshape(bsz, t_len, n_h, v.shape[-1])
    return o.astype(v.dtype), s_fin


def bidir_scan(q, k_f, k_b, v, la_f, la_b, s0_f, s0_b):
    rev = lambda a: jnp.flip(a, axis=1)
    o_f, s_f = chunked_scan(q, k_f, v, la_f, s0_f)
    o_b, s_b = chunked_scan(rev(q), rev(k_b), rev(v), rev(la_b), s0_b)
    return o_f + rev(o_b), s_f, s_b


def ssd_branch(u, conv_w, conv_b, dt_bias, a_log, d_skip, norm_w, s0_f, s0_b, need_out):
    bsz, t_len, _ = u.shape
    z = u[..., :SSD_INNER]
    xbc = jax.nn.silu(centred_dwconv(u[..., SSD_INNER:SSD_INNER + SSD_XBC], conv_w, conv_b))
    dt_raw = u[..., SSD_INNER + SSD_XBC:]
    xs = xbc[..., :SSD_INNER].reshape(bsz, t_len, SSD_HEADS, SSD_HEAD_DIM)
    rep = SSD_HEADS // SSD_GROUPS
    gn = SSD_GROUPS * SSD_STATE
    b_mat = jnp.repeat(xbc[..., SSD_INNER:SSD_INNER + gn].reshape(bsz, t_len, SSD_GROUPS, SSD_STATE), rep, axis=2)
    c_mat = jnp.repeat(xbc[..., SSD_INNER + gn:].reshape(bsz, t_len, SSD_GROUPS, SSD_STATE), rep, axis=2)
    dt = jax.nn.softplus(dt_raw.astype(jnp.float32).reshape(bsz, t_len, 2, SSD_HEADS) + dt_bias)
    log_a = dt * (-jnp.exp(a_log.astype(jnp.float32)))
    y, s_f, s_b = bidir_scan(c_mat, b_mat * dt[:, :, 0, :, None], b_mat * dt[:, :, 1, :, None], xs,
                             log_a[:, :, 0, :, None], log_a[:, :, 1, :, None], s0_f, s0_b)
    if not need_out:
        return None, s_f, s_b
    y = (y + d_skip[:, None] * xs).reshape(bsz, t_len, SSD_INNER)
    return rmsnorm(y * jax.nn.silu(z), norm_w), s_f, s_b


def hgrn_branch(u, lb_f, lb_b, norm_w, s0_f, s0_b, need_out):
    bsz, t_len, _ = u.shape
    q, f_raw_f, f_raw_b, i, g = jnp.split(u, 5, axis=-1)
    shape4 = (bsz, t_len, HG_HEADS, HG_DK)
    q = (jax.nn.silu(q) * HG_DK ** -0.5).reshape(shape4)

    def forget(f_raw, lb):
        return (lb + (1 - lb) * jax.nn.sigmoid(f_raw.astype(jnp.float32))).reshape(shape4)

    f_f = forget(f_raw_f, lb_f)
    f_b = forget(f_raw_b, lb_b)
    o, s_f, s_b = bidir_scan(q, 1 - f_f, 1 - f_b, i.reshape(bsz, t_len, HG_HEADS, HG_DV),
                             jnp.log(f_f), jnp.log(f_b), s0_f, s0_b)
    if not need_out:
        return None, s_f, s_b
    o = rmsnorm(o, norm_w.reshape(HG_HEADS, HG_DV)).reshape(bsz, t_len, HG_INNER)
    return o * jax.nn.silu(g), s_f, s_b


def to_col_major(a):
    bsz, t_len, ch = a.shape
    rows = t_len // GRID_W
    return a.reshape(bsz, rows, GRID_W, ch).transpose(0, 2, 1, 3).reshape(bsz, t_len, ch)


def from_col_major(a):
    bsz, t_len, ch = a.shape
    rows = t_len // GRID_W
    return a.reshape(bsz, GRID_W, rows, ch).transpose(0, 2, 1, 3).reshape(bsz, t_len, ch)


def moe_ffn(h, router_w, router_b, w1, b1, w2, b2):
    n_tok, d = h.shape
    n_pairs = n_tok * TOP_K
    logits = h.astype(jnp.float32) @ router_w.astype(jnp.float32) + router_b.astype(jnp.float32)
    top_v, top_i = lax.top_k(logits, TOP_K)
    gates = jax.nn.softmax(top_v, axis=-1)
    flat_e = top_i.reshape(-1)
    flat_tok = jnp.arange(n_pairs) // TOP_K
    order = jnp.argsort(flat_e)
    e_sorted, tok_sorted, w_sorted = flat_e[order], flat_tok[order], gates.reshape(-1)[order]
    counts = jnp.bincount(flat_e, length=N_EXPERTS)
    start = jnp.cumsum(counts) - counts
    padded = (counts + MOE_BLOCK - 1) // MOE_BLOCK * MOE_BLOCK
    pend = jnp.cumsum(padded)
    pstart = pend - padded
    dest = pstart[e_sorted] + jnp.arange(n_pairs) - start[e_sorted]
    n_blocks = (n_pairs + N_EXPERTS * (MOE_BLOCK - 1) + MOE_BLOCK - 1) // MOE_BLOCK
    xbuf = jnp.zeros((n_blocks * MOE_BLOCK, d), h.dtype).at[dest].set(h[tok_sorted])
    block_e = jnp.clip(jnp.searchsorted(pend, jnp.arange(n_blocks) * MOE_BLOCK, side='right'), 0, N_EXPERTS - 1)

    def expert_block(args):
        xb, e = args
        u = xb @ w1[e] + b1[e]
        glu = jnp.minimum(u[:, ::2], SWIGLU_LIMIT)
        lin = jnp.clip(u[:, 1::2], -SWIGLU_LIMIT, SWIGLU_LIMIT)
        return (glu * jax.nn.sigmoid(SWIGLU_ALPHA * glu) * (lin + 1)) @ w2[e] + b2[e]

    ybuf = lax.map(expert_block, (xbuf.reshape(n_blocks, MOE_BLOCK, d), block_e)).reshape(-1, d)
    y_pairs = ybuf[dest] * w_sorted[:, None].astype(h.dtype)
    return jax.ops.segment_sum(y_pairs, tok_sorted, num_segments=n_tok)


def setup_inputs(seed: int = 0) -> dict:
    key = jax.random.key(seed)
    ks = jax.random.split(key, 26)
    f32 = jnp.float32
    nrm = lambda k, shape, s: jax.random.normal(k, shape, f32) * s
    dt0 = jnp.exp(jax.random.uniform(ks[10], (DEPTH, 2, SSD_HEADS), f32, math.log(1e-3), math.log(1e-1)))
    return {
        'x': nrm(ks[0], (BATCH, SEQ, D_MODEL), 1.0),
        'c': nrm(ks[1], (BATCH, D_MODEL), 1.0),
        'ctx': nrm(ks[2], (BATCH, CTX_LEN, D_MODEL), 1.0),
        'c_ctx': nrm(ks[3], (D_MODEL,), 1.0),
        'ada_w': nrm(ks[4], (DEPTH, D_MODEL, 6 * D_MODEL), 0.5 * D_MODEL ** -0.5),
        'ada_b': nrm(ks[5], (DEPTH, 6 * D_MODEL), 0.02),
        'mix_pre_norm': 1.0 + nrm(ks[6], (DEPTH, D_MODEL), 0.02),
        'mix_post_norm': 1.0 + nrm(ks[7], (DEPTH, D_MODEL), 0.02),
        'w_in': nrm(ks[8], (DEPTH, D_MODEL, IN_COLS), D_MODEL ** -0.5),
        'w_out': nrm(ks[9], (DEPTH, D_MIX, D_MODEL), D_MIX ** -0.5),
        'ssd_conv_w': nrm(ks[11], (DEPTH, SSD_CONV, SSD_XBC), SSD_CONV ** -0.5),
        'ssd_conv_b': nrm(ks[12], (DEPTH, SSD_XBC), 0.02),
        'ssd_dt_bias': dt0 + jnp.log(-jnp.expm1(-dt0)),
        'ssd_a_log': jnp.log(jax.random.uniform(ks[13], (DEPTH, 2, SSD_HEADS), f32, 1.0, 16.0)),
        'ssd_d': 1.0 + nrm(ks[14], (DEPTH, SSD_HEADS), 0.1),
        'ssd_norm': 1.0 + nrm(ks[15], (DEPTH, SSD_INNER), 0.02),
        'hg_lb': 1.0 + nrm(ks[16], (DEPTH + 1, 2, HG_INNER), 0.1),
        'hg_norm': 1.0 + nrm(ks[17], (DEPTH, HG_INNER), 0.02),
        'ffn_pre_norm': 1.0 + nrm(ks[18], (DEPTH, D_MODEL), 0.02),
        'ffn_post_norm': 1.0 + nrm(ks[19], (DEPTH, D_MODEL), 0.02),
        'router_w': nrm(ks[20], (DEPTH, D_MODEL, N_EXPERTS), D_MODEL ** -0.5),
        'router_b': nrm(ks[21], (DEPTH, N_EXPERTS), 0.01),
        'moe_w1': nrm(ks[22], (DEPTH, N_EXPERTS, D_MODEL, 2 * D_FF), D_MODEL ** -0.5),
        'moe_b1': nrm(ks[23], (DEPTH, N_EXPERTS, 2 * D_FF), 0.02),
        'moe_w2': nrm(ks[24], (DEPTH, N_EXPERTS, D_FF, D_MODEL), D_FF ** -0.5),
        'moe_b2': nrm(ks[25], (DEPTH, N_EXPERTS, D_MODEL), 0.02),
    }


def reference(x, c, ctx, c_ctx, ada_w, ada_b, mix_pre_norm, mix_post_norm, w_in, w_out,
              ssd_conv_w, ssd_conv_b, ssd_dt_bias, ssd_a_log, ssd_d, ssd_norm, hg_lb, hg_norm,
              ffn_pre_norm, ffn_post_norm, router_w, router_b, moe_w1, moe_b1, moe_w2, moe_b2):
    bsz = x.shape[0]
    lb = jnp.cumsum(jax.nn.softmax(hg_lb.astype(jnp.float32), axis=0), axis=0)
    zero_ssd = jnp.zeros((bsz, SSD_HEADS, SSD_STATE, SSD_HEAD_DIM), jnp.float32)
    zero_hg = jnp.zeros((bsz, HG_HEADS, HG_DK, HG_DV), jnp.float32)
    for l in range(DEPTH):
        update_ctx = l < DEPTH - 1
        sh_m, sc_m, g_m, sh_f, sc_f, g_f = [m[:, None, :] for m in
                                            jnp.split(jax.nn.silu(c) @ ada_w[l] + ada_b[l], 6, axis=-1)]
        csh_m, csc_m, cg_m, csh_f, csc_f, cg_f = jnp.split(jax.nn.silu(c_ctx) @ ada_w[l] + ada_b[l], 6, axis=-1)
        ux = modulate(rmsnorm(x, mix_pre_norm[l]), sh_m, sc_m) @ w_in[l]
        uc = modulate(rmsnorm(ctx, mix_pre_norm[l]), csh_m, csc_m) @ w_in[l]
        ssd_p = (ssd_conv_w[l], ssd_conv_b[l], ssd_dt_bias[l], ssd_a_log[l], ssd_d[l], ssd_norm[l])
        yc_ssd, sf, sb = ssd_branch(uc[..., :SSD_COLS], *ssd_p, zero_ssd, zero_ssd, update_ctx)
        yx_ssd, _, _ = ssd_branch(ux[..., :SSD_COLS], *ssd_p, sf, sb, True)
        yc_hg, hf, hb = hgrn_branch(uc[..., SSD_COLS:], lb[l, 0], lb[l, 1], hg_norm[l], zero_hg, zero_hg, update_ctx)
        yx_hg, _, _ = hgrn_branch(to_col_major(ux[..., SSD_COLS:]), lb[l, 0], lb[l, 1], hg_norm[l], hf, hb, True)
        mix_x = jnp.concatenate([yx_ssd, from_col_major(yx_hg)], axis=-1) @ w_out[l]
        x = x + g_m * rmsnorm(mix_x, mix_post_norm[l])
        hx = modulate(rmsnorm(x, ffn_pre_norm[l]), sh_f, sc_f)
        moe_p = (router_w[l], router_b[l], moe_w1[l], moe_b1[l], moe_w2[l], moe_b2[l])
        if update_ctx:
            mix_c = jnp.concatenate([yc_ssd, yc_hg], axis=-1) @ w_out[l]
            ctx = ctx + cg_m * rmsnorm(mix_c, mix_post_norm[l])
            hc = modulate(rmsnorm(ctx, ffn_pre_norm[l]), csh_f, csc_f)
            n_lat = hx.shape[0] * hx.shape[1]
            f_all = moe_ffn(jnp.concatenate([hx.reshape(-1, D_MODEL), hc.reshape(-1, D_MODEL)], axis=0), *moe_p)
            fx = f_all[:n_lat].reshape(hx.shape)
            ctx = ctx + cg_f * rmsnorm(f_all[n_lat:].reshape(hc.shape), ffn_post_norm[l])
        else:
            fx = moe_ffn(hx.reshape(-1, D_MODEL), *moe_p).reshape(hx.shape)
        x = x + g_f * rmsnorm(fx, ffn_post_norm[l])
    return x
```

```python
import functools
import math

import numpy as np
import jax
import jax.numpy as jnp
from jax import lax
from jax.experimental import pallas as pl
from jax.experimental.pallas import tpu as pltpu

F32 = jnp.float32
BF16 = jnp.bfloat16
HIGHEST = lax.Precision.HIGHEST

D_MODEL = 1024
GRID_W = 64
SSD_HEADS = 8
SSD_HEAD_DIM = 64
SSD_INNER = 512
SSD_STATE = 64
SSD_GROUPS = 2
SSD_CONV = 5
SSD_XBC = 768
SSD_COLS = 1296
HG_HEADS = 4
HG_DK = 128
HG_INNER = 512
N_EXPERTS = 32
TOP_K = 4
D_FF = 1024
SWIGLU_ALPHA = 1.702
SWIGLU_LIMIT = 7.0
EPS = 1e-6

LANES = 128
SUBLANES = 8
VMEM_LIMIT = 56 * 1024 * 1024

SSD_CHUNK = 128
HG_CHUNK = 64
HG_LEVELS = 6
MOE_ROWS = 256
TOK_TILE = 256
NEG_BIG = -1e30


def _cparams(sem):
    return pltpu.CompilerParams(dimension_semantics=sem, vmem_limit_bytes=VMEM_LIMIT)


def _sigmoid(x):
    return 1.0 / (1.0 + jnp.exp(-x))


def _silu(x):
    return x * _sigmoid(x)


def _rms(x, w):
    return x * lax.rsqrt(jnp.mean(x * x, axis=-1, keepdims=True) + EPS) * w


def _split3(v):
    hi = v.astype(BF16)
    r1 = v - hi.astype(F32)
    mid = r1.astype(BF16)
    lo = (r1 - mid.astype(F32)).astype(BF16)
    return hi, mid, lo


def _ada_kernel(c_ref, w_ref, b_ref, o_ref):
    s = _silu(c_ref[...])
    o_ref[...] = jnp.dot(s, w_ref[...], precision=HIGHEST, preferred_element_type=F32) + b_ref[...]


def ada_call(cc, w, b):
    n = w.shape[1]
    tn = 1536
    return pl.pallas_call(
        _ada_kernel,
        grid=(n // tn,),
        in_specs=[pl.BlockSpec((SUBLANES, D_MODEL), lambda j: (0, 0)),
                  pl.BlockSpec((D_MODEL, tn), lambda j: (0, j)),
                  pl.BlockSpec((1, tn), lambda j: (0, j))],
        out_specs=pl.BlockSpec((SUBLANES, tn), lambda j: (0, j)),
        out_shape=jax.ShapeDtypeStruct((SUBLANES, n), F32),
        compiler_params=_cparams(("arbitrary",)),
        name="ada",
    )(cc, w, b)


A_COLS = SSD_INNER + SSD_XBC + HG_INNER


def _prep(xt, nw, sh, sc):
    return (_rms(xt, nw) * (1.0 + sc) + sh).astype(BF16)


def _inproj_a_kernel(x_ref, xp_ref, xn_ref, sh_ref, sc_ref, nw_ref, wa_ref, wdt_ref, cw_ref, cb_ref, dtb_ref,
                     z_ref, xbc_ref, g_ref, dt_ref, dtT_ref, scr, *, tm):
    i = pl.program_id(1)
    last = pl.num_programs(1) - 1
    nw, sh, sc = nw_ref[...], sh_ref[...], sc_ref[...]
    h = _prep(x_ref[...], nw, sh, sc)
    ua = jnp.dot(h, wa_ref[...], preferred_element_type=F32)
    z_ref[...] = ua[:, :SSD_INNER]
    g_ref[...] = ua[:, SSD_INNER + SSD_XBC:]
    wx = wa_ref[:, SSD_INNER:SSD_INNER + SSD_XBC]
    up = jnp.dot(_prep(xp_ref[...], nw, sh, sc), wx, preferred_element_type=F32)
    un = jnp.dot(_prep(xn_ref[...], nw, sh, sc), wx, preferred_element_type=F32)
    scr[0:SUBLANES, :] = jnp.where(i > 0, up, 0.0)
    scr[SUBLANES:SUBLANES + tm, :] = ua[:, SSD_INNER:SSD_INNER + SSD_XBC]
    scr[SUBLANES + tm:, :] = jnp.where(i < last, un, 0.0)
    acc = jnp.broadcast_to(cb_ref[...], (tm, SSD_XBC))
    pad = SSD_CONV // 2
    for k in range(SSD_CONV):
        off = SUBLANES - pad + k
        acc = acc + cw_ref[k:k + 1, :] * scr[off:off + tm, :]
    xbc_ref[...] = _silu(acc)
    draw = jnp.dot(h, wdt_ref[...], preferred_element_type=F32) + dtb_ref[...]
    dt = jnp.maximum(draw, 0.0) + jnp.log(1.0 + jnp.exp(-jnp.abs(draw)))
    dt_ref[...] = dt[:, :2 * SSD_HEADS]
    dtT_ref[...] = dt.T[:2 * SSD_HEADS, :]


def inproj_a_call(x, sh, sc, nw, wa, wdt, cw, cb, dtb, tm):
    bsz, t_len, _ = x.shape
    nt = t_len // tm
    r8 = tm // SUBLANES
    n8 = t_len // SUBLANES
    full = lambda shape: pl.BlockSpec(shape, lambda b, i: (0,) * len(shape))
    tok = lambda c: pl.BlockSpec((None, tm, c), lambda b, i: (b, i, 0))
    return pl.pallas_call(
        functools.partial(_inproj_a_kernel, tm=tm),
        grid=(bsz, nt),
        in_specs=[tok(D_MODEL),
                  pl.BlockSpec((None, SUBLANES, D_MODEL), lambda b, i: (b, jnp.maximum(i * r8 - 1, 0), 0)),
                  pl.BlockSpec((None, SUBLANES, D_MODEL), lambda b, i: (b, jnp.minimum((i + 1) * r8, n8 - 1), 0)),
                  pl.BlockSpec((None, 1, D_MODEL), lambda b, i: (b, 0, 0)),
                  pl.BlockSpec((None, 1, D_MODEL), lambda b, i: (b, 0, 0)),
                  full((1, D_MODEL)), full((D_MODEL, A_COLS)), full((D_MODEL, LANES)),
                  full((SSD_CONV, SSD_XBC)), full((1, SSD_XBC)), full((1, LANES))],
        out_specs=[tok(SSD_INNER), tok(SSD_XBC), tok(HG_INNER), tok(2 * SSD_HEADS),
                   pl.BlockSpec((None, 2 * SSD_HEADS, tm), lambda b, i: (b, 0, i))],
        out_shape=[jax.ShapeDtypeStruct((bsz, t_len, SSD_INNER), F32),
                   jax.ShapeDtypeStruct((bsz, t_len, SSD_XBC), F32),
                   jax.ShapeDtypeStruct((bsz, t_len, HG_INNER), F32),
                   jax.ShapeDtypeStruct((bsz, t_len, 2 * SSD_HEADS), F32),
                   jax.ShapeDtypeStruct((bsz, 2 * SSD_HEADS, t_len), F32)],
        scratch_shapes=[pltpu.VMEM((tm + 2 * SUBLANES, SSD_XBC), F32)],
        compiler_params=_cparams(("arbitrary", "arbitrary")),
        name="inproj_a",
    )(x, x, x, sh, sc, nw, wa, wdt, cw, cb, dtb)


B_COLS = 4 * HG_INNER


def _inproj_b_kernel(x_ref, sh_ref, sc_ref, nw_ref, w_ref, o_ref, *, ncol):
    if ncol:
        xt = jnp.concatenate([x_ref[:, w * D_MODEL:(w + 1) * D_MODEL] for w in range(ncol)], axis=0)
    else:
        xt = x_ref[...]
    h = _prep(xt, nw_ref[...], sh_ref[...], sc_ref[...])
    o_ref[...] = jnp.dot(h, w_ref[...], preferred_element_type=F32)


def inproj_b_call(x, sh, sc, nw, wb, col_major):
    bsz, t_len, _ = x.shape
    if col_major:
        ncol = 8
        tm = ncol * GRID_W
        rows = t_len // GRID_W
        assert rows == GRID_W
        xin = x.reshape(bsz, rows, GRID_W * D_MODEL)
        x_spec = pl.BlockSpec((None, rows, ncol * D_MODEL), lambda b, i: (b, 0, i))
    else:
        ncol = 0
        tm = t_len
        xin = x
        x_spec = pl.BlockSpec((None, tm, D_MODEL), lambda b, i: (b, i, 0))
    full = lambda shape: pl.BlockSpec(shape, lambda b, i: (0,) * len(shape))
    return pl.pallas_call(
        functools.partial(_inproj_b_kernel, ncol=ncol),
        grid=(bsz, t_len // tm),
        in_specs=[x_spec,
                  pl.BlockSpec((None, 1, D_MODEL), lambda b, i: (b, 0, 0)),
                  pl.BlockSpec((None, 1, D_MODEL), lambda b, i: (b, 0, 0)),
                  full((1, D_MODEL)), full((D_MODEL, B_COLS))],
        out_specs=pl.BlockSpec((None, tm, B_COLS), lambda b, i: (b, i, 0)),
        out_shape=jax.ShapeDtypeStruct((bsz, t_len, B_COLS), F32),
        compiler_params=_cparams(("arbitrary", "arbitrary")),
        name="inproj_b",
    )(xin, sh, sc, nw, wb)


N_PAIRS = SSD_HEADS // 2


def _ssd_dir(xbc, dt, dtT, s_ref, tri, triT, na_row, na_col, fwd):
    c = SSD_CHUNK
    xs = xbc[:, :SSD_INNER]
    bm = xbc[:, SSD_INNER:SSD_INNER + LANES]
    cm = xbc[:, SSD_INNER + LANES:]
    col0 = 0 if fwd else SSD_HEADS
    la = dt[:, col0:col0 + SSD_HEADS] * na_row[:, col0:col0 + SSD_HEADS]
    dtr = dtT[col0:col0 + SSD_HEADS, :]
    laT = dtr * na_col[col0:col0 + SSD_HEADS, :]
    g = sum(jnp.dot(tri, p, preferred_element_type=F32) for p in _split3(la))
    gT = sum(jnp.dot(p, triT, preferred_element_type=F32) for p in _split3(laT))
    end = c - 1 if fwd else 0
    bmT = bm.T
    ii = lax.broadcasted_iota(jnp.int32, (c, c), 0)
    jj = lax.broadcasted_iota(jnp.int32, (c, c), 1)
    causal = (jj <= ii) if fwd else (jj >= ii)
    lane = lax.broadcasted_iota(jnp.int32, (c, LANES), 1)
    lo_half = lane < SSD_HEAD_DIM
    lane_s = lax.broadcasted_iota(jnp.int32, (SSD_STATE, LANES), 1) < SSD_HEAD_DIM
    outs = []
    for grp in range(SSD_GROUPS):
        in_grp = (lane >= grp * SSD_STATE) & (lane < (grp + 1) * SSD_STATE)
        cm_g = jnp.where(in_grp, cm, 0.0).astype(BF16)
        gmat = jnp.dot(cm_g, bmT.astype(BF16), preferred_element_type=F32)
        bmT_g = bmT[grp * SSD_STATE:(grp + 1) * SSD_STATE, :]
        for pp in range(N_PAIRS // SSD_GROUPS):
            pair = grp * (N_PAIRS // SSD_GROUPS) + pp
            heads = (2 * pair, 2 * pair + 1)
            xs_p = xs[:, pair * LANES:(pair + 1) * LANES]
            xbd = jnp.concatenate([jnp.where(lo_half, xs_p, 0.0), jnp.where(lo_half, 0.0, xs_p)],
                                  axis=0).astype(BF16)
            ms, bws, ecols, arows = [], [], [], []
            for hd in heads:
                gcol = g[:, hd:hd + 1]
                grow = gT[hd:hd + 1, :]
                dec = jnp.exp(jnp.where(causal, gcol - grow, -jnp.inf))
                ms.append(gmat * dec * dtr[hd:hd + 1, :])
                glast = grow[:, end:end + 1]
                bws.append(bmT_g * (dtr[hd:hd + 1, :] * jnp.exp(glast - grow)))
                ecols.append(jnp.exp(gcol))
                arows.append(jnp.exp(glast))
            mcat = jnp.concatenate(ms, axis=1).astype(BF16)
            s_old = s_ref[pair]
            zeros = jnp.zeros_like(s_old)
            s_pad = jnp.concatenate([s_old, zeros] if grp == 0 else [zeros, s_old], axis=0).astype(BF16)
            o_inter = jnp.dot(cm.astype(BF16), s_pad, preferred_element_type=F32)
            o_inter = o_inter * jnp.where(lo_half, ecols[0], ecols[1])
            outs.append(jnp.dot(mcat, xbd, preferred_element_type=F32) + o_inter)
            bw = jnp.concatenate(bws, axis=1).astype(BF16)
            s_ref[pair] = (s_old * jnp.where(lane_s, arows[0], arows[1])
                           + jnp.dot(bw, xbd, preferred_element_type=F32))
    return outs


def _ssd_scan_kernel(xf_ref, dtf_ref, dtTf_ref, xb_ref, dtb_ref, dtTb_ref, s0f_ref, s0b_ref,
                     trif_ref, trifT_ref, trib_ref, tribT_ref, nar_ref, nac_ref, dsk_ref,
                     of_ref, ob_ref, sfo_ref, sbo_ref, sf, sb):
    n = pl.program_id(1)

    @pl.when(n == 0)
    def _():
        sf[...] = s0f_ref[...]
        sb[...] = s0b_ref[...]

    xf = xf_ref[...]
    outs = _ssd_dir(xf, dtf_ref[...], dtTf_ref[...], sf, trif_ref[...], trifT_ref[...],
                    nar_ref[...], nac_ref[...], True)
    of_ref[...] = jnp.concatenate(outs, axis=1) + dsk_ref[...] * xf[:, :SSD_INNER]
    outs = _ssd_dir(xb_ref[...], dtb_ref[...], dtTb_ref[...], sb, trib_ref[...], tribT_ref[...],
                    nar_ref[...], nac_ref[...], False)
    ob_ref[...] = jnp.concatenate(outs, axis=1)

    @pl.when(n == pl.num_programs(1) - 1)
    def _():
        sfo_ref[...] = sf[...]
        sbo_ref[...] = sb[...]


def _ssd_consts():
    c = SSD_CHUNK
    i = np.arange(c)
    trif = (i[:, None] >= i[None, :]).astype(np.float32)
    trib = (i[:, None] <= i[None, :]).astype(np.float32)
    return [jnp.asarray(a, BF16) for a in (trif, trif.T, trib, trib.T)]


def ssd_scan_call(xbc, dt, dtT, s0f, s0b, na_row, na_col, dskip):
    bsz, t_len, _ = xbc.shape
    c = SSD_CHUNK
    nc = t_len // c
    fw = lambda w: pl.BlockSpec((None, c, w), lambda b, n: (b, n, 0))
    bw = lambda w: pl.BlockSpec((None, c, w), lambda b, n: (b, nc - 1 - n, 0))
    full = lambda shape: pl.BlockSpec(shape, lambda b, n: (0,) * len(shape))
    st = pl.BlockSpec((None, N_PAIRS, SSD_STATE, LANES), lambda b, n: (b, 0, 0, 0))
    st_shape = jax.ShapeDtypeStruct((bsz, N_PAIRS, SSD_STATE, LANES), F32)
    return pl.pallas_call(
        _ssd_scan_kernel,
        grid=(bsz, nc),
        in_specs=[fw(SSD_XBC), fw(2 * SSD_HEADS),
                  pl.BlockSpec((None, 2 * SSD_HEADS, c), lambda b, n: (b, 0, n)),
                  bw(SSD_XBC), bw(2 * SSD_HEADS),
                  pl.BlockSpec((None, 2 * SSD_HEADS, c), lambda b, n: (b, 0, nc - 1 - n)),
                  st, st, full((c, c)), full((c, c)), full((c, c)), full((c, c)),
                  full((1, 2 * SSD_HEADS)), full((2 * SSD_HEADS, 1)), full((1, SSD_INNER))],
        out_specs=[fw(SSD_INNER), bw(SSD_INNER), st, st],
        out_shape=[jax.ShapeDtypeStruct((bsz, t_len, SSD_INNER), F32),
                   jax.ShapeDtypeStruct((bsz, t_len, SSD_INNER), F32), st_shape, st_shape],
        scratch_shapes=[pltpu.VMEM((N_PAIRS, SSD_STATE, LANES), F32),
                        pltpu.VMEM((N_PAIRS, SSD_STATE, LANES), F32)],
        compiler_params=_cparams(("arbitrary", "arbitrary")),
        name="ssd_scan",
    )(xbc, dt, dtT, xbc, dt, dtT, s0f, s0b, *_ssd_consts(), na_row, na_col, dskip)


HG_NSUM = HG_LEVELS + 2


def _hg_consts():
    c = HG_CHUNK
    t = np.arange(c)
    a_f = np.zeros((HG_NSUM, c, c), np.float32)
    a_b = np.zeros((HG_NSUM, c, c), np.float32)
    m_f = np.zeros((HG_LEVELS + 1, c, c), np.float32)
    m_f[0] = np.eye(c)
    for lv in range(1, HG_LEVELS + 1):
        m = 2 ** lv
        half = m // 2
        for r in range(c):
            base = (r // m) * m
            mid = base + half - 1
            if r - base >= half:
                a_f[lv - 1, r, mid + 1:r + 1] = 1.0
                a_b[lv - 1, r, mid + 1:r] = 1.0
            else:
                a_f[lv - 1, r, r + 1:mid + 1] = 1.0
                a_b[lv - 1, r, r:mid + 1] = 1.0
        blk = t // m
        right = (t % m) >= half
        m_f[lv] = (blk[:, None] == blk[None, :]) & right[:, None] & (~right[None, :])
    a_f[HG_LEVELS] = t[:, None] >= t[None, :]
    a_f[HG_LEVELS + 1] = t[None, :] > t[:, None]
    a_b[HG_LEVELS] = t[None, :] >= t[:, None]
    a_b[HG_LEVELS + 1] = t[None, :] < t[:, None]
    m_b = np.transpose(m_f, (0, 2, 1))
    side = np.stack([(t % (2 ** lv)) >= (2 ** lv) // 2 for lv in range(1, HG_LEVELS + 1)]).astype(np.float32)
    side = np.repeat(side[:, :, None], LANES, axis=2)
    return (jnp.asarray(a_f.reshape(HG_NSUM * c, c), BF16), jnp.asarray(a_b.reshape(HG_NSUM * c, c), BF16),
            jnp.asarray(m_f, F32), jnp.asarray(m_b, F32), jnp.asarray(side, F32))


def _hg_dir(u, lb_row, st_ref, a_ref, mask_ref, side_ref, fwd):
    c = HG_CHUNK
    outs = []
    fcol = HG_INNER if fwd else 2 * HG_INNER
    nt = (((1,), (1,)), ((), ()))
    for hd in range(HG_HEADS):
        sl = slice(hd * HG_DK, (hd + 1) * HG_DK)
        q = _silu(u[:, sl]) * (HG_DK ** -0.5)
        lb = lb_row[:, sl]
        f = lb + (1.0 - lb) * _sigmoid(u[:, fcol + hd * HG_DK:fcol + (hd + 1) * HG_DK])
        k = 1.0 - f
        la = jnp.log(f)
        v = u[:, 3 * HG_INNER + hd * HG_DK:3 * HG_INNER + (hd + 1) * HG_DK]
        pieces = jnp.concatenate(_split3(la), axis=1)
        ne3 = jnp.dot(a_ref[...], pieces, preferred_element_type=F32)
        x_all = jnp.exp(ne3[:, :LANES] + ne3[:, LANES:2 * LANES] + ne3[:, 2 * LANES:])
        scores = mask_ref[0] * lax.dot_general(q.astype(BF16), k.astype(BF16), nt, preferred_element_type=F32)
        for lv in range(1, HG_LEVELS + 1):
            x_l = x_all[(lv - 1) * c:lv * c, :]
            is_q = side_ref[lv - 1] > 0.5
            if not fwd:
                is_q = jnp.logical_not(is_q)
            y = (jnp.where(is_q, q, k) * x_l).astype(BF16)
            scores = scores + mask_ref[lv] * lax.dot_general(y, y, nt, preferred_element_type=F32)
        x_cum = x_all[HG_LEVELS * c:(HG_LEVELS + 1) * c, :]
        x_rem = x_all[(HG_LEVELS + 1) * c:, :]
        st = st_ref[hd]
        o = jnp.dot(scores.astype(BF16), v.astype(BF16), preferred_element_type=F32)
        o = o + lax.dot_general((q * x_cum).astype(BF16), st.astype(BF16), nt, preferred_element_type=F32)
        outs.append(o)
        end = c - 1 if fwd else 0
        st_ref[hd] = (st * x_cum[end:end + 1, :]
                      + jnp.dot(v.T.astype(BF16), (k * x_rem).astype(BF16), preferred_element_type=F32))
    return jnp.concatenate(outs, axis=1)


def _hg_scan_kernel(uf_ref, ub_ref, lbp_ref, s0f_ref, s0b_ref, af_ref, ab_ref, mf_ref, mb_ref, side_ref,
                    of_ref, ob_ref, sfo_ref, sbo_ref, sf, sb):
    n = pl.program_id(1)

    @pl.when(n == 0)
    def _():
        sf[...] = s0f_ref[...]
        sb[...] = s0b_ref[...]

    p = lbp_ref[...]
    mx = jnp.max(p, axis=0, keepdims=True)
    e = jnp.exp(p - mx)
    lb = e[0:1, :] / jnp.sum(e, axis=0, keepdims=True)
    of_ref[...] = _hg_dir(uf_ref[...], lb[:, :HG_INNER], sf, af_ref, mf_ref, side_ref, True)
    ob_ref[...] = _hg_dir(ub_ref[...], lb[:, HG_INNER:], sb, ab_ref, mb_ref, side_ref, False)

    @pl.when(n == pl.num_programs(1) - 1)
    def _():
        sfo_ref[...] = sf[...]
        sbo_ref[...] = sb[...]


def hg_scan_call(u, lbp, s0f, s0b, row_major_out):
    bsz, t_len, _ = u.shape
    c = HG_CHUNK
    nc = t_len // c
    full = lambda shape: pl.BlockSpec(shape, lambda b, n: (0,) * len(shape))
    st = pl.BlockSpec((None, HG_HEADS, HG_DK, HG_DK), lambda b, n: (b, 0, 0, 0))
    st_shape = jax.ShapeDtypeStruct((bsz, HG_HEADS, HG_DK, HG_DK), F32)
    if row_major_out:
        assert nc == GRID_W and c == t_len // GRID_W
        o_shape = jax.ShapeDtypeStruct((bsz, c, nc * HG_INNER), F32)
        of_spec = pl.BlockSpec((None, c, HG_INNER), lambda b, n: (b, 0, n))
        ob_spec = pl.BlockSpec((None, c, HG_INNER), lambda b, n: (b, 0, nc - 1 - n))
    else:
        o_shape = jax.ShapeDtypeStruct((bsz, t_len, HG_INNER), F32)
        of_spec = pl.BlockSpec((None, c, HG_INNER), lambda b, n: (b, n, 0))
        ob_spec = pl.BlockSpec((None, c, HG_INNER), lambda b, n: (b, nc - 1 - n, 0))
    a_f, a_b, m_f, m_b, side = _hg_consts()
    of, ob, sfo, sbo = pl.pallas_call(
        _hg_scan_kernel,
        grid=(bsz, nc),
        in_specs=[pl.BlockSpec((None, c, B_COLS), lambda b, n: (b, n, 0)),
                  pl.BlockSpec((None, c, B_COLS), lambda b, n: (b, nc - 1 - n, 0)),
                  full((2, 2 * HG_INNER)), st, st,
                  full(a_f.shape), full(a_b.shape), full(m_f.shape), full(m_b.shape), full(side.shape)],
        out_specs=[of_spec, ob_spec, st, st],
        out_shape=[o_shape, o_shape, st_shape, st_shape],
        scratch_shapes=[pltpu.VMEM((HG_HEADS, HG_DK, HG_DK), F32), pltpu.VMEM((HG_HEADS, HG_DK, HG_DK), F32)],
        compiler_params=_cparams(("arbitrary", "arbitrary")),
        name="hg_scan",
    )(u, u, lbp, s0f, s0b, a_f, a_b, m_f, m_b, side)
    if row_major_out:
        of = of.reshape(bsz, t_len, HG_INNER)
        ob = ob.reshape(bsz, t_len, HG_INNER)
    return of, ob, sfo, sbo


def _post_kernel(x_ref, sof_ref, sob_ref, z_ref, hof_ref, hob_ref, hg_ref,
                 snw_ref, hnw_ref, wo_ref, pnw_ref, gm_ref, fnw_ref, shf_ref, scf_ref, rw_ref, rb_ref, tri_ref,
                 x1_ref, hx_ref, idx_ref, rank_ref, gate_ref, cnt_ref, carry, *, tm):
    first = (pl.program_id(0) == 0) & (pl.program_id(1) == 0)

    @pl.when(first)
    def _():
        carry[...] = jnp.zeros_like(carry)

    y = (sof_ref[...] + sob_ref[...]) * _silu(z_ref[...])
    y = _rms(y, snw_ref[...])
    o = hof_ref[...] + hob_ref[...]
    hnw = hnw_ref[...]
    o = jnp.concatenate([_rms(o[:, h * HG_DK:(h + 1) * HG_DK], hnw[:, h * HG_DK:(h + 1) * HG_DK])
                         for h in range(HG_HEADS)], axis=1)
    o = o * _silu(hg_ref[...])
    mix = (jnp.dot(y.astype(BF16), wo_ref[:SSD_INNER, :], preferred_element_type=F32)
           + jnp.dot(o.astype(BF16), wo_ref[SSD_INNER:, :], preferred_element_type=F32))
    x1 = x_ref[...] + gm_ref[...] * _rms(mix, pnw_ref[...])
    x1_ref[...] = x1
    hx = _rms(x1, fnw_ref[...]) * (1.0 + scf_ref[...]) + shf_ref[...]
    hx_ref[...] = hx
    logits = jnp.dot(hx, rw_ref[...], precision=HIGHEST, preferred_element_type=F32) + rb_ref[...]
    lane = lax.broadcasted_iota(jnp.int32, (tm, LANES), 1)
    vals, idxs = [], []
    work = logits
    for _ in range(TOP_K):
        m = jnp.max(work, axis=-1, keepdims=True)
        ix = jnp.min(jnp.where(work == m, lane, LANES), axis=-1, keepdims=True)
        vals.append(m)
        idxs.append(ix)
        work = jnp.where(lane == ix, -jnp.inf, work)
    es = [jnp.exp(v - vals[0]) for v in vals]
    den = es[0] + es[1] + es[2] + es[3]
    onehots = [(lane == ix) for ix in idxs]
    multi = sum(oh.astype(F32) for oh in onehots)
    before = jnp.dot(tri_ref[...], multi.astype(BF16), preferred_element_type=F32) + carry[...]
    carry[...] = carry[...] + jnp.sum(multi, axis=0, keepdims=True)
    idx_o = jnp.zeros((tm, LANES), jnp.int32)
    rank_o = jnp.zeros((tm, LANES), jnp.int32)
    gate_o = jnp.zeros((tm, LANES), F32)
    for k in range(TOP_K):
        rk = jnp.sum(jnp.where(onehots[k], before, 0.0), axis=-1, keepdims=True)
        idx_o = jnp.where(lane == k, idxs[k], idx_o)
        rank_o = jnp.where(lane == k, rk.astype(jnp.int32), rank_o)
        gate_o = jnp.where(lane == k, es[k] / den, gate_o)
    idx_ref[...] = idx_o
    rank_ref[...] = rank_o
    gate_ref[...] = gate_o
    cnt_ref[...] = carry[...]


def post_call(x, sof, sob, z, hof, hob, hg, snw, hnw, wo, pnw, gm, fnw, shf, scf, rw, rb):
    bsz, t_len, _ = x.shape
    tm = TOK_TILE
    nt = t_len // tm
    n_tok = bsz * t_len
    tok = lambda c: pl.BlockSpec((None, tm, c), lambda b, i: (b, i, 0))
    full = lambda shape: pl.BlockSpec(shape, lambda b, i: (0,) * len(shape))
    per_b = pl.BlockSpec((None, 1, D_MODEL), lambda b, i: (b, 0, 0))
    flat = lambda c: pl.BlockSpec((tm, c), lambda b, i: (b * nt + i, 0))
    ii = np.arange(tm)
    tri = jnp.asarray(ii[:, None] > ii[None, :], BF16)
    return pl.pallas_call(
        functools.partial(_post_kernel, tm=tm),
        grid=(bsz, nt),
        in_specs=[tok(D_MODEL), tok(SSD_INNER), tok(SSD_INNER), tok(SSD_INNER), tok(HG_INNER), tok(HG_INNER),
                  tok(HG_INNER), full((1, SSD_INNER)), full((1, HG_INNER)), full((D_MODEL, D_MODEL)),
                  full((1, D_MODEL)), per_b, full((1, D_MODEL)), per_b, per_b,
                  full((D_MODEL, LANES)), full((1, LANES)), full((tm, tm))],
        out_specs=[flat(D_MODEL), flat(D_MODEL), flat(LANES), flat(LANES), flat(LANES),
                   pl.BlockSpec((1, LANES), lambda b, i: (0, 0))],
        out_shape=[jax.ShapeDtypeStruct((n_tok, D_MODEL), F32), jax.ShapeDtypeStruct((n_tok, D_MODEL), F32),
                   jax.ShapeDtypeStruct((n_tok, LANES), jnp.int32), jax.ShapeDtypeStruct((n_tok, LANES), jnp.int32),
                   jax.ShapeDtypeStruct((n_tok, LANES), F32), jax.ShapeDtypeStruct((1, LANES), F32)],
        scratch_shapes=[pltpu.VMEM((1, LANES), F32)],
        compiler_params=_cparams(("arbitrary", "arbitrary")),
        name="post",
    )(x, sof, sob, z, hof, hob, hg, snw, hnw, wo, pnw, gm, fnw, shf, scf, rw, rb, tri)


def _row_copy(src, dst, s_row, d_row, sem):
    return pltpu.make_async_copy(src.at[pl.ds(s_row, 1)], dst.at[pl.ds(d_row, 1)], sem)


def _dispatch_kernel(dest_ref, hx_ref, buf_in_ref, buf_ref, sem, *, tm):
    del buf_in_ref
    base = pl.program_id(0) * tm

    def issue(t, carry):
        for k in range(TOP_K):
            _row_copy(hx_ref, buf_ref, base + t, dest_ref[t * TOP_K + k], sem).start()
        return carry

    lax.fori_loop(0, tm, issue, 0)

    def drain(t, carry):
        _row_copy(hx_ref, buf_ref, 0, 0, sem).wait()
        return carry

    lax.fori_loop(0, tm * TOP_K, drain, 0)


def dispatch_call(dest_flat, hx, buf):
    n_tok = hx.shape[0]
    tm = TOK_TILE
    return pl.pallas_call(
        functools.partial(_dispatch_kernel, tm=tm),
        grid=(n_tok // tm,),
        in_specs=[pl.BlockSpec((tm * TOP_K,), lambda i: (i,), memory_space=pltpu.SMEM),
                  pl.BlockSpec(memory_space=pl.ANY), pl.BlockSpec(memory_space=pl.ANY)],
        out_specs=pl.BlockSpec(memory_space=pl.ANY),
        out_shape=jax.ShapeDtypeStruct(buf.shape, buf.dtype),
        scratch_shapes=[pltpu.SemaphoreType.DMA(())],
        input_output_aliases={2: 0},
        compiler_params=_cparams(("arbitrary",)),
        name="dispatch",
    )(dest_flat, hx, buf)


def _experts_kernel(be_ref, nu_ref, x_ref, w1_ref, b1_ref, w2_ref, b2_ref, y_ref):
    i = pl.program_id(0)

    @pl.when(i < nu_ref[0])
    def _():
        u = jnp.dot(x_ref[...].astype(BF16), w1_ref[...], preferred_element_type=F32) + b1_ref[...]
        glu = jnp.minimum(u[:, :D_FF], SWIGLU_LIMIT)
        lin = jnp.clip(u[:, D_FF:], -SWIGLU_LIMIT, SWIGLU_LIMIT)
        a = glu * _sigmoid(SWIGLU_ALPHA * glu) * (lin + 1.0)
        y_ref[...] = jnp.dot(a.astype(BF16), w2_ref[...], preferred_element_type=F32) + b2_ref[...]

    @pl.when(i >= nu_ref[0])
    def _():
        y_ref[...] = jnp.zeros_like(y_ref)


def experts_call(block_e, n_used, xs, w1p, b1p, w2b, b2):
    rows = xs.shape[0]
    r = MOE_ROWS
    nb = rows // r
    grid_spec = pltpu.PrefetchScalarGridSpec(
        num_scalar_prefetch=2,
        grid=(nb,),
        in_specs=[pl.BlockSpec((r, D_MODEL), lambda i, be, nu: (jnp.minimum(i, nu[0] - 1), 0)),
                  pl.BlockSpec((None, D_MODEL, 2 * D_FF), lambda i, be, nu: (be[i], 0, 0)),
                  pl.BlockSpec((None, 1, 2 * D_FF), lambda i, be, nu: (be[i], 0, 0)),
                  pl.BlockSpec((None, D_FF, D_MODEL), lambda i, be, nu: (be[i], 0, 0)),
                  pl.BlockSpec((None, 1, D_MODEL), lambda i, be, nu: (be[i], 0, 0))],
        out_specs=pl.BlockSpec((r, D_MODEL), lambda i, be, nu: (i, 0)),
    )
    return pl.pallas_call(
        _experts_kernel,
        grid_spec=grid_spec,
        out_shape=jax.ShapeDtypeStruct((rows, D_MODEL), F32),
        compiler_params=_cparams(("arbitrary",)),
        name="experts",
    )(block_e, n_used, xs, w1p, b1p, w2b, b2)


def _combine_kernel(dest_ref, y_ref, gate_ref, x1_ref, gf_ref, nw_ref, o_ref, buf, sem, *, tm):
    def issue(t, carry):
        for k in range(TOP_K):
            pltpu.make_async_copy(y_ref.at[pl.ds(dest_ref[t * TOP_K + k], 1)],
                                  buf.at[k, pl.ds(t, 1)], sem).start()
        return carry

    lax.fori_loop(0, tm, issue, 0)

    def drain(t, carry):
        pltpu.make_async_copy(y_ref.at[pl.ds(0, 1)], buf.at[0, pl.ds(0, 1)], sem).wait()
        return carry

    lax.fori_loop(0, tm * TOP_K, drain, 0)
    gate = gate_ref[...]
    fx = gate[:, 0:1] * buf[0]
    for k in range(1, TOP_K):
        fx = fx + gate[:, k:k + 1] * buf[k]
    o_ref[...] = x1_ref[...] + gf_ref[...] * _rms(fx, nw_ref[...])


def combine_call(dest_flat, ys, gates, x1, gf, nw, bsz):
    n_tok = x1.shape[0]
    tm = TOK_TILE
    nt = n_tok // bsz // tm
    return pl.pallas_call(
        functools.partial(_combine_kernel, tm=tm),
        grid=(n_tok // tm,),
        in_specs=[pl.BlockSpec((tm * TOP_K,), lambda i: (i,), memory_space=pltpu.SMEM),
                  pl.BlockSpec(memory_space=pl.ANY),
                  pl.BlockSpec((tm, LANES), lambda i: (i, 0)),
                  pl.BlockSpec((tm, D_MODEL), lambda i: (i, 0)),
                  pl.BlockSpec((None, 1, D_MODEL), lambda i: (i // nt, 0, 0)),
                  pl.BlockSpec((1, D_MODEL), lambda i: (0, 0))],
        out_specs=pl.BlockSpec((tm, D_MODEL), lambda i: (i, 0)),
        out_shape=jax.ShapeDtypeStruct((n_tok, D_MODEL), F32),
        scratch_shapes=[pltpu.VMEM((TOP_K, tm, D_MODEL), F32), pltpu.SemaphoreType.DMA(())],
        compiler_params=_cparams(("arbitrary",)),
        name="combine",
    )(dest_flat, ys, gates, x1, gf, nw)


def kernel(x, c, ctx, c_ctx, ada_w, ada_b, mix_pre_norm, mix_post_norm, w_in, w_out, ssd_conv_w, ssd_conv_b,
           ssd_dt_bias, ssd_a_log, ssd_d, ssd_norm, hg_lb, hg_norm, ffn_pre_norm, ffn_post_norm, router_w,
           router_b, moe_w1, moe_b1, moe_w2, moe_b2):
    bsz, t_len, d = x.shape
    assert ada_w.shape[0] == 1 and d == D_MODEL and bsz <= SUBLANES - 1
    n_tok = bsz * t_len

    cc = jnp.zeros((SUBLANES, d), F32).at[:bsz].set(c).at[bsz].set(c_ctx)
    mod = ada_call(cc, ada_w[0], ada_b[0][None, :])
    sh_m, sc_m, g_m, sh_f, sc_f, g_f = [m[:bsz, None, :] for m in jnp.split(mod, 6, axis=-1)]
    csh_m, csc_m = [jnp.broadcast_to(m[bsz][None, None, :], (bsz, 1, d)) for m in jnp.split(mod, 6, axis=-1)[:2]]

    w = w_in[0]
    wa = jnp.concatenate([w[:, :SSD_INNER + SSD_XBC], w[:, SSD_COLS + 4 * HG_INNER:]], axis=1).astype(BF16)
    wdt = jnp.zeros((d, LANES), F32).at[:, :2 * SSD_HEADS].set(w[:, SSD_INNER + SSD_XBC:SSD_COLS]).astype(BF16)
    wb = w[:, SSD_COLS:SSD_COLS + 4 * HG_INNER].astype(BF16)
    dtb = jnp.zeros((1, LANES), F32).at[0, :2 * SSD_HEADS].set(ssd_dt_bias[0].reshape(-1))
    nw = mix_pre_norm[0][None, :]
    cw, cb = ssd_conv_w[0], ssd_conv_b[0][None, :]
    neg_a = -jnp.exp(ssd_a_log[0].astype(F32)).reshape(1, 2 * SSD_HEADS)
    dskip = jnp.repeat(ssd_d[0], SSD_HEAD_DIM)[None, :]
    lbp = hg_lb.astype(F32).reshape(2, 2 * HG_INNER)

    _, cxbc, _, cdt, cdtT = inproj_a_call(ctx, csh_m, csc_m, nw, wa, wdt, cw, cb, dtb, tm=ctx.shape[1])
    cu = inproj_b_call(ctx, csh_m, csc_m, nw, wb, col_major=False)
    z_ssd = jnp.zeros((bsz, N_PAIRS, SSD_STATE, LANES), F32)
    z_hg = jnp.zeros((bsz, HG_HEADS, HG_DK, HG_DK), F32)
    _, _, ssf, ssb = ssd_scan_call(cxbc, cdt, cdtT, z_ssd, z_ssd, neg_a, neg_a.reshape(-1, 1), dskip)
    _, _, hsf, hsb = hg_scan_call(cu, lbp, z_hg, z_hg, row_major_out=False)

    zg, xbc, hgate, dt, dtT = inproj_a_call(x, sh_m, sc_m, nw, wa, wdt, cw, cb, dtb, tm=512)
    ub = inproj_b_call(x, sh_m, sc_m, nw, wb, col_major=True)
    sof, sob, _, _ = ssd_scan_call(xbc, dt, dtT, ssf, ssb, neg_a, neg_a.reshape(-1, 1), dskip)
    hof, hob, _, _ = hg_scan_call(ub, lbp, hsf, hsb, row_major_out=True)

    rw = jnp.zeros((d, LANES), F32).at[:, :N_EXPERTS].set(router_w[0])
    rb = jnp.full((1, LANES), NEG_BIG, F32).at[0, :N_EXPERTS].set(router_b[0])
    x1, hx, idx, rank, gates, cnt = post_call(
        x, sof, sob, zg, hof, hob, hgate, ssd_norm[0][None, :], hg_norm[0][None, :], w_out[0].astype(BF16),
        mix_post_norm[0][None, :], g_m, ffn_pre_norm[0][None, :], sh_f, sc_f, rw, rb)

    r = MOE_ROWS
    n_blocks = (n_tok * TOP_K + N_EXPERTS * (r - 1) + r - 1) // r
    counts = cnt[0, :N_EXPERTS].astype(jnp.int32)
    padded = (counts + r - 1) // r * r
    pend = jnp.cumsum(padded)
    pstart = pend - padded
    dest = (pstart[idx[:, :TOP_K]] + rank[:, :TOP_K]).reshape(-1)
    block_e = jnp.clip(jnp.searchsorted(pend, jnp.arange(n_blocks, dtype=jnp.int32) * r, side='right'),
                       0, N_EXPERTS - 1).astype(jnp.int32)
    n_used = (pend[-1:] // r).astype(jnp.int32)

    xs = dispatch_call(dest, hx, jnp.zeros((n_blocks * r, d), F32))
    w1p = jnp.concatenate([moe_w1[0][:, :, 0::2], moe_w1[0][:, :, 1::2]], axis=-1).astype(BF16)
    b1p = jnp.concatenate([moe_b1[0][:, 0::2], moe_b1[0][:, 1::2]], axis=-1)[:, None, :]
    ys = experts_call(block_e, n_used, xs, w1p, b1p, moe_w2[0].astype(BF16), moe_b2[0][:, None, :])
    out = combine_call(dest, ys, gates, x1, g_f, ffn_post_norm[0][None, :], bsz)
    return out.reshape(bsz, t_len, d)
```

```python
import functools
import math

import numpy as np
import jax
import jax.numpy as jnp
from jax import lax
from jax.experimental import pallas as pl
from jax.experimental.pallas import tpu as pltpu

F32 = jnp.float32
BF16 = jnp.bfloat16
HIGHEST = lax.Precision.HIGHEST

D_MODEL = 1024
GRID_W = 64
SSD_HEADS = 8
SSD_HEAD_DIM = 64
SSD_INNER = 512
SSD_STATE = 64
SSD_GROUPS = 2
SSD_CONV = 5
SSD_XBC = 768
SSD_COLS = 1296
HG_HEADS = 4
HG_DK = 128
HG_INNER = 512
N_EXPERTS = 32
TOP_K = 4
D_FF = 1024
SWIGLU_ALPHA = 1.702
SWIGLU_LIMIT = 7.0
EPS = 1e-6

LANES = 128
SUBLANES = 8
VMEM_LIMIT = 56 * 1024 * 1024

SSD_CHUNK = 128
HG_CHUNK = 64
HG_LEVELS = 6
MOE_ROWS = 256
TOK_TILE = 256
NEG_BIG = -1e30


def _cparams(sem):
    return pltpu.CompilerParams(dimension_semantics=sem, vmem_limit_bytes=VMEM_LIMIT)


def _sigmoid(x):
    return 1.0 / (1.0 + jnp.exp(-x))


def _silu(x):
    return x * _sigmoid(x)


def _rms(x, w):
    return x * lax.rsqrt(jnp.mean(x * x, axis=-1, keepdims=True) + EPS) * w


def _split3(v):
    hi = v.astype(BF16)
    r1 = v - hi.astype(F32)
    mid = r1.astype(BF16)
    lo = (r1 - mid.astype(F32)).astype(BF16)
    return hi, mid, lo


def _ada_kernel(c_ref, w_ref, b_ref, o_ref):
    s = _silu(c_ref[...])
    o_ref[...] = jnp.dot(s, w_ref[...], precision=HIGHEST, preferred_element_type=F32) + b_ref[...]


def ada_call(cc, w, b):
    n = w.shape[1]
    tn = 1536
    return pl.pallas_call(
        _ada_kernel,
        grid=(n // tn,),
        in_specs=[pl.BlockSpec((SUBLANES, D_MODEL), lambda j: (0, 0)),
                  pl.BlockSpec((D_MODEL, tn), lambda j: (0, j)),
                  pl.BlockSpec((1, tn), lambda j: (0, j))],
        out_specs=pl.BlockSpec((SUBLANES, tn), lambda j: (0, j)),
        out_shape=jax.ShapeDtypeStruct((SUBLANES, n), F32),
        compiler_params=_cparams(("arbitrary",)),
        name="ada",
    )(cc, w, b)


A_COLS = SSD_INNER + SSD_XBC + HG_INNER


def _prep(xt, nw, sh, sc):
    return (_rms(xt, nw) * (1.0 + sc) + sh).astype(BF16)


def _inproj_a_kernel(x_ref, xp_ref, xn_ref, sh_ref, sc_ref, nw_ref, wa_ref, wdt_ref, cw_ref, cb_ref, dtb_ref,
                     z_ref, xbc_ref, g_ref, dt_ref, dtT_ref, scr, *, tm):
    i = pl.program_id(1)
    last = pl.num_programs(1) - 1
    nw, sh, sc = nw_ref[...], sh_ref[...], sc_ref[...]
    h = _prep(x_ref[...], nw, sh, sc)
    ua = jnp.dot(h, wa_ref[...], preferred_element_type=F32)
    z_ref[...] = ua[:, :SSD_INNER]
    g_ref[...] = ua[:, SSD_INNER + SSD_XBC:]
    wx = wa_ref[:, SSD_INNER:SSD_INNER + SSD_XBC]
    up = jnp.dot(_prep(xp_ref[...], nw, sh, sc), wx, preferred_element_type=F32)
    un = jnp.dot(_prep(xn_ref[...], nw, sh, sc), wx, preferred_element_type=F32)
    scr[0:SUBLANES, :] = jnp.where(i > 0, up, 0.0)
    scr[SUBLANES:SUBLANES + tm, :] = ua[:, SSD_INNER:SSD_INNER + SSD_XBC]
    scr[SUBLANES + tm:, :] = jnp.where(i < last, un, 0.0)
    acc = jnp.broadcast_to(cb_ref[...], (tm, SSD_XBC))
    pad = SSD_CONV // 2
    for k in range(SSD_CONV):
        off = SUBLANES - pad + k
        acc = acc + cw_ref[k:k + 1, :] * scr[off:off + tm, :]
    xbc_ref[...] = _silu(acc)
    draw = jnp.dot(h, wdt_ref[...], preferred_element_type=F32) + dtb_ref[...]
    dt = jnp.maximum(draw, 0.0) + jnp.log(1.0 + jnp.exp(-jnp.abs(draw)))
    dt_ref[...] = dt[:, :2 * SSD_HEADS]
    dtT_ref[...] = dt.T[:2 * SSD_HEADS, :]


def inproj_a_call(x, sh, sc, nw, wa, wdt, cw, cb, dtb, tm):
    bsz, t_len, _ = x.shape
    nt = t_len // tm
    r8 = tm // SUBLANES
    n8 = t_len // SUBLANES
    full = lambda shape: pl.BlockSpec(shape, lambda b, i: (0,) * len(shape))
    tok = lambda c: pl.BlockSpec((None, tm, c), lambda b, i: (b, i, 0))
    return pl.pallas_call(
        functools.partial(_inproj_a_kernel, tm=tm),
        grid=(bsz, nt),
        in_specs=[tok(D_MODEL),
                  pl.BlockSpec((None, SUBLANES, D_MODEL), lambda b, i: (b, jnp.maximum(i * r8 - 1, 0), 0)),
                  pl.BlockSpec((None, SUBLANES, D_MODEL), lambda b, i: (b, jnp.minimum((i + 1) * r8, n8 - 1), 0)),
                  pl.BlockSpec((None, 1, D_MODEL), lambda b, i: (b, 0, 0)),
                  pl.BlockSpec((None, 1, D_MODEL), lambda b, i: (b, 0, 0)),
                  full((1, D_MODEL)), full((D_MODEL, A_COLS)), full((D_MODEL, LANES)),
                  full((SSD_CONV, SSD_XBC)), full((1, SSD_XBC)), full((1, LANES))],
        out_specs=[tok(SSD_INNER), tok(SSD_XBC), tok(HG_INNER), tok(2 * SSD_HEADS),
                   pl.BlockSpec((None, 2 * SSD_HEADS, tm), lambda b, i: (b, 0, i))],
        out_shape=[jax.ShapeDtypeStruct((bsz, t_len, SSD_INNER), F32),
                   jax.ShapeDtypeStruct((bsz, t_len, SSD_XBC), F32),
                   jax.ShapeDtypeStruct((bsz, t_len, HG_INNER), F32),
                   jax.ShapeDtypeStruct((bsz, t_len, 2 * SSD_HEADS), F32),
                   jax.ShapeDtypeStruct((bsz, 2 * SSD_HEADS, t_len), F32)],
        scratch_shapes=[pltpu.VMEM((tm + 2 * SUBLANES, SSD_XBC), F32)],
        compiler_params=_cparams(("arbitrary", "arbitrary")),
        name="inproj_a",
    )(x, x, x, sh, sc, nw, wa, wdt, cw, cb, dtb)


B_COLS = 4 * HG_INNER


def _inproj_b_kernel(x_ref, sh_ref, sc_ref, nw_ref, w_ref, o_ref, *, ncol):
    if ncol:
        xt = jnp.concatenate([x_ref[:, w * D_MODEL:(w + 1) * D_MODEL] for w in range(ncol)], axis=0)
    else:
        xt = x_ref[...]
    h = _prep(xt, nw_ref[...], sh_ref[...], sc_ref[...])
    o_ref[...] = jnp.dot(h, w_ref[...], preferred_element_type=F32)


def inproj_b_call(x, sh, sc, nw, wb, col_major):
    bsz, t_len, _ = x.shape
    if col_major:
        ncol = 8
        tm = ncol * GRID_W
        rows = t_len // GRID_W
        assert rows == GRID_W
        xin = x.reshape(bsz, rows, GRID_W * D_MODEL)
        x_spec = pl.BlockSpec((None, rows, ncol * D_MODEL), lambda b, i: (b, 0, i))
    else:
        ncol = 0
        tm = t_len
        xin = x
        x_spec = pl.BlockSpec((None, tm, D_MODEL), lambda b, i: (b, i, 0))
    full = lambda shape: pl.BlockSpec(shape, lambda b, i: (0,) * len(shape))
    return pl.pallas_call(
        functools.partial(_inproj_b_kernel, ncol=ncol),
        grid=(bsz, t_len // tm),
        in_specs=[x_spec,
                  pl.BlockSpec((None, 1, D_MODEL), lambda b, i: (b, 0, 0)),
                  pl.BlockSpec((None, 1, D_MODEL), lambda b, i: (b, 0, 0)),
                  full((1, D_MODEL)), full((D_MODEL, B_COLS))],
        out_specs=pl.BlockSpec((None, tm, B_COLS), lambda b, i: (b, i, 0)),
        out_shape=jax.ShapeDtypeStruct((bsz, t_len, B_COLS), F32),
        compiler_params=_cparams(("arbitrary", "arbitrary")),
        name="inproj_b",
    )(xin, sh, sc, nw, wb)


N_PAIRS = SSD_HEADS // 2


def _ssd_dir(xbc, dt, dtT, s_ref, tri, triT, na_row, na_col, fwd):
    c = SSD_CHUNK
    xs = xbc[:, :SSD_INNER]
    bm = xbc[:, SSD_INNER:SSD_INNER + LANES]
    cm = xbc[:, SSD_INNER + LANES:]
    col0 = 0 if fwd else SSD_HEADS
    la = dt[:, col0:col0 + SSD_HEADS] * na_row[:, col0:col0 + SSD_HEADS]
    dtr = dtT[col0:col0 + SSD_HEADS, :]
    laT = dtr * na_col[col0:col0 + SSD_HEADS, :]
    g = sum(jnp.dot(tri, p, preferred_element_type=F32) for p in _split3(la))
    gT = sum(jnp.dot(p, triT, preferred_element_type=F32) for p in _split3(laT))
    end = c - 1 if fwd else 0
    bmT = bm.T
    ii = lax.broadcasted_iota(jnp.int32, (c, c), 0)
    jj = lax.broadcasted_iota(jnp.int32, (c, c), 1)
    causal = (jj <= ii) if fwd else (jj >= ii)
    lane = lax.broadcasted_iota(jnp.int32, (c, LANES), 1)
    lo_half = lane < SSD_HEAD_DIM
    lane_s = lax.broadcasted_iota(jnp.int32, (SSD_STATE, LANES), 1) < SSD_HEAD_DIM
    outs = []
    for grp in range(SSD_GROUPS):
        in_grp = (lane >= grp * SSD_STATE) & (lane < (grp + 1) * SSD_STATE)
        cm_g = jnp.where(in_grp, cm, 0.0).astype(BF16)
        gmat = jnp.dot(cm_g, bmT.astype(BF16), preferred_element_type=F32)
        bmT_g = bmT[grp * SSD_STATE:(grp + 1) * SSD_STATE, :]
        for pp in range(N_PAIRS // SSD_GROUPS):
            pair = grp * (N_PAIRS // SSD_GROUPS) + pp
            heads = (2 * pair, 2 * pair + 1)
            xs_p = xs[:, pair * LANES:(pair + 1) * LANES]
            xbd = jnp.concatenate([jnp.where(lo_half, xs_p, 0.0), jnp.where(lo_half, 0.0, xs_p)],
                                  axis=0).astype(BF16)
            ms, bws, ecols, arows = [], [], [], []
            for hd in heads:
                gcol = g[:, hd:hd + 1]
                grow = gT[hd:hd + 1, :]
                dec = jnp.exp(jnp.where(causal, gcol - grow, -jnp.inf))
                ms.append(gmat * dec * dtr[hd:hd + 1, :])
                glast = grow[:, end:end + 1]
                bws.append(bmT_g * (dtr[hd:hd + 1, :] * jnp.exp(glast - grow)))
                ecols.append(jnp.exp(gcol))
                arows.append(jnp.exp(glast))
            mcat = jnp.concatenate(ms, axis=1).astype(BF16)
            s_old = s_ref[pair]
            zeros = jnp.zeros_like(s_old)
            s_pad = jnp.concatenate([s_old, zeros] if grp == 0 else [zeros, s_old], axis=0).astype(BF16)
            o_inter = jnp.dot(cm.astype(BF16), s_pad, preferred_element_type=F32)
            o_inter = o_inter * jnp.where(lo_half, ecols[0], ecols[1])
            outs.append(jnp.dot(mcat, xbd, preferred_element_type=F32) + o_inter)
            bw = jnp.concatenate(bws, axis=1).astype(BF16)
            s_ref[pair] = (s_old * jnp.where(lane_s, arows[0], arows[1])
                           + jnp.dot(bw, xbd, preferred_element_type=F32))
    return outs


def _ssd_scan_kernel(xf_ref, dtf_ref, dtTf_ref, xb_ref, dtb_ref, dtTb_ref, s0f_ref, s0b_ref,
                     trif_ref, trifT_ref, trib_ref, tribT_ref, nar_ref, nac_ref, dsk_ref,
                     of_ref, ob_ref, sfo_ref, sbo_ref, sf, sb):
    n = pl.program_id(1)

    @pl.when(n == 0)
    def _():
        sf[...] = s0f_ref[...]
        sb[...] = s0b_ref[...]

    xf = xf_ref[...]
    outs = _ssd_dir(xf, dtf_ref[...], dtTf_ref[...], sf, trif_ref[...], trifT_ref[...],
                    nar_ref[...], nac_ref[...], True)
    of_ref[...] = jnp.concatenate(outs, axis=1) + dsk_ref[...] * xf[:, :SSD_INNER]
    outs = _ssd_dir(xb_ref[...], dtb_ref[...], dtTb_ref[...], sb, trib_ref[...], tribT_ref[...],
                    nar_ref[...], nac_ref[...], False)
    ob_ref[...] = jnp.concatenate(outs, axis=1)

    @pl.when(n == pl.num_programs(1) - 1)
    def _():
        sfo_ref[...] = sf[...]
        sbo_ref[...] = sb[...]


def _ssd_consts():
    c = SSD_CHUNK
    i = np.arange(c)
    trif = (i[:, None] >= i[None, :]).astype(np.float32)
    trib = (i[:, None] <= i[None, :]).astype(np.float32)
    return [jnp.asarray(a, BF16) for a in (trif, trif.T, trib, trib.T)]


def ssd_scan_call(xbc, dt, dtT, s0f, s0b, na_row, na_col, dskip):
    bsz, t_len, _ = xbc.shape
    c = SSD_CHUNK
    nc = t_len // c
    fw = lambda w: pl.BlockSpec((None, c, w), lambda b, n: (b, n, 0))
    bw = lambda w: pl.BlockSpec((None, c, w), lambda b, n: (b, nc - 1 - n, 0))
    full = lambda shape: pl.BlockSpec(shape, lambda b, n: (0,) * len(shape))
    st = pl.BlockSpec((None, N_PAIRS, SSD_STATE, LANES), lambda b, n: (b, 0, 0, 0))
    st_shape = jax.ShapeDtypeStruct((bsz, N_PAIRS, SSD_STATE, LANES), F32)
    return pl.pallas_call(
        _ssd_scan_kernel,
        grid=(bsz, nc),
        in_specs=[fw(SSD_XBC), fw(2 * SSD_HEADS),
                  pl.BlockSpec((None, 2 * SSD_HEADS, c), lambda b, n: (b, 0, n)),
                  bw(SSD_XBC), bw(2 * SSD_HEADS),
                  pl.BlockSpec((None, 2 * SSD_HEADS, c), lambda b, n: (b, 0, nc - 1 - n)),
                  st, st, full((c, c)), full((c, c)), full((c, c)), full((c, c)),
                  full((1, 2 * SSD_HEADS)), full((2 * SSD_HEADS, 1)), full((1, SSD_INNER))],
        out_specs=[fw(SSD_INNER), bw(SSD_INNER), st, st],
        out_shape=[jax.ShapeDtypeStruct((bsz, t_len, SSD_INNER), F32),
                   jax.ShapeDtypeStruct((bsz, t_len, SSD_INNER), F32), st_shape, st_shape],
        scratch_shapes=[pltpu.VMEM((N_PAIRS, SSD_STATE, LANES), F32),
                        pltpu.VMEM((N_PAIRS, SSD_STATE, LANES), F32)],
        compiler_params=_cparams(("arbitrary", "arbitrary")),
        name="ssd_scan",
    )(xbc, dt, dtT, xbc, dt, dtT, s0f, s0b, *_ssd_consts(), na_row, na_col, dskip)


HG_NSUM = HG_LEVELS + 2


def _hg_consts():
    c = HG_CHUNK
    t = np.arange(c)
    a_f = np.zeros((HG_NSUM, c, c), np.float32)
    a_b = np.zeros((HG_NSUM, c, c), np.float32)
    m_f = np.zeros((HG_LEVELS + 1, c, c), np.float32)
    m_f[0] = np.eye(c)
    for lv in range(1, HG_LEVELS + 1):
        m = 2 ** lv
        half = m // 2
        for r in range(c):
            base = (r // m) * m
            mid = base + half - 1
            if r - base >= half:
                a_f[lv - 1, r, mid + 1:r + 1] = 1.0
                a_b[lv - 1, r, mid + 1:r] = 1.0
            else:
                a_f[lv - 1, r, r + 1:mid + 1] = 1.0
                a_b[lv - 1, r, r:mid + 1] = 1.0
        blk = t // m
        right = (t % m) >= half
        m_f[lv] = (blk[:, None] == blk[None, :]) & right[:, None] & (~right[None, :])
    a_f[HG_LEVELS] = t[:, None] >= t[None, :]
    a_f[HG_LEVELS + 1] = t[None, :] > t[:, None]
    a_b[HG_LEVELS] = t[None, :] >= t[:, None]
    a_b[HG_LEVELS + 1] = t[None, :] < t[:, None]
    m_b = np.transpose(m_f, (0, 2, 1))
    side = np.stack([(t % (2 ** lv)) >= (2 ** lv) // 2 for lv in range(1, HG_LEVELS + 1)]).astype(np.float32)
    side = np.repeat(side[:, :, None], LANES, axis=2)
    return (jnp.asarray(a_f.reshape(HG_NSUM * c, c), BF16), jnp.asarray(a_b.reshape(HG_NSUM * c, c), BF16),
            jnp.asarray(m_f, F32), jnp.asarray(m_b, F32), jnp.asarray(side, F32))


def _hg_dir(u, lb_row, st_ref, a_ref, mask_ref, side_ref, fwd):
    c = HG_CHUNK
    outs = []
    fcol = HG_INNER if fwd else 2 * HG_INNER
    nt = (((1,), (1,)), ((), ()))
    for hd in range(HG_HEADS):
        sl = slice(hd * HG_DK, (hd + 1) * HG_DK)
        q = _silu(u[:, sl]) * (HG_DK ** -0.5)
        lb = lb_row[:, sl]
        f = lb + (1.0 - lb) * _sigmoid(u[:, fcol + hd * HG_DK:fcol + (hd + 1) * HG_DK])
        k = 1.0 - f
        la = jnp.log(f)
        v = u[:, 3 * HG_INNER + hd * HG_DK:3 * HG_INNER + (hd + 1) * HG_DK]
        pieces = jnp.concatenate(_split3(la), axis=1)
        ne3 = jnp.dot(a_ref[...], pieces, preferred_element_type=F32)
        x_all = jnp.exp(ne3[:, :LANES] + ne3[:, LANES:2 * LANES] + ne3[:, 2 * LANES:])
        scores = mask_ref[0] * lax.dot_general(q.astype(BF16), k.astype(BF16), nt, preferred_element_type=F32)
        for lv in range(1, HG_LEVELS + 1):
            x_l = x_all[(lv - 1) * c:lv * c, :]
            is_q = side_ref[lv - 1] > 0.5
            if not fwd:
                is_q = jnp.logical_not(is_q)
            y = (jnp.where(is_q, q, k) * x_l).astype(BF16)
            scores = scores + mask_ref[lv] * lax.dot_general(y, y, nt, preferred_element_type=F32)
        x_cum = x_all[HG_LEVELS * c:(HG_LEVELS + 1) * c, :]
        x_rem = x_all[(HG_LEVELS + 1) * c:, :]
        st = st_ref[hd]
        o = jnp.dot(scores.astype(BF16), v.astype(BF16), preferred_element_type=F32)
        o = o + lax.dot_general((q * x_cum).astype(BF16), st.astype(BF16), nt, preferred_element_type=F32)
        outs.append(o)
        end = c - 1 if fwd else 0
        st_ref[hd] = (st * x_cum[end:end + 1, :]
                      + jnp.dot(v.T.astype(BF16), (k * x_rem).astype(BF16), preferred_element_type=F32))
    return jnp.concatenate(outs, axis=1)


def _hg_scan_kernel(uf_ref, ub_ref, lbp_ref, s0f_ref, s0b_ref, af_ref, ab_ref, mf_ref, mb_ref, side_ref,
                    of_ref, ob_ref, sfo_ref, sbo_ref, sf, sb):
    n = pl.program_id(1)

    @pl.when(n == 0)
    def _():
        sf[...] = s0f_ref[...]
        sb[...] = s0b_ref[...]

    p = lbp_ref[...]
    mx = jnp.max(p, axis=0, keepdims=True)
    e = jnp.exp(p - mx)
    lb = e[0:1, :] / jnp.sum(e, axis=0, keepdims=True)
    of_ref[...] = _hg_dir(uf_ref[...], lb[:, :HG_INNER], sf, af_ref, mf_ref, side_ref, True)
    ob_ref[...] = _hg_dir(ub_ref[...], lb[:, HG_INNER:], sb, ab_ref, mb_ref, side_ref, False)

    @pl.when(n == pl.num_programs(1) - 1)
    def _():
        sfo_ref[...] = sf[...]
        sbo_ref[...] = sb[...]


def hg_scan_call(u, lbp, s0f, s0b, row_major_out):
    bsz, t_len, _ = u.shape
    c = HG_CHUNK
    nc = t_len // c
    full = lambda shape: pl.BlockSpec(shape, lambda b, n: (0,) * len(shape))
    st = pl.BlockSpec((None, HG_HEADS, HG_DK, HG_DK), lambda b, n: (b, 0, 0, 0))
    st_shape = jax.ShapeDtypeStruct((bsz, HG_HEADS, HG_DK, HG_DK), F32)
    if row_major_out:
        assert nc == GRID_W and c == t_len // GRID_W
        o_shape = jax.ShapeDtypeStruct((bsz, c, nc * HG_INNER), F32)
        of_spec = pl.BlockSpec((None, c, HG_INNER), lambda b, n: (b, 0, n))
        ob_spec = pl.BlockSpec((None, c, HG_INNER), lambda b, n: (b, 0, nc - 1 - n))
    else:
        o_shape = jax.ShapeDtypeStruct((bsz, t_len, HG_INNER), F32)
        of_spec = pl.BlockSpec((None, c, HG_INNER), lambda b, n: (b, n, 0))
        ob_spec = pl.BlockSpec((None, c, HG_INNER), lambda b, n: (b, nc - 1 - n, 0))
    a_f, a_b, m_f, m_b, side = _hg_consts()
    of, ob, sfo, sbo = pl.pallas_call(
        _hg_scan_kernel,
        grid=(bsz, nc),
        in_specs=[pl.BlockSpec((None, c, B_COLS), lambda b, n: (b, n, 0)),
                  pl.BlockSpec((None, c, B_COLS), lambda b, n: (b, nc - 1 - n, 0)),
                  full((2, 2 * HG_INNER)), st, st,
                  full(a_f.shape), full(a_b.shape), full(m_f.shape), full(m_b.shape), full(side.shape)],
        out_specs=[of_spec, ob_spec, st, st],
        out_shape=[o_shape, o_shape, st_shape, st_shape],
        scratch_shapes=[pltpu.VMEM((HG_HEADS, HG_DK, HG_DK), F32), pltpu.VMEM((HG_HEADS, HG_DK, HG_DK), F32)],
        compiler_params=_cparams(("arbitrary", "arbitrary")),
        name="hg_scan",
    )(u, u, lbp, s0f, s0b, a_f, a_b, m_f, m_b, side)
    if row_major_out:
        of = of.reshape(bsz, t_len, HG_INNER)
        ob = ob.reshape(bsz, t_len, HG_INNER)
    return of, ob, sfo, sbo


def _post_kernel(x_ref, sof_ref, sob_ref, z_ref, hof_ref, hob_ref, hg_ref,
                 snw_ref, hnw_ref, wo_ref, pnw_ref, gm_ref, fnw_ref, shf_ref, scf_ref, rw_ref, rb_ref, tri_ref,
                 x1_ref, hx_ref, idx_ref, rank_ref, gate_ref, cnt_ref, carry, *, tm):
    first = (pl.program_id(0) == 0) & (pl.program_id(1) == 0)

    @pl.when(first)
    def _():
        carry[...] = jnp.zeros_like(carry)

    y = (sof_ref[...] + sob_ref[...]) * _silu(z_ref[...])
    y = _rms(y, snw_ref[...])
    o = hof_ref[...] + hob_ref[...]
    hnw = hnw_ref[...]
    o = jnp.concatenate([_rms(o[:, h * HG_DK:(h + 1) * HG_DK], hnw[:, h * HG_DK:(h + 1) * HG_DK])
                         for h in range(HG_HEADS)], axis=1)
    o = o * _silu(hg_ref[...])
    mix = (jnp.dot(y.astype(BF16), wo_ref[:SSD_INNER, :], preferred_element_type=F32)
           + jnp.dot(o.astype(BF16), wo_ref[SSD_INNER:, :], preferred_element_type=F32))
    x1 = x_ref[...] + gm_ref[...] * _rms(mix, pnw_ref[...])
    x1_ref[...] = x1
    hx = _rms(x1, fnw_ref[...]) * (1.0 + scf_ref[...]) + shf_ref[...]
    hx_ref[...] = hx
    logits = jnp.dot(hx, rw_ref[...], precision=HIGHEST, preferred_element_type=F32) + rb_ref[...]
    lane = lax.broadcasted_iota(jnp.int32, (tm, LANES), 1)
    vals, idxs = [], []
    work = logits
    for _ in range(TOP_K):
        m = jnp.max(work, axis=-1, keepdims=True)
        ix = jnp.min(jnp.where(work == m, lane, LANES), axis=-1, keepdims=True)
        vals.append(m)
        idxs.append(ix)
        work = jnp.where(lane == ix, -jnp.inf, work)
    es = [jnp.exp(v - vals[0]) for v in vals]
    den = es[0] + es[1] + es[2] + es[3]
    onehots = [(lane == ix) for ix in idxs]
    multi = sum(oh.astype(F32) for oh in onehots)
    before = jnp.dot(tri_ref[...], multi.astype(BF16), preferred_element_type=F32) + carry[...]
    carry[...] = carry[...] + jnp.sum(multi, axis=0, keepdims=True)
    idx_o = jnp.zeros((tm, LANES), jnp.int32)
    rank_o = jnp.zeros((tm, LANES), jnp.int32)
    gate_o = jnp.zeros((tm, LANES), F32)
    for k in range(TOP_K):
        rk = jnp.sum(jnp.where(onehots[k], before, 0.0), axis=-1, keepdims=True)
        idx_o = jnp.where(lane == k, idxs[k], idx_o)
        rank_o = jnp.where(lane == k, rk.astype(jnp.int32), rank_o)
        gate_o = jnp.where(lane == k, es[k] / den, gate_o)
    idx_ref[...] = idx_o
    rank_ref[...] = rank_o
    gate_ref[...] = gate_o
    cnt_ref[...] = carry[...]


def post_call(x, sof, sob, z, hof, hob, hg, snw, hnw, wo, pnw, gm, fnw, shf, scf, rw, rb):
    bsz, t_len, _ = x.shape
    tm = TOK_TILE
    nt = t_len // tm
    n_tok = bsz * t_len
    tok = lambda c: pl.BlockSpec((None, tm, c), lambda b, i: (b, i, 0))
    full = lambda shape: pl.BlockSpec(shape, lambda b, i: (0,) * len(shape))
    per_b = pl.BlockSpec((None, 1, D_MODEL), lambda b, i: (b, 0, 0))
    flat = lambda c: pl.BlockSpec((tm, c), lambda b, i: (b * nt + i, 0))
    ii = np.arange(tm)
    tri = jnp.asarray(ii[:, None] > ii[None, :], BF16)
    return pl.pallas_call(
        functools.partial(_post_kernel, tm=tm),
        grid=(bsz, nt),
        in_specs=[tok(D_MODEL), tok(SSD_INNER), tok(SSD_INNER), tok(SSD_INNER), tok(HG_INNER), tok(HG_INNER),
                  tok(HG_INNER), full((1, SSD_INNER)), full((1, HG_INNER)), full((D_MODEL, D_MODEL)),
                  full((1, D_MODEL)), per_b, full((1, D_MODEL)), per_b, per_b,
                  full((D_MODEL, LANES)), full((1, LANES)), full((tm, tm))],
        out_specs=[flat(D_MODEL), flat(D_MODEL), flat(LANES), flat(LANES), flat(LANES),
                   pl.BlockSpec((1, LANES), lambda b, i: (0, 0))],
        out_shape=[jax.ShapeDtypeStruct((n_tok, D_MODEL), F32), jax.ShapeDtypeStruct((n_tok, D_MODEL), F32),
                   jax.ShapeDtypeStruct((n_tok, LANES), jnp.int32), jax.ShapeDtypeStruct((n_tok, LANES), jnp.int32),
                   jax.ShapeDtypeStruct((n_tok, LANES), F32), jax.ShapeDtypeStruct((1, LANES), F32)],
        scratch_shapes=[pltpu.VMEM((1, LANES), F32)],
        compiler_params=_cparams(("arbitrary", "arbitrary")),
        name="post",
    )(x, sof, sob, z, hof, hob, hg, snw, hnw, wo, pnw, gm, fnw, shf, scf, rw, rb, tri)


def _wait_rows(hbm_ref, n_rows, sem):
    pltpu.make_async_copy(hbm_ref.at[pl.ds(0, n_rows)], hbm_ref.at[pl.ds(0, n_rows)], sem).wait()


def _dispatch_kernel(dest_ref, hx_ref, buf_in_ref, buf_ref, sem, *, tm):
    del buf_in_ref

    def issue(t, carry):
        for k in range(TOP_K):
            pltpu.make_async_copy(hx_ref.at[pl.ds(t, 1)], buf_ref.at[pl.ds(dest_ref[t * TOP_K + k], 1)],
                                  sem).start()
        return carry

    lax.fori_loop(0, tm, issue, 0, unroll=4)
    _wait_rows(buf_ref, tm * TOP_K, sem)


def dispatch_call(dest_flat, hx, buf):
    n_tok = hx.shape[0]
    tm = TOK_TILE
    return pl.pallas_call(
        functools.partial(_dispatch_kernel, tm=tm),
        grid=(n_tok // tm,),
        in_specs=[pl.BlockSpec((tm * TOP_K,), lambda i: (i,), memory_space=pltpu.SMEM),
                  pl.BlockSpec((tm, D_MODEL), lambda i: (i, 0)), pl.BlockSpec(memory_space=pl.ANY)],
        out_specs=pl.BlockSpec(memory_space=pl.ANY),
        out_shape=jax.ShapeDtypeStruct(buf.shape, buf.dtype),
        scratch_shapes=[pltpu.SemaphoreType.DMA(())],
        input_output_aliases={2: 0},
        compiler_params=_cparams(("arbitrary",)),
        name="dispatch",
    )(dest_flat, hx, buf)


W1_TCOLS = 256


def _experts_kernel(be_ref, nu_ref, x_ref, w1_ref, b1_ref, w2_ref, b2_ref, y_ref, w1t, w2s, tbuf):
    i = pl.program_id(0)
    live = i < nu_ref[0]
    new_expert = (i == 0) | (be_ref[i] != be_ref[jnp.maximum(i - 1, 0)])

    @pl.when(live & new_expert)
    def _():
        half = W1_TCOLS // 2
        for c in range(2 * D_FF // W1_TCOLS):
            for j in range(D_MODEL // LANES):
                ks = slice(j * LANES, (j + 1) * LANES)
                tbuf[j] = w1_ref[ks, c * W1_TCOLS:(c + 1) * W1_TCOLS].T
                w1t[c * half:(c + 1) * half, ks] = tbuf[j, pl.ds(0, half, stride=2), :].astype(BF16)
                w1t[D_FF + c * half:D_FF + (c + 1) * half, ks] = tbuf[j, pl.ds(1, half, stride=2), :].astype(BF16)
        w2s[...] = w2_ref[...].astype(BF16)

    @pl.when(live)
    def _():
        u = lax.dot_general(x_ref[...].astype(BF16), w1t[...], (((1,), (1,)), ((), ())),
                            preferred_element_type=F32) + b1_ref[...]
        glu = jnp.minimum(u[:, :D_FF], SWIGLU_LIMIT)
        lin = jnp.clip(u[:, D_FF:], -SWIGLU_LIMIT, SWIGLU_LIMIT)
        a = glu * _sigmoid(SWIGLU_ALPHA * glu) * (lin + 1.0)
        y_ref[...] = jnp.dot(a.astype(BF16), w2s[...], preferred_element_type=F32) + b2_ref[...]

    @pl.when(jnp.logical_not(live))
    def _():
        y_ref[...] = jnp.zeros_like(y_ref)


def experts_call(block_e, n_used, xs, w1, b1p, w2, b2):
    rows = xs.shape[0]
    r = MOE_ROWS
    nb = rows // r
    grid_spec = pltpu.PrefetchScalarGridSpec(
        num_scalar_prefetch=2,
        grid=(nb,),
        in_specs=[pl.BlockSpec((r, D_MODEL), lambda i, be, nu: (jnp.minimum(i, nu[0] - 1), 0)),
                  pl.BlockSpec((None, D_MODEL, 2 * D_FF), lambda i, be, nu: (be[i], 0, 0)),
                  pl.BlockSpec((None, 1, 2 * D_FF), lambda i, be, nu: (be[i], 0, 0)),
                  pl.BlockSpec((None, D_FF, D_MODEL), lambda i, be, nu: (be[i], 0, 0)),
                  pl.BlockSpec((None, 1, D_MODEL), lambda i, be, nu: (be[i], 0, 0))],
        out_specs=pl.BlockSpec((r, D_MODEL), lambda i, be, nu: (i, 0)),
        scratch_shapes=[pltpu.VMEM((2 * D_FF, D_MODEL), BF16), pltpu.VMEM((D_FF, D_MODEL), BF16),
                        pltpu.VMEM((D_MODEL // LANES, W1_TCOLS, LANES), F32)],
    )
    return pl.pallas_call(
        _experts_kernel,
        grid_spec=grid_spec,
        out_shape=jax.ShapeDtypeStruct((rows, D_MODEL), F32),
        compiler_params=_cparams(("arbitrary",)),
        name="experts",
    )(block_e, n_used, xs, w1, b1p, w2, b2)


def _combine_kernel(dest_ref, y_ref, gate_ref, x1_ref, gf_ref, nw_ref, o_ref, buf, sem, *, tm):
    def issue(t, carry):
        for k in range(TOP_K):
            pltpu.make_async_copy(y_ref.at[pl.ds(dest_ref[t * TOP_K + k], 1)],
                                  buf.at[k, pl.ds(t, 1)], sem).start()
        return carry

    lax.fori_loop(0, tm, issue, 0, unroll=4)
    _wait_rows(y_ref, tm * TOP_K, sem)
    gate = gate_ref[...]
    fx = gate[:, 0:1] * buf[0]
    for k in range(1, TOP_K):
        fx = fx + gate[:, k:k + 1] * buf[k]
    o_ref[...] = x1_ref[...] + gf_ref[...] * _rms(fx, nw_ref[...])


def combine_call(dest_flat, ys, gates, x1, gf, nw, bsz):
    n_tok = x1.shape[0]
    tm = TOK_TILE
    nt = n_tok // bsz // tm
    return pl.pallas_call(
        functools.partial(_combine_kernel, tm=tm),
        grid=(n_tok // tm,),
        in_specs=[pl.BlockSpec((tm * TOP_K,), lambda i: (i,), memory_space=pltpu.SMEM),
                  pl.BlockSpec(memory_space=pl.ANY),
                  pl.BlockSpec((tm, LANES), lambda i: (i, 0)),
                  pl.BlockSpec((tm, D_MODEL), lambda i: (i, 0)),
                  pl.BlockSpec((None, 1, D_MODEL), lambda i: (i // nt, 0, 0)),
                  pl.BlockSpec((1, D_MODEL), lambda i: (0, 0))],
        out_specs=pl.BlockSpec((tm, D_MODEL), lambda i: (i, 0)),
        out_shape=jax.ShapeDtypeStruct((n_tok, D_MODEL), F32),
        scratch_shapes=[pltpu.VMEM((TOP_K, tm, D_MODEL), F32), pltpu.SemaphoreType.DMA(())],
        compiler_params=_cparams(("arbitrary",)),
        name="combine",
    )(dest_flat, ys, gates, x1, gf, nw)


def kernel(x, c, ctx, c_ctx, ada_w, ada_b, mix_pre_norm, mix_post_norm, w_in, w_out, ssd_conv_w, ssd_conv_b,
           ssd_dt_bias, ssd_a_log, ssd_d, ssd_norm, hg_lb, hg_norm, ffn_pre_norm, ffn_post_norm, router_w,
           router_b, moe_w1, moe_b1, moe_w2, moe_b2):
    bsz, t_len, d = x.shape
    assert ada_w.shape[0] == 1 and d == D_MODEL and bsz <= SUBLANES - 1
    n_tok = bsz * t_len

    cc = jnp.zeros((SUBLANES, d), F32).at[:bsz].set(c).at[bsz].set(c_ctx)
    mod = ada_call(cc, ada_w[0], ada_b[0][None, :])
    sh_m, sc_m, g_m, sh_f, sc_f, g_f = [m[:bsz, None, :] for m in jnp.split(mod, 6, axis=-1)]
    csh_m, csc_m = [jnp.broadcast_to(m[bsz][None, None, :], (bsz, 1, d)) for m in jnp.split(mod, 6, axis=-1)[:2]]

    w = w_in[0]
    wa = jnp.concatenate([w[:, :SSD_INNER + SSD_XBC], w[:, SSD_COLS + 4 * HG_INNER:]], axis=1).astype(BF16)
    wdt = jnp.zeros((d, LANES), F32).at[:, :2 * SSD_HEADS].set(w[:, SSD_INNER + SSD_XBC:SSD_COLS]).astype(BF16)
    wb = w[:, SSD_COLS:SSD_COLS + 4 * HG_INNER].astype(BF16)
    dtb = jnp.zeros((1, LANES), F32).at[0, :2 * SSD_HEADS].set(ssd_dt_bias[0].reshape(-1))
    nw = mix_pre_norm[0][None, :]
    cw, cb = ssd_conv_w[0], ssd_conv_b[0][None, :]
    neg_a = -jnp.exp(ssd_a_log[0].astype(F32)).reshape(1, 2 * SSD_HEADS)
    dskip = jnp.repeat(ssd_d[0], SSD_HEAD_DIM)[None, :]
    lbp = hg_lb.astype(F32).reshape(2, 2 * HG_INNER)

    _, cxbc, _, cdt, cdtT = inproj_a_call(ctx, csh_m, csc_m, nw, wa, wdt, cw, cb, dtb, tm=ctx.shape[1])
    cu = inproj_b_call(ctx, csh_m, csc_m, nw, wb, col_major=False)
    z_ssd = jnp.zeros((bsz, N_PAIRS, SSD_STATE, LANES), F32)
    z_hg = jnp.zeros((bsz, HG_HEADS, HG_DK, HG_DK), F32)
    _, _, ssf, ssb = ssd_scan_call(cxbc, cdt, cdtT, z_ssd, z_ssd, neg_a, neg_a.reshape(-1, 1), dskip)
    _, _, hsf, hsb = hg_scan_call(cu, lbp, z_hg, z_hg, row_major_out=False)

    zg, xbc, hgate, dt, dtT = inproj_a_call(x, sh_m, sc_m, nw, wa, wdt, cw, cb, dtb, tm=512)
    ub = inproj_b_call(x, sh_m, sc_m, nw, wb, col_major=True)
    sof, sob, _, _ = ssd_scan_call(xbc, dt, dtT, ssf, ssb, neg_a, neg_a.reshape(-1, 1), dskip)
    hof, hob, _, _ = hg_scan_call(ub, lbp, hsf, hsb, row_major_out=True)

    rw = jnp.zeros((d, LANES), F32).at[:, :N_EXPERTS].set(router_w[0])
    rb = jnp.full((1, LANES), NEG_BIG, F32).at[0, :N_EXPERTS].set(router_b[0])
    x1, hx, idx, rank, gates, cnt = post_call(
        x, sof, sob, zg, hof, hob, hgate, ssd_norm[0][None, :], hg_norm[0][None, :], w_out[0].astype(BF16),
        mix_post_norm[0][None, :], g_m, ffn_pre_norm[0][None, :], sh_f, sc_f, rw, rb)

    r = MOE_ROWS
    n_blocks = (n_tok * TOP_K + N_EXPERTS * (r - 1) + r - 1) // r
    counts = cnt[0, :N_EXPERTS].astype(jnp.int32)
    padded = (counts + r - 1) // r * r
    pend = jnp.cumsum(padded)
    pstart = pend - padded
    dest = (pstart[idx[:, :TOP_K]] + rank[:, :TOP_K]).reshape(-1)
    starts = jnp.arange(n_blocks, dtype=jnp.int32) * r
    block_e = jnp.minimum(jnp.sum((pend[None, :] <= starts[:, None]).astype(jnp.int32), axis=1), N_EXPERTS - 1)
    n_used = (pend[-1:] // r).astype(jnp.int32)

    xs = dispatch_call(dest, hx, jnp.zeros((n_blocks * r, d), F32))
    b1p = jnp.concatenate([moe_b1[0][:, 0::2], moe_b1[0][:, 1::2]], axis=-1)[:, None, :]
    ys = experts_call(block_e, n_used, xs, moe_w1[0], b1p, moe_w2[0], moe_b2[0][:, None, :])
    out = combine_call(dest, ys, gates, x1, g_f, ffn_post_norm[0][None, :], bsz)
    return out.reshape(bsz, t_len, d)
```

```python
import functools
import math

import numpy as np
import jax
import jax.numpy as jnp
from jax import lax
from jax.experimental import pallas as pl
from jax.experimental.pallas import tpu as pltpu

F32 = jnp.float32
BF16 = jnp.bfloat16
HIGHEST = lax.Precision.HIGHEST

D_MODEL = 1024
GRID_W = 64
SSD_HEADS = 8
SSD_HEAD_DIM = 64
SSD_INNER = 512
SSD_STATE = 64
SSD_GROUPS = 2
SSD_CONV = 5
SSD_XBC = 768
SSD_COLS = 1296
HG_HEADS = 4
HG_DK = 128
HG_INNER = 512
N_EXPERTS = 32
TOP_K = 4
D_FF = 1024
SWIGLU_ALPHA = 1.702
SWIGLU_LIMIT = 7.0
EPS = 1e-6

LANES = 128
SUBLANES = 8
VMEM_LIMIT = 56 * 1024 * 1024

SSD_CHUNK = 128
HG_CHUNK = 128
HG_LEVELS = 7
MOE_ROWS = 256
TOK_TILE = 256
NEG_BIG = -1e30
LOG2E = 1.4426950408889634


def _cparams(sem):
    return pltpu.CompilerParams(dimension_semantics=sem, vmem_limit_bytes=VMEM_LIMIT)


def _sigmoid(x):
    return 1.0 / (1.0 + jnp.exp(-x))


def _silu(x):
    return x * _sigmoid(x)


def _rms(x, w):
    return x * lax.rsqrt(jnp.mean(x * x, axis=-1, keepdims=True) + EPS) * w


def _split3(v):
    hi = v.astype(BF16)
    r1 = v - hi.astype(F32)
    mid = r1.astype(BF16)
    lo = (r1 - mid.astype(F32)).astype(BF16)
    return hi, mid, lo


def _ada_kernel(c_ref, w_ref, b_ref, o_ref):
    s = _silu(c_ref[...])
    o_ref[...] = jnp.dot(s, w_ref[...], precision=HIGHEST, preferred_element_type=F32) + b_ref[...]


def ada_call(cc, w, b):
    n = w.shape[1]
    tn = 1536
    return pl.pallas_call(
        _ada_kernel,
        grid=(n // tn,),
        in_specs=[pl.BlockSpec((SUBLANES, D_MODEL), lambda j: (0, 0)),
                  pl.BlockSpec((D_MODEL, tn), lambda j: (0, j)),
                  pl.BlockSpec((1, tn), lambda j: (0, j))],
        out_specs=pl.BlockSpec((SUBLANES, tn), lambda j: (0, j)),
        out_shape=jax.ShapeDtypeStruct((SUBLANES, n), F32),
        compiler_params=_cparams(("arbitrary",)),
        name="ada",
    )(cc, w, b)


A_COLS = SSD_INNER + SSD_XBC + HG_INNER


def _prep(xt, nw, sh, sc):
    return (_rms(xt, nw) * (1.0 + sc) + sh).astype(BF16)


def _inproj_a_kernel(x_ref, xp_ref, xn_ref, sh_ref, sc_ref, nw_ref, wa_ref, wdt_ref, cw_ref, cb_ref, dtb_ref,
                     z_ref, xbc_ref, g_ref, dt_ref, dtT_ref, scr, *, tm):
    i = pl.program_id(1)
    last = pl.num_programs(1) - 1
    nw, sh, sc = nw_ref[...], sh_ref[...], sc_ref[...]
    h = _prep(x_ref[...], nw, sh, sc)
    ua = jnp.dot(h, wa_ref[...], preferred_element_type=F32)
    z_ref[...] = ua[:, :SSD_INNER]
    g_ref[...] = ua[:, SSD_INNER + SSD_XBC:]
    wx = wa_ref[:, SSD_INNER:SSD_INNER + SSD_XBC]
    up = jnp.dot(_prep(xp_ref[...], nw, sh, sc), wx, preferred_element_type=F32)
    un = jnp.dot(_prep(xn_ref[...], nw, sh, sc), wx, preferred_element_type=F32)
    scr[0:SUBLANES, :] = jnp.where(i > 0, up, 0.0)
    scr[SUBLANES:SUBLANES + tm, :] = ua[:, SSD_INNER:SSD_INNER + SSD_XBC]
    scr[SUBLANES + tm:, :] = jnp.where(i < last, un, 0.0)
    acc = jnp.broadcast_to(cb_ref[...], (tm, SSD_XBC))
    pad = SSD_CONV // 2
    for k in range(SSD_CONV):
        off = SUBLANES - pad + k
        acc = acc + cw_ref[k:k + 1, :] * scr[off:off + tm, :]
    xbc_ref[...] = _silu(acc)
    draw = jnp.dot(h, wdt_ref[...], preferred_element_type=F32) + dtb_ref[...]
    dt = jnp.maximum(draw, 0.0) + jnp.log(1.0 + jnp.exp(-jnp.abs(draw)))
    dt_ref[...] = dt[:, :2 * SSD_HEADS]
    dtT_ref[...] = dt.T[:2 * SSD_HEADS, :]


def inproj_a_call(x, sh, sc, nw, wa, wdt, cw, cb, dtb, tm):
    bsz, t_len, _ = x.shape
    nt = t_len // tm
    r8 = tm // SUBLANES
    n8 = t_len // SUBLANES
    full = lambda shape: pl.BlockSpec(shape, lambda b, i: (0,) * len(shape))
    tok = lambda c: pl.BlockSpec((None, tm, c), lambda b, i: (b, i, 0))
    return pl.pallas_call(
        functools.partial(_inproj_a_kernel, tm=tm),
        grid=(bsz, nt),
        in_specs=[tok(D_MODEL),
                  pl.BlockSpec((None, SUBLANES, D_MODEL), lambda b, i: (b, jnp.maximum(i * r8 - 1, 0), 0)),
                  pl.BlockSpec((None, SUBLANES, D_MODEL), lambda b, i: (b, jnp.minimum((i + 1) * r8, n8 - 1), 0)),
                  pl.BlockSpec((None, 1, D_MODEL), lambda b, i: (b, 0, 0)),
                  pl.BlockSpec((None, 1, D_MODEL), lambda b, i: (b, 0, 0)),
                  full((1, D_MODEL)), full((D_MODEL, A_COLS)), full((D_MODEL, LANES)),
                  full((SSD_CONV, SSD_XBC)), full((1, SSD_XBC)), full((1, LANES))],
        out_specs=[tok(SSD_INNER), tok(SSD_XBC), tok(HG_INNER), tok(2 * SSD_HEADS),
                   pl.BlockSpec((None, 2 * SSD_HEADS, tm), lambda b, i: (b, 0, i))],
        out_shape=[jax.ShapeDtypeStruct((bsz, t_len, SSD_INNER), F32),
                   jax.ShapeDtypeStruct((bsz, t_len, SSD_XBC), F32),
                   jax.ShapeDtypeStruct((bsz, t_len, HG_INNER), F32),
                   jax.ShapeDtypeStruct((bsz, t_len, 2 * SSD_HEADS), F32),
                   jax.ShapeDtypeStruct((bsz, 2 * SSD_HEADS, t_len), F32)],
        scratch_shapes=[pltpu.VMEM((tm + 2 * SUBLANES, SSD_XBC), F32)],
        compiler_params=_cparams(("arbitrary", "arbitrary")),
        name="inproj_a",
    )(x, x, x, sh, sc, nw, wa, wdt, cw, cb, dtb)


B_COLS = 4 * HG_INNER


def _inproj_b_kernel(x_ref, sh_ref, sc_ref, nw_ref, w_ref, o_ref, *, ncol):
    if ncol:
        xt = jnp.concatenate([x_ref[:, w * D_MODEL:(w + 1) * D_MODEL] for w in range(ncol)], axis=0)
    else:
        xt = x_ref[...]
    h = _prep(xt, nw_ref[...], sh_ref[...], sc_ref[...])
    o_ref[...] = jnp.dot(h, w_ref[...], preferred_element_type=F32)


def inproj_b_call(x, sh, sc, nw, wb, col_major):
    bsz, t_len, _ = x.shape
    if col_major:
        ncol = 8
        tm = ncol * GRID_W
        rows = t_len // GRID_W
        assert rows == GRID_W
        xin = x.reshape(bsz, rows, GRID_W * D_MODEL)
        x_spec = pl.BlockSpec((None, rows, ncol * D_MODEL), lambda b, i: (b, 0, i))
    else:
        ncol = 0
        tm = t_len
        xin = x
        x_spec = pl.BlockSpec((None, tm, D_MODEL), lambda b, i: (b, i, 0))
    full = lambda shape: pl.BlockSpec(shape, lambda b, i: (0,) * len(shape))
    return pl.pallas_call(
        functools.partial(_inproj_b_kernel, ncol=ncol),
        grid=(bsz, t_len // tm),
        in_specs=[x_spec,
                  pl.BlockSpec((None, 1, D_MODEL), lambda b, i: (b, 0, 0)),
                  pl.BlockSpec((None, 1, D_MODEL), lambda b, i: (b, 0, 0)),
                  full((1, D_MODEL)), full((D_MODEL, B_COLS))],
        out_specs=pl.BlockSpec((None, tm, B_COLS), lambda b, i: (b, i, 0)),
        out_shape=jax.ShapeDtypeStruct((bsz, t_len, B_COLS), F32),
        compiler_params=_cparams(("arbitrary", "arbitrary")),
        name="inproj_b",
    )(xin, sh, sc, nw, wb)


N_PAIRS = SSD_HEADS // 2


def _ssd_dir(xbc, dt, dtT, s_ref, tri, triT, na_row, na_col, fwd):
    c = SSD_CHUNK
    xs = xbc[:, :SSD_INNER]
    bm = xbc[:, SSD_INNER:SSD_INNER + LANES]
    cm = xbc[:, SSD_INNER + LANES:]
    col0 = 0 if fwd else SSD_HEADS
    la = dt[:, col0:col0 + SSD_HEADS] * na_row[:, col0:col0 + SSD_HEADS]
    dtr = dtT[col0:col0 + SSD_HEADS, :]
    laT = dtr * na_col[col0:col0 + SSD_HEADS, :]
    g = sum(jnp.dot(tri, p, preferred_element_type=F32) for p in _split3(la))
    gT = sum(jnp.dot(p, triT, preferred_element_type=F32) for p in _split3(laT))
    end = c - 1 if fwd else 0
    bmT = bm.T
    ii = lax.broadcasted_iota(jnp.int32, (c, c), 0)
    jj = lax.broadcasted_iota(jnp.int32, (c, c), 1)
    causal = (jj <= ii) if fwd else (jj >= ii)
    lane = lax.broadcasted_iota(jnp.int32, (c, LANES), 1)
    lo_half = lane < SSD_HEAD_DIM
    lane_s = lax.broadcasted_iota(jnp.int32, (SSD_STATE, LANES), 1) < SSD_HEAD_DIM
    outs = []
    for grp in range(SSD_GROUPS):
        in_grp = (lane >= grp * SSD_STATE) & (lane < (grp + 1) * SSD_STATE)
        cm_g = jnp.where(in_grp, cm, 0.0).astype(BF16)
        gmat = jnp.dot(cm_g, bmT.astype(BF16), preferred_element_type=F32)
        bmT_g = bmT[grp * SSD_STATE:(grp + 1) * SSD_STATE, :]
        for pp in range(N_PAIRS // SSD_GROUPS):
            pair = grp * (N_PAIRS // SSD_GROUPS) + pp
            heads = (2 * pair, 2 * pair + 1)
            xs_p = xs[:, pair * LANES:(pair + 1) * LANES]
            xbd = jnp.concatenate([jnp.where(lo_half, xs_p, 0.0), jnp.where(lo_half, 0.0, xs_p)],
                                  axis=0).astype(BF16)
            ms, bws, ecols, arows = [], [], [], []
            for hd in heads:
                gcol = g[:, hd:hd + 1]
                grow = gT[hd:hd + 1, :]
                dec = jnp.exp(jnp.where(causal, gcol - grow, -jnp.inf))
                ms.append(gmat * dec * dtr[hd:hd + 1, :])
                glast = grow[:, end:end + 1]
                bws.append(bmT_g * (dtr[hd:hd + 1, :] * jnp.exp(glast - grow)))
                ecols.append(jnp.exp(gcol))
                arows.append(jnp.exp(glast))
            mcat = jnp.concatenate(ms, axis=1).astype(BF16)
            s_old = s_ref[pair]
            zeros = jnp.zeros_like(s_old)
            s_pad = jnp.concatenate([s_old, zeros] if grp == 0 else [zeros, s_old], axis=0).astype(BF16)
            o_inter = jnp.dot(cm.astype(BF16), s_pad, preferred_element_type=F32)
            o_inter = o_inter * jnp.where(lo_half, ecols[0], ecols[1])
            outs.append(jnp.dot(mcat, xbd, preferred_element_type=F32) + o_inter)
            bw = jnp.concatenate(bws, axis=1).astype(BF16)
            s_ref[pair] = (s_old * jnp.where(lane_s, arows[0], arows[1])
                           + jnp.dot(bw, xbd, preferred_element_type=F32))
    return outs


def _ssd_scan_kernel(xf_ref, dtf_ref, dtTf_ref, xb_ref, dtb_ref, dtTb_ref, s0f_ref, s0b_ref,
                     trif_ref, trifT_ref, trib_ref, tribT_ref, nar_ref, nac_ref, dsk_ref,
                     of_ref, ob_ref, sfo_ref, sbo_ref, sf, sb):
    n = pl.program_id(1)

    @pl.when(n == 0)
    def _():
        sf[...] = s0f_ref[...]
        sb[...] = s0b_ref[...]

    xf = xf_ref[...]
    outs = _ssd_dir(xf, dtf_ref[...], dtTf_ref[...], sf, trif_ref[...], trifT_ref[...],
                    nar_ref[...], nac_ref[...], True)
    of_ref[...] = jnp.concatenate(outs, axis=1) + dsk_ref[...] * xf[:, :SSD_INNER]
    outs = _ssd_dir(xb_ref[...], dtb_ref[...], dtTb_ref[...], sb, trib_ref[...], tribT_ref[...],
                    nar_ref[...], nac_ref[...], False)
    ob_ref[...] = jnp.concatenate(outs, axis=1)

    @pl.when(n == pl.num_programs(1) - 1)
    def _():
        sfo_ref[...] = sf[...]
        sbo_ref[...] = sb[...]


def _ssd_consts():
    c = SSD_CHUNK
    i = np.arange(c)
    trif = (i[:, None] >= i[None, :]).astype(np.float32)
    trib = (i[:, None] <= i[None, :]).astype(np.float32)
    return [jnp.asarray(a, BF16) for a in (trif, trif.T, trib, trib.T)]


def ssd_scan_call(xbc, dt, dtT, s0f, s0b, na_row, na_col, dskip):
    bsz, t_len, _ = xbc.shape
    c = SSD_CHUNK
    nc = t_len // c
    fw = lambda w: pl.BlockSpec((None, c, w), lambda b, n: (b, n, 0))
    bw = lambda w: pl.BlockSpec((None, c, w), lambda b, n: (b, nc - 1 - n, 0))
    full = lambda shape: pl.BlockSpec(shape, lambda b, n: (0,) * len(shape))
    st = pl.BlockSpec((None, N_PAIRS, SSD_STATE, LANES), lambda b, n: (b, 0, 0, 0))
    st_shape = jax.ShapeDtypeStruct((bsz, N_PAIRS, SSD_STATE, LANES), F32)
    return pl.pallas_call(
        _ssd_scan_kernel,
        grid=(bsz, nc),
        in_specs=[fw(SSD_XBC), fw(2 * SSD_HEADS),
                  pl.BlockSpec((None, 2 * SSD_HEADS, c), lambda b, n: (b, 0, n)),
                  bw(SSD_XBC), bw(2 * SSD_HEADS),
                  pl.BlockSpec((None, 2 * SSD_HEADS, c), lambda b, n: (b, 0, nc - 1 - n)),
                  st, st, full((c, c)), full((c, c)), full((c, c)), full((c, c)),
                  full((1, 2 * SSD_HEADS)), full((2 * SSD_HEADS, 1)), full((1, SSD_INNER))],
        out_specs=[fw(SSD_INNER), bw(SSD_INNER), st, st],
        out_shape=[jax.ShapeDtypeStruct((bsz, t_len, SSD_INNER), F32),
                   jax.ShapeDtypeStruct((bsz, t_len, SSD_INNER), F32), st_shape, st_shape],
        scratch_shapes=[pltpu.VMEM((N_PAIRS, SSD_STATE, LANES), F32),
                        pltpu.VMEM((N_PAIRS, SSD_STATE, LANES), F32)],
        compiler_params=_cparams(("arbitrary", "arbitrary")),
        name="ssd_scan",
    )(xbc, dt, dtT, xbc, dt, dtT, s0f, s0b, *_ssd_consts(), na_row, na_col, dskip)


def _hg_consts():
    c = HG_CHUNK
    t = np.arange(c)
    m_f = np.zeros((HG_LEVELS + 1, c, c), np.float32)
    m_f[0] = np.eye(c)
    for lv in range(1, HG_LEVELS + 1):
        m = 2 ** lv
        blk = t // m
        right = (t % m) >= m // 2
        m_f[lv] = (blk[:, None] == blk[None, :]) & right[:, None] & (~right[None, :])
    m_b = np.transpose(m_f, (0, 2, 1))
    tri_f = (t[:, None] >= t[None, :]).astype(np.float32)
    tri_b = (t[:, None] <= t[None, :]).astype(np.float32)
    return jnp.asarray(tri_f, BF16), jnp.asarray(tri_b, BF16), jnp.asarray(m_f, F32), jnp.asarray(m_b, F32)


def _hg_dir(u, lb_row, st_ref, tri_ref, mask_ref, fwd):
    c = HG_CHUNK
    outs = []
    fcol = HG_INNER if fwd else 2 * HG_INNER
    nt = (((1,), (1,)), ((), ()))
    row = lax.broadcasted_iota(jnp.int32, (c, HG_DK), 0)

    def halves(lo, hi, half):
        if half % SUBLANES == 0:
            return jnp.concatenate([(hi if (s // half) % 2 else lo)[s:s + half] for s in range(0, c, half)], axis=0)
        return jnp.where(((row // half) % 2) == 1, hi, lo)

    def shift(x, s):
        s = s % c
        if s % SUBLANES == 0:
            return jnp.concatenate([x[c - s:], x[:c - s]], axis=0)
        return pltpu.roll(x, s, 0)

    def by_side(query_side, key_side, half):
        return halves(key_side, query_side, half) if fwd else halves(query_side, key_side, half)

    for hd in range(HG_HEADS):
        sl = slice(hd * HG_DK, (hd + 1) * HG_DK)
        q = _silu(u[:, sl]) * (HG_DK ** -0.5)
        lb = lb_row[:, sl]
        f = lb + (1.0 - lb) * _sigmoid(u[:, fcol + hd * HG_DK:fcol + (hd + 1) * HG_DK])
        k = 1.0 - f
        la = jnp.log(f)
        v = u[:, 3 * HG_INNER + hd * HG_DK:3 * HG_INNER + (hd + 1) * HG_DK]
        la_hi = la.astype(BF16)
        la_lo = (la - la_hi.astype(F32)).astype(BF16)
        g2 = jnp.dot(tri_ref[...], jnp.concatenate([la_hi, la_lo], axis=1), preferred_element_type=F32)
        g = g2[:, :HG_DK] + g2[:, HG_DK:]
        scores = mask_ref[0] * lax.dot_general(q.astype(BF16), k.astype(BF16), nt, preferred_element_type=F32)
        fill = g
        for lv in range(1, HG_LEVELS + 1):
            half = 2 ** (lv - 1)
            if fwd:
                ref = halves(fill, shift(fill, half), half)
            else:
                ref = halves(shift(fill, -half), fill, half)
            decay = jnp.exp2(jnp.abs(g - ref) * (-LOG2E))
            y = (by_side(q, k, half) * decay).astype(BF16)
            scores = scores + mask_ref[lv] * lax.dot_general(y, y, nt, preferred_element_type=F32)
            if fwd:
                fill = halves(shift(fill, -half), fill, half)
            else:
                fill = halves(fill, shift(fill, half), half)
        x_cum = jnp.exp(g)
        x_rem = jnp.exp(fill - g)
        st = st_ref[hd]
        o = jnp.dot(scores.astype(BF16), v.astype(BF16), preferred_element_type=F32)
        o = o + lax.dot_general((q * x_cum).astype(BF16), st.astype(BF16), nt, preferred_element_type=F32)
        outs.append(o)
        end = c - 1 if fwd else 0
        st_ref[hd] = (st * x_cum[end:end + 1, :]
                      + jnp.dot(v.T.astype(BF16), (k * x_rem).astype(BF16), preferred_element_type=F32))
    return jnp.concatenate(outs, axis=1)


def _hg_scan_kernel(uf_ref, ub_ref, lbp_ref, s0f_ref, s0b_ref, trif_ref, trib_ref, mf_ref, mb_ref,
                    of_ref, ob_ref, sfo_ref, sbo_ref, sf, sb, *, img_rows):
    n = pl.program_id(1)

    def put(o_ref, o):
        if img_rows:
            for j in range(HG_CHUNK // img_rows):
                o_ref[:, j * HG_INNER:(j + 1) * HG_INNER] = o[j * img_rows:(j + 1) * img_rows]
        else:
            o_ref[...] = o


    @pl.when(n == 0)
    def _():
        sf[...] = s0f_ref[...]
        sb[...] = s0b_ref[...]

    p = lbp_ref[...]
    mx = jnp.max(p, axis=0, keepdims=True)
    e = jnp.exp(p - mx)
    lb = e[0:1, :] / jnp.sum(e, axis=0, keepdims=True)
    put(of_ref, _hg_dir(uf_ref[...], lb[:, :HG_INNER], sf, trif_ref, mf_ref, True))
    put(ob_ref, _hg_dir(ub_ref[...], lb[:, HG_INNER:], sb, trib_ref, mb_ref, False))

    @pl.when(n == pl.num_programs(1) - 1)
    def _():
        sfo_ref[...] = sf[...]
        sbo_ref[...] = sb[...]


def hg_scan_call(u, lbp, s0f, s0b, row_major_out):
    bsz, t_len, _ = u.shape
    c = HG_CHUNK
    nc = t_len // c
    full = lambda shape: pl.BlockSpec(shape, lambda b, n: (0,) * len(shape))
    st = pl.BlockSpec((None, HG_HEADS, HG_DK, HG_DK), lambda b, n: (b, 0, 0, 0))
    st_shape = jax.ShapeDtypeStruct((bsz, HG_HEADS, HG_DK, HG_DK), F32)
    if row_major_out:
        img_rows = t_len // GRID_W
        cols = c // img_rows
        assert cols * img_rows == c
        o_shape = jax.ShapeDtypeStruct((bsz, img_rows, GRID_W * HG_INNER), F32)
        of_spec = pl.BlockSpec((None, img_rows, cols * HG_INNER), lambda b, n: (b, 0, n))
        ob_spec = pl.BlockSpec((None, img_rows, cols * HG_INNER), lambda b, n: (b, 0, nc - 1 - n))
    else:
        img_rows = 0
        o_shape = jax.ShapeDtypeStruct((bsz, t_len, HG_INNER), F32)
        of_spec = pl.BlockSpec((None, c, HG_INNER), lambda b, n: (b, n, 0))
        ob_spec = pl.BlockSpec((None, c, HG_INNER), lambda b, n: (b, nc - 1 - n, 0))
    tri_f, tri_b, m_f, m_b = _hg_consts()
    of, ob, sfo, sbo = pl.pallas_call(
        functools.partial(_hg_scan_kernel, img_rows=img_rows),
        grid=(bsz, nc),
        in_specs=[pl.BlockSpec((None, c, B_COLS), lambda b, n: (b, n, 0)),
                  pl.BlockSpec((None, c, B_COLS), lambda b, n: (b, nc - 1 - n, 0)),
                  full((2, 2 * HG_INNER)), st, st,
                  full(tri_f.shape), full(tri_b.shape), full(m_f.shape), full(m_b.shape)],
        out_specs=[of_spec, ob_spec, st, st],
        out_shape=[o_shape, o_shape, st_shape, st_shape],
        scratch_shapes=[pltpu.VMEM((HG_HEADS, HG_DK, HG_DK), F32), pltpu.VMEM((HG_HEADS, HG_DK, HG_DK), F32)],
        compiler_params=_cparams(("arbitrary", "arbitrary")),
        name="hg_scan",
    )(u, u, lbp, s0f, s0b, tri_f, tri_b, m_f, m_b)
    if row_major_out:
        of = of.reshape(bsz, t_len, HG_INNER)
        ob = ob.reshape(bsz, t_len, HG_INNER)
    return of, ob, sfo, sbo


def _post_kernel(x_ref, sof_ref, sob_ref, z_ref, hof_ref, hob_ref, hg_ref,
                 snw_ref, hnw_ref, wo_ref, pnw_ref, gm_ref, fnw_ref, shf_ref, scf_ref, rwh_ref, rwl_ref, rb_ref,
                 tri_ref, x1_ref, hx_ref, idx_ref, rank_ref, gate_ref, cnt_ref, carry, *, tm):
    first = (pl.program_id(0) == 0) & (pl.program_id(1) == 0)

    @pl.when(first)
    def _():
        carry[...] = jnp.zeros_like(carry)

    y = (sof_ref[...] + sob_ref[...]) * _silu(z_ref[...])
    y = _rms(y, snw_ref[...])
    o = hof_ref[...] + hob_ref[...]
    hnw = hnw_ref[...]
    o = jnp.concatenate([_rms(o[:, h * HG_DK:(h + 1) * HG_DK], hnw[:, h * HG_DK:(h + 1) * HG_DK])
                         for h in range(HG_HEADS)], axis=1)
    o = o * _silu(hg_ref[...])
    mix = (jnp.dot(y.astype(BF16), wo_ref[:SSD_INNER, :], preferred_element_type=F32)
           + jnp.dot(o.astype(BF16), wo_ref[SSD_INNER:, :], preferred_element_type=F32))
    x1 = x_ref[...] + gm_ref[...] * _rms(mix, pnw_ref[...])
    x1_ref[...] = x1
    hx = _rms(x1, fnw_ref[...]) * (1.0 + scf_ref[...]) + shf_ref[...]
    hx_ref[...] = hx
    hx_hi = hx.astype(BF16)
    hx_lo = (hx - hx_hi.astype(F32)).astype(BF16)
    rwh = rwh_ref[...]
    logits = (jnp.dot(hx_hi, rwh, preferred_element_type=F32) + jnp.dot(hx_lo, rwh, preferred_element_type=F32)
              + jnp.dot(hx_hi, rwl_ref[...], preferred_element_type=F32)) + rb_ref[...]
    work = logits.T[:N_EXPERTS, :]
    erow = lax.broadcasted_iota(jnp.int32, (N_EXPERTS, tm), 0)
    vals, idxs = [], []
    for _ in range(TOP_K):
        m = jnp.max(work, axis=0, keepdims=True)
        ix = jnp.min(jnp.where(work == m, erow, N_EXPERTS), axis=0, keepdims=True)
        vals.append(m)
        idxs.append(ix)
        work = jnp.where(erow == ix, -jnp.inf, work)
    es = [jnp.exp(v - vals[0]) for v in vals]
    den = es[0] + es[1] + es[2] + es[3]
    onehots = [(erow == ix) for ix in idxs]
    multi = sum(oh.astype(F32) for oh in onehots)
    before = jnp.dot(multi.astype(BF16), tri_ref[...], preferred_element_type=F32) + carry[...]
    carry[...] = carry[...] + jnp.sum(multi, axis=1, keepdims=True)
    sub = lax.broadcasted_iota(jnp.int32, (SUBLANES, tm), 0)
    idx_o = jnp.zeros((SUBLANES, tm), jnp.int32)
    rank_o = jnp.zeros((SUBLANES, tm), jnp.int32)
    gate_o = jnp.zeros((SUBLANES, tm), F32)
    for k in range(TOP_K):
        rk = jnp.sum(jnp.where(onehots[k], before, 0.0), axis=0, keepdims=True)
        idx_o = jnp.where(sub == k, idxs[k], idx_o)
        rank_o = jnp.where(sub == k, rk.astype(jnp.int32), rank_o)
        gate_o = jnp.where(sub == k, es[k] / den, gate_o)
    idx_ref[...] = idx_o
    rank_ref[...] = rank_o
    gate_ref[...] = gate_o
    cnt_ref[...] = jnp.broadcast_to(carry[...], (N_EXPERTS, LANES))


def post_call(x, sof, sob, z, hof, hob, hg, snw, hnw, wo, pnw, gm, fnw, shf, scf, rwh, rwl, rb):
    bsz, t_len, _ = x.shape
    tm = TOK_TILE
    nt = t_len // tm
    n_tok = bsz * t_len
    tok = lambda c: pl.BlockSpec((None, tm, c), lambda b, i: (b, i, 0))
    full = lambda shape: pl.BlockSpec(shape, lambda b, i: (0,) * len(shape))
    per_b = pl.BlockSpec((None, 1, D_MODEL), lambda b, i: (b, 0, 0))
    flat = lambda c: pl.BlockSpec((tm, c), lambda b, i: (b * nt + i, 0))
    rout = pl.BlockSpec((SUBLANES, tm), lambda b, i: (b * nt + i, 0))
    n_tiles = n_tok // tm
    ii = np.arange(tm)
    tri = jnp.asarray(ii[:, None] < ii[None, :], BF16)
    return pl.pallas_call(
        functools.partial(_post_kernel, tm=tm),
        grid=(bsz, nt),
        in_specs=[tok(D_MODEL), tok(SSD_INNER), tok(SSD_INNER), tok(SSD_INNER), tok(HG_INNER), tok(HG_INNER),
                  tok(HG_INNER), full((1, SSD_INNER)), full((1, HG_INNER)), full((D_MODEL, D_MODEL)),
                  full((1, D_MODEL)), per_b, full((1, D_MODEL)), per_b, per_b,
                  full((D_MODEL, LANES)), full((D_MODEL, LANES)), full((1, LANES)), full((tm, tm))],
        out_specs=[flat(D_MODEL), flat(D_MODEL), rout, rout, rout,
                   pl.BlockSpec((N_EXPERTS, LANES), lambda b, i: (0, 0))],
        out_shape=[jax.ShapeDtypeStruct((n_tok, D_MODEL), F32), jax.ShapeDtypeStruct((n_tok, D_MODEL), F32),
                   jax.ShapeDtypeStruct((n_tiles * SUBLANES, tm), jnp.int32),
                   jax.ShapeDtypeStruct((n_tiles * SUBLANES, tm), jnp.int32),
                   jax.ShapeDtypeStruct((n_tiles * SUBLANES, tm), F32),
                   jax.ShapeDtypeStruct((N_EXPERTS, LANES), F32)],
        scratch_shapes=[pltpu.VMEM((N_EXPERTS, 1), F32)],
        compiler_params=_cparams(("arbitrary", "arbitrary")),
        name="post",
    )(x, sof, sob, z, hof, hob, hg, snw, hnw, wo, pnw, gm, fnw, shf, scf, rwh, rwl, rb, tri)


def _wait_rows(hbm_ref, n_rows, sem):
    pltpu.make_async_copy(hbm_ref.at[pl.ds(0, n_rows)], hbm_ref.at[pl.ds(0, n_rows)], sem).wait()


PAD_RUNS = tuple(2 ** j for j in range(int(math.log2(MOE_ROWS))))


def _dispatch_kernel(zstart_ref, zpad_ref, dest_ref, hx_ref, buf_ref, zrows, sem, zsem, *, tm):
    i = pl.program_id(0)

    def for_pad_runs(act):
        def per_expert(e, carry):
            pad = zpad_ref[e]
            off = zstart_ref[e]
            for run in PAD_RUNS:
                @pl.when((pad & run) != 0)
                def _():
                    if run < SUBLANES:
                        for j in range(run):
                            act(pltpu.make_async_copy(zrows.at[pl.ds(0, 1)], buf_ref.at[pl.ds(off + j, 1)], zsem))
                    else:
                        act(pltpu.make_async_copy(zrows.at[pl.ds(0, run)],
                                                  buf_ref.at[pl.ds(pl.multiple_of(off, SUBLANES), run)], zsem))
                off = off + (pad & run)
            return carry

        lax.fori_loop(0, N_EXPERTS, per_expert, 0)

        def tail(j, carry):
            act(pltpu.make_async_copy(zrows, buf_ref.at[pl.ds(pl.multiple_of(j * PAD_RUNS[-1], SUBLANES),
                                                              PAD_RUNS[-1])], zsem))
            return carry

        lax.fori_loop(zstart_ref[N_EXPERTS] // PAD_RUNS[-1], buf_ref.shape[0] // PAD_RUNS[-1], tail, 0)

    @pl.when(i == 0)
    def _():
        zrows[...] = jnp.zeros_like(zrows)
        for_pad_runs(lambda cp: cp.start())

    def issue(t, carry):
        for k in range(TOP_K):
            pltpu.make_async_copy(hx_ref.at[pl.ds(t, 1)], buf_ref.at[pl.ds(dest_ref[k * tm + t], 1)], sem).start()
        return carry

    lax.fori_loop(0, tm, issue, 0, unroll=4)
    _wait_rows(buf_ref, tm * TOP_K, sem)

    @pl.when(i == 0)
    def _():
        for_pad_runs(lambda cp: cp.wait())


def dispatch_call(zstart, zpad, dest_flat, hx, n_rows):
    n_tok = hx.shape[0]
    tm = TOK_TILE
    return pl.pallas_call(
        functools.partial(_dispatch_kernel, tm=tm),
        grid=(n_tok // tm,),
        in_specs=[pl.BlockSpec(memory_space=pltpu.SMEM), pl.BlockSpec(memory_space=pltpu.SMEM),
                  pl.BlockSpec((tm * TOP_K,), lambda i: (i,), memory_space=pltpu.SMEM),
                  pl.BlockSpec((tm, D_MODEL), lambda i: (i, 0))],
        out_specs=pl.BlockSpec(memory_space=pl.ANY),
        out_shape=jax.ShapeDtypeStruct((n_rows, D_MODEL), F32),
        scratch_shapes=[pltpu.VMEM((PAD_RUNS[-1], D_MODEL), F32), pltpu.SemaphoreType.DMA(()),
                        pltpu.SemaphoreType.DMA(())],
        compiler_params=_cparams(("arbitrary",)),
        name="dispatch",
    )(zstart, zpad, dest_flat, hx)


W1_TCOLS = 256


def _experts_kernel(be_ref, nu_ref, nxt_ref, x_ref, w1_hbm, b1_ref, w2_hbm, b2_ref, y_ref,
                    w1f, w2f, w1t, w2s, tbuf, wsem):
    i = pl.program_id(0)
    live = i < nu_ref[0]
    new_expert = (i == 0) | (be_ref[i] != be_ref[jnp.maximum(i - 1, 0)])

    def weight_copies(e):
        return (pltpu.make_async_copy(w1_hbm.at[e], w1f, wsem.at[0]),
                pltpu.make_async_copy(w2_hbm.at[e], w2f, wsem.at[1]))

    @pl.when(live & new_expert)
    def _():
        @pl.when(i == 0)
        def _():
            for cp in weight_copies(be_ref[0]):
                cp.start()

        for cp in weight_copies(be_ref[i]):
            cp.wait()
        half = W1_TCOLS // 2
        for c in range(2 * D_FF // W1_TCOLS):
            for j in range(D_MODEL // LANES):
                ks = slice(j * LANES, (j + 1) * LANES)
                tbuf[j] = w1f[ks, c * W1_TCOLS:(c + 1) * W1_TCOLS].T
                w1t[c * half:(c + 1) * half, ks] = tbuf[j, pl.ds(0, half, stride=2), :].astype(BF16)
                w1t[D_FF + c * half:D_FF + (c + 1) * half, ks] = tbuf[j, pl.ds(1, half, stride=2), :].astype(BF16)
        w2s[...] = w2f[...].astype(BF16)

        @pl.when(nxt_ref[i] >= 0)
        def _():
            for cp in weight_copies(nxt_ref[i]):
                cp.start()

    @pl.when(live)
    def _():
        u = lax.dot_general(x_ref[...].astype(BF16), w1t[...], (((1,), (1,)), ((), ())),
                            preferred_element_type=F32) + b1_ref[...]
        glu = jnp.minimum(u[:, :D_FF], SWIGLU_LIMIT)
        lin = jnp.clip(u[:, D_FF:], -SWIGLU_LIMIT, SWIGLU_LIMIT)
        a = glu * _sigmoid(SWIGLU_ALPHA * glu) * (lin + 1.0)
        y_ref[...] = jnp.dot(a.astype(BF16), w2s[...], preferred_element_type=F32) + b2_ref[...]

    @pl.when(jnp.logical_not(live))
    def _():
        y_ref[...] = jnp.zeros_like(y_ref)


def experts_call(block_e, n_used, next_e, xs, w1, b1p, w2, b2):
    rows = xs.shape[0]
    r = MOE_ROWS
    nb = rows // r
    grid_spec = pltpu.PrefetchScalarGridSpec(
        num_scalar_prefetch=3,
        grid=(nb,),
        in_specs=[pl.BlockSpec((r, D_MODEL), lambda i, be, nu, nx: (jnp.maximum(jnp.minimum(i, nu[0] - 1), 0), 0)),
                  pl.BlockSpec(memory_space=pl.ANY),
                  pl.BlockSpec((None, 1, 2 * D_FF), lambda i, be, nu, nx: (be[i], 0, 0)),
                  pl.BlockSpec(memory_space=pl.ANY),
                  pl.BlockSpec((None, 1, D_MODEL), lambda i, be, nu, nx: (be[i], 0, 0))],
        out_specs=pl.BlockSpec((r, D_MODEL), lambda i, be, nu, nx: (i, 0)),
        scratch_shapes=[pltpu.VMEM((D_MODEL, 2 * D_FF), F32), pltpu.VMEM((D_FF, D_MODEL), F32),
                        pltpu.VMEM((2 * D_FF, D_MODEL), BF16), pltpu.VMEM((D_FF, D_MODEL), BF16),
                        pltpu.VMEM((D_MODEL // LANES, W1_TCOLS, LANES), F32),
                        pltpu.SemaphoreType.DMA((2,))],
    )
    return pl.pallas_call(
        _experts_kernel,
        grid_spec=grid_spec,
        out_shape=jax.ShapeDtypeStruct((rows, D_MODEL), F32),
        compiler_params=_cparams(("arbitrary",)),
        name="experts",
    )(block_e, n_used, next_e, xs, w1, b1p, w2, b2)


def _combine_kernel(dest_ref, y_ref, gate_ref, x1_ref, gf_ref, nw_ref, o_ref, buf, sem, *, tm):
    def issue(t, carry):
        for k in range(TOP_K):
            pltpu.make_async_copy(y_ref.at[pl.ds(dest_ref[k * tm + t], 1)],
                                  buf.at[k, pl.ds(t, 1)], sem).start()
        return carry

    lax.fori_loop(0, tm, issue, 0, unroll=4)
    _wait_rows(y_ref, tm * TOP_K, sem)
    gate = gate_ref[...].T
    fx = gate[:, 0:1] * buf[0]
    for k in range(1, TOP_K):
        fx = fx + gate[:, k:k + 1] * buf[k]
    o_ref[...] = x1_ref[...] + gf_ref[...] * _rms(fx, nw_ref[...])


def combine_call(dest_flat, ys, gates, x1, gf, nw, bsz):
    n_tok = x1.shape[0]
    tm = TOK_TILE
    nt = n_tok // bsz // tm
    return pl.pallas_call(
        functools.partial(_combine_kernel, tm=tm),
        grid=(n_tok // tm,),
        in_specs=[pl.BlockSpec((tm * TOP_K,), lambda i: (i,), memory_space=pltpu.SMEM),
                  pl.BlockSpec(memory_space=pl.ANY),
                  pl.BlockSpec((SUBLANES, tm), lambda i: (i, 0)),
                  pl.BlockSpec((tm, D_MODEL), lambda i: (i, 0)),
                  pl.BlockSpec((None, 1, D_MODEL), lambda i: (i // nt, 0, 0)),
                  pl.BlockSpec((1, D_MODEL), lambda i: (0, 0))],
        out_specs=pl.BlockSpec((tm, D_MODEL), lambda i: (i, 0)),
        out_shape=jax.ShapeDtypeStruct((n_tok, D_MODEL), F32),
        scratch_shapes=[pltpu.VMEM((TOP_K, tm, D_MODEL), F32), pltpu.SemaphoreType.DMA(())],
        compiler_params=_cparams(("arbitrary",)),
        name="combine",
    )(dest_flat, ys, gates, x1, gf, nw)


def kernel(x, c, ctx, c_ctx, ada_w, ada_b, mix_pre_norm, mix_post_norm, w_in, w_out, ssd_conv_w, ssd_conv_b,
           ssd_dt_bias, ssd_a_log, ssd_d, ssd_norm, hg_lb, hg_norm, ffn_pre_norm, ffn_post_norm, router_w,
           router_b, moe_w1, moe_b1, moe_w2, moe_b2):
    bsz, t_len, d = x.shape
    assert ada_w.shape[0] == 1 and d == D_MODEL and bsz <= SUBLANES - 1
    n_tok = bsz * t_len

    cc = jnp.zeros((SUBLANES, d), F32).at[:bsz].set(c).at[bsz].set(c_ctx)
    mod = ada_call(cc, ada_w[0], ada_b[0][None, :])
    sh_m, sc_m, g_m, sh_f, sc_f, g_f = [m[:bsz, None, :] for m in jnp.split(mod, 6, axis=-1)]
    csh_m, csc_m = [jnp.broadcast_to(m[bsz][None, None, :], (bsz, 1, d)) for m in jnp.split(mod, 6, axis=-1)[:2]]

    w = w_in[0]
    wa = jnp.concatenate([w[:, :SSD_INNER + SSD_XBC], w[:, SSD_COLS + 4 * HG_INNER:]], axis=1).astype(BF16)
    wdt = jnp.zeros((d, LANES), F32).at[:, :2 * SSD_HEADS].set(w[:, SSD_INNER + SSD_XBC:SSD_COLS]).astype(BF16)
    wb = w[:, SSD_COLS:SSD_COLS + 4 * HG_INNER].astype(BF16)
    dtb = jnp.zeros((1, LANES), F32).at[0, :2 * SSD_HEADS].set(ssd_dt_bias[0].reshape(-1))
    nw = mix_pre_norm[0][None, :]
    cw, cb = ssd_conv_w[0], ssd_conv_b[0][None, :]
    neg_a = -jnp.exp(ssd_a_log[0].astype(F32)).reshape(1, 2 * SSD_HEADS)
    dskip = jnp.repeat(ssd_d[0], SSD_HEAD_DIM)[None, :]
    lbp = hg_lb.astype(F32).reshape(2, 2 * HG_INNER)

    _, cxbc, _, cdt, cdtT = inproj_a_call(ctx, csh_m, csc_m, nw, wa, wdt, cw, cb, dtb, tm=ctx.shape[1])
    cu = inproj_b_call(ctx, csh_m, csc_m, nw, wb, col_major=False)
    z_ssd = jnp.zeros((bsz, N_PAIRS, SSD_STATE, LANES), F32)
    z_hg = jnp.zeros((bsz, HG_HEADS, HG_DK, HG_DK), F32)
    _, _, ssf, ssb = ssd_scan_call(cxbc, cdt, cdtT, z_ssd, z_ssd, neg_a, neg_a.reshape(-1, 1), dskip)
    _, _, hsf, hsb = hg_scan_call(cu, lbp, z_hg, z_hg, row_major_out=False)

    zg, xbc, hgate, dt, dtT = inproj_a_call(x, sh_m, sc_m, nw, wa, wdt, cw, cb, dtb, tm=512)
    ub = inproj_b_call(x, sh_m, sc_m, nw, wb, col_major=True)
    sof, sob, _, _ = ssd_scan_call(xbc, dt, dtT, ssf, ssb, neg_a, neg_a.reshape(-1, 1), dskip)
    hof, hob, _, _ = hg_scan_call(ub, lbp, hsf, hsb, row_major_out=True)

    rw = jnp.zeros((d, LANES), F32).at[:, :N_EXPERTS].set(router_w[0])
    rwh = rw.astype(BF16)
    rwl = (rw - rwh.astype(F32)).astype(BF16)
    rb = jnp.zeros((1, LANES), F32).at[0, :N_EXPERTS].set(router_b[0])
    x1, hx, idx, rank, gates, cnt = post_call(
        x, sof, sob, zg, hof, hob, hgate, ssd_norm[0][None, :], hg_norm[0][None, :], w_out[0].astype(BF16),
        mix_post_norm[0][None, :], g_m, ffn_pre_norm[0][None, :], sh_f, sc_f, rwh, rwl, rb)

    r = MOE_ROWS
    n_blocks = (n_tok * TOP_K + N_EXPERTS * (r - 1) + r - 1) // r
    n_tiles = n_tok // TOK_TILE
    counts = cnt[:, 0].astype(jnp.int32)
    padded = (counts + r - 1) // r * r
    pend = jnp.cumsum(padded)
    pstart = pend - padded
    idx3 = idx.reshape(n_tiles, SUBLANES, TOK_TILE)[:, :TOP_K, :]
    rank3 = rank.reshape(n_tiles, SUBLANES, TOK_TILE)[:, :TOP_K, :]
    dest = (pstart[idx3] + rank3).reshape(-1)
    starts = jnp.arange(n_blocks, dtype=jnp.int32) * r
    block_e = jnp.minimum(jnp.sum((pend[None, :] <= starts[:, None]).astype(jnp.int32), axis=1), N_EXPERTS - 1)
    n_used = (pend[-1:] // r).astype(jnp.int32)
    run_end = pend[block_e] // r
    next_e = jnp.where(run_end < n_used[0], block_e[jnp.minimum(run_end, n_blocks - 1)], -1).astype(jnp.int32)

    zstart = jnp.concatenate([pstart + counts, pend[-1:]])
    xs = dispatch_call(zstart, padded - counts, dest, hx, n_blocks * r)
    b1p = jnp.concatenate([moe_b1[0][:, 0::2], moe_b1[0][:, 1::2]], axis=-1)[:, None, :]
    ys = experts_call(block_e, n_used, next_e, xs, moe_w1[0], b1p, moe_w2[0], moe_b2[0][:, None, :])
    out = combine_call(dest, ys, gates, x1, g_f, ffn_post_norm[0][None, :], bsz)
    return out.reshape(bsz, t_len, d)
```

```python
import functools
import math

import numpy as np
import jax
import jax.numpy as jnp
from jax import lax
from jax.experimental import pallas as pl
from jax.experimental.pallas import tpu as pltpu

F32 = jnp.float32
BF16 = jnp.bfloat16
HIGHEST = lax.Precision.HIGHEST

D_MODEL = 1024
GRID_W = 64
SSD_HEADS = 8
SSD_HEAD_DIM = 64
SSD_INNER = 512
SSD_STATE = 64
SSD_GROUPS = 2
SSD_CONV = 5
SSD_XBC = 768
SSD_COLS = 1296
HG_HEADS = 4
HG_DK = 128
HG_INNER = 512
N_EXPERTS = 32
TOP_K = 4
D_FF = 1024
SWIGLU_ALPHA = 1.702
SWIGLU_LIMIT = 7.0
EPS = 1e-6

LANES = 128
SUBLANES = 8
VMEM_LIMIT = 56 * 1024 * 1024

SSD_CHUNK = 128
HG_CHUNK = 128
HG_LEVELS = 7
MOE_ROWS = 256
TOK_TILE = 256
NEG_BIG = -1e30
LOG2E = 1.4426950408889634


def _cparams(sem):
    return pltpu.CompilerParams(dimension_semantics=sem, vmem_limit_bytes=VMEM_LIMIT)


def _sigmoid(x):
    return 1.0 / (1.0 + jnp.exp(-x))


def _silu(x):
    return x * _sigmoid(x)


def _rms(x, w):
    return x * lax.rsqrt(jnp.mean(x * x, axis=-1, keepdims=True) + EPS) * w


def _split3(v):
    hi = v.astype(BF16)
    r1 = v - hi.astype(F32)
    mid = r1.astype(BF16)
    lo = (r1 - mid.astype(F32)).astype(BF16)
    return hi, mid, lo


def _ada_kernel(c_ref, w_ref, b_ref, o_ref):
    s = _silu(c_ref[...])
    o_ref[...] = jnp.dot(s, w_ref[...], precision=HIGHEST, preferred_element_type=F32) + b_ref[...]


def ada_call(cc, w, b):
    n = w.shape[1]
    tn = 1536
    return pl.pallas_call(
        _ada_kernel,
        grid=(n // tn,),
        in_specs=[pl.BlockSpec((SUBLANES, D_MODEL), lambda j: (0, 0)),
                  pl.BlockSpec((D_MODEL, tn), lambda j: (0, j)),
                  pl.BlockSpec((1, tn), lambda j: (0, j))],
        out_specs=pl.BlockSpec((SUBLANES, tn), lambda j: (0, j)),
        out_shape=jax.ShapeDtypeStruct((SUBLANES, n), F32),
        compiler_params=_cparams(("arbitrary",)),
        name="ada",
    )(cc, w, b)


A_COLS = SSD_INNER + SSD_XBC + HG_INNER


def _prep(xt, nw, sh, sc):
    return (_rms(xt, nw) * (1.0 + sc) + sh).astype(BF16)


def _inproj_a_kernel(x_ref, xp_ref, xn_ref, sh_ref, sc_ref, nw_ref, wa_ref, wdt_ref, cw_ref, cb_ref, dtb_ref,
                     z_ref, xbc_ref, g_ref, dt_ref, dtT_ref, scr, *, tm):
    i = pl.program_id(1)
    last = pl.num_programs(1) - 1
    nw, sh, sc = nw_ref[...], sh_ref[...], sc_ref[...]
    h = _prep(x_ref[...], nw, sh, sc)
    ua = jnp.dot(h, wa_ref[...], preferred_element_type=F32)
    z_ref[...] = ua[:, :SSD_INNER]
    g_ref[...] = ua[:, SSD_INNER + SSD_XBC:]
    wx = wa_ref[:, SSD_INNER:SSD_INNER + SSD_XBC]
    up = jnp.dot(_prep(xp_ref[...], nw, sh, sc), wx, preferred_element_type=F32)
    un = jnp.dot(_prep(xn_ref[...], nw, sh, sc), wx, preferred_element_type=F32)
    scr[0:SUBLANES, :] = jnp.where(i > 0, up, 0.0)
    scr[SUBLANES:SUBLANES + tm, :] = ua[:, SSD_INNER:SSD_INNER + SSD_XBC]
    scr[SUBLANES + tm:, :] = jnp.where(i < last, un, 0.0)
    acc = jnp.broadcast_to(cb_ref[...], (tm, SSD_XBC))
    pad = SSD_CONV // 2
    for k in range(SSD_CONV):
        off = SUBLANES - pad + k
        acc = acc + cw_ref[k:k + 1, :] * scr[off:off + tm, :]
    xbc_ref[...] = _silu(acc)
    draw = jnp.dot(h, wdt_ref[...], preferred_element_type=F32) + dtb_ref[...]
    dt = jnp.maximum(draw, 0.0) + jnp.log(1.0 + jnp.exp(-jnp.abs(draw)))
    dt_ref[...] = dt[:, :2 * SSD_HEADS]
    dtT_ref[...] = dt.T[:2 * SSD_HEADS, :]


def inproj_a_call(x, sh, sc, nw, wa, wdt, cw, cb, dtb, tm):
    bsz, t_len, _ = x.shape
    nt = t_len // tm
    r8 = tm // SUBLANES
    n8 = t_len // SUBLANES
    full = lambda shape: pl.BlockSpec(shape, lambda b, i: (0,) * len(shape))
    tok = lambda c: pl.BlockSpec((None, tm, c), lambda b, i: (b, i, 0))
    return pl.pallas_call(
        functools.partial(_inproj_a_kernel, tm=tm),
        grid=(bsz, nt),
        in_specs=[tok(D_MODEL),
                  pl.BlockSpec((None, SUBLANES, D_MODEL), lambda b, i: (b, jnp.maximum(i * r8 - 1, 0), 0)),
                  pl.BlockSpec((None, SUBLANES, D_MODEL), lambda b, i: (b, jnp.minimum((i + 1) * r8, n8 - 1), 0)),
                  pl.BlockSpec((None, 1, D_MODEL), lambda b, i: (b, 0, 0)),
                  pl.BlockSpec((None, 1, D_MODEL), lambda b, i: (b, 0, 0)),
                  full((1, D_MODEL)), full((D_MODEL, A_COLS)), full((D_MODEL, LANES)),
                  full((SSD_CONV, SSD_XBC)), full((1, SSD_XBC)), full((1, LANES))],
        out_specs=[tok(SSD_INNER), tok(SSD_XBC), tok(HG_INNER), tok(2 * SSD_HEADS),
                   pl.BlockSpec((None, 2 * SSD_HEADS, tm), lambda b, i: (b, 0, i))],
        out_shape=[jax.ShapeDtypeStruct((bsz, t_len, SSD_INNER), F32),
                   jax.ShapeDtypeStruct((bsz, t_len, SSD_XBC), F32),
                   jax.ShapeDtypeStruct((bsz, t_len, HG_INNER), F32),
                   jax.ShapeDtypeStruct((bsz, t_len, 2 * SSD_HEADS), F32),
                   jax.ShapeDtypeStruct((bsz, 2 * SSD_HEADS, t_len), F32)],
        scratch_shapes=[pltpu.VMEM((tm + 2 * SUBLANES, SSD_XBC), F32)],
        compiler_params=_cparams(("arbitrary", "arbitrary")),
        name="inproj_a",
    )(x, x, x, sh, sc, nw, wa, wdt, cw, cb, dtb)


B_COLS = 4 * HG_INNER


def _inproj_b_kernel(x_ref, sh_ref, sc_ref, nw_ref, w_ref, o_ref, *, ncol):
    if ncol:
        xt = jnp.concatenate([x_ref[:, w * D_MODEL:(w + 1) * D_MODEL] for w in range(ncol)], axis=0)
    else:
        xt = x_ref[...]
    h = _prep(xt, nw_ref[...], sh_ref[...], sc_ref[...])
    o_ref[...] = jnp.dot(h, w_ref[...], preferred_element_type=F32)


def inproj_b_call(x, sh, sc, nw, wb, col_major):
    bsz, t_len, _ = x.shape
    if col_major:
        ncol = 8
        tm = ncol * GRID_W
        rows = t_len // GRID_W
        assert rows == GRID_W
        xin = x.reshape(bsz, rows, GRID_W * D_MODEL)
        x_spec = pl.BlockSpec((None, rows, ncol * D_MODEL), lambda b, i: (b, 0, i))
    else:
        ncol = 0
        tm = t_len
        xin = x
        x_spec = pl.BlockSpec((None, tm, D_MODEL), lambda b, i: (b, i, 0))
    full = lambda shape: pl.BlockSpec(shape, lambda b, i: (0,) * len(shape))
    return pl.pallas_call(
        functools.partial(_inproj_b_kernel, ncol=ncol),
        grid=(bsz, t_len // tm),
        in_specs=[x_spec,
                  pl.BlockSpec((None, 1, D_MODEL), lambda b, i: (b, 0, 0)),
                  pl.BlockSpec((None, 1, D_MODEL), lambda b, i: (b, 0, 0)),
                  full((1, D_MODEL)), full((D_MODEL, B_COLS))],
        out_specs=pl.BlockSpec((None, tm, B_COLS), lambda b, i: (b, i, 0)),
        out_shape=jax.ShapeDtypeStruct((bsz, t_len, B_COLS), F32),
        compiler_params=_cparams(("arbitrary", "arbitrary")),
        name="inproj_b",
    )(xin, sh, sc, nw, wb)


N_PAIRS = SSD_HEADS // 2


def _ssd_dir(xbc, dt, dtT, s_ref, tri, triT, na_row, na_col, fwd):
    c = SSD_CHUNK
    xs = xbc[:, :SSD_INNER]
    bm = xbc[:, SSD_INNER:SSD_INNER + LANES]
    cm = xbc[:, SSD_INNER + LANES:]
    col0 = 0 if fwd else SSD_HEADS
    la = dt[:, col0:col0 + SSD_HEADS] * na_row[:, col0:col0 + SSD_HEADS]
    dtr = dtT[col0:col0 + SSD_HEADS, :]
    laT = dtr * na_col[col0:col0 + SSD_HEADS, :]
    g = sum(jnp.dot(tri, p, preferred_element_type=F32) for p in _split3(la))
    gT = sum(jnp.dot(p, triT, preferred_element_type=F32) for p in _split3(laT))
    end = c - 1 if fwd else 0
    bmT = bm.T
    ii = lax.broadcasted_iota(jnp.int32, (c, c), 0)
    jj = lax.broadcasted_iota(jnp.int32, (c, c), 1)
    causal = (jj <= ii) if fwd else (jj >= ii)
    lane = lax.broadcasted_iota(jnp.int32, (c, LANES), 1)
    lo_half = lane < SSD_HEAD_DIM
    lane_s = lax.broadcasted_iota(jnp.int32, (SSD_STATE, LANES), 1) < SSD_HEAD_DIM
    outs = []
    for grp in range(SSD_GROUPS):
        in_grp = (lane >= grp * SSD_STATE) & (lane < (grp + 1) * SSD_STATE)
        cm_g = jnp.where(in_grp, cm, 0.0).astype(BF16)
        gmat = jnp.dot(cm_g, bmT.astype(BF16), preferred_element_type=F32)
        bmT_g = bmT[grp * SSD_STATE:(grp + 1) * SSD_STATE, :]
        for pp in range(N_PAIRS // SSD_GROUPS):
            pair = grp * (N_PAIRS // SSD_GROUPS) + pp
            heads = (2 * pair, 2 * pair + 1)
            xs_p = xs[:, pair * LANES:(pair + 1) * LANES]
            xbd = jnp.concatenate([jnp.where(lo_half, xs_p, 0.0), jnp.where(lo_half, 0.0, xs_p)],
                                  axis=0).astype(BF16)
            ms, bws, ecols, arows = [], [], [], []
            for hd in heads:
                gcol = g[:, hd:hd + 1]
                grow = gT[hd:hd + 1, :]
                dec = jnp.exp(jnp.where(causal, gcol - grow, -jnp.inf))
                ms.append(gmat * dec * dtr[hd:hd + 1, :])
                glast = grow[:, end:end + 1]
                bws.append(bmT_g * (dtr[hd:hd + 1, :] * jnp.exp(glast - grow)))
                ecols.append(jnp.exp(gcol))
                arows.append(jnp.exp(glast))
            mcat = jnp.concatenate(ms, axis=1).astype(BF16)
            s_old = s_ref[pair]
            zeros = jnp.zeros_like(s_old)
            s_pad = jnp.concatenate([s_old, zeros] if grp == 0 else [zeros, s_old], axis=0).astype(BF16)
            o_inter = jnp.dot(cm.astype(BF16), s_pad, preferred_element_type=F32)
            o_inter = o_inter * jnp.where(lo_half, ecols[0], ecols[1])
            outs.append(jnp.dot(mcat, xbd, preferred_element_type=F32) + o_inter)
            bw = jnp.concatenate(bws, axis=1).astype(BF16)
            s_ref[pair] = (s_old * jnp.where(lane_s, arows[0], arows[1])
                           + jnp.dot(bw, xbd, preferred_element_type=F32))
    return outs


def _ssd_scan_kernel(xf_ref, dtf_ref, dtTf_ref, xb_ref, dtb_ref, dtTb_ref, s0f_ref, s0b_ref,
                     trif_ref, trifT_ref, trib_ref, tribT_ref, nar_ref, nac_ref, dsk_ref,
                     of_ref, ob_ref, sfo_ref, sbo_ref, sf, sb):
    n = pl.program_id(1)

    @pl.when(n == 0)
    def _():
        sf[...] = s0f_ref[...]
        sb[...] = s0b_ref[...]

    xf = xf_ref[...]
    outs = _ssd_dir(xf, dtf_ref[...], dtTf_ref[...], sf, trif_ref[...], trifT_ref[...],
                    nar_ref[...], nac_ref[...], True)
    of_ref[...] = jnp.concatenate(outs, axis=1) + dsk_ref[...] * xf[:, :SSD_INNER]
    outs = _ssd_dir(xb_ref[...], dtb_ref[...], dtTb_ref[...], sb, trib_ref[...], tribT_ref[...],
                    nar_ref[...], nac_ref[...], False)
    ob_ref[...] = jnp.concatenate(outs, axis=1)

    @pl.when(n == pl.num_programs(1) - 1)
    def _():
        sfo_ref[...] = sf[...]
        sbo_ref[...] = sb[...]


def _ssd_consts():
    c = SSD_CHUNK
    i = np.arange(c)
    trif = (i[:, None] >= i[None, :]).astype(np.float32)
    trib = (i[:, None] <= i[None, :]).astype(np.float32)
    return [jnp.asarray(a, BF16) for a in (trif, trif.T, trib, trib.T)]


def ssd_scan_call(xbc, dt, dtT, s0f, s0b, na_row, na_col, dskip):
    bsz, t_len, _ = xbc.shape
    c = SSD_CHUNK
    nc = t_len // c
    fw = lambda w: pl.BlockSpec((None, c, w), lambda b, n: (b, n, 0))
    bw = lambda w: pl.BlockSpec((None, c, w), lambda b, n: (b, nc - 1 - n, 0))
    full = lambda shape: pl.BlockSpec(shape, lambda b, n: (0,) * len(shape))
    st = pl.BlockSpec((None, N_PAIRS, SSD_STATE, LANES), lambda b, n: (b, 0, 0, 0))
    st_shape = jax.ShapeDtypeStruct((bsz, N_PAIRS, SSD_STATE, LANES), F32)
    return pl.pallas_call(
        _ssd_scan_kernel,
        grid=(bsz, nc),
        in_specs=[fw(SSD_XBC), fw(2 * SSD_HEADS),
                  pl.BlockSpec((None, 2 * SSD_HEADS, c), lambda b, n: (b, 0, n)),
                  bw(SSD_XBC), bw(2 * SSD_HEADS),
                  pl.BlockSpec((None, 2 * SSD_HEADS, c), lambda b, n: (b, 0, nc - 1 - n)),
                  st, st, full((c, c)), full((c, c)), full((c, c)), full((c, c)),
                  full((1, 2 * SSD_HEADS)), full((2 * SSD_HEADS, 1)), full((1, SSD_INNER))],
        out_specs=[fw(SSD_INNER), bw(SSD_INNER), st, st],
        out_shape=[jax.ShapeDtypeStruct((bsz, t_len, SSD_INNER), F32),
                   jax.ShapeDtypeStruct((bsz, t_len, SSD_INNER), F32), st_shape, st_shape],
        scratch_shapes=[pltpu.VMEM((N_PAIRS, SSD_STATE, LANES), F32),
                        pltpu.VMEM((N_PAIRS, SSD_STATE, LANES), F32)],
        compiler_params=_cparams(("arbitrary", "arbitrary")),
        name="ssd_scan",
    )(xbc, dt, dtT, xbc, dt, dtT, s0f, s0b, *_ssd_consts(), na_row, na_col, dskip)


def _hg_consts():
    c = HG_CHUNK
    t = np.arange(c)
    m_f = np.zeros((HG_LEVELS + 1, c, c), np.float32)
    m_f[0] = np.eye(c)
    for lv in range(1, HG_LEVELS + 1):
        m = 2 ** lv
        blk = t // m
        right = (t % m) >= m // 2
        m_f[lv] = (blk[:, None] == blk[None, :]) & right[:, None] & (~right[None, :])
    m_b = np.transpose(m_f, (0, 2, 1))
    tri_f = (t[:, None] >= t[None, :]).astype(np.float32)
    tri_b = (t[:, None] <= t[None, :]).astype(np.float32)
    return jnp.asarray(tri_f, BF16), jnp.asarray(tri_b, BF16), jnp.asarray(m_f, F32), jnp.asarray(m_b, F32)


def _hg_dir(u, lb_row, st_ref, tri_ref, mask_ref, fwd):
    c = HG_CHUNK
    outs = []
    fcol = HG_INNER if fwd else 2 * HG_INNER
    nt = (((1,), (1,)), ((), ()))
    row = lax.broadcasted_iota(jnp.int32, (c, HG_DK), 0)

    def halves(lo, hi, half):
        if half % SUBLANES == 0:
            return jnp.concatenate([(hi if (s // half) % 2 else lo)[s:s + half] for s in range(0, c, half)], axis=0)
        return jnp.where(((row // half) % 2) == 1, hi, lo)

    def shift(x, s):
        s = s % c
        if s % SUBLANES == 0:
            return jnp.concatenate([x[c - s:], x[:c - s]], axis=0)
        return pltpu.roll(x, s, 0)

    def by_side(query_side, key_side, half):
        return halves(key_side, query_side, half) if fwd else halves(query_side, key_side, half)

    for hd in range(HG_HEADS):
        sl = slice(hd * HG_DK, (hd + 1) * HG_DK)
        q = _silu(u[:, sl]) * (HG_DK ** -0.5)
        lb = lb_row[:, sl]
        f = lb + (1.0 - lb) * _sigmoid(u[:, fcol + hd * HG_DK:fcol + (hd + 1) * HG_DK])
        k = 1.0 - f
        la = jnp.log(f)
        v = u[:, 3 * HG_INNER + hd * HG_DK:3 * HG_INNER + (hd + 1) * HG_DK]
        la_hi = la.astype(BF16)
        la_lo = (la - la_hi.astype(F32)).astype(BF16)
        g2 = jnp.dot(tri_ref[...], jnp.concatenate([la_hi, la_lo], axis=1), preferred_element_type=F32)
        g = g2[:, :HG_DK] + g2[:, HG_DK:]
        scores = mask_ref[0] * lax.dot_general(q.astype(BF16), k.astype(BF16), nt, preferred_element_type=F32)
        fill = g
        for lv in range(1, HG_LEVELS + 1):
            half = 2 ** (lv - 1)
            if fwd:
                ref = halves(fill, shift(fill, half), half)
            else:
                ref = halves(shift(fill, -half), fill, half)
            decay = jnp.exp2(jnp.abs(g - ref) * (-LOG2E))
            y = (by_side(q, k, half) * decay).astype(BF16)
            scores = scores + mask_ref[lv] * lax.dot_general(y, y, nt, preferred_element_type=F32)
            if fwd:
                fill = halves(shift(fill, -half), fill, half)
            else:
                fill = halves(fill, shift(fill, half), half)
        x_cum = jnp.exp(g)
        x_rem = jnp.exp(fill - g)
        st = st_ref[hd]
        o = jnp.dot(scores.astype(BF16), v.astype(BF16), preferred_element_type=F32)
        o = o + lax.dot_general((q * x_cum).astype(BF16), st.astype(BF16), nt, preferred_element_type=F32)
        outs.append(o)
        end = c - 1 if fwd else 0
        st_ref[hd] = (st * x_cum[end:end + 1, :]
                      + jnp.dot(v.T.astype(BF16), (k * x_rem).astype(BF16), preferred_element_type=F32))
    return jnp.concatenate(outs, axis=1)


def _hg_scan_kernel(uf_ref, ub_ref, lbp_ref, s0f_ref, s0b_ref, trif_ref, trib_ref, mf_ref, mb_ref,
                    of_ref, ob_ref, sfo_ref, sbo_ref, sf, sb, *, img_rows):
    n = pl.program_id(1)

    def put(o_ref, o):
        if img_rows:
            for j in range(HG_CHUNK // img_rows):
                o_ref[:, j * HG_INNER:(j + 1) * HG_INNER] = o[j * img_rows:(j + 1) * img_rows]
        else:
            o_ref[...] = o


    @pl.when(n == 0)
    def _():
        sf[...] = s0f_ref[...]
        sb[...] = s0b_ref[...]

    p = lbp_ref[...]
    mx = jnp.max(p, axis=0, keepdims=True)
    e = jnp.exp(p - mx)
    lb = e[0:1, :] / jnp.sum(e, axis=0, keepdims=True)
    put(of_ref, _hg_dir(uf_ref[...], lb[:, :HG_INNER], sf, trif_ref, mf_ref, True))
    put(ob_ref, _hg_dir(ub_ref[...], lb[:, HG_INNER:], sb, trib_ref, mb_ref, False))

    @pl.when(n == pl.num_programs(1) - 1)
    def _():
        sfo_ref[...] = sf[...]
        sbo_ref[...] = sb[...]


def hg_scan_call(u, lbp, s0f, s0b, row_major_out):
    bsz, t_len, _ = u.shape
    c = HG_CHUNK
    nc = t_len // c
    full = lambda shape: pl.BlockSpec(shape, lambda b, n: (0,) * len(shape))
    st = pl.BlockSpec((None, HG_HEADS, HG_DK, HG_DK), lambda b, n: (b, 0, 0, 0))
    st_shape = jax.ShapeDtypeStruct((bsz, HG_HEADS, HG_DK, HG_DK), F32)
    if row_major_out:
        img_rows = t_len // GRID_W
        cols = c // img_rows
        assert cols * img_rows == c
        o_shape = jax.ShapeDtypeStruct((bsz, img_rows, GRID_W * HG_INNER), F32)
        of_spec = pl.BlockSpec((None, img_rows, cols * HG_INNER), lambda b, n: (b, 0, n))
        ob_spec = pl.BlockSpec((None, img_rows, cols * HG_INNER), lambda b, n: (b, 0, nc - 1 - n))
    else:
        img_rows = 0
        o_shape = jax.ShapeDtypeStruct((bsz, t_len, HG_INNER), F32)
        of_spec = pl.BlockSpec((None, c, HG_INNER), lambda b, n: (b, n, 0))
        ob_spec = pl.BlockSpec((None, c, HG_INNER), lambda b, n: (b, nc - 1 - n, 0))
    tri_f, tri_b, m_f, m_b = _hg_consts()
    of, ob, sfo, sbo = pl.pallas_call(
        functools.partial(_hg_scan_kernel, img_rows=img_rows),
        grid=(bsz, nc),
        in_specs=[pl.BlockSpec((None, c, B_COLS), lambda b, n: (b, n, 0)),
                  pl.BlockSpec((None, c, B_COLS), lambda b, n: (b, nc - 1 - n, 0)),
                  full((2, 2 * HG_INNER)), st, st,
                  full(tri_f.shape), full(tri_b.shape), full(m_f.shape), full(m_b.shape)],
        out_specs=[of_spec, ob_spec, st, st],
        out_shape=[o_shape, o_shape, st_shape, st_shape],
        scratch_shapes=[pltpu.VMEM((HG_HEADS, HG_DK, HG_DK), F32), pltpu.VMEM((HG_HEADS, HG_DK, HG_DK), F32)],
        compiler_params=_cparams(("arbitrary", "arbitrary")),
        name="hg_scan",
    )(u, u, lbp, s0f, s0b, tri_f, tri_b, m_f, m_b)
    if row_major_out:
        of = of.reshape(bsz, t_len, HG_INNER)
        ob = ob.reshape(bsz, t_len, HG_INNER)
    return of, ob, sfo, sbo


ROW_LINES = D_MODEL // LANES


def _to_token_tiles(ref, val, n_rows):
    for c in range(ROW_LINES):
        ref[pl.ds(c, n_rows, stride=ROW_LINES), :] = val[:, c * LANES:(c + 1) * LANES]


def _from_token_tiles(ref, n_rows, first_row=0):
    return jnp.concatenate([ref[pl.ds(first_row * ROW_LINES + c, n_rows, stride=ROW_LINES), :]
                            for c in range(ROW_LINES)], axis=1)


def _tile_of(ref, row):
    return ref.at[pl.ds(pl.multiple_of(row * ROW_LINES, ROW_LINES), ROW_LINES)]


def _post_kernel(x_ref, sof_ref, sob_ref, z_ref, hof_ref, hob_ref, hg_ref,
                 snw_ref, hnw_ref, wo_ref, pnw_ref, gm_ref, fnw_ref, shf_ref, scf_ref, rwh_ref, rwl_ref, rb_ref,
                 tri_ref, x1_ref, hx_ref, idx_ref, rank_ref, gate_ref, cnt_ref, carry, *, tm):
    first = (pl.program_id(0) == 0) & (pl.program_id(1) == 0)

    @pl.when(first)
    def _():
        carry[...] = jnp.zeros_like(carry)

    y = (sof_ref[...] + sob_ref[...]) * _silu(z_ref[...])
    y = _rms(y, snw_ref[...])
    o = hof_ref[...] + hob_ref[...]
    hnw = hnw_ref[...]
    o = jnp.concatenate([_rms(o[:, h * HG_DK:(h + 1) * HG_DK], hnw[:, h * HG_DK:(h + 1) * HG_DK])
                         for h in range(HG_HEADS)], axis=1)
    o = o * _silu(hg_ref[...])
    mix = (jnp.dot(y.astype(BF16), wo_ref[:SSD_INNER, :], preferred_element_type=F32)
           + jnp.dot(o.astype(BF16), wo_ref[SSD_INNER:, :], preferred_element_type=F32))
    x1 = x_ref[...] + gm_ref[...] * _rms(mix, pnw_ref[...])
    x1_ref[...] = x1
    hx = _rms(x1, fnw_ref[...]) * (1.0 + scf_ref[...]) + shf_ref[...]
    _to_token_tiles(hx_ref, hx, tm)
    hx_hi = hx.astype(BF16)
    hx_lo = (hx - hx_hi.astype(F32)).astype(BF16)
    rwh = rwh_ref[...]
    logits = (jnp.dot(hx_hi, rwh, preferred_element_type=F32) + jnp.dot(hx_lo, rwh, preferred_element_type=F32)
              + jnp.dot(hx_hi, rwl_ref[...], preferred_element_type=F32)) + rb_ref[...]
    work = logits.T[:N_EXPERTS, :]
    erow = lax.broadcasted_iota(jnp.int32, (N_EXPERTS, tm), 0)
    vals, idxs = [], []
    for _ in range(TOP_K):
        m = jnp.max(work, axis=0, keepdims=True)
        ix = jnp.min(jnp.where(work == m, erow, N_EXPERTS), axis=0, keepdims=True)
        vals.append(m)
        idxs.append(ix)
        work = jnp.where(erow == ix, -jnp.inf, work)
    es = [jnp.exp(v - vals[0]) for v in vals]
    den = es[0] + es[1] + es[2] + es[3]
    onehots = [(erow == ix) for ix in idxs]
    multi = sum(oh.astype(F32) for oh in onehots)
    before = jnp.dot(multi.astype(BF16), tri_ref[...], preferred_element_type=F32) + carry[...]
    carry[...] = carry[...] + jnp.sum(multi, axis=1, keepdims=True)
    sub = lax.broadcasted_iota(jnp.int32, (SUBLANES, tm), 0)
    idx_o = jnp.zeros((SUBLANES, tm), jnp.int32)
    rank_o = jnp.zeros((SUBLANES, tm), jnp.int32)
    gate_o = jnp.zeros((SUBLANES, tm), F32)
    for k in range(TOP_K):
        rk = jnp.sum(jnp.where(onehots[k], before, 0.0), axis=0, keepdims=True)
        idx_o = jnp.where(sub == k, idxs[k], idx_o)
        rank_o = jnp.where(sub == k, rk.astype(jnp.int32), rank_o)
        gate_o = jnp.where(sub == k, es[k] / den, gate_o)
    idx_ref[...] = idx_o
    rank_ref[...] = rank_o
    gate_ref[...] = gate_o
    cnt_ref[...] = jnp.broadcast_to(carry[...], (N_EXPERTS, LANES))


def post_call(x, sof, sob, z, hof, hob, hg, snw, hnw, wo, pnw, gm, fnw, shf, scf, rwh, rwl, rb):
    bsz, t_len, _ = x.shape
    tm = TOK_TILE
    nt = t_len // tm
    n_tok = bsz * t_len
    tok = lambda c: pl.BlockSpec((None, tm, c), lambda b, i: (b, i, 0))
    full = lambda shape: pl.BlockSpec(shape, lambda b, i: (0,) * len(shape))
    per_b = pl.BlockSpec((None, 1, D_MODEL), lambda b, i: (b, 0, 0))
    flat = lambda c: pl.BlockSpec((tm, c), lambda b, i: (b * nt + i, 0))
    rout = pl.BlockSpec((SUBLANES, tm), lambda b, i: (b * nt + i, 0))
    n_tiles = n_tok // tm
    ii = np.arange(tm)
    tri = jnp.asarray(ii[:, None] < ii[None, :], BF16)
    return pl.pallas_call(
        functools.partial(_post_kernel, tm=tm),
        grid=(bsz, nt),
        in_specs=[tok(D_MODEL), tok(SSD_INNER), tok(SSD_INNER), tok(SSD_INNER), tok(HG_INNER), tok(HG_INNER),
                  tok(HG_INNER), full((1, SSD_INNER)), full((1, HG_INNER)), full((D_MODEL, D_MODEL)),
                  full((1, D_MODEL)), per_b, full((1, D_MODEL)), per_b, per_b,
                  full((D_MODEL, LANES)), full((D_MODEL, LANES)), full((1, LANES)), full((tm, tm))],
        out_specs=[flat(D_MODEL), pl.BlockSpec((tm * ROW_LINES, LANES), lambda b, i: (b * nt + i, 0)),
                   rout, rout, rout, pl.BlockSpec((N_EXPERTS, LANES), lambda b, i: (0, 0))],
        out_shape=[jax.ShapeDtypeStruct((n_tok, D_MODEL), F32),
                   jax.ShapeDtypeStruct((n_tok * ROW_LINES, LANES), F32),
                   jax.ShapeDtypeStruct((n_tiles * SUBLANES, tm), jnp.int32),
                   jax.ShapeDtypeStruct((n_tiles * SUBLANES, tm), jnp.int32),
                   jax.ShapeDtypeStruct((n_tiles * SUBLANES, tm), F32),
                   jax.ShapeDtypeStruct((N_EXPERTS, LANES), F32)],
        scratch_shapes=[pltpu.VMEM((N_EXPERTS, 1), F32)],
        compiler_params=_cparams(("arbitrary", "arbitrary")),
        name="post",
    )(x, sof, sob, z, hof, hob, hg, snw, hnw, wo, pnw, gm, fnw, shf, scf, rwh, rwl, rb, tri)


def _wait_rows(hbm_ref, n_rows, sem):
    n = n_rows * ROW_LINES
    pltpu.make_async_copy(hbm_ref.at[pl.ds(0, n)], hbm_ref.at[pl.ds(0, n)], sem).wait()


PAD_RUNS = tuple(2 ** j for j in range(int(math.log2(MOE_ROWS))))


def _dispatch_kernel(zstart_ref, zpad_ref, dest_ref, hx_ref, buf_ref, zrows, sem, zsem, *, tm):
    i = pl.program_id(0)

    def zero_rows(first, n):
        first = pl.multiple_of(first * ROW_LINES, ROW_LINES)
        return pltpu.make_async_copy(zrows.at[pl.ds(0, n * ROW_LINES)], buf_ref.at[pl.ds(first, n * ROW_LINES)], zsem)

    def for_pad_runs(act):
        def per_expert(e, carry):
            pad = zpad_ref[e]
            off = zstart_ref[e]
            for run in PAD_RUNS:
                @pl.when((pad & run) != 0)
                def _():
                    act(zero_rows(off, run))
                off = off + (pad & run)
            return carry

        lax.fori_loop(0, N_EXPERTS, per_expert, 0)

        def tail(j, carry):
            act(zero_rows(j * PAD_RUNS[-1], PAD_RUNS[-1]))
            return carry

        n_rows = buf_ref.shape[0] // ROW_LINES
        lax.fori_loop(zstart_ref[N_EXPERTS] // PAD_RUNS[-1], n_rows // PAD_RUNS[-1], tail, 0)

    @pl.when(i == 0)
    def _():
        zrows[...] = jnp.zeros_like(zrows)
        for_pad_runs(lambda cp: cp.start())

    def issue(t, carry):
        for k in range(TOP_K):
            pltpu.make_async_copy(_tile_of(hx_ref, t), _tile_of(buf_ref, dest_ref[k * tm + t]), sem).start()
        return carry

    lax.fori_loop(0, tm, issue, 0, unroll=4)
    _wait_rows(buf_ref, tm * TOP_K, sem)

    @pl.when(i == 0)
    def _():
        for_pad_runs(lambda cp: cp.wait())


def dispatch_call(zstart, zpad, dest_flat, hx, n_rows):
    n_tok = hx.shape[0] // ROW_LINES
    tm = TOK_TILE
    return pl.pallas_call(
        functools.partial(_dispatch_kernel, tm=tm),
        grid=(n_tok // tm,),
        in_specs=[pl.BlockSpec(memory_space=pltpu.SMEM), pl.BlockSpec(memory_space=pltpu.SMEM),
                  pl.BlockSpec((tm * TOP_K,), lambda i: (i,), memory_space=pltpu.SMEM),
                  pl.BlockSpec((tm * ROW_LINES, LANES), lambda i: (i, 0))],
        out_specs=pl.BlockSpec(memory_space=pl.ANY),
        out_shape=jax.ShapeDtypeStruct((n_rows * ROW_LINES, LANES), F32),
        scratch_shapes=[pltpu.VMEM((PAD_RUNS[-1] * ROW_LINES, LANES), F32), pltpu.SemaphoreType.DMA(()),
                        pltpu.SemaphoreType.DMA(())],
        compiler_params=_cparams(("arbitrary",)),
        name="dispatch",
    )(zstart, zpad, dest_flat, hx)


W1_TCOLS = 256


def _experts_kernel(be_ref, nu_ref, nxt_ref, x_ref, w1_hbm, b1_ref, w2_hbm, b2_ref, y_ref,
                    w1f, w2f, w1t, w2s, tbuf, wsem):
    i = pl.program_id(0)
    live = i < nu_ref[0]
    new_expert = (i == 0) | (be_ref[i] != be_ref[jnp.maximum(i - 1, 0)])

    def weight_copies(e):
        return (pltpu.make_async_copy(w1_hbm.at[e], w1f, wsem.at[0]),
                pltpu.make_async_copy(w2_hbm.at[e], w2f, wsem.at[1]))

    @pl.when(live & new_expert)
    def _():
        @pl.when(i == 0)
        def _():
            for cp in weight_copies(be_ref[0]):
                cp.start()

        for cp in weight_copies(be_ref[i]):
            cp.wait()
        half = W1_TCOLS // 2
        for c in range(2 * D_FF // W1_TCOLS):
            for j in range(D_MODEL // LANES):
                ks = slice(j * LANES, (j + 1) * LANES)
                tbuf[j] = w1f[ks, c * W1_TCOLS:(c + 1) * W1_TCOLS].T
                w1t[c * half:(c + 1) * half, ks] = tbuf[j, pl.ds(0, half, stride=2), :].astype(BF16)
                w1t[D_FF + c * half:D_FF + (c + 1) * half, ks] = tbuf[j, pl.ds(1, half, stride=2), :].astype(BF16)
        w2s[...] = w2f[...].astype(BF16)

        @pl.when(nxt_ref[i] >= 0)
        def _():
            for cp in weight_copies(nxt_ref[i]):
                cp.start()

    @pl.when(live)
    def _():
        xb = _from_token_tiles(x_ref, MOE_ROWS).astype(BF16)
        u = lax.dot_general(xb, w1t[...], (((1,), (1,)), ((), ())), preferred_element_type=F32) + b1_ref[...]
        glu = jnp.minimum(u[:, :D_FF], SWIGLU_LIMIT)
        lin = jnp.clip(u[:, D_FF:], -SWIGLU_LIMIT, SWIGLU_LIMIT)
        a = glu * _sigmoid(SWIGLU_ALPHA * glu) * (lin + 1.0)
        y = jnp.dot(a.astype(BF16), w2s[...], preferred_element_type=F32) + b2_ref[...]
        _to_token_tiles(y_ref, y, MOE_ROWS)

    @pl.when(jnp.logical_not(live))
    def _():
        y_ref[...] = jnp.zeros_like(y_ref)


def experts_call(block_e, n_used, next_e, xs, w1, b1p, w2, b2):
    rows = xs.shape[0] // ROW_LINES
    r = MOE_ROWS
    nb = rows // r
    blk = (r * ROW_LINES, LANES)
    grid_spec = pltpu.PrefetchScalarGridSpec(
        num_scalar_prefetch=3,
        grid=(nb,),
        in_specs=[pl.BlockSpec(blk, lambda i, be, nu, nx: (jnp.maximum(jnp.minimum(i, nu[0] - 1), 0), 0)),
                  pl.BlockSpec(memory_space=pl.ANY),
                  pl.BlockSpec((None, 1, 2 * D_FF), lambda i, be, nu, nx: (be[i], 0, 0)),
                  pl.BlockSpec(memory_space=pl.ANY),
                  pl.BlockSpec((None, 1, D_MODEL), lambda i, be, nu, nx: (be[i], 0, 0))],
        out_specs=pl.BlockSpec(blk, lambda i, be, nu, nx: (i, 0)),
        scratch_shapes=[pltpu.VMEM((D_MODEL, 2 * D_FF), F32), pltpu.VMEM((D_FF, D_MODEL), F32),
                        pltpu.VMEM((2 * D_FF, D_MODEL), BF16), pltpu.VMEM((D_FF, D_MODEL), BF16),
                        pltpu.VMEM((D_MODEL // LANES, W1_TCOLS, LANES), F32),
                        pltpu.SemaphoreType.DMA((2,))],
    )
    return pl.pallas_call(
        _experts_kernel,
        grid_spec=grid_spec,
        out_shape=jax.ShapeDtypeStruct((rows * ROW_LINES, LANES), F32),
        compiler_params=_cparams(("arbitrary",)),
        name="experts",
    )(block_e, n_used, next_e, xs, w1, b1p, w2, b2)


def _combine_kernel(dest_ref, y_ref, gate_ref, x1_ref, gf_ref, nw_ref, o_ref, buf, sem, *, tm):
    def issue(t, carry):
        for k in range(TOP_K):
            pltpu.make_async_copy(_tile_of(y_ref, dest_ref[k * tm + t]), _tile_of(buf, k * tm + t), sem).start()
        return carry

    lax.fori_loop(0, tm, issue, 0, unroll=4)
    _wait_rows(y_ref, tm * TOP_K, sem)
    gate = gate_ref[...].T
    fx = gate[:, 0:1] * _from_token_tiles(buf, tm)
    for k in range(1, TOP_K):
        fx = fx + gate[:, k:k + 1] * _from_token_tiles(buf, tm, first_row=k * tm)
    o_ref[...] = x1_ref[...] + gf_ref[...] * _rms(fx, nw_ref[...])


def combine_call(dest_flat, ys, gates, x1, gf, nw, bsz):
    n_tok = x1.shape[0]
    tm = TOK_TILE
    nt = n_tok // bsz // tm
    return pl.pallas_call(
        functools.partial(_combine_kernel, tm=tm),
        grid=(n_tok // tm,),
        in_specs=[pl.BlockSpec((tm * TOP_K,), lambda i: (i,), memory_space=pltpu.SMEM),
                  pl.BlockSpec(memory_space=pl.ANY),
                  pl.BlockSpec((SUBLANES, tm), lambda i: (i, 0)),
                  pl.BlockSpec((tm, D_MODEL), lambda i: (i, 0)),
                  pl.BlockSpec((None, 1, D_MODEL), lambda i: (i // nt, 0, 0)),
                  pl.BlockSpec((1, D_MODEL), lambda i: (0, 0))],
        out_specs=pl.BlockSpec((tm, D_MODEL), lambda i: (i, 0)),
        out_shape=jax.ShapeDtypeStruct((n_tok, D_MODEL), F32),
        scratch_shapes=[pltpu.VMEM((TOP_K * tm * ROW_LINES, LANES), F32), pltpu.SemaphoreType.DMA(())],
        compiler_params=_cparams(("arbitrary",)),
        name="combine",
    )(dest_flat, ys, gates, x1, gf, nw)


def kernel(x, c, ctx, c_ctx, ada_w, ada_b, mix_pre_norm, mix_post_norm, w_in, w_out, ssd_conv_w, ssd_conv_b,
           ssd_dt_bias, ssd_a_log, ssd_d, ssd_norm, hg_lb, hg_norm, ffn_pre_norm, ffn_post_norm, router_w,
           router_b, moe_w1, moe_b1, moe_w2, moe_b2):
    bsz, t_len, d = x.shape
    assert ada_w.shape[0] == 1 and d == D_MODEL and bsz <= SUBLANES - 1
    n_tok = bsz * t_len

    cc = jnp.zeros((SUBLANES, d), F32).at[:bsz].set(c).at[bsz].set(c_ctx)
    mod = ada_call(cc, ada_w[0], ada_b[0][None, :])
    sh_m, sc_m, g_m, sh_f, sc_f, g_f = [m[:bsz, None, :] for m in jnp.split(mod, 6, axis=-1)]
    csh_m, csc_m = [jnp.broadcast_to(m[bsz][None, None, :], (bsz, 1, d)) for m in jnp.split(mod, 6, axis=-1)[:2]]

    w = w_in[0]
    wa = jnp.concatenate([w[:, :SSD_INNER + SSD_XBC], w[:, SSD_COLS + 4 * HG_INNER:]], axis=1).astype(BF16)
    wdt = jnp.zeros((d, LANES), F32).at[:, :2 * SSD_HEADS].set(w[:, SSD_INNER + SSD_XBC:SSD_COLS]).astype(BF16)
    wb = w[:, SSD_COLS:SSD_COLS + 4 * HG_INNER].astype(BF16)
    dtb = jnp.zeros((1, LANES), F32).at[0, :2 * SSD_HEADS].set(ssd_dt_bias[0].reshape(-1))
    nw = mix_pre_norm[0][None, :]
    cw, cb = ssd_conv_w[0], ssd_conv_b[0][None, :]
    neg_a = -jnp.exp(ssd_a_log[0].astype(F32)).reshape(1, 2 * SSD_HEADS)
    dskip = jnp.repeat(ssd_d[0], SSD_HEAD_DIM)[None, :]
    lbp = hg_lb.astype(F32).reshape(2, 2 * HG_INNER)

    _, cxbc, _, cdt, cdtT = inproj_a_call(ctx, csh_m, csc_m, nw, wa, wdt, cw, cb, dtb, tm=ctx.shape[1])
    cu = inproj_b_call(ctx, csh_m, csc_m, nw, wb, col_major=False)
    z_ssd = jnp.zeros((bsz, N_PAIRS, SSD_STATE, LANES), F32)
    z_hg = jnp.zeros((bsz, HG_HEADS, HG_DK, HG_DK), F32)
    _, _, ssf, ssb = ssd_scan_call(cxbc, cdt, cdtT, z_ssd, z_ssd, neg_a, neg_a.reshape(-1, 1), dskip)
    _, _, hsf, hsb = hg_scan_call(cu, lbp, z_hg, z_hg, row_major_out=False)

    zg, xbc, hgate, dt, dtT = inproj_a_call(x, sh_m, sc_m, nw, wa, wdt, cw, cb, dtb, tm=512)
    ub = inproj_b_call(x, sh_m, sc_m, nw, wb, col_major=True)
    sof, sob, _, _ = ssd_scan_call(xbc, dt, dtT, ssf, ssb, neg_a, neg_a.reshape(-1, 1), dskip)
    hof, hob, _, _ = hg_scan_call(ub, lbp, hsf, hsb, row_major_out=True)

    rw = jnp.zeros((d, LANES), F32).at[:, :N_EXPERTS].set(router_w[0])
    rwh = rw.astype(BF16)
    rwl = (rw - rwh.astype(F32)).astype(BF16)
    rb = jnp.zeros((1, LANES), F32).at[0, :N_EXPERTS].set(router_b[0])
    x1, hx, idx, rank, gates, cnt = post_call(
        x, sof, sob, zg, hof, hob, hgate, ssd_norm[0][None, :], hg_norm[0][None, :], w_out[0].astype(BF16),
        mix_post_norm[0][None, :], g_m, ffn_pre_norm[0][None, :], sh_f, sc_f, rwh, rwl, rb)

    r = MOE_ROWS
    n_blocks = (n_tok * TOP_K + N_EXPERTS * (r - 1) + r - 1) // r
    n_tiles = n_tok // TOK_TILE
    counts = cnt[:, 0].astype(jnp.int32)
    padded = (counts + r - 1) // r * r
    pend = jnp.cumsum(padded)
    pstart = pend - padded
    idx3 = idx.reshape(n_tiles, SUBLANES, TOK_TILE)[:, :TOP_K, :]
    rank3 = rank.reshape(n_tiles, SUBLANES, TOK_TILE)[:, :TOP_K, :]
    dest = rank3
    for e in range(N_EXPERTS):
        dest = dest + jnp.where(idx3 == e, pstart[e], 0)
    dest = dest.reshape(-1)
    starts = jnp.arange(n_blocks, dtype=jnp.int32) * r
    block_e = jnp.minimum(jnp.sum((pend[None, :] <= starts[:, None]).astype(jnp.int32), axis=1), N_EXPERTS - 1)
    n_used = (pend[-1:] // r).astype(jnp.int32)
    run_end = pend[block_e] // r
    next_e = jnp.where(run_end < n_used[0], block_e[jnp.minimum(run_end, n_blocks - 1)], -1).astype(jnp.int32)

    zstart = jnp.concatenate([pstart + counts, pend[-1:]])
    xs = dispatch_call(zstart, padded - counts, dest, hx, n_blocks * r)
    b1p = jnp.concatenate([moe_b1[0][:, 0::2], moe_b1[0][:, 1::2]], axis=-1)[:, None, :]
    ys = experts_call(block_e, n_used, next_e, xs, moe_w1[0], b1p, moe_w2[0], moe_b2[0][:, None, :])
    out = combine_call(dest, ys, gates, x1, g_f, ffn_post_norm[0][None, :], bsz)
    return out.reshape(bsz, t_len, d)
```

```python
import functools
import math

import numpy as np
import jax
import jax.numpy as jnp
from jax import lax
from jax.experimental import pallas as pl
from jax.experimental.pallas import tpu as pltpu

F32 = jnp.float32
BF16 = jnp.bfloat16
HIGHEST = lax.Precision.HIGHEST

D_MODEL = 1024
GRID_W = 64
SSD_HEADS = 8
SSD_HEAD_DIM = 64
SSD_INNER = 512
SSD_STATE = 64
SSD_GROUPS = 2
SSD_CONV = 5
SSD_XBC = 768
SSD_COLS = 1296
HG_HEADS = 4
HG_DK = 128
HG_INNER = 512
N_EXPERTS = 32
TOP_K = 4
D_FF = 1024
SWIGLU_ALPHA = 1.702
SWIGLU_LIMIT = 7.0
EPS = 1e-6

LANES = 128
SUBLANES = 8
VMEM_LIMIT = 56 * 1024 * 1024

SSD_CHUNK = 128
HG_CHUNK = 128
HG_LEVELS = 7
MOE_ROWS = 256
TOK_TILE = 256
NEG_BIG = -1e30
LOG2E = 1.4426950408889634


def _cparams(sem):
    return pltpu.CompilerParams(dimension_semantics=sem, vmem_limit_bytes=VMEM_LIMIT)


def _sigmoid(x):
    return 1.0 / (1.0 + jnp.exp(-x))


def _silu(x):
    return x * _sigmoid(x)


def _rms(x, w):
    return x * lax.rsqrt(jnp.mean(x * x, axis=-1, keepdims=True) + EPS) * w


def _split3(v):
    hi = v.astype(BF16)
    r1 = v - hi.astype(F32)
    mid = r1.astype(BF16)
    lo = (r1 - mid.astype(F32)).astype(BF16)
    return hi, mid, lo


def _ada_kernel(c_ref, w_ref, b_ref, o_ref):
    s = _silu(c_ref[...])
    o_ref[...] = jnp.dot(s, w_ref[...], precision=HIGHEST, preferred_element_type=F32) + b_ref[...]


def ada_call(cc, w, b):
    n = w.shape[1]
    tn = 1536
    return pl.pallas_call(
        _ada_kernel,
        grid=(n // tn,),
        in_specs=[pl.BlockSpec((SUBLANES, D_MODEL), lambda j: (0, 0)),
                  pl.BlockSpec((D_MODEL, tn), lambda j: (0, j)),
                  pl.BlockSpec((1, tn), lambda j: (0, j))],
        out_specs=pl.BlockSpec((SUBLANES, tn), lambda j: (0, j)),
        out_shape=jax.ShapeDtypeStruct((SUBLANES, n), F32),
        compiler_params=_cparams(("arbitrary",)),
        name="ada",
    )(cc, w, b)


A_COLS = SSD_INNER + SSD_XBC + HG_INNER


def _prep(xt, nw, sh, sc):
    return (_rms(xt, nw) * (1.0 + sc) + sh).astype(BF16)


def _inproj_a_kernel(x_ref, xp_ref, xn_ref, sh_ref, sc_ref, nw_ref, wa_ref, wdt_ref, cw_ref, cb_ref, dtb_ref,
                     z_ref, xbc_ref, g_ref, dt_ref, dtT_ref, scr, *, tm):
    i = pl.program_id(1)
    last = pl.num_programs(1) - 1
    nw, sh, sc = nw_ref[...], sh_ref[...], sc_ref[...]
    h = _prep(x_ref[...], nw, sh, sc)
    ua = jnp.dot(h, wa_ref[...], preferred_element_type=F32)
    z_ref[...] = ua[:, :SSD_INNER]
    g_ref[...] = ua[:, SSD_INNER + SSD_XBC:]
    wx = wa_ref[:, SSD_INNER:SSD_INNER + SSD_XBC]
    up = jnp.dot(_prep(xp_ref[...], nw, sh, sc), wx, preferred_element_type=F32)
    un = jnp.dot(_prep(xn_ref[...], nw, sh, sc), wx, preferred_element_type=F32)
    scr[0:SUBLANES, :] = jnp.where(i > 0, up, 0.0)
    scr[SUBLANES:SUBLANES + tm, :] = ua[:, SSD_INNER:SSD_INNER + SSD_XBC]
    scr[SUBLANES + tm:, :] = jnp.where(i < last, un, 0.0)
    acc = jnp.broadcast_to(cb_ref[...], (tm, SSD_XBC))
    pad = SSD_CONV // 2
    for k in range(SSD_CONV):
        off = SUBLANES - pad + k
        acc = acc + cw_ref[k:k + 1, :] * scr[off:off + tm, :]
    xbc_ref[...] = _silu(acc)
    draw = jnp.dot(h, wdt_ref[...], preferred_element_type=F32) + dtb_ref[...]
    dt = jnp.maximum(draw, 0.0) + jnp.log(1.0 + jnp.exp(-jnp.abs(draw)))
    dt_ref[...] = dt[:, :2 * SSD_HEADS]
    dtT_ref[...] = dt.T[:2 * SSD_HEADS, :]


def inproj_a_call(x, sh, sc, nw, wa, wdt, cw, cb, dtb, tm):
    bsz, t_len, _ = x.shape
    nt = t_len // tm
    r8 = tm // SUBLANES
    n8 = t_len // SUBLANES
    full = lambda shape: pl.BlockSpec(shape, lambda b, i: (0,) * len(shape))
    tok = lambda c: pl.BlockSpec((None, tm, c), lambda b, i: (b, i, 0))
    return pl.pallas_call(
        functools.partial(_inproj_a_kernel, tm=tm),
        grid=(bsz, nt),
        in_specs=[tok(D_MODEL),
                  pl.BlockSpec((None, SUBLANES, D_MODEL), lambda b, i: (b, jnp.maximum(i * r8 - 1, 0), 0)),
                  pl.BlockSpec((None, SUBLANES, D_MODEL), lambda b, i: (b, jnp.minimum((i + 1) * r8, n8 - 1), 0)),
                  pl.BlockSpec((None, 1, D_MODEL), lambda b, i: (b, 0, 0)),
                  pl.BlockSpec((None, 1, D_MODEL), lambda b, i: (b, 0, 0)),
                  full((1, D_MODEL)), full((D_MODEL, A_COLS)), full((D_MODEL, LANES)),
                  full((SSD_CONV, SSD_XBC)), full((1, SSD_XBC)), full((1, LANES))],
        out_specs=[tok(SSD_INNER), tok(SSD_XBC), tok(HG_INNER), tok(2 * SSD_HEADS),
                   pl.BlockSpec((None, 2 * SSD_HEADS, tm), lambda b, i: (b, 0, i))],
        out_shape=[jax.ShapeDtypeStruct((bsz, t_len, SSD_INNER), F32),
                   jax.ShapeDtypeStruct((bsz, t_len, SSD_XBC), F32),
                   jax.ShapeDtypeStruct((bsz, t_len, HG_INNER), F32),
                   jax.ShapeDtypeStruct((bsz, t_len, 2 * SSD_HEADS), F32),
                   jax.ShapeDtypeStruct((bsz, 2 * SSD_HEADS, t_len), F32)],
        scratch_shapes=[pltpu.VMEM((tm + 2 * SUBLANES, SSD_XBC), F32)],
        compiler_params=_cparams(("arbitrary", "arbitrary")),
        name="inproj_a",
    )(x, x, x, sh, sc, nw, wa, wdt, cw, cb, dtb)


B_COLS = 4 * HG_INNER


def _inproj_b_kernel(x_ref, sh_ref, sc_ref, nw_ref, w_ref, o_ref, *, ncol):
    if ncol:
        xt = jnp.concatenate([x_ref[:, w * D_MODEL:(w + 1) * D_MODEL] for w in range(ncol)], axis=0)
    else:
        xt = x_ref[...]
    h = _prep(xt, nw_ref[...], sh_ref[...], sc_ref[...])
    o_ref[...] = jnp.dot(h, w_ref[...], preferred_element_type=F32)


def inproj_b_call(x, sh, sc, nw, wb, col_major):
    bsz, t_len, _ = x.shape
    if col_major:
        ncol = 8
        tm = ncol * GRID_W
        rows = t_len // GRID_W
        assert rows == GRID_W
        xin = x.reshape(bsz, rows, GRID_W * D_MODEL)
        x_spec = pl.BlockSpec((None, rows, ncol * D_MODEL), lambda b, i: (b, 0, i))
    else:
        ncol = 0
        tm = t_len
        xin = x
        x_spec = pl.BlockSpec((None, tm, D_MODEL), lambda b, i: (b, i, 0))
    full = lambda shape: pl.BlockSpec(shape, lambda b, i: (0,) * len(shape))
    return pl.pallas_call(
        functools.partial(_inproj_b_kernel, ncol=ncol),
        grid=(bsz, t_len // tm),
        in_specs=[x_spec,
                  pl.BlockSpec((None, 1, D_MODEL), lambda b, i: (b, 0, 0)),
                  pl.BlockSpec((None, 1, D_MODEL), lambda b, i: (b, 0, 0)),
                  full((1, D_MODEL)), full((D_MODEL, B_COLS))],
        out_specs=pl.BlockSpec((None, tm, B_COLS), lambda b, i: (b, i, 0)),
        out_shape=jax.ShapeDtypeStruct((bsz, t_len, B_COLS), F32),
        compiler_params=_cparams(("arbitrary", "arbitrary")),
        name="inproj_b",
    )(xin, sh, sc, nw, wb)


N_PAIRS = SSD_HEADS // 2


def _ssd_dir(xbc, dt, dtT, s_ref, tri, triT, na_row, na_col, fwd):
    c = SSD_CHUNK
    xs = xbc[:, :SSD_INNER]
    bm = xbc[:, SSD_INNER:SSD_INNER + LANES]
    cm = xbc[:, SSD_INNER + LANES:]
    col0 = 0 if fwd else SSD_HEADS
    la = dt[:, col0:col0 + SSD_HEADS] * na_row[:, col0:col0 + SSD_HEADS]
    dtr = dtT[col0:col0 + SSD_HEADS, :]
    laT = dtr * na_col[col0:col0 + SSD_HEADS, :]
    g = sum(jnp.dot(tri, p, preferred_element_type=F32) for p in _split3(la))
    gT = sum(jnp.dot(p, triT, preferred_element_type=F32) for p in _split3(laT))
    end = c - 1 if fwd else 0
    bmT = bm.T
    ii = lax.broadcasted_iota(jnp.int32, (c, c), 0)
    jj = lax.broadcasted_iota(jnp.int32, (c, c), 1)
    causal = (jj <= ii) if fwd else (jj >= ii)
    lane = lax.broadcasted_iota(jnp.int32, (c, LANES), 1)
    lo_half = lane < SSD_HEAD_DIM
    lane_s = lax.broadcasted_iota(jnp.int32, (SSD_STATE, LANES), 1) < SSD_HEAD_DIM
    outs = []
    for grp in range(SSD_GROUPS):
        in_grp = (lane >= grp * SSD_STATE) & (lane < (grp + 1) * SSD_STATE)
        cm_g = jnp.where(in_grp, cm, 0.0).astype(BF16)
        gmat = jnp.dot(cm_g, bmT.astype(BF16), preferred_element_type=F32)
        bmT_g = bmT[grp * SSD_STATE:(grp + 1) * SSD_STATE, :]
        for pp in range(N_PAIRS // SSD_GROUPS):
            pair = grp * (N_PAIRS // SSD_GROUPS) + pp
            heads = (2 * pair, 2 * pair + 1)
            xs_p = xs[:, pair * LANES:(pair + 1) * LANES]
            xbd = jnp.concatenate([jnp.where(lo_half, xs_p, 0.0), jnp.where(lo_half, 0.0, xs_p)],
                                  axis=0).astype(BF16)
            ms, bws, ecols, arows = [], [], [], []
            for hd in heads:
                gcol = g[:, hd:hd + 1]
                grow = gT[hd:hd + 1, :]
                dec = jnp.exp(jnp.where(causal, gcol - grow, -jnp.inf))
                ms.append(gmat * dec * dtr[hd:hd + 1, :])
                glast = grow[:, end:end + 1]
                bws.append(bmT_g * (dtr[hd:hd + 1, :] * jnp.exp(glast - grow)))
                ecols.append(jnp.exp(gcol))
                arows.append(jnp.exp(glast))
            mcat = jnp.concatenate(ms, axis=1).astype(BF16)
            s_old = s_ref[pair]
            zeros = jnp.zeros_like(s_old)
            s_pad = jnp.concatenate([s_old, zeros] if grp == 0 else [zeros, s_old], axis=0).astype(BF16)
            o_inter = jnp.dot(cm.astype(BF16), s_pad, preferred_element_type=F32)
            o_inter = o_inter * jnp.where(lo_half, ecols[0], ecols[1])
            outs.append(jnp.dot(mcat, xbd, preferred_element_type=F32) + o_inter)
            bw = jnp.concatenate(bws, axis=1).astype(BF16)
            s_ref[pair] = (s_old * jnp.where(lane_s, arows[0], arows[1])
                           + jnp.dot(bw, xbd, preferred_element_type=F32))
    return outs


def _ssd_scan_kernel(xf_ref, dtf_ref, dtTf_ref, xb_ref, dtb_ref, dtTb_ref, s0f_ref, s0b_ref,
                     trif_ref, trifT_ref, trib_ref, tribT_ref, nar_ref, nac_ref, dsk_ref,
                     of_ref, ob_ref, sfo_ref, sbo_ref, sf, sb):
    n = pl.program_id(1)

    @pl.when(n == 0)
    def _():
        sf[...] = s0f_ref[...]
        sb[...] = s0b_ref[...]

    xf = xf_ref[...]
    outs = _ssd_dir(xf, dtf_ref[...], dtTf_ref[...], sf, trif_ref[...], trifT_ref[...],
                    nar_ref[...], nac_ref[...], True)
    of_ref[...] = jnp.concatenate(outs, axis=1) + dsk_ref[...] * xf[:, :SSD_INNER]
    outs = _ssd_dir(xb_ref[...], dtb_ref[...], dtTb_ref[...], sb, trib_ref[...], tribT_ref[...],
                    nar_ref[...], nac_ref[...], False)
    ob_ref[...] = jnp.concatenate(outs, axis=1)

    @pl.when(n == pl.num_programs(1) - 1)
    def _():
        sfo_ref[...] = sf[...]
        sbo_ref[...] = sb[...]


def _ssd_consts():
    c = SSD_CHUNK
    i = np.arange(c)
    trif = (i[:, None] >= i[None, :]).astype(np.float32)
    trib = (i[:, None] <= i[None, :]).astype(np.float32)
    return [jnp.asarray(a, BF16) for a in (trif, trif.T, trib, trib.T)]


def ssd_scan_call(xbc, dt, dtT, s0f, s0b, na_row, na_col, dskip):
    bsz, t_len, _ = xbc.shape
    c = SSD_CHUNK
    nc = t_len // c
    fw = lambda w: pl.BlockSpec((None, c, w), lambda b, n: (b, n, 0))
    bw = lambda w: pl.BlockSpec((None, c, w), lambda b, n: (b, nc - 1 - n, 0))
    full = lambda shape: pl.BlockSpec(shape, lambda b, n: (0,) * len(shape))
    st = pl.BlockSpec((None, N_PAIRS, SSD_STATE, LANES), lambda b, n: (b, 0, 0, 0))
    st_shape = jax.ShapeDtypeStruct((bsz, N_PAIRS, SSD_STATE, LANES), F32)
    return pl.pallas_call(
        _ssd_scan_kernel,
        grid=(bsz, nc),
        in_specs=[fw(SSD_XBC), fw(2 * SSD_HEADS),
                  pl.BlockSpec((None, 2 * SSD_HEADS, c), lambda b, n: (b, 0, n)),
                  bw(SSD_XBC), bw(2 * SSD_HEADS),
                  pl.BlockSpec((None, 2 * SSD_HEADS, c), lambda b, n: (b, 0, nc - 1 - n)),
                  st, st, full((c, c)), full((c, c)), full((c, c)), full((c, c)),
                  full((1, 2 * SSD_HEADS)), full((2 * SSD_HEADS, 1)), full((1, SSD_INNER))],
        out_specs=[fw(SSD_INNER), bw(SSD_INNER), st, st],
        out_shape=[jax.ShapeDtypeStruct((bsz, t_len, SSD_INNER), F32),
                   jax.ShapeDtypeStruct((bsz, t_len, SSD_INNER), F32), st_shape, st_shape],
        scratch_shapes=[pltpu.VMEM((N_PAIRS, SSD_STATE, LANES), F32),
                        pltpu.VMEM((N_PAIRS, SSD_STATE, LANES), F32)],
        compiler_params=_cparams(("arbitrary", "arbitrary")),
        name="ssd_scan",
    )(xbc, dt, dtT, xbc, dt, dtT, s0f, s0b, *_ssd_consts(), na_row, na_col, dskip)


def _hg_consts():
    c = HG_CHUNK
    t = np.arange(c)
    m_f = np.zeros((HG_LEVELS + 1, c, c), np.float32)
    m_f[0] = np.eye(c)
    for lv in range(1, HG_LEVELS + 1):
        m = 2 ** lv
        blk = t // m
        right = (t % m) >= m // 2
        m_f[lv] = (blk[:, None] == blk[None, :]) & right[:, None] & (~right[None, :])
    m_b = np.transpose(m_f, (0, 2, 1))
    tri_f = (t[:, None] >= t[None, :]).astype(np.float32)
    tri_b = (t[:, None] <= t[None, :]).astype(np.float32)
    return jnp.asarray(tri_f, BF16), jnp.asarray(tri_b, BF16), jnp.asarray(m_f, F32), jnp.asarray(m_b, F32)


def _hg_dir(u, lb_row, st_ref, tri_ref, mask_ref, fwd):
    c = HG_CHUNK
    outs = []
    fcol = HG_INNER if fwd else 2 * HG_INNER
    nt = (((1,), (1,)), ((), ()))
    row = lax.broadcasted_iota(jnp.int32, (c, HG_DK), 0)

    def halves(lo, hi, half):
        if half % SUBLANES == 0:
            return jnp.concatenate([(hi if (s // half) % 2 else lo)[s:s + half] for s in range(0, c, half)], axis=0)
        return jnp.where(((row // half) % 2) == 1, hi, lo)

    def shift(x, s):
        s = s % c
        if s % SUBLANES == 0:
            return jnp.concatenate([x[c - s:], x[:c - s]], axis=0)
        return pltpu.roll(x, s, 0)

    def by_side(query_side, key_side, half):
        return halves(key_side, query_side, half) if fwd else halves(query_side, key_side, half)

    for hd in range(HG_HEADS):
        sl = slice(hd * HG_DK, (hd + 1) * HG_DK)
        q = _silu(u[:, sl]) * (HG_DK ** -0.5)
        lb = lb_row[:, sl]
        f = lb + (1.0 - lb) * _sigmoid(u[:, fcol + hd * HG_DK:fcol + (hd + 1) * HG_DK])
        k = 1.0 - f
        la = jnp.log(f)
        v = u[:, 3 * HG_INNER + hd * HG_DK:3 * HG_INNER + (hd + 1) * HG_DK]
        la_hi = la.astype(BF16)
        la_lo = (la - la_hi.astype(F32)).astype(BF16)
        g2 = jnp.dot(tri_ref[...], jnp.concatenate([la_hi, la_lo], axis=1), preferred_element_type=F32)
        g = g2[:, :HG_DK] + g2[:, HG_DK:]
        scores = mask_ref[0] * lax.dot_general(q.astype(BF16), k.astype(BF16), nt, preferred_element_type=F32)
        fill = g
        for lv in range(1, HG_LEVELS + 1):
            half = 2 ** (lv - 1)
            if fwd:
                ref = halves(fill, shift(fill, half), half)
            else:
                ref = halves(shift(fill, -half), fill, half)
            decay = jnp.exp2(jnp.abs(g - ref) * (-LOG2E))
            y = (by_side(q, k, half) * decay).astype(BF16)
            scores = scores + mask_ref[lv] * lax.dot_general(y, y, nt, preferred_element_type=F32)
            if fwd:
                fill = halves(shift(fill, -half), fill, half)
            else:
                fill = halves(fill, shift(fill, half), half)
        x_cum = jnp.exp(g)
        x_rem = jnp.exp(fill - g)
        st = st_ref[hd]
        o = jnp.dot(scores.astype(BF16), v.astype(BF16), preferred_element_type=F32)
        o = o + lax.dot_general((q * x_cum).astype(BF16), st.astype(BF16), nt, preferred_element_type=F32)
        outs.append(o)
        end = c - 1 if fwd else 0
        st_ref[hd] = (st * x_cum[end:end + 1, :]
                      + jnp.dot(v.T.astype(BF16), (k * x_rem).astype(BF16), preferred_element_type=F32))
    return jnp.concatenate(outs, axis=1)


def _hg_scan_kernel(uf_ref, ub_ref, lbp_ref, s0f_ref, s0b_ref, trif_ref, trib_ref, mf_ref, mb_ref,
                    of_ref, ob_ref, sfo_ref, sbo_ref, sf, sb, *, img_rows):
    n = pl.program_id(1)

    def put(o_ref, o):
        if img_rows:
            for j in range(HG_CHUNK // img_rows):
                o_ref[:, j * HG_INNER:(j + 1) * HG_INNER] = o[j * img_rows:(j + 1) * img_rows]
        else:
            o_ref[...] = o


    @pl.when(n == 0)
    def _():
        sf[...] = s0f_ref[...]
        sb[...] = s0b_ref[...]

    p = lbp_ref[...]
    mx = jnp.max(p, axis=0, keepdims=True)
    e = jnp.exp(p - mx)
    lb = e[0:1, :] / jnp.sum(e, axis=0, keepdims=True)
    put(of_ref, _hg_dir(uf_ref[...], lb[:, :HG_INNER], sf, trif_ref, mf_ref, True))
    put(ob_ref, _hg_dir(ub_ref[...], lb[:, HG_INNER:], sb, trib_ref, mb_ref, False))

    @pl.when(n == pl.num_programs(1) - 1)
    def _():
        sfo_ref[...] = sf[...]
        sbo_ref[...] = sb[...]


def hg_scan_call(u, lbp, s0f, s0b, row_major_out):
    bsz, t_len, _ = u.shape
    c = HG_CHUNK
    nc = t_len // c
    full = lambda shape: pl.BlockSpec(shape, lambda b, n: (0,) * len(shape))
    st = pl.BlockSpec((None, HG_HEADS, HG_DK, HG_DK), lambda b, n: (b, 0, 0, 0))
    st_shape = jax.ShapeDtypeStruct((bsz, HG_HEADS, HG_DK, HG_DK), F32)
    if row_major_out:
        img_rows = t_len // GRID_W
        cols = c // img_rows
        assert cols * img_rows == c
        o_shape = jax.ShapeDtypeStruct((bsz, img_rows, GRID_W * HG_INNER), F32)
        of_spec = pl.BlockSpec((None, img_rows, cols * HG_INNER), lambda b, n: (b, 0, n))
        ob_spec = pl.BlockSpec((None, img_rows, cols * HG_INNER), lambda b, n: (b, 0, nc - 1 - n))
    else:
        img_rows = 0
        o_shape = jax.ShapeDtypeStruct((bsz, t_len, HG_INNER), F32)
        of_spec = pl.BlockSpec((None, c, HG_INNER), lambda b, n: (b, n, 0))
        ob_spec = pl.BlockSpec((None, c, HG_INNER), lambda b, n: (b, nc - 1 - n, 0))
    tri_f, tri_b, m_f, m_b = _hg_consts()
    of, ob, sfo, sbo = pl.pallas_call(
        functools.partial(_hg_scan_kernel, img_rows=img_rows),
        grid=(bsz, nc),
        in_specs=[pl.BlockSpec((None, c, B_COLS), lambda b, n: (b, n, 0)),
                  pl.BlockSpec((None, c, B_COLS), lambda b, n: (b, nc - 1 - n, 0)),
                  full((2, 2 * HG_INNER)), st, st,
                  full(tri_f.shape), full(tri_b.shape), full(m_f.shape), full(m_b.shape)],
        out_specs=[of_spec, ob_spec, st, st],
        out_shape=[o_shape, o_shape, st_shape, st_shape],
        scratch_shapes=[pltpu.VMEM((HG_HEADS, HG_DK, HG_DK), F32), pltpu.VMEM((HG_HEADS, HG_DK, HG_DK), F32)],
        compiler_params=_cparams(("arbitrary", "arbitrary")),
        name="hg_scan",
    )(u, u, lbp, s0f, s0b, tri_f, tri_b, m_f, m_b)
    if row_major_out:
        of = of.reshape(bsz, t_len, HG_INNER)
        ob = ob.reshape(bsz, t_len, HG_INNER)
    return of, ob, sfo, sbo


ROW_LINES = D_MODEL // LANES


def _to_token_tiles(ref, val, n_rows):
    for c in range(ROW_LINES):
        ref[pl.ds(c, n_rows, stride=ROW_LINES), :] = val[:, c * LANES:(c + 1) * LANES]


def _from_token_tiles(ref, n_rows, first_row=0):
    return jnp.concatenate([ref[pl.ds(first_row * ROW_LINES + c, n_rows, stride=ROW_LINES), :]
                            for c in range(ROW_LINES)], axis=1)


def _tile_of(ref, row):
    return ref.at[pl.ds(pl.multiple_of(row * ROW_LINES, ROW_LINES), ROW_LINES)]


def _post_kernel(x_ref, sof_ref, sob_ref, z_ref, hof_ref, hob_ref, hg_ref,
                 snw_ref, hnw_ref, wo_ref, pnw_ref, gm_ref, fnw_ref, shf_ref, scf_ref, rwh_ref, rwl_ref, rb_ref,
                 tri_ref, x1_ref, hx_ref, idx_ref, rank_ref, gate_ref, cnt_ref, carry, *, tm):
    first = (pl.program_id(0) == 0) & (pl.program_id(1) == 0)

    @pl.when(first)
    def _():
        carry[...] = jnp.zeros_like(carry)

    y = (sof_ref[...] + sob_ref[...]) * _silu(z_ref[...])
    y = _rms(y, snw_ref[...])
    o = hof_ref[...] + hob_ref[...]
    hnw = hnw_ref[...]
    o = jnp.concatenate([_rms(o[:, h * HG_DK:(h + 1) * HG_DK], hnw[:, h * HG_DK:(h + 1) * HG_DK])
                         for h in range(HG_HEADS)], axis=1)
    o = o * _silu(hg_ref[...])
    mix = (jnp.dot(y.astype(BF16), wo_ref[:SSD_INNER, :], preferred_element_type=F32)
           + jnp.dot(o.astype(BF16), wo_ref[SSD_INNER:, :], preferred_element_type=F32))
    x1 = x_ref[...] + gm_ref[...] * _rms(mix, pnw_ref[...])
    x1_ref[...] = x1
    hx = _rms(x1, fnw_ref[...]) * (1.0 + scf_ref[...]) + shf_ref[...]
    _to_token_tiles(hx_ref, hx, tm)
    hx_hi = hx.astype(BF16)
    hx_lo = (hx - hx_hi.astype(F32)).astype(BF16)
    rwh = rwh_ref[...]
    logits = (jnp.dot(hx_hi, rwh, preferred_element_type=F32) + jnp.dot(hx_lo, rwh, preferred_element_type=F32)
              + jnp.dot(hx_hi, rwl_ref[...], preferred_element_type=F32)) + rb_ref[...]
    work = logits.T[:N_EXPERTS, :]
    erow = lax.broadcasted_iota(jnp.int32, (N_EXPERTS, tm), 0)
    vals, idxs = [], []
    for _ in range(TOP_K):
        m = jnp.max(work, axis=0, keepdims=True)
        ix = jnp.min(jnp.where(work == m, erow, N_EXPERTS), axis=0, keepdims=True)
        vals.append(m)
        idxs.append(ix)
        work = jnp.where(erow == ix, -jnp.inf, work)
    es = [jnp.exp(v - vals[0]) for v in vals]
    den = es[0] + es[1] + es[2] + es[3]
    onehots = [(erow == ix) for ix in idxs]
    multi = sum(oh.astype(F32) for oh in onehots)
    before = jnp.dot(multi.astype(BF16), tri_ref[...], preferred_element_type=F32) + carry[...]
    carry[...] = carry[...] + jnp.sum(multi, axis=1, keepdims=True)
    sub = lax.broadcasted_iota(jnp.int32, (SUBLANES, tm), 0)
    idx_o = jnp.zeros((SUBLANES, tm), jnp.int32)
    rank_o = jnp.zeros((SUBLANES, tm), jnp.int32)
    gate_o = jnp.zeros((SUBLANES, tm), F32)
    for k in range(TOP_K):
        rk = jnp.sum(jnp.where(onehots[k], before, 0.0), axis=0, keepdims=True)
        idx_o = jnp.where(sub == k, idxs[k], idx_o)
        rank_o = jnp.where(sub == k, rk.astype(jnp.int32), rank_o)
        gate_o = jnp.where(sub == k, es[k] / den, gate_o)
    idx_ref[...] = idx_o
    rank_ref[...] = rank_o
    gate_ref[...] = gate_o
    cnt_ref[...] = jnp.broadcast_to(carry[...], (N_EXPERTS, LANES))


def post_call(x, sof, sob, z, hof, hob, hg, snw, hnw, wo, pnw, gm, fnw, shf, scf, rwh, rwl, rb):
    bsz, t_len, _ = x.shape
    tm = TOK_TILE
    nt = t_len // tm
    n_tok = bsz * t_len
    tok = lambda c: pl.BlockSpec((None, tm, c), lambda b, i: (b, i, 0))
    full = lambda shape: pl.BlockSpec(shape, lambda b, i: (0,) * len(shape))
    per_b = pl.BlockSpec((None, 1, D_MODEL), lambda b, i: (b, 0, 0))
    flat = lambda c: pl.BlockSpec((tm, c), lambda b, i: (b * nt + i, 0))
    rout = pl.BlockSpec((SUBLANES, tm), lambda b, i: (b * nt + i, 0))
    n_tiles = n_tok // tm
    ii = np.arange(tm)
    tri = jnp.asarray(ii[:, None] < ii[None, :], BF16)
    return pl.pallas_call(
        functools.partial(_post_kernel, tm=tm),
        grid=(bsz, nt),
        in_specs=[tok(D_MODEL), tok(SSD_INNER), tok(SSD_INNER), tok(SSD_INNER), tok(HG_INNER), tok(HG_INNER),
                  tok(HG_INNER), full((1, SSD_INNER)), full((1, HG_INNER)), full((D_MODEL, D_MODEL)),
                  full((1, D_MODEL)), per_b, full((1, D_MODEL)), per_b, per_b,
                  full((D_MODEL, LANES)), full((D_MODEL, LANES)), full((1, LANES)), full((tm, tm))],
        out_specs=[flat(D_MODEL), pl.BlockSpec((tm * ROW_LINES, LANES), lambda b, i: (b * nt + i, 0)),
                   rout, rout, rout, pl.BlockSpec((N_EXPERTS, LANES), lambda b, i: (0, 0))],
        out_shape=[jax.ShapeDtypeStruct((n_tok, D_MODEL), F32),
                   jax.ShapeDtypeStruct((n_tok * ROW_LINES, LANES), F32),
                   jax.ShapeDtypeStruct((n_tiles * SUBLANES, tm), jnp.int32),
                   jax.ShapeDtypeStruct((n_tiles * SUBLANES, tm), jnp.int32),
                   jax.ShapeDtypeStruct((n_tiles * SUBLANES, tm), F32),
                   jax.ShapeDtypeStruct((N_EXPERTS, LANES), F32)],
        scratch_shapes=[pltpu.VMEM((N_EXPERTS, 1), F32)],
        compiler_params=_cparams(("arbitrary", "arbitrary")),
        name="post",
    )(x, sof, sob, z, hof, hob, hg, snw, hnw, wo, pnw, gm, fnw, shf, scf, rwh, rwl, rb, tri)


W1_TCOLS = 256
IDX_WIN = 1024
LEAD_BLOCKS = 2
ZERO_RUN = 128


def _wait_rows(hbm_ref, n_rows, sem):
    n = n_rows * ROW_LINES
    pltpu.make_async_copy(hbm_ref.at[pl.ds(0, n)], hbm_ref.at[pl.ds(0, n)], sem).wait()


def _experts_kernel(be_ref, nu_ref, nxt_ref, zfrom_ref, tok_ref, pair_ref, hx_hbm, w1_hbm, b1_ref, w2_hbm, b2_ref,
                    yp_hbm, w1f, w2f, w1t, w2s, tbuf, xbuf, ybuf, xb_s, zrows, wsem, gsem, ssem, zsem):
    s = pl.program_id(0)
    nu = nu_ref[0]
    cur = s % 2
    prv = (s + 1) % 2
    woff = (s % (IDX_WIN // MOE_ROWS)) * MOE_ROWS

    def weight_copies(e):
        return (pltpu.make_async_copy(w1_hbm.at[e], w1f, wsem.at[0]),
                pltpu.make_async_copy(w2_hbm.at[e], w2f, wsem.at[1]))

    def issue_gather():
        def body(j, carry):
            pltpu.make_async_copy(_tile_of(hx_hbm, tok_ref[woff + j]), _tile_of(xbuf.at[cur], j),
                                  gsem.at[cur]).start()
            return carry

        lax.fori_loop(0, MOE_ROWS, body, 0, unroll=16)

    def issue_scatter():
        def body(j, carry):
            pltpu.make_async_copy(_tile_of(ybuf.at[cur], j), _tile_of(yp_hbm, pair_ref[woff + j]),
                                  ssem.at[cur]).start()
            return carry

        lax.fori_loop(0, MOE_ROWS, body, 0, unroll=16)

    def zero_tail(act):
        def zero_rows(first, n):
            first = pl.multiple_of(first * ROW_LINES, ROW_LINES)
            return pltpu.make_async_copy(zrows.at[pl.ds(0, n * ROW_LINES)],
                                         yp_hbm.at[pl.ds(first, n * ROW_LINES)], zsem)

        off = zfrom_ref[0]
        head = (-off) & (ZERO_RUN - 1)
        for run in [2 ** k for k in range(int(math.log2(ZERO_RUN)))]:
            @pl.when((head & run) != 0)
            def _():
                act(zero_rows(off, run))
            off = off + (head & run)

        def body(j, carry):
            act(zero_rows(j * ZERO_RUN, ZERO_RUN))
            return carry

        lax.fori_loop(off // ZERO_RUN, yp_hbm.shape[0] // (ZERO_RUN * ROW_LINES), body, 0)

    @pl.when(s == 0)
    def _():
        ybuf[...] = jnp.zeros_like(ybuf)
        zrows[...] = jnp.zeros_like(zrows)
        for cp in weight_copies(be_ref[0]):
            cp.start()
        issue_gather()
        issue_scatter()
        zero_tail(lambda cp: cp.start())

    @pl.when((s >= 1) & (s <= nu))
    def _():
        c = s - 1
        new_expert = (c == 0) | (be_ref[c] != be_ref[jnp.maximum(c - 1, 0)])

        @pl.when(new_expert)
        def _():
            for cp in weight_copies(be_ref[c]):
                cp.wait()
            half = W1_TCOLS // 2
            for cc in range(2 * D_FF // W1_TCOLS):
                for j in range(D_MODEL // LANES):
                    ks = slice(j * LANES, (j + 1) * LANES)
                    tbuf[j] = w1f[ks, cc * W1_TCOLS:(cc + 1) * W1_TCOLS].T
                    w1t[cc * half:(cc + 1) * half, ks] = tbuf[j, pl.ds(0, half, stride=2), :].astype(BF16)
                    w1t[D_FF + cc * half:D_FF + (cc + 1) * half, ks] = (
                        tbuf[j, pl.ds(1, half, stride=2), :].astype(BF16))
            w2s[...] = w2f[...].astype(BF16)

            @pl.when(nxt_ref[c] >= 0)
            def _():
                for cp in weight_copies(nxt_ref[c]):
                    cp.start()

        _wait_rows(hx_hbm, MOE_ROWS, gsem.at[prv])
        for k in range(ROW_LINES):
            xb_s[:, k * LANES:(k + 1) * LANES] = xbuf[prv, pl.ds(k, MOE_ROWS, stride=ROW_LINES), :].astype(BF16)
        issue_gather()
        issue_scatter()
        _wait_rows(hx_hbm, MOE_ROWS, ssem.at[prv])
        u = lax.dot_general(xb_s[...], w1t[...], (((1,), (1,)), ((), ())), preferred_element_type=F32) + b1_ref[...]
        glu = jnp.minimum(u[:, :D_FF], SWIGLU_LIMIT)
        lin = jnp.clip(u[:, D_FF:], -SWIGLU_LIMIT, SWIGLU_LIMIT)
        a = glu * _sigmoid(SWIGLU_ALPHA * glu) * (lin + 1.0)
        y = jnp.dot(a.astype(BF16), w2s[...], preferred_element_type=F32) + b2_ref[...]
        for k in range(ROW_LINES):
            ybuf[prv, pl.ds(k, MOE_ROWS, stride=ROW_LINES), :] = y[:, k * LANES:(k + 1) * LANES]

    @pl.when(s == nu + 1)
    def _():
        issue_scatter()
        _wait_rows(hx_hbm, MOE_ROWS, ssem.at[cur])
        _wait_rows(hx_hbm, MOE_ROWS, ssem.at[prv])
        _wait_rows(hx_hbm, MOE_ROWS, gsem.at[prv])
        zero_tail(lambda cp: cp.wait())


def experts_call(block_e, n_used, next_e, zfrom, tok_rows, pair_rows, hx, w1, b1p, w2, b2, n_out_rows):
    nb = block_e.shape[0]
    n_steps = nb + 2
    per_win = IDX_WIN // MOE_ROWS
    be_of = lambda s, be: be[jnp.clip(s - 1, 0, nb - 1)]
    grid_spec = pltpu.PrefetchScalarGridSpec(
        num_scalar_prefetch=4,
        grid=(n_steps,),
        in_specs=[pl.BlockSpec((IDX_WIN,), lambda s, be, nu, nx, zf: (s // per_win,), memory_space=pltpu.SMEM),
                  pl.BlockSpec((IDX_WIN,), lambda s, be, nu, nx, zf: (s // per_win,), memory_space=pltpu.SMEM),
                  pl.BlockSpec(memory_space=pl.ANY),
                  pl.BlockSpec(memory_space=pl.ANY),
                  pl.BlockSpec((None, 1, 2 * D_FF), lambda s, be, nu, nx, zf: (be_of(s, be), 0, 0)),
                  pl.BlockSpec(memory_space=pl.ANY),
                  pl.BlockSpec((None, 1, D_MODEL), lambda s, be, nu, nx, zf: (be_of(s, be), 0, 0))],
        out_specs=pl.BlockSpec(memory_space=pl.ANY),
        scratch_shapes=[pltpu.VMEM((D_MODEL, 2 * D_FF), F32), pltpu.VMEM((D_FF, D_MODEL), F32),
                        pltpu.VMEM((2 * D_FF, D_MODEL), BF16), pltpu.VMEM((D_FF, D_MODEL), BF16),
                        pltpu.VMEM((D_MODEL // LANES, W1_TCOLS, LANES), F32),
                        pltpu.VMEM((2, MOE_ROWS * ROW_LINES, LANES), F32),
                        pltpu.VMEM((2, MOE_ROWS * ROW_LINES, LANES), F32),
                        pltpu.VMEM((MOE_ROWS, D_MODEL), BF16),
                        pltpu.VMEM((ZERO_RUN * ROW_LINES, LANES), F32),
                        pltpu.SemaphoreType.DMA((2,)), pltpu.SemaphoreType.DMA((2,)),
                        pltpu.SemaphoreType.DMA((2,)), pltpu.SemaphoreType.DMA(())],
    )
    return pl.pallas_call(
        _experts_kernel,
        grid_spec=grid_spec,
        out_shape=jax.ShapeDtypeStruct((n_out_rows * ROW_LINES, LANES), F32),
        compiler_params=_cparams(("arbitrary",)),
        name="experts",
    )(block_e, n_used, next_e, zfrom, tok_rows, pair_rows, hx, w1, b1p, w2, b2)


def _combine_kernel(y_ref, gate_ref, x1_ref, gf_ref, nw_ref, o_ref, *, tm):
    gate = gate_ref[...].T
    fx = gate[:, 0:1] * _from_token_tiles(y_ref, tm)
    for k in range(1, TOP_K):
        fx = fx + gate[:, k:k + 1] * _from_token_tiles(y_ref, tm, first_row=k * tm)
    o_ref[...] = x1_ref[...] + gf_ref[...] * _rms(fx, nw_ref[...])


def combine_call(yp, gates, x1, gf, nw, bsz):
    n_tok = x1.shape[0]
    tm = TOK_TILE
    nt = n_tok // bsz // tm
    return pl.pallas_call(
        functools.partial(_combine_kernel, tm=tm),
        grid=(n_tok // tm,),
        in_specs=[pl.BlockSpec((TOP_K * tm * ROW_LINES, LANES), lambda i: (i, 0)),
                  pl.BlockSpec((SUBLANES, tm), lambda i: (i, 0)),
                  pl.BlockSpec((tm, D_MODEL), lambda i: (i, 0)),
                  pl.BlockSpec((None, 1, D_MODEL), lambda i: (i // nt, 0, 0)),
                  pl.BlockSpec((1, D_MODEL), lambda i: (0, 0))],
        out_specs=pl.BlockSpec((tm, D_MODEL), lambda i: (i, 0)),
        out_shape=jax.ShapeDtypeStruct((n_tok, D_MODEL), F32),
        compiler_params=_cparams(("arbitrary",)),
        name="combine",
    )(yp, gates, x1, gf, nw)


def kernel(x, c, ctx, c_ctx, ada_w, ada_b, mix_pre_norm, mix_post_norm, w_in, w_out, ssd_conv_w, ssd_conv_b,
           ssd_dt_bias, ssd_a_log, ssd_d, ssd_norm, hg_lb, hg_norm, ffn_pre_norm, ffn_post_norm, router_w,
           router_b, moe_w1, moe_b1, moe_w2, moe_b2):
    bsz, t_len, d = x.shape
    assert ada_w.shape[0] == 1 and d == D_MODEL and bsz <= SUBLANES - 1
    n_tok = bsz * t_len

    cc = jnp.zeros((SUBLANES, d), F32).at[:bsz].set(c).at[bsz].set(c_ctx)
    mod = ada_call(cc, ada_w[0], ada_b[0][None, :])
    sh_m, sc_m, g_m, sh_f, sc_f, g_f = [m[:bsz, None, :] for m in jnp.split(mod, 6, axis=-1)]
    csh_m, csc_m = [jnp.broadcast_to(m[bsz][None, None, :], (bsz, 1, d)) for m in jnp.split(mod, 6, axis=-1)[:2]]

    w = w_in[0]
    wa = jnp.concatenate([w[:, :SSD_INNER + SSD_XBC], w[:, SSD_COLS + 4 * HG_INNER:]], axis=1).astype(BF16)
    wdt = jnp.zeros((d, LANES), F32).at[:, :2 * SSD_HEADS].set(w[:, SSD_INNER + SSD_XBC:SSD_COLS]).astype(BF16)
    wb = w[:, SSD_COLS:SSD_COLS + 4 * HG_INNER].astype(BF16)
    dtb = jnp.zeros((1, LANES), F32).at[0, :2 * SSD_HEADS].set(ssd_dt_bias[0].reshape(-1))
    nw = mix_pre_norm[0][None, :]
    cw, cb = ssd_conv_w[0], ssd_conv_b[0][None, :]
    neg_a = -jnp.exp(ssd_a_log[0].astype(F32)).reshape(1, 2 * SSD_HEADS)
    dskip = jnp.repeat(ssd_d[0], SSD_HEAD_DIM)[None, :]
    lbp = hg_lb.astype(F32).reshape(2, 2 * HG_INNER)

    _, cxbc, _, cdt, cdtT = inproj_a_call(ctx, csh_m, csc_m, nw, wa, wdt, cw, cb, dtb, tm=ctx.shape[1])
    cu = inproj_b_call(ctx, csh_m, csc_m, nw, wb, col_major=False)
    z_ssd = jnp.zeros((bsz, N_PAIRS, SSD_STATE, LANES), F32)
    z_hg = jnp.zeros((bsz, HG_HEADS, HG_DK, HG_DK), F32)
    _, _, ssf, ssb = ssd_scan_call(cxbc, cdt, cdtT, z_ssd, z_ssd, neg_a, neg_a.reshape(-1, 1), dskip)
    _, _, hsf, hsb = hg_scan_call(cu, lbp, z_hg, z_hg, row_major_out=False)

    zg, xbc, hgate, dt, dtT = inproj_a_call(x, sh_m, sc_m, nw, wa, wdt, cw, cb, dtb, tm=512)
    ub = inproj_b_call(x, sh_m, sc_m, nw, wb, col_major=True)
    sof, sob, _, _ = ssd_scan_call(xbc, dt, dtT, ssf, ssb, neg_a, neg_a.reshape(-1, 1), dskip)
    hof, hob, _, _ = hg_scan_call(ub, lbp, hsf, hsb, row_major_out=True)

    rw = jnp.zeros((d, LANES), F32).at[:, :N_EXPERTS].set(router_w[0])
    rwh = rw.astype(BF16)
    rwl = (rw - rwh.astype(F32)).astype(BF16)
    rb = jnp.zeros((1, LANES), F32).at[0, :N_EXPERTS].set(router_b[0])
    x1, hx, idx, rank, gates, cnt = post_call(
        x, sof, sob, zg, hof, hob, hgate, ssd_norm[0][None, :], hg_norm[0][None, :], w_out[0].astype(BF16),
        mix_post_norm[0][None, :], g_m, ffn_pre_norm[0][None, :], sh_f, sc_f, rwh, rwl, rb)

    r = MOE_ROWS
    n_blocks = (n_tok * TOP_K + N_EXPERTS * (r - 1) + r - 1) // r
    n_tiles = n_tok // TOK_TILE
    counts = cnt[:, 0].astype(jnp.int32)
    padded = (counts + r - 1) // r * r
    pend = jnp.cumsum(padded)
    pstart = pend - padded
    idx3 = idx.reshape(n_tiles, SUBLANES, TOK_TILE)[:, :TOP_K, :]
    rank3 = rank.reshape(n_tiles, SUBLANES, TOK_TILE)[:, :TOP_K, :]
    dest = rank3
    for e in range(N_EXPERTS):
        dest = dest + jnp.where(idx3 == e, pstart[e], 0)
    dest = dest.reshape(-1)
    starts = jnp.arange(n_blocks, dtype=jnp.int32) * r
    block_e = jnp.minimum(jnp.sum((pend[None, :] <= starts[:, None]).astype(jnp.int32), axis=1), N_EXPERTS - 1)
    n_used = (pend[-1:] // r).astype(jnp.int32)
    run_end = pend[block_e] // r
    next_e = jnp.where(run_end < n_used[0], block_e[jnp.minimum(run_end, n_blocks - 1)], -1).astype(jnp.int32)

    n_pairs = n_tok * TOP_K
    order = jnp.argsort(dest).astype(jnp.int32)
    order_pad = jnp.concatenate([order, jnp.zeros((r,), jnp.int32)])
    n_ext = n_blocks + 2
    blk = jnp.arange(n_ext, dtype=jnp.int32)
    e_of = block_e[jnp.minimum(blk, n_blocks - 1)]
    live = blk < n_used[0]
    local0 = blk * r - pstart[e_of]
    first = jnp.where(live, (jnp.cumsum(counts) - counts)[e_of] + local0, 0)
    n_valid = jnp.where(live, jnp.clip(counts[e_of] - local0, 0, r), 0)
    pairs = jax.vmap(lambda c0: lax.dynamic_slice(order_pad, (c0,), (r,)))(first)
    per_tile = TOP_K * TOK_TILE
    tok_rows = (pairs // per_tile) * TOK_TILE + pairs % TOK_TILE
    j = jnp.arange(r, dtype=jnp.int32)[None, :]
    n_pad = r - n_valid
    spare0 = n_pairs + LEAD_BLOCKS * r
    pad_base = (jnp.cumsum(n_pad) - n_pad)[:, None]
    pair_rows = jnp.where(j < n_valid[:, None], pairs, spare0 + pad_base + j - n_valid[:, None])
    lead = n_pairs + jnp.arange(LEAD_BLOCKS * r, dtype=jnp.int32)
    pair_rows = jnp.concatenate([lead, pair_rows.reshape(-1)])
    zfrom = spare0 + jnp.sum(jnp.where(live, n_pad, 0), keepdims=True)
    n_out_rows = n_pairs + LEAD_BLOCKS * r + N_EXPERTS * r
    win_pad = lambda a: jnp.pad(a, (0, -a.shape[0] % IDX_WIN))

    b1p = jnp.concatenate([moe_b1[0][:, 0::2], moe_b1[0][:, 1::2]], axis=-1)[:, None, :]
    yp = experts_call(block_e, n_used, next_e, zfrom.astype(jnp.int32), win_pad(tok_rows.reshape(-1)),
                      win_pad(pair_rows), hx, moe_w1[0], b1p, moe_w2[0], moe_b2[0][:, None, :], n_out_rows)
    out = combine_call(yp, gates, x1, g_f, ffn_post_norm[0][None, :], bsz)
    return out.reshape(bsz, t_len, d)
```

```python
import functools
import math

import numpy as np
import jax
import jax.numpy as jnp
from jax import lax
from jax.experimental import pallas as pl
from jax.experimental.pallas import tpu as pltpu

F32 = jnp.float32
BF16 = jnp.bfloat16
HIGHEST = lax.Precision.HIGHEST

D_MODEL = 1024
GRID_W = 64
SSD_HEADS = 8
SSD_HEAD_DIM = 64
SSD_INNER = 512
SSD_STATE = 64
SSD_GROUPS = 2
SSD_CONV = 5
SSD_XBC = 768
SSD_COLS = 1296
HG_HEADS = 4
HG_DK = 128
HG_INNER = 512
N_EXPERTS = 32
TOP_K = 4
D_FF = 1024
SWIGLU_ALPHA = 1.702
SWIGLU_LIMIT = 7.0
EPS = 1e-6

LANES = 128
SUBLANES = 8
VMEM_LIMIT = 56 * 1024 * 1024

SSD_CHUNK = 256
HG_CHUNK = 128
HG_LEVELS = 7
MOE_ROWS = 256
TOK_TILE = 256
LOG2E = 1.4426950408889634


def _cparams(sem):
    return pltpu.CompilerParams(dimension_semantics=sem, vmem_limit_bytes=VMEM_LIMIT)


def _sigmoid(x):
    return 1.0 / (1.0 + jnp.exp(-x))


def _silu(x):
    return x * _sigmoid(x)


def _rms(x, w):
    return x * lax.rsqrt(jnp.mean(x * x, axis=-1, keepdims=True) + EPS) * w


def _split3(v):
    hi = v.astype(BF16)
    r1 = v - hi.astype(F32)
    mid = r1.astype(BF16)
    lo = (r1 - mid.astype(F32)).astype(BF16)
    return hi, mid, lo


def _ada_kernel(c_ref, w_ref, b_ref, o_ref):
    s = _silu(c_ref[...])
    o_ref[...] = jnp.dot(s, w_ref[...], precision=HIGHEST, preferred_element_type=F32) + b_ref[...]


def ada_call(cc, w, b):
    n = w.shape[1]
    tn = 1536
    return pl.pallas_call(
        _ada_kernel,
        grid=(n // tn,),
        in_specs=[pl.BlockSpec((SUBLANES, D_MODEL), lambda j: (0, 0)),
                  pl.BlockSpec((D_MODEL, tn), lambda j: (0, j)),
                  pl.BlockSpec((1, tn), lambda j: (0, j))],
        out_specs=pl.BlockSpec((SUBLANES, tn), lambda j: (0, j)),
        out_shape=jax.ShapeDtypeStruct((SUBLANES, n), F32),
        compiler_params=_cparams(("arbitrary",)),
        name="ada",
    )(cc, w, b)


A_COLS = SSD_INNER + SSD_XBC + HG_INNER


def _prep(xt, nw, sh, sc):
    return (_rms(xt, nw) * (1.0 + sc) + sh).astype(BF16)


def _inproj_a_kernel(x_ref, xp_ref, xn_ref, sh_ref, sc_ref, nw_ref, wa_ref, wdt_ref, cw_ref, cb_ref, dtb_ref,
                     z_ref, xbc_ref, g_ref, dt_ref, dtT_ref, scr, *, tm):
    i = pl.program_id(1)
    last = pl.num_programs(1) - 1
    nw, sh, sc = nw_ref[...], sh_ref[...], sc_ref[...]
    h = _prep(x_ref[...], nw, sh, sc)
    ua = jnp.dot(h, wa_ref[...], preferred_element_type=F32)
    z_ref[...] = ua[:, :SSD_INNER]
    g_ref[...] = ua[:, SSD_INNER + SSD_XBC:]
    wx = wa_ref[:, SSD_INNER:SSD_INNER + SSD_XBC]
    up = jnp.dot(_prep(xp_ref[...], nw, sh, sc), wx, preferred_element_type=F32)
    un = jnp.dot(_prep(xn_ref[...], nw, sh, sc), wx, preferred_element_type=F32)
    scr[0:SUBLANES, :] = jnp.where(i > 0, up, 0.0)
    scr[SUBLANES:SUBLANES + tm, :] = ua[:, SSD_INNER:SSD_INNER + SSD_XBC]
    scr[SUBLANES + tm:, :] = jnp.where(i < last, un, 0.0)
    acc = jnp.broadcast_to(cb_ref[...], (tm, SSD_XBC))
    pad = SSD_CONV // 2
    for k in range(SSD_CONV):
        off = SUBLANES - pad + k
        acc = acc + cw_ref[k:k + 1, :] * scr[off:off + tm, :]
    xbc_ref[...] = _silu(acc)
    draw = jnp.dot(h, wdt_ref[...], preferred_element_type=F32) + dtb_ref[...]
    dt = jnp.maximum(draw, 0.0) + jnp.log(1.0 + jnp.exp(-jnp.abs(draw)))
    dt_ref[...] = dt[:, :2 * SSD_HEADS]
    dtT_ref[...] = dt.T[:2 * SSD_HEADS, :]


def inproj_a_call(x, sh, sc, nw, wa, wdt, cw, cb, dtb, tm):
    bsz, t_len, _ = x.shape
    nt = t_len // tm
    r8 = tm // SUBLANES
    n8 = t_len // SUBLANES
    full = lambda shape: pl.BlockSpec(shape, lambda b, i: (0,) * len(shape))
    tok = lambda c: pl.BlockSpec((None, tm, c), lambda b, i: (b, i, 0))
    return pl.pallas_call(
        functools.partial(_inproj_a_kernel, tm=tm),
        grid=(bsz, nt),
        in_specs=[tok(D_MODEL),
                  pl.BlockSpec((None, SUBLANES, D_MODEL), lambda b, i: (b, jnp.maximum(i * r8 - 1, 0), 0)),
                  pl.BlockSpec((None, SUBLANES, D_MODEL), lambda b, i: (b, jnp.minimum((i + 1) * r8, n8 - 1), 0)),
                  pl.BlockSpec((None, 1, D_MODEL), lambda b, i: (b, 0, 0)),
                  pl.BlockSpec((None, 1, D_MODEL), lambda b, i: (b, 0, 0)),
                  full((1, D_MODEL)), full((D_MODEL, A_COLS)), full((D_MODEL, LANES)),
                  full((SSD_CONV, SSD_XBC)), full((1, SSD_XBC)), full((1, LANES))],
        out_specs=[tok(SSD_INNER), tok(SSD_XBC), tok(HG_INNER), tok(2 * SSD_HEADS),
                   pl.BlockSpec((None, 2 * SSD_HEADS, tm), lambda b, i: (b, 0, i))],
        out_shape=[jax.ShapeDtypeStruct((bsz, t_len, SSD_INNER), F32),
                   jax.ShapeDtypeStruct((bsz, t_len, SSD_XBC), F32),
                   jax.ShapeDtypeStruct((bsz, t_len, HG_INNER), F32),
                   jax.ShapeDtypeStruct((bsz, t_len, 2 * SSD_HEADS), F32),
                   jax.ShapeDtypeStruct((bsz, 2 * SSD_HEADS, t_len), F32)],
        scratch_shapes=[pltpu.VMEM((tm + 2 * SUBLANES, SSD_XBC), F32)],
        compiler_params=_cparams(("arbitrary", "arbitrary")),
        name="inproj_a",
    )(x, x, x, sh, sc, nw, wa, wdt, cw, cb, dtb)


B_COLS = 4 * HG_INNER


def _inproj_b_kernel(x_ref, sh_ref, sc_ref, nw_ref, w_ref, o_ref, *, ncol):
    if ncol:
        xt = jnp.concatenate([x_ref[:, w * D_MODEL:(w + 1) * D_MODEL] for w in range(ncol)], axis=0)
    else:
        xt = x_ref[...]
    h = _prep(xt, nw_ref[...], sh_ref[...], sc_ref[...])
    o_ref[...] = jnp.dot(h, w_ref[...], preferred_element_type=F32)


def inproj_b_call(x, sh, sc, nw, wb, col_major):
    bsz, t_len, _ = x.shape
    if col_major:
        ncol = 8
        tm = ncol * GRID_W
        rows = t_len // GRID_W
        assert rows == GRID_W
        xin = x.reshape(bsz, rows, GRID_W * D_MODEL)
        x_spec = pl.BlockSpec((None, rows, ncol * D_MODEL), lambda b, i: (b, 0, i))
    else:
        ncol = 0
        tm = t_len
        xin = x
        x_spec = pl.BlockSpec((None, tm, D_MODEL), lambda b, i: (b, i, 0))
    full = lambda shape: pl.BlockSpec(shape, lambda b, i: (0,) * len(shape))
    return pl.pallas_call(
        functools.partial(_inproj_b_kernel, ncol=ncol),
        grid=(bsz, t_len // tm),
        in_specs=[x_spec,
                  pl.BlockSpec((None, 1, D_MODEL), lambda b, i: (b, 0, 0)),
                  pl.BlockSpec((None, 1, D_MODEL), lambda b, i: (b, 0, 0)),
                  full((1, D_MODEL)), full((D_MODEL, B_COLS))],
        out_specs=pl.BlockSpec((None, tm, B_COLS), lambda b, i: (b, i, 0)),
        out_shape=jax.ShapeDtypeStruct((bsz, t_len, B_COLS), F32),
        compiler_params=_cparams(("arbitrary", "arbitrary")),
        name="inproj_b",
    )(xin, sh, sc, nw, wb)


N_PAIRS = SSD_HEADS // 2


def _ssd_dir(xbc, dt, dtT, s_ref, tri, triT, na_row, na_col, fwd):
    c = SSD_CHUNK
    xs = xbc[:, :SSD_INNER]
    bm = xbc[:, SSD_INNER:SSD_INNER + LANES]
    cm = xbc[:, SSD_INNER + LANES:]
    col0 = 0 if fwd else SSD_HEADS
    la = dt[:, col0:col0 + SSD_HEADS] * na_row[:, col0:col0 + SSD_HEADS]
    dtr = dtT[col0:col0 + SSD_HEADS, :]
    laT = dtr * na_col[col0:col0 + SSD_HEADS, :]
    g = sum(jnp.dot(tri, p, preferred_element_type=F32) for p in _split3(la))
    gT = sum(jnp.dot(p, triT, preferred_element_type=F32) for p in _split3(laT))
    end = c - 1 if fwd else 0
    bmT = bm.T
    ii = lax.broadcasted_iota(jnp.int32, (c, c), 0)
    jj = lax.broadcasted_iota(jnp.int32, (c, c), 1)
    causal = (jj <= ii) if fwd else (jj >= ii)
    lane = lax.broadcasted_iota(jnp.int32, (c, LANES), 1)
    lo_half = lane < SSD_HEAD_DIM
    lane_s = lax.broadcasted_iota(jnp.int32, (SSD_STATE, LANES), 1) < SSD_HEAD_DIM
    outs = []
    for grp in range(SSD_GROUPS):
        in_grp = (lane >= grp * SSD_STATE) & (lane < (grp + 1) * SSD_STATE)
        cm_g = jnp.where(in_grp, cm, 0.0).astype(BF16)
        gmat = jnp.dot(cm_g, bmT.astype(BF16), preferred_element_type=F32)
        bmT_g = bmT[grp * SSD_STATE:(grp + 1) * SSD_STATE, :]
        for pp in range(N_PAIRS // SSD_GROUPS):
            pair = grp * (N_PAIRS // SSD_GROUPS) + pp
            heads = (2 * pair, 2 * pair + 1)
            xs_p = xs[:, pair * LANES:(pair + 1) * LANES]
            xbd = jnp.concatenate([jnp.where(lo_half, xs_p, 0.0), jnp.where(lo_half, 0.0, xs_p)],
                                  axis=0).astype(BF16)
            ms, bws, ecols, arows = [], [], [], []
            for hd in heads:
                gcol = g[:, hd:hd + 1]
                grow = gT[hd:hd + 1, :]
                dec = jnp.exp(jnp.where(causal, gcol - grow, -jnp.inf))
                ms.append(gmat * dec * dtr[hd:hd + 1, :])
                glast = grow[:, end:end + 1]
                bws.append(bmT_g * (dtr[hd:hd + 1, :] * jnp.exp(glast - grow)))
                ecols.append(jnp.exp(gcol))
                arows.append(jnp.exp(glast))
            mcat = jnp.concatenate(ms, axis=1).astype(BF16)
            s_old = s_ref[pair]
            zeros = jnp.zeros_like(s_old)
            s_pad = jnp.concatenate([s_old, zeros] if grp == 0 else [zeros, s_old], axis=0).astype(BF16)
            o_inter = jnp.dot(cm.astype(BF16), s_pad, preferred_element_type=F32)
            o_inter = o_inter * jnp.where(lo_half, ecols[0], ecols[1])
            outs.append(jnp.dot(mcat, xbd, preferred_element_type=F32) + o_inter)
            bw = jnp.concatenate(bws, axis=1).astype(BF16)
            s_ref[pair] = (s_old * jnp.where(lane_s, arows[0], arows[1])
                           + jnp.dot(bw, xbd, preferred_element_type=F32))
    return outs


def _ssd_scan_kernel(xf_ref, dtf_ref, dtTf_ref, xb_ref, dtb_ref, dtTb_ref, s0f_ref, s0b_ref,
                     trif_ref, trifT_ref, trib_ref, tribT_ref, nar_ref, nac_ref, dsk_ref,
                     of_ref, ob_ref, sfo_ref, sbo_ref, sf, sb):
    n = pl.program_id(1)

    @pl.when(n == 0)
    def _():
        sf[...] = s0f_ref[...]
        sb[...] = s0b_ref[...]

    xf = xf_ref[...]
    outs = _ssd_dir(xf, dtf_ref[...], dtTf_ref[...], sf, trif_ref[...], trifT_ref[...],
                    nar_ref[...], nac_ref[...], True)
    of_ref[...] = jnp.concatenate(outs, axis=1) + dsk_ref[...] * xf[:, :SSD_INNER]
    outs = _ssd_dir(xb_ref[...], dtb_ref[...], dtTb_ref[...], sb, trib_ref[...], tribT_ref[...],
                    nar_ref[...], nac_ref[...], False)
    ob_ref[...] = jnp.concatenate(outs, axis=1)

    @pl.when(n == pl.num_programs(1) - 1)
    def _():
        sfo_ref[...] = sf[...]
        sbo_ref[...] = sb[...]


def _ssd_consts():
    c = SSD_CHUNK
    i = np.arange(c)
    trif = (i[:, None] >= i[None, :]).astype(np.float32)
    trib = (i[:, None] <= i[None, :]).astype(np.float32)
    return [jnp.asarray(a, BF16) for a in (trif, trif.T, trib, trib.T)]


def ssd_scan_call(xbc, dt, dtT, s0f, s0b, na_row, na_col, dskip):
    bsz, t_len, _ = xbc.shape
    c = SSD_CHUNK
    nc = t_len // c
    fw = lambda w: pl.BlockSpec((None, c, w), lambda b, n: (b, n, 0))
    bw = lambda w: pl.BlockSpec((None, c, w), lambda b, n: (b, nc - 1 - n, 0))
    full = lambda shape: pl.BlockSpec(shape, lambda b, n: (0,) * len(shape))
    st = pl.BlockSpec((None, N_PAIRS, SSD_STATE, LANES), lambda b, n: (b, 0, 0, 0))
    st_shape = jax.ShapeDtypeStruct((bsz, N_PAIRS, SSD_STATE, LANES), F32)
    return pl.pallas_call(
        _ssd_scan_kernel,
        grid=(bsz, nc),
        in_specs=[fw(SSD_XBC), fw(2 * SSD_HEADS),
                  pl.BlockSpec((None, 2 * SSD_HEADS, c), lambda b, n: (b, 0, n)),
                  bw(SSD_XBC), bw(2 * SSD_HEADS),
                  pl.BlockSpec((None, 2 * SSD_HEADS, c), lambda b, n: (b, 0, nc - 1 - n)),
                  st, st, full((c, c)), full((c, c)), full((c, c)), full((c, c)),
                  full((1, 2 * SSD_HEADS)), full((2 * SSD_HEADS, 1)), full((1, SSD_INNER))],
        out_specs=[fw(SSD_INNER), bw(SSD_INNER), st, st],
        out_shape=[jax.ShapeDtypeStruct((bsz, t_len, SSD_INNER), F32),
                   jax.ShapeDtypeStruct((bsz, t_len, SSD_INNER), F32), st_shape, st_shape],
        scratch_shapes=[pltpu.VMEM((N_PAIRS, SSD_STATE, LANES), F32),
                        pltpu.VMEM((N_PAIRS, SSD_STATE, LANES), F32)],
        compiler_params=_cparams(("arbitrary", "arbitrary")),
        name="ssd_scan",
    )(xbc, dt, dtT, xbc, dt, dtT, s0f, s0b, *_ssd_consts(), na_row, na_col, dskip)


def _hg_consts():
    c = HG_CHUNK
    t = np.arange(c)
    m_f = np.zeros((HG_LEVELS + 1, c, c), np.float32)
    m_f[0] = np.eye(c)
    for lv in range(1, HG_LEVELS + 1):
        m = 2 ** lv
        blk = t // m
        right = (t % m) >= m // 2
        m_f[lv] = (blk[:, None] == blk[None, :]) & right[:, None] & (~right[None, :])
    m_b = np.transpose(m_f, (0, 2, 1))
    tri_f = (t[:, None] >= t[None, :]).astype(np.float32)
    tri_b = (t[:, None] <= t[None, :]).astype(np.float32)
    return jnp.asarray(tri_f, BF16), jnp.asarray(tri_b, BF16), jnp.asarray(m_f, F32), jnp.asarray(m_b, F32)


def _hg_dir(u, lb_row, st_ref, tri_ref, mask_ref, fwd):
    c = HG_CHUNK
    outs = []
    fcol = HG_INNER if fwd else 2 * HG_INNER
    nt = (((1,), (1,)), ((), ()))
    row = lax.broadcasted_iota(jnp.int32, (c, HG_DK), 0)

    def halves(lo, hi, half):
        if half % SUBLANES == 0:
            return jnp.concatenate([(hi if (s // half) % 2 else lo)[s:s + half] for s in range(0, c, half)], axis=0)
        return jnp.where(((row // half) % 2) == 1, hi, lo)

    def shift(x, s):
        s = s % c
        if s % SUBLANES == 0:
            return jnp.concatenate([x[c - s:], x[:c - s]], axis=0)
        return pltpu.roll(x, s, 0)

    def by_side(query_side, key_side, half):
        return halves(key_side, query_side, half) if fwd else halves(query_side, key_side, half)

    for hd in range(HG_HEADS):
        sl = slice(hd * HG_DK, (hd + 1) * HG_DK)
        q = _silu(u[:, sl]) * (HG_DK ** -0.5)
        lb = lb_row[:, sl]
        f = lb + (1.0 - lb) * _sigmoid(u[:, fcol + hd * HG_DK:fcol + (hd + 1) * HG_DK])
        k = 1.0 - f
        la = jnp.log(f)
        v = u[:, 3 * HG_INNER + hd * HG_DK:3 * HG_INNER + (hd + 1) * HG_DK]
        la_hi = la.astype(BF16)
        la_lo = (la - la_hi.astype(F32)).astype(BF16)
        g2 = jnp.dot(tri_ref[...], jnp.concatenate([la_hi, la_lo], axis=1), preferred_element_type=F32)
        g = g2[:, :HG_DK] + g2[:, HG_DK:]
        scores = mask_ref[0] * lax.dot_general(q.astype(BF16), k.astype(BF16), nt, preferred_element_type=F32)
        fill = g
        for lv in range(1, HG_LEVELS + 1):
            half = 2 ** (lv - 1)
            if fwd:
                ref = halves(fill, shift(fill, half), half)
            else:
                ref = halves(shift(fill, -half), fill, half)
            decay = jnp.exp2(jnp.abs(g - ref) * (-LOG2E))
            y = (by_side(q, k, half) * decay).astype(BF16)
            scores = scores + mask_ref[lv] * lax.dot_general(y, y, nt, preferred_element_type=F32)
            if fwd:
                fill = halves(shift(fill, -half), fill, half)
            else:
                fill = halves(fill, shift(fill, half), half)
        x_cum = jnp.exp(g)
        x_rem = jnp.exp(fill - g)
        st = st_ref[hd]
        o = jnp.dot(scores.astype(BF16), v.astype(BF16), preferred_element_type=F32)
        o = o + lax.dot_general((q * x_cum).astype(BF16), st.astype(BF16), nt, preferred_element_type=F32)
        outs.append(o)
        end = c - 1 if fwd else 0
        st_ref[hd] = (st * x_cum[end:end + 1, :]
                      + jnp.dot(v.T.astype(BF16), (k * x_rem).astype(BF16), preferred_element_type=F32))
    return jnp.concatenate(outs, axis=1)


def _hg_scan_kernel(uf_ref, ub_ref, lbp_ref, s0f_ref, s0b_ref, trif_ref, trib_ref, mf_ref, mb_ref,
                    of_ref, ob_ref, sfo_ref, sbo_ref, sf, sb, *, img_rows):
    n = pl.program_id(1)

    def put(o_ref, o):
        if img_rows:
            for j in range(HG_CHUNK // img_rows):
                o_ref[:, j * HG_INNER:(j + 1) * HG_INNER] = o[j * img_rows:(j + 1) * img_rows]
        else:
            o_ref[...] = o

    @pl.when(n == 0)
    def _():
        sf[...] = s0f_ref[...]
        sb[...] = s0b_ref[...]

    p = lbp_ref[...]
    mx = jnp.max(p, axis=0, keepdims=True)
    e = jnp.exp(p - mx)
    lb = e[0:1, :] / jnp.sum(e, axis=0, keepdims=True)
    put(of_ref, _hg_dir(uf_ref[...], lb[:, :HG_INNER], sf, trif_ref, mf_ref, True))
    put(ob_ref, _hg_dir(ub_ref[...], lb[:, HG_INNER:], sb, trib_ref, mb_ref, False))

    @pl.when(n == pl.num_programs(1) - 1)
    def _():
        sfo_ref[...] = sf[...]
        sbo_ref[...] = sb[...]


def hg_scan_call(u, lbp, s0f, s0b, row_major_out):
    bsz, t_len, _ = u.shape
    c = HG_CHUNK
    nc = t_len // c
    full = lambda shape: pl.BlockSpec(shape, lambda b, n: (0,) * len(shape))
    st = pl.BlockSpec((None, HG_HEADS, HG_DK, HG_DK), lambda b, n: (b, 0, 0, 0))
    st_shape = jax.ShapeDtypeStruct((bsz, HG_HEADS, HG_DK, HG_DK), F32)
    if row_major_out:
        img_rows = t_len // GRID_W
        cols = c // img_rows
        assert cols * img_rows == c
        o_shape = jax.ShapeDtypeStruct((bsz, img_rows, GRID_W * HG_INNER), F32)
        of_spec = pl.BlockSpec((None, img_rows, cols * HG_INNER), lambda b, n: (b, 0, n))
        ob_spec = pl.BlockSpec((None, img_rows, cols * HG_INNER), lambda b, n: (b, 0, nc - 1 - n))
    else:
        img_rows = 0
        o_shape = jax.ShapeDtypeStruct((bsz, t_len, HG_INNER), F32)
        of_spec = pl.BlockSpec((None, c, HG_INNER), lambda b, n: (b, n, 0))
        ob_spec = pl.BlockSpec((None, c, HG_INNER), lambda b, n: (b, nc - 1 - n, 0))
    tri_f, tri_b, m_f, m_b = _hg_consts()
    of, ob, sfo, sbo = pl.pallas_call(
        functools.partial(_hg_scan_kernel, img_rows=img_rows),
        grid=(bsz, nc),
        in_specs=[pl.BlockSpec((None, c, B_COLS), lambda b, n: (b, n, 0)),
                  pl.BlockSpec((None, c, B_COLS), lambda b, n: (b, nc - 1 - n, 0)),
                  full((2, 2 * HG_INNER)), st, st,
                  full(tri_f.shape), full(tri_b.shape), full(m_f.shape), full(m_b.shape)],
        out_specs=[of_spec, ob_spec, st, st],
        out_shape=[o_shape, o_shape, st_shape, st_shape],
        scratch_shapes=[pltpu.VMEM((HG_HEADS, HG_DK, HG_DK), F32), pltpu.VMEM((HG_HEADS, HG_DK, HG_DK), F32)],
        compiler_params=_cparams(("arbitrary", "arbitrary")),
        name="hg_scan",
    )(u, u, lbp, s0f, s0b, tri_f, tri_b, m_f, m_b)
    if row_major_out:
        of = of.reshape(bsz, t_len, HG_INNER)
        ob = ob.reshape(bsz, t_len, HG_INNER)
    return of, ob, sfo, sbo


ROW_LINES = D_MODEL // LANES


def _to_token_tiles(ref, val, n_rows):
    for c in range(ROW_LINES):
        ref[pl.ds(c, n_rows, stride=ROW_LINES), :] = val[:, c * LANES:(c + 1) * LANES]


def _from_token_tiles(ref, n_rows, first_row=0):
    return jnp.concatenate([ref[pl.ds(first_row * ROW_LINES + c, n_rows, stride=ROW_LINES), :]
                            for c in range(ROW_LINES)], axis=1)


def _tile_of(ref, row):
    return ref.at[pl.ds(pl.multiple_of(row * ROW_LINES, ROW_LINES), ROW_LINES)]


def _post_kernel(x_ref, sof_ref, sob_ref, z_ref, hof_ref, hob_ref, hg_ref,
                 snw_ref, hnw_ref, wo_ref, pnw_ref, gm_ref, fnw_ref, shf_ref, scf_ref, rwh_ref, rwl_ref, rb_ref,
                 tri_ref, x1_ref, hx_ref, idx_ref, rank_ref, gate_ref, cnt_ref, carry, *, tm):
    first = (pl.program_id(0) == 0) & (pl.program_id(1) == 0)

    @pl.when(first)
    def _():
        carry[...] = jnp.zeros_like(carry)

    y = (sof_ref[...] + sob_ref[...]) * _silu(z_ref[...])
    y = _rms(y, snw_ref[...])
    o = hof_ref[...] + hob_ref[...]
    hnw = hnw_ref[...]
    o = jnp.concatenate([_rms(o[:, h * HG_DK:(h + 1) * HG_DK], hnw[:, h * HG_DK:(h + 1) * HG_DK])
                         for h in range(HG_HEADS)], axis=1)
    o = o * _silu(hg_ref[...])
    mix = (jnp.dot(y.astype(BF16), wo_ref[:SSD_INNER, :], preferred_element_type=F32)
           + jnp.dot(o.astype(BF16), wo_ref[SSD_INNER:, :], preferred_element_type=F32))
    x1 = x_ref[...] + gm_ref[...] * _rms(mix, pnw_ref[...])
    x1_ref[...] = x1
    hx = _rms(x1, fnw_ref[...]) * (1.0 + scf_ref[...]) + shf_ref[...]
    _to_token_tiles(hx_ref, hx, tm)
    hx_hi = hx.astype(BF16)
    hx_lo = (hx - hx_hi.astype(F32)).astype(BF16)
    rwh = rwh_ref[...]
    logits = (jnp.dot(hx_hi, rwh, preferred_element_type=F32) + jnp.dot(hx_lo, rwh, preferred_element_type=F32)
              + jnp.dot(hx_hi, rwl_ref[...], preferred_element_type=F32)) + rb_ref[...]
    work = logits.T[:N_EXPERTS, :]
    erow = lax.broadcasted_iota(jnp.int32, (N_EXPERTS, tm), 0)
    vals, idxs = [], []
    for _ in range(TOP_K):
        m = jnp.max(work, axis=0, keepdims=True)
        ix = jnp.min(jnp.where(work == m, erow, N_EXPERTS), axis=0, keepdims=True)
        vals.append(m)
        idxs.append(ix)
        work = jnp.where(erow == ix, -jnp.inf, work)
    es = [jnp.exp(v - vals[0]) for v in vals]
    den = es[0] + es[1] + es[2] + es[3]
    onehots = [(erow == ix) for ix in idxs]
    multi = sum(oh.astype(F32) for oh in onehots)
    before = jnp.dot(multi.astype(BF16), tri_ref[...], preferred_element_type=F32) + carry[...]
    carry[...] = carry[...] + jnp.sum(multi, axis=1, keepdims=True)
    sub = lax.broadcasted_iota(jnp.int32, (SUBLANES, tm), 0)
    idx_o = jnp.zeros((SUBLANES, tm), jnp.int32)
    rank_o = jnp.zeros((SUBLANES, tm), jnp.int32)
    gate_o = jnp.zeros((SUBLANES, tm), F32)
    for k in range(TOP_K):
        rk = jnp.sum(jnp.where(onehots[k], before, 0.0), axis=0, keepdims=True)
        idx_o = jnp.where(sub == k, idxs[k], idx_o)
        rank_o = jnp.where(sub == k, rk.astype(jnp.int32), rank_o)
        gate_o = jnp.where(sub == k, es[k] / den, gate_o)
    idx_ref[...] = idx_o
    rank_ref[...] = rank_o
    gate_ref[...] = gate_o
    cnt_ref[...] = jnp.broadcast_to(carry[...], (N_EXPERTS, LANES))


def post_call(x, sof, sob, z, hof, hob, hg, snw, hnw, wo, pnw, gm, fnw, shf, scf, rwh, rwl, rb):
    bsz, t_len, _ = x.shape
    tm = TOK_TILE
    nt = t_len // tm
    n_tok = bsz * t_len
    tok = lambda c: pl.BlockSpec((None, tm, c), lambda b, i: (b, i, 0))
    full = lambda shape: pl.BlockSpec(shape, lambda b, i: (0,) * len(shape))
    per_b = pl.BlockSpec((None, 1, D_MODEL), lambda b, i: (b, 0, 0))
    flat = lambda c: pl.BlockSpec((tm, c), lambda b, i: (b * nt + i, 0))
    rout = pl.BlockSpec((SUBLANES, tm), lambda b, i: (b * nt + i, 0))
    n_tiles = n_tok // tm
    ii = np.arange(tm)
    tri = jnp.asarray(ii[:, None] < ii[None, :], BF16)
    return pl.pallas_call(
        functools.partial(_post_kernel, tm=tm),
        grid=(bsz, nt),
        in_specs=[tok(D_MODEL), tok(SSD_INNER), tok(SSD_INNER), tok(SSD_INNER), tok(HG_INNER), tok(HG_INNER),
                  tok(HG_INNER), full((1, SSD_INNER)), full((1, HG_INNER)), full((D_MODEL, D_MODEL)),
                  full((1, D_MODEL)), per_b, full((1, D_MODEL)), per_b, per_b,
                  full((D_MODEL, LANES)), full((D_MODEL, LANES)), full((1, LANES)), full((tm, tm))],
        out_specs=[flat(D_MODEL), pl.BlockSpec((tm * ROW_LINES, LANES), lambda b, i: (b * nt + i, 0)),
                   rout, rout, rout, pl.BlockSpec((N_EXPERTS, LANES), lambda b, i: (0, 0))],
        out_shape=[jax.ShapeDtypeStruct((n_tok, D_MODEL), F32),
                   jax.ShapeDtypeStruct((n_tok * ROW_LINES, LANES), F32),
                   jax.ShapeDtypeStruct((n_tiles * SUBLANES, tm), jnp.int32),
                   jax.ShapeDtypeStruct((n_tiles * SUBLANES, tm), jnp.int32),
                   jax.ShapeDtypeStruct((n_tiles * SUBLANES, tm), F32),
                   jax.ShapeDtypeStruct((N_EXPERTS, LANES), F32)],
        scratch_shapes=[pltpu.VMEM((N_EXPERTS, 1), F32)],
        compiler_params=_cparams(("arbitrary", "arbitrary")),
        name="post",
    )(x, sof, sob, z, hof, hob, hg, snw, hnw, wo, pnw, gm, fnw, shf, scf, rwh, rwl, rb, tri)


def _wait_rows(hbm_ref, n_rows, sem):
    n = n_rows * ROW_LINES
    pltpu.make_async_copy(hbm_ref.at[pl.ds(0, n)], hbm_ref.at[pl.ds(0, n)], sem).wait()


PAD_RUNS = tuple(2 ** j for j in range(int(math.log2(MOE_ROWS))))


def _dispatch_kernel(zstart_ref, zpad_ref, dest_ref, hx_ref, buf_ref, zrows, sem, zsem, *, tm):
    i = pl.program_id(0)

    def zero_rows(first, n):
        first = pl.multiple_of(first * ROW_LINES, ROW_LINES)
        return pltpu.make_async_copy(zrows.at[pl.ds(0, n * ROW_LINES)], buf_ref.at[pl.ds(first, n * ROW_LINES)], zsem)

    def for_pad_runs(act):
        def per_expert(e, carry):
            pad = zpad_ref[e]
            off = zstart_ref[e]
            for run in PAD_RUNS:
                @pl.when((pad & run) != 0)
                def _():
                    act(zero_rows(off, run))
                off = off + (pad & run)
            return carry

        lax.fori_loop(0, N_EXPERTS, per_expert, 0)

        def tail(j, carry):
            act(zero_rows(j * PAD_RUNS[-1], PAD_RUNS[-1]))
            return carry

        n_rows = buf_ref.shape[0] // ROW_LINES
        lax.fori_loop(zstart_ref[N_EXPERTS] // PAD_RUNS[-1], n_rows // PAD_RUNS[-1], tail, 0)

    @pl.when(i == 0)
    def _():
        zrows[...] = jnp.zeros_like(zrows)
        for_pad_runs(lambda cp: cp.start())

    def issue(t, carry):
        for k in range(TOP_K):
            pltpu.make_async_copy(_tile_of(hx_ref, t), _tile_of(buf_ref, dest_ref[k * tm + t]), sem).start()
        return carry

    lax.fori_loop(0, tm, issue, 0, unroll=4)
    _wait_rows(buf_ref, tm * TOP_K, sem)

    @pl.when(i == 0)
    def _():
        for_pad_runs(lambda cp: cp.wait())


def dispatch_call(zstart, zpad, dest_flat, hx, n_rows):
    n_tok = hx.shape[0] // ROW_LINES
    tm = TOK_TILE
    return pl.pallas_call(
        functools.partial(_dispatch_kernel, tm=tm),
        grid=(n_tok // tm,),
        in_specs=[pl.BlockSpec(memory_space=pltpu.SMEM), pl.BlockSpec(memory_space=pltpu.SMEM),
                  pl.BlockSpec((tm * TOP_K,), lambda i: (i,), memory_space=pltpu.SMEM),
                  pl.BlockSpec((tm * ROW_LINES, LANES), lambda i: (i, 0))],
        out_specs=pl.BlockSpec(memory_space=pl.ANY),
        out_shape=jax.ShapeDtypeStruct((n_rows * ROW_LINES, LANES), F32),
        scratch_shapes=[pltpu.VMEM((PAD_RUNS[-1] * ROW_LINES, LANES), F32), pltpu.SemaphoreType.DMA(()),
                        pltpu.SemaphoreType.DMA(())],
        compiler_params=_cparams(("arbitrary",)),
        name="dispatch",
    )(zstart, zpad, dest_flat, hx)


W1_TCOLS = 256


def _experts_kernel(be_ref, nu_ref, nxt_ref, x_ref, w1_hbm, b1_ref, w2_hbm, b2_ref, y_ref,
                    w1f, w2f, w1t, w2s, tbuf, wsem):
    i = pl.program_id(0)
    live = i < nu_ref[0]
    new_expert = (i == 0) | (be_ref[i] != be_ref[jnp.maximum(i - 1, 0)])

    def weight_copies(e):
        return (pltpu.make_async_copy(w1_hbm.at[e], w1f, wsem.at[0]),
                pltpu.make_async_copy(w2_hbm.at[e], w2f, wsem.at[1]))

    @pl.when(live & new_expert)
    def _():
        @pl.when(i == 0)
        def _():
            for cp in weight_copies(be_ref[0]):
                cp.start()

        for cp in weight_copies(be_ref[i]):
            cp.wait()
        half = W1_TCOLS // 2
        for c in range(2 * D_FF // W1_TCOLS):
            for j in range(D_MODEL // LANES):
                ks = slice(j * LANES, (j + 1) * LANES)
                tbuf[j] = w1f[ks, c * W1_TCOLS:(c + 1) * W1_TCOLS].T
                w1t[c * half:(c + 1) * half, ks] = tbuf[j, pl.ds(0, half, stride=2), :].astype(BF16)
                w1t[D_FF + c * half:D_FF + (c + 1) * half, ks] = tbuf[j, pl.ds(1, half, stride=2), :].astype(BF16)
        w2s[...] = w2f[...].astype(BF16)

        @pl.when(nxt_ref[i] >= 0)
        def _():
            for cp in weight_copies(nxt_ref[i]):
                cp.start()

    @pl.when(live)
    def _():
        xb = _from_token_tiles(x_ref, MOE_ROWS).astype(BF16)
        u = lax.dot_general(xb, w1t[...], (((1,), (1,)), ((), ())), preferred_element_type=F32) + b1_ref[...]
        glu = jnp.minimum(u[:, :D_FF], SWIGLU_LIMIT)
        lin = jnp.clip(u[:, D_FF:], -SWIGLU_LIMIT, SWIGLU_LIMIT)
        a = glu * _sigmoid(SWIGLU_ALPHA * glu) * (lin + 1.0)
        y = jnp.dot(a.astype(BF16), w2s[...], preferred_element_type=F32) + b2_ref[...]
        _to_token_tiles(y_ref, y, MOE_ROWS)

    @pl.when(jnp.logical_not(live))
    def _():
        y_ref[...] = jnp.zeros_like(y_ref)


def experts_call(block_e, n_used, next_e, xs, w1, b1p, w2, b2):
    rows = xs.shape[0] // ROW_LINES
    r = MOE_ROWS
    nb = rows // r
    blk = (r * ROW_LINES, LANES)
    grid_spec = pltpu.PrefetchScalarGridSpec(
        num_scalar_prefetch=3,
        grid=(nb,),
        in_specs=[pl.BlockSpec(blk, lambda i, be, nu, nx: (jnp.maximum(jnp.minimum(i, nu[0] - 1), 0), 0)),
                  pl.BlockSpec(memory_space=pl.ANY),
                  pl.BlockSpec((None, 1, 2 * D_FF), lambda i, be, nu, nx: (be[i], 0, 0)),
                  pl.BlockSpec(memory_space=pl.ANY),
                  pl.BlockSpec((None, 1, D_MODEL), lambda i, be, nu, nx: (be[i], 0, 0))],
        out_specs=pl.BlockSpec(blk, lambda i, be, nu, nx: (i, 0)),
        scratch_shapes=[pltpu.VMEM((D_MODEL, 2 * D_FF), F32), pltpu.VMEM((D_FF, D_MODEL), F32),
                        pltpu.VMEM((2 * D_FF, D_MODEL), BF16), pltpu.VMEM((D_FF, D_MODEL), BF16),
                        pltpu.VMEM((D_MODEL // LANES, W1_TCOLS, LANES), F32),
                        pltpu.SemaphoreType.DMA((2,))],
    )
    return pl.pallas_call(
        _experts_kernel,
        grid_spec=grid_spec,
        out_shape=jax.ShapeDtypeStruct((rows * ROW_LINES, LANES), F32),
        compiler_params=_cparams(("arbitrary",)),
        name="experts",
    )(block_e, n_used, next_e, xs, w1, b1p, w2, b2)


def _combine_kernel(dest_ref, y_ref, gate_ref, x1_ref, gf_ref, nw_ref, o_ref, buf, sem, *, tm):
    def issue(t, carry):
        for k in range(TOP_K):
            pltpu.make_async_copy(_tile_of(y_ref, dest_ref[k * tm + t]), _tile_of(buf, k * tm + t),
                                  sem).start(priority=k % 2)
        return carry

    lax.fori_loop(0, tm, issue, 0, unroll=4)
    _wait_rows(y_ref, tm * TOP_K, sem)
    gate = gate_ref[...].T
    fx = gate[:, 0:1] * _from_token_tiles(buf, tm)
    for k in range(1, TOP_K):
        fx = fx + gate[:, k:k + 1] * _from_token_tiles(buf, tm, first_row=k * tm)
    o_ref[...] = x1_ref[...] + gf_ref[...] * _rms(fx, nw_ref[...])


def combine_call(dest_flat, ys, gates, x1, gf, nw, bsz):
    n_tok = x1.shape[0]
    tm = TOK_TILE
    nt = n_tok // bsz // tm
    return pl.pallas_call(
        functools.partial(_combine_kernel, tm=tm),
        grid=(n_tok // tm,),
        in_specs=[pl.BlockSpec((tm * TOP_K,), lambda i: (i,), memory_space=pltpu.SMEM),
                  pl.BlockSpec(memory_space=pl.ANY),
                  pl.BlockSpec((SUBLANES, tm), lambda i: (i, 0)),
                  pl.BlockSpec((tm, D_MODEL), lambda i: (i, 0)),
                  pl.BlockSpec((None, 1, D_MODEL), lambda i: (i // nt, 0, 0)),
                  pl.BlockSpec((1, D_MODEL), lambda i: (0, 0))],
        out_specs=pl.BlockSpec((tm, D_MODEL), lambda i: (i, 0)),
        out_shape=jax.ShapeDtypeStruct((n_tok, D_MODEL), F32),
        scratch_shapes=[pltpu.VMEM((TOP_K * tm * ROW_LINES, LANES), F32), pltpu.SemaphoreType.DMA(())],
        compiler_params=_cparams(("arbitrary",)),
        name="combine",
    )(dest_flat, ys, gates, x1, gf, nw)


def kernel(x, c, ctx, c_ctx, ada_w, ada_b, mix_pre_norm, mix_post_norm, w_in, w_out, ssd_conv_w, ssd_conv_b,
           ssd_dt_bias, ssd_a_log, ssd_d, ssd_norm, hg_lb, hg_norm, ffn_pre_norm, ffn_post_norm, router_w,
           router_b, moe_w1, moe_b1, moe_w2, moe_b2):
    bsz, t_len, d = x.shape
    assert ada_w.shape[0] == 1 and d == D_MODEL and bsz <= SUBLANES - 1
    n_tok = bsz * t_len

    cc = jnp.zeros((SUBLANES, d), F32).at[:bsz].set(c).at[bsz].set(c_ctx)
    mod = ada_call(cc, ada_w[0], ada_b[0][None, :])
    sh_m, sc_m, g_m, sh_f, sc_f, g_f = [m[:bsz, None, :] for m in jnp.split(mod, 6, axis=-1)]
    csh_m, csc_m = [jnp.broadcast_to(m[bsz][None, None, :], (bsz, 1, d)) for m in jnp.split(mod, 6, axis=-1)[:2]]

    w = w_in[0]
    wa = jnp.concatenate([w[:, :SSD_INNER + SSD_XBC], w[:, SSD_COLS + 4 * HG_INNER:]], axis=1).astype(BF16)
    wdt = jnp.zeros((d, LANES), F32).at[:, :2 * SSD_HEADS].set(w[:, SSD_INNER + SSD_XBC:SSD_COLS]).astype(BF16)
    wb = w[:, SSD_COLS:SSD_COLS + 4 * HG_INNER].astype(BF16)
    dtb = jnp.zeros((1, LANES), F32).at[0, :2 * SSD_HEADS].set(ssd_dt_bias[0].reshape(-1))
    nw = mix_pre_norm[0][None, :]
    cw, cb = ssd_conv_w[0], ssd_conv_b[0][None, :]
    neg_a = -jnp.exp(ssd_a_log[0].astype(F32)).reshape(1, 2 * SSD_HEADS)
    dskip = jnp.repeat(ssd_d[0], SSD_HEAD_DIM)[None, :]
    lbp = hg_lb.astype(F32).reshape(2, 2 * HG_INNER)

    _, cxbc, _, cdt, cdtT = inproj_a_call(ctx, csh_m, csc_m, nw, wa, wdt, cw, cb, dtb, tm=ctx.shape[1])
    cu = inproj_b_call(ctx, csh_m, csc_m, nw, wb, col_major=False)
    z_ssd = jnp.zeros((bsz, N_PAIRS, SSD_STATE, LANES), F32)
    z_hg = jnp.zeros((bsz, HG_HEADS, HG_DK, HG_DK), F32)
    _, _, ssf, ssb = ssd_scan_call(cxbc, cdt, cdtT, z_ssd, z_ssd, neg_a, neg_a.reshape(-1, 1), dskip)
    _, _, hsf, hsb = hg_scan_call(cu, lbp, z_hg, z_hg, row_major_out=False)

    zg, xbc, hgate, dt, dtT = inproj_a_call(x, sh_m, sc_m, nw, wa, wdt, cw, cb, dtb, tm=512)
    ub = inproj_b_call(x, sh_m, sc_m, nw, wb, col_major=True)
    sof, sob, _, _ = ssd_scan_call(xbc, dt, dtT, ssf, ssb, neg_a, neg_a.reshape(-1, 1), dskip)
    hof, hob, _, _ = hg_scan_call(ub, lbp, hsf, hsb, row_major_out=True)

    rw = jnp.zeros((d, LANES), F32).at[:, :N_EXPERTS].set(router_w[0])
    rwh = rw.astype(BF16)
    rwl = (rw - rwh.astype(F32)).astype(BF16)
    rb = jnp.zeros((1, LANES), F32).at[0, :N_EXPERTS].set(router_b[0])
    x1, hx, idx, rank, gates, cnt = post_call(
        x, sof, sob, zg, hof, hob, hgate, ssd_norm[0][None, :], hg_norm[0][None, :], w_out[0].astype(BF16),
        mix_post_norm[0][None, :], g_m, ffn_pre_norm[0][None, :], sh_f, sc_f, rwh, rwl, rb)

    r = MOE_ROWS
    n_blocks = (n_tok * TOP_K + N_EXPERTS * (r - 1) + r - 1) // r
    n_tiles = n_tok // TOK_TILE
    counts = cnt[:, 0].astype(jnp.int32)
    padded = (counts + r - 1) // r * r
    pend = jnp.cumsum(padded)
    pstart = pend - padded
    idx3 = idx.reshape(n_tiles, SUBLANES, TOK_TILE)[:, :TOP_K, :]
    rank3 = rank.reshape(n_tiles, SUBLANES, TOK_TILE)[:, :TOP_K, :]
    dest = rank3
    for e in range(N_EXPERTS):
        dest = dest + jnp.where(idx3 == e, pstart[e], 0)
    dest = dest.reshape(-1)
    starts = jnp.arange(n_blocks, dtype=jnp.int32) * r
    block_e = jnp.minimum(jnp.sum((pend[None, :] <= starts[:, None]).astype(jnp.int32), axis=1), N_EXPERTS - 1)
    n_used = (pend[-1:] // r).astype(jnp.int32)
    run_end = pend[block_e] // r
    next_e = jnp.where(run_end < n_used[0], block_e[jnp.minimum(run_end, n_blocks - 1)], -1).astype(jnp.int32)

    zstart = jnp.concatenate([pstart + counts, pend[-1:]])
    xs = dispatch_call(zstart, padded - counts, dest, hx, n_blocks * r)
    b1p = jnp.concatenate([moe_b1[0][:, 0::2], moe_b1[0][:, 1::2]], axis=-1)[:, None, :]
    ys = experts_call(block_e, n_used, next_e, xs, moe_w1[0], b1p, moe_w2[0], moe_b2[0][:, None, :])
    out = combine_call(dest, ys, gates, x1, g_f, ffn_post_norm[0][None, :], bsz)
    return out.reshape(bsz, t_len, d)
```

```python
import functools
import math

import numpy as np
import jax
import jax.numpy as jnp
from jax import lax
from jax.experimental import pallas as pl
from jax.experimental.pallas import tpu as pltpu

F32 = jnp.float32
BF16 = jnp.bfloat16
HIGHEST = lax.Precision.HIGHEST

D_MODEL = 1024
GRID_W = 64
SSD_HEADS = 8
SSD_HEAD_DIM = 64
SSD_INNER = 512
SSD_STATE = 64
SSD_GROUPS = 2
SSD_CONV = 5
SSD_XBC = 768
SSD_COLS = 1296
HG_HEADS = 4
HG_DK = 128
HG_INNER = 512
N_EXPERTS = 32
TOP_K = 4
D_FF = 1024
SWIGLU_ALPHA = 1.702
SWIGLU_LIMIT = 7.0
EPS = 1e-6

LANES = 128
SUBLANES = 8
VMEM_LIMIT = 56 * 1024 * 1024

SSD_CHUNK = 256
HG_CHUNK = 128
HG_LEVELS = 7
MOE_ROWS = 256
TOK_TILE = 256
LOG2E = 1.4426950408889634


def _cparams(sem):
    return pltpu.CompilerParams(dimension_semantics=sem, vmem_limit_bytes=VMEM_LIMIT)


def _sigmoid(x):
    return 1.0 / (1.0 + jnp.exp(-x))


def _silu(x):
    return x * _sigmoid(x)


def _rms(x, w):
    return x * lax.rsqrt(jnp.mean(x * x, axis=-1, keepdims=True) + EPS) * w


def _split3(v):
    hi = v.astype(BF16)
    r1 = v - hi.astype(F32)
    mid = r1.astype(BF16)
    lo = (r1 - mid.astype(F32)).astype(BF16)
    return hi, mid, lo


def _ada_kernel(c_ref, w_ref, b_ref, o_ref):
    s = _silu(c_ref[...])
    o_ref[...] = jnp.dot(s, w_ref[...], precision=HIGHEST, preferred_element_type=F32) + b_ref[...]


def ada_call(cc, w, b):
    n = w.shape[1]
    tn = 1536
    return pl.pallas_call(
        _ada_kernel,
        grid=(n // tn,),
        in_specs=[pl.BlockSpec((SUBLANES, D_MODEL), lambda j: (0, 0)),
                  pl.BlockSpec((D_MODEL, tn), lambda j: (0, j)),
                  pl.BlockSpec((1, tn), lambda j: (0, j))],
        out_specs=pl.BlockSpec((SUBLANES, tn), lambda j: (0, j)),
        out_shape=jax.ShapeDtypeStruct((SUBLANES, n), F32),
        compiler_params=_cparams(("arbitrary",)),
        name="ada",
    )(cc, w, b)


A_COLS = SSD_INNER + SSD_XBC + HG_INNER


def _prep(xt, nw, sh, sc):
    return (_rms(xt, nw) * (1.0 + sc) + sh).astype(BF16)


def _inproj_a_kernel(x_ref, xp_ref, xn_ref, sh_ref, sc_ref, nw_ref, wa_ref, wdt_ref, cw_ref, cb_ref, dtb_ref,
                     z_ref, xbc_ref, g_ref, dt_ref, dtT_ref, scr, *, tm):
    i = pl.program_id(1)
    last = pl.num_programs(1) - 1
    nw, sh, sc = nw_ref[...], sh_ref[...], sc_ref[...]
    h = _prep(x_ref[...], nw, sh, sc)
    ua = jnp.dot(h, wa_ref[...], preferred_element_type=F32)
    z_ref[...] = ua[:, :SSD_INNER]
    g_ref[...] = ua[:, SSD_INNER + SSD_XBC:]
    wx = wa_ref[:, SSD_INNER:SSD_INNER + SSD_XBC]
    up = jnp.dot(_prep(xp_ref[...], nw, sh, sc), wx, preferred_element_type=F32)
    un = jnp.dot(_prep(xn_ref[...], nw, sh, sc), wx, preferred_element_type=F32)
    scr[0:SUBLANES, :] = jnp.where(i > 0, up, 0.0)
    scr[SUBLANES:SUBLANES + tm, :] = ua[:, SSD_INNER:SSD_INNER + SSD_XBC]
    scr[SUBLANES + tm:, :] = jnp.where(i < last, un, 0.0)
    acc = jnp.broadcast_to(cb_ref[...], (tm, SSD_XBC))
    pad = SSD_CONV // 2
    for k in range(SSD_CONV):
        off = SUBLANES - pad + k
        acc = acc + cw_ref[k:k + 1, :] * scr[off:off + tm, :]
    xbc_ref[...] = _silu(acc)
    draw = jnp.dot(h, wdt_ref[...], preferred_element_type=F32) + dtb_ref[...]
    dt = jnp.maximum(draw, 0.0) + jnp.log(1.0 + jnp.exp(-jnp.abs(draw)))
    dt_ref[...] = dt[:, :2 * SSD_HEADS]
    dtT_ref[...] = dt.T[:2 * SSD_HEADS, :]


def inproj_a_call(x, sh, sc, nw, wa, wdt, cw, cb, dtb, tm):
    bsz, t_len, _ = x.shape
    nt = t_len // tm
    r8 = tm // SUBLANES
    n8 = t_len // SUBLANES
    full = lambda shape: pl.BlockSpec(shape, lambda b, i: (0,) * len(shape))
    tok = lambda c: pl.BlockSpec((None, tm, c), lambda b, i: (b, i, 0))
    return pl.pallas_call(
        functools.partial(_inproj_a_kernel, tm=tm),
        grid=(bsz, nt),
        in_specs=[tok(D_MODEL),
                  pl.BlockSpec((None, SUBLANES, D_MODEL), lambda b, i: (b, jnp.maximum(i * r8 - 1, 0), 0)),
                  pl.BlockSpec((None, SUBLANES, D_MODEL), lambda b, i: (b, jnp.minimum((i + 1) * r8, n8 - 1), 0)),
                  pl.BlockSpec((None, 1, D_MODEL), lambda b, i: (b, 0, 0)),
                  pl.BlockSpec((None, 1, D_MODEL), lambda b, i: (b, 0, 0)),
                  full((1, D_MODEL)), full((D_MODEL, A_COLS)), full((D_MODEL, LANES)),
                  full((SSD_CONV, SSD_XBC)), full((1, SSD_XBC)), full((1, LANES))],
        out_specs=[tok(SSD_INNER), tok(SSD_XBC), tok(HG_INNER), tok(2 * SSD_HEADS),
                   pl.BlockSpec((None, 2 * SSD_HEADS, tm), lambda b, i: (b, 0, i))],
        out_shape=[jax.ShapeDtypeStruct((bsz, t_len, SSD_INNER), F32),
                   jax.ShapeDtypeStruct((bsz, t_len, SSD_XBC), F32),
                   jax.ShapeDtypeStruct((bsz, t_len, HG_INNER), F32),
                   jax.ShapeDtypeStruct((bsz, t_len, 2 * SSD_HEADS), F32),
                   jax.ShapeDtypeStruct((bsz, 2 * SSD_HEADS, t_len), F32)],
        scratch_shapes=[pltpu.VMEM((tm + 2 * SUBLANES, SSD_XBC), F32)],
        compiler_params=_cparams(("arbitrary", "arbitrary")),
        name="inproj_a",
    )(x, x, x, sh, sc, nw, wa, wdt, cw, cb, dtb)


B_COLS = 4 * HG_INNER


def _inproj_b_kernel(x_ref, sh_ref, sc_ref, nw_ref, w_ref, o_ref, *, ncol):
    if ncol:
        xt = jnp.concatenate([x_ref[:, w * D_MODEL:(w + 1) * D_MODEL] for w in range(ncol)], axis=0)
    else:
        xt = x_ref[...]
    h = _prep(xt, nw_ref[...], sh_ref[...], sc_ref[...])
    o_ref[...] = jnp.dot(h, w_ref[...], preferred_element_type=F32)


def inproj_b_call(x, sh, sc, nw, wb, col_major):
    bsz, t_len, _ = x.shape
    if col_major:
        ncol = 8
        tm = ncol * GRID_W
        rows = t_len // GRID_W
        assert rows == GRID_W
        xin = x.reshape(bsz, rows, GRID_W * D_MODEL)
        x_spec = pl.BlockSpec((None, rows, ncol * D_MODEL), lambda b, i: (b, 0, i))
    else:
        ncol = 0
        tm = t_len
        xin = x
        x_spec = pl.BlockSpec((None, tm, D_MODEL), lambda b, i: (b, i, 0))
    full = lambda shape: pl.BlockSpec(shape, lambda b, i: (0,) * len(shape))
    return pl.pallas_call(
        functools.partial(_inproj_b_kernel, ncol=ncol),
        grid=(bsz, t_len // tm),
        in_specs=[x_spec,
                  pl.BlockSpec((None, 1, D_MODEL), lambda b, i: (b, 0, 0)),
                  pl.BlockSpec((None, 1, D_MODEL), lambda b, i: (b, 0, 0)),
                  full((1, D_MODEL)), full((D_MODEL, B_COLS))],
        out_specs=pl.BlockSpec((None, tm, B_COLS), lambda b, i: (b, i, 0)),
        out_shape=jax.ShapeDtypeStruct((bsz, t_len, B_COLS), F32),
        compiler_params=_cparams(("arbitrary", "arbitrary")),
        name="inproj_b",
    )(xin, sh, sc, nw, wb)


N_PAIRS = SSD_HEADS // 2


def _ssd_dir(xbc, dt, dtT, s_ref, tri, triT, na_row, na_col, fwd):
    c = SSD_CHUNK
    xs = xbc[:, :SSD_INNER]
    bm = xbc[:, SSD_INNER:SSD_INNER + LANES]
    cm = xbc[:, SSD_INNER + LANES:]
    col0 = 0 if fwd else SSD_HEADS
    la = dt[:, col0:col0 + SSD_HEADS] * na_row[:, col0:col0 + SSD_HEADS]
    dtr = dtT[col0:col0 + SSD_HEADS, :]
    laT = dtr * na_col[col0:col0 + SSD_HEADS, :]
    g = sum(jnp.dot(tri, p, preferred_element_type=F32) for p in _split3(la))
    gT = sum(jnp.dot(p, triT, preferred_element_type=F32) for p in _split3(laT))
    end = c - 1 if fwd else 0
    bmT = bm.T
    ii = lax.broadcasted_iota(jnp.int32, (c, c), 0)
    jj = lax.broadcasted_iota(jnp.int32, (c, c), 1)
    causal = (jj <= ii) if fwd else (jj >= ii)
    lane = lax.broadcasted_iota(jnp.int32, (c, LANES), 1)
    lo_half = lane < SSD_HEAD_DIM
    lane_s = lax.broadcasted_iota(jnp.int32, (SSD_STATE, LANES), 1) < SSD_HEAD_DIM
    outs = []
    for grp in range(SSD_GROUPS):
        in_grp = (lane >= grp * SSD_STATE) & (lane < (grp + 1) * SSD_STATE)
        cm_g = jnp.where(in_grp, cm, 0.0).astype(BF16)
        gmat = jnp.dot(cm_g, bmT.astype(BF16), preferred_element_type=F32)
        bmT_g = bmT[grp * SSD_STATE:(grp + 1) * SSD_STATE, :]
        for pp in range(N_PAIRS // SSD_GROUPS):
            pair = grp * (N_PAIRS // SSD_GROUPS) + pp
            heads = (2 * pair, 2 * pair + 1)
            xs_p = xs[:, pair * LANES:(pair + 1) * LANES]
            xbd = jnp.concatenate([jnp.where(lo_half, xs_p, 0.0), jnp.where(lo_half, 0.0, xs_p)],
                                  axis=0).astype(BF16)
            ms, bws, ecols, arows = [], [], [], []
            for hd in heads:
                gcol = g[:, hd:hd + 1]
                grow = gT[hd:hd + 1, :]
                dec = jnp.exp(jnp.where(causal, gcol - grow, -jnp.inf))
                ms.append(gmat * dec * dtr[hd:hd + 1, :])
                glast = grow[:, end:end + 1]
                bws.append(bmT_g * (dtr[hd:hd + 1, :] * jnp.exp(glast - grow)))
                ecols.append(jnp.exp(gcol))
                arows.append(jnp.exp(glast))
            mcat = jnp.concatenate(ms, axis=1).astype(BF16)
            s_old = s_ref[pair]
            zeros = jnp.zeros_like(s_old)
            s_pad = jnp.concatenate([s_old, zeros] if grp == 0 else [zeros, s_old], axis=0).astype(BF16)
            o_inter = jnp.dot(cm.astype(BF16), s_pad, preferred_element_type=F32)
            o_inter = o_inter * jnp.where(lo_half, ecols[0], ecols[1])
            outs.append(jnp.dot(mcat, xbd, preferred_element_type=F32) + o_inter)
            bw = jnp.concatenate(bws, axis=1).astype(BF16)
            s_ref[pair] = (s_old * jnp.where(lane_s, arows[0], arows[1])
                           + jnp.dot(bw, xbd, preferred_element_type=F32))
    return outs


def _ssd_scan_kernel(xf_ref, dtf_ref, dtTf_ref, xb_ref, dtb_ref, dtTb_ref, s0f_ref, s0b_ref,
                     trif_ref, trifT_ref, trib_ref, tribT_ref, nar_ref, nac_ref, dsk_ref,
                     of_ref, ob_ref, sfo_ref, sbo_ref, sf, sb):
    n = pl.program_id(1)

    @pl.when(n == 0)
    def _():
        sf[...] = s0f_ref[...]
        sb[...] = s0b_ref[...]

    xf = xf_ref[...]
    outs = _ssd_dir(xf, dtf_ref[...], dtTf_ref[...], sf, trif_ref[...], trifT_ref[...],
                    nar_ref[...], nac_ref[...], True)
    of_ref[...] = jnp.concatenate(outs, axis=1) + dsk_ref[...] * xf[:, :SSD_INNER]
    outs = _ssd_dir(xb_ref[...], dtb_ref[...], dtTb_ref[...], sb, trib_ref[...], tribT_ref[...],
                    nar_ref[...], nac_ref[...], False)
    ob_ref[...] = jnp.concatenate(outs, axis=1)

    @pl.when(n == pl.num_programs(1) - 1)
    def _():
        sfo_ref[...] = sf[...]
        sbo_ref[...] = sb[...]


def _ssd_consts():
    c = SSD_CHUNK
    i = np.arange(c)
    trif = (i[:, None] >= i[None, :]).astype(np.float32)
    trib = (i[:, None] <= i[None, :]).astype(np.float32)
    return [jnp.asarray(a, BF16) for a in (trif, trif.T, trib, trib.T)]


def ssd_scan_call(xbc, dt, dtT, s0f, s0b, na_row, na_col, dskip):
    bsz, t_len, _ = xbc.shape
    c = SSD_CHUNK
    nc = t_len // c
    fw = lambda w: pl.BlockSpec((None, c, w), lambda b, n: (b, n, 0))
    bw = lambda w: pl.BlockSpec((None, c, w), lambda b, n: (b, nc - 1 - n, 0))
    full = lambda shape: pl.BlockSpec(shape, lambda b, n: (0,) * len(shape))
    st = pl.BlockSpec((None, N_PAIRS, SSD_STATE, LANES), lambda b, n: (b, 0, 0, 0))
    st_shape = jax.ShapeDtypeStruct((bsz, N_PAIRS, SSD_STATE, LANES), F32)
    return pl.pallas_call(
        _ssd_scan_kernel,
        grid=(bsz, nc),
        in_specs=[fw(SSD_XBC), fw(2 * SSD_HEADS),
                  pl.BlockSpec((None, 2 * SSD_HEADS, c), lambda b, n: (b, 0, n)),
                  bw(SSD_XBC), bw(2 * SSD_HEADS),
                  pl.BlockSpec((None, 2 * SSD_HEADS, c), lambda b, n: (b, 0, nc - 1 - n)),
                  st, st, full((c, c)), full((c, c)), full((c, c)), full((c, c)),
                  full((1, 2 * SSD_HEADS)), full((2 * SSD_HEADS, 1)), full((1, SSD_INNER))],
        out_specs=[fw(SSD_INNER), bw(SSD_INNER), st, st],
        out_shape=[jax.ShapeDtypeStruct((bsz, t_len, SSD_INNER), F32),
                   jax.ShapeDtypeStruct((bsz, t_len, SSD_INNER), F32), st_shape, st_shape],
        scratch_shapes=[pltpu.VMEM((N_PAIRS, SSD_STATE, LANES), F32),
                        pltpu.VMEM((N_PAIRS, SSD_STATE, LANES), F32)],
        compiler_params=_cparams(("arbitrary", "arbitrary")),
        name="ssd_scan",
    )(xbc, dt, dtT, xbc, dt, dtT, s0f, s0b, *_ssd_consts(), na_row, na_col, dskip)


def _hg_consts():
    c = HG_CHUNK
    t = np.arange(c)
    m_f = np.zeros((HG_LEVELS + 1, c, c), np.float32)
    m_f[0] = np.eye(c)
    for lv in range(1, HG_LEVELS + 1):
        m = 2 ** lv
        blk = t // m
        right = (t % m) >= m // 2
        m_f[lv] = (blk[:, None] == blk[None, :]) & right[:, None] & (~right[None, :])
    m_b = np.transpose(m_f, (0, 2, 1))
    tri_f = (t[:, None] >= t[None, :]).astype(np.float32)
    tri_b = (t[:, None] <= t[None, :]).astype(np.float32)
    return jnp.asarray(tri_f, BF16), jnp.asarray(tri_b, BF16), jnp.asarray(m_f, F32), jnp.asarray(m_b, F32)


def _hg_dir(u, lb_row, st_ref, tri_ref, mask_ref, fwd):
    c = HG_CHUNK
    outs = []
    fcol = HG_INNER if fwd else 2 * HG_INNER
    nt = (((1,), (1,)), ((), ()))
    row = lax.broadcasted_iota(jnp.int32, (c, HG_DK), 0)

    def halves(lo, hi, half):
        if half % SUBLANES == 0:
            return jnp.concatenate([(hi if (s // half) % 2 else lo)[s:s + half] for s in range(0, c, half)], axis=0)
        return jnp.where(((row // half) % 2) == 1, hi, lo)

    def shift(x, s):
        s = s % c
        if s % SUBLANES == 0:
            return jnp.concatenate([x[c - s:], x[:c - s]], axis=0)
        return pltpu.roll(x, s, 0)

    def by_side(query_side, key_side, half):
        return halves(key_side, query_side, half) if fwd else halves(query_side, key_side, half)

    for hd in range(HG_HEADS):
        sl = slice(hd * HG_DK, (hd + 1) * HG_DK)
        q = _silu(u[:, sl]) * (HG_DK ** -0.5)
        lb = lb_row[:, sl]
        f = lb + (1.0 - lb) * _sigmoid(u[:, fcol + hd * HG_DK:fcol + (hd + 1) * HG_DK])
        k = 1.0 - f
        la = jnp.log(f)
        v = u[:, 3 * HG_INNER + hd * HG_DK:3 * HG_INNER + (hd + 1) * HG_DK]
        la_hi = la.astype(BF16)
        la_lo = (la - la_hi.astype(F32)).astype(BF16)
        g2 = jnp.dot(tri_ref[...], jnp.concatenate([la_hi, la_lo], axis=1), preferred_element_type=F32)
        g = g2[:, :HG_DK] + g2[:, HG_DK:]
        scores = mask_ref[0] * lax.dot_general(q.astype(BF16), k.astype(BF16), nt, preferred_element_type=F32)
        fill = g
        for lv in range(1, HG_LEVELS + 1):
            half = 2 ** (lv - 1)
            if fwd:
                ref = halves(fill, shift(fill, half), half)
            else:
                ref = halves(shift(fill, -half), fill, half)
            decay = jnp.exp2(jnp.abs(g - ref) * (-LOG2E))
            y = (by_side(q, k, half) * decay).astype(BF16)
            scores = scores + mask_ref[lv] * lax.dot_general(y, y, nt, preferred_element_type=F32)
            if fwd:
                fill = halves(shift(fill, -half), fill, half)
            else:
                fill = halves(fill, shift(fill, half), half)
        x_cum = jnp.exp(g)
        x_rem = jnp.exp(fill - g)
        st = st_ref[hd]
        o = jnp.dot(scores.astype(BF16), v.astype(BF16), preferred_element_type=F32)
        o = o + lax.dot_general((q * x_cum).astype(BF16), st.astype(BF16), nt, preferred_element_type=F32)
        outs.append(o)
        end = c - 1 if fwd else 0
        st_ref[hd] = (st * x_cum[end:end + 1, :]
                      + jnp.dot(v.T.astype(BF16), (k * x_rem).astype(BF16), preferred_element_type=F32))
    return jnp.concatenate(outs, axis=1)


def _hg_scan_kernel(uf_ref, ub_ref, lbp_ref, s0f_ref, s0b_ref, trif_ref, trib_ref, mf_ref, mb_ref,
                    of_ref, ob_ref, sfo_ref, sbo_ref, sf, sb, *, img_rows):
    n = pl.program_id(1)

    def put(o_ref, o):
        if img_rows:
            for j in range(HG_CHUNK // img_rows):
                o_ref[:, j * HG_INNER:(j + 1) * HG_INNER] = o[j * img_rows:(j + 1) * img_rows]
        else:
            o_ref[...] = o

    @pl.when(n == 0)
    def _():
        sf[...] = s0f_ref[...]
        sb[...] = s0b_ref[...]

    p = lbp_ref[...]
    mx = jnp.max(p, axis=0, keepdims=True)
    e = jnp.exp(p - mx)
    lb = e[0:1, :] / jnp.sum(e, axis=0, keepdims=True)
    put(of_ref, _hg_dir(uf_ref[...], lb[:, :HG_INNER], sf, trif_ref, mf_ref, True))
    put(ob_ref, _hg_dir(ub_ref[...], lb[:, HG_INNER:], sb, trib_ref, mb_ref, False))

    @pl.when(n == pl.num_programs(1) - 1)
    def _():
        sfo_ref[...] = sf[...]
        sbo_ref[...] = sb[...]


def hg_scan_call(u, lbp, s0f, s0b, row_major_out):
    bsz, t_len, _ = u.shape
    c = HG_CHUNK
    nc = t_len // c
    full = lambda shape: pl.BlockSpec(shape, lambda b, n: (0,) * len(shape))
    st = pl.BlockSpec((None, HG_HEADS, HG_DK, HG_DK), lambda b, n: (b, 0, 0, 0))
    st_shape = jax.ShapeDtypeStruct((bsz, HG_HEADS, HG_DK, HG_DK), F32)
    if row_major_out:
        img_rows = t_len // GRID_W
        cols = c // img_rows
        assert cols * img_rows == c
        o_shape = jax.ShapeDtypeStruct((bsz, img_rows, GRID_W * HG_INNER), F32)
        of_spec = pl.BlockSpec((None, img_rows, cols * HG_INNER), lambda b, n: (b, 0, n))
        ob_spec = pl.BlockSpec((None, img_rows, cols * HG_INNER), lambda b, n: (b, 0, nc - 1 - n))
    else:
        img_rows = 0
        o_shape = jax.ShapeDtypeStruct((bsz, t_len, HG_INNER), F32)
        of_spec = pl.BlockSpec((None, c, HG_INNER), lambda b, n: (b, n, 0))
        ob_spec = pl.BlockSpec((None, c, HG_INNER), lambda b, n: (b, nc - 1 - n, 0))
    tri_f, tri_b, m_f, m_b = _hg_consts()
    of, ob, sfo, sbo = pl.pallas_call(
        functools.partial(_hg_scan_kernel, img_rows=img_rows),
        grid=(bsz, nc),
        in_specs=[pl.BlockSpec((None, c, B_COLS), lambda b, n: (b, n, 0)),
                  pl.BlockSpec((None, c, B_COLS), lambda b, n: (b, nc - 1 - n, 0)),
                  full((2, 2 * HG_INNER)), st, st,
                  full(tri_f.shape), full(tri_b.shape), full(m_f.shape), full(m_b.shape)],
        out_specs=[of_spec, ob_spec, st, st],
        out_shape=[o_shape, o_shape, st_shape, st_shape],
        scratch_shapes=[pltpu.VMEM((HG_HEADS, HG_DK, HG_DK), F32), pltpu.VMEM((HG_HEADS, HG_DK, HG_DK), F32)],
        compiler_params=_cparams(("arbitrary", "arbitrary")),
        name="hg_scan",
    )(u, u, lbp, s0f, s0b, tri_f, tri_b, m_f, m_b)
    if row_major_out:
        of = of.reshape(bsz, t_len, HG_INNER)
        ob = ob.reshape(bsz, t_len, HG_INNER)
    return of, ob, sfo, sbo


ROW_LINES = D_MODEL // LANES


def _to_token_tiles(ref, val, n_rows):
    for c in range(ROW_LINES):
        ref[pl.ds(c, n_rows, stride=ROW_LINES), :] = val[:, c * LANES:(c + 1) * LANES]


def _from_token_tiles(ref, n_rows, first_row=0):
    return jnp.concatenate([ref[pl.ds(first_row * ROW_LINES + c, n_rows, stride=ROW_LINES), :]
                            for c in range(ROW_LINES)], axis=1)


def _tile_of(ref, row):
    return ref.at[pl.ds(pl.multiple_of(row * ROW_LINES, ROW_LINES), ROW_LINES)]


def _post_kernel(x_ref, sof_ref, sob_ref, z_ref, hof_ref, hob_ref, hg_ref,
                 snw_ref, hnw_ref, wo_ref, pnw_ref, gm_ref, fnw_ref, shf_ref, scf_ref, rwh_ref, rwl_ref, rb_ref,
                 tri_ref, x1_ref, hx_ref, idx_ref, rank_ref, gate_ref, cnt_ref, carry, *, tm):
    first = (pl.program_id(0) == 0) & (pl.program_id(1) == 0)

    @pl.when(first)
    def _():
        carry[...] = jnp.zeros_like(carry)

    y = (sof_ref[...] + sob_ref[...]) * _silu(z_ref[...])
    y = _rms(y, snw_ref[...])
    o = hof_ref[...] + hob_ref[...]
    hnw = hnw_ref[...]
    o = jnp.concatenate([_rms(o[:, h * HG_DK:(h + 1) * HG_DK], hnw[:, h * HG_DK:(h + 1) * HG_DK])
                         for h in range(HG_HEADS)], axis=1)
    o = o * _silu(hg_ref[...])
    mix = (jnp.dot(y.astype(BF16), wo_ref[:SSD_INNER, :], preferred_element_type=F32)
           + jnp.dot(o.astype(BF16), wo_ref[SSD_INNER:, :], preferred_element_type=F32))
    x1 = x_ref[...] + gm_ref[...] * _rms(mix, pnw_ref[...])
    x1_ref[...] = x1
    hx = _rms(x1, fnw_ref[...]) * (1.0 + scf_ref[...]) + shf_ref[...]
    _to_token_tiles(hx_ref, hx, tm)
    hx_hi = hx.astype(BF16)
    hx_lo = (hx - hx_hi.astype(F32)).astype(BF16)
    rwh = rwh_ref[...]
    logits = (jnp.dot(hx_hi, rwh, preferred_element_type=F32) + jnp.dot(hx_lo, rwh, preferred_element_type=F32)
              + jnp.dot(hx_hi, rwl_ref[...], preferred_element_type=F32)) + rb_ref[...]
    work = logits.T[:N_EXPERTS, :]
    erow = lax.broadcasted_iota(jnp.int32, (N_EXPERTS, tm), 0)
    vals, idxs = [], []
    for _ in range(TOP_K):
        m = jnp.max(work, axis=0, keepdims=True)
        ix = jnp.min(jnp.where(work == m, erow, N_EXPERTS), axis=0, keepdims=True)
        vals.append(m)
        idxs.append(ix)
        work = jnp.where(erow == ix, -jnp.inf, work)
    es = [jnp.exp(v - vals[0]) for v in vals]
    den = es[0] + es[1] + es[2] + es[3]
    onehots = [(erow == ix) for ix in idxs]
    multi = sum(oh.astype(F32) for oh in onehots)
    before = jnp.dot(multi.astype(BF16), tri_ref[...], preferred_element_type=F32) + carry[...]
    carry[...] = carry[...] + jnp.sum(multi, axis=1, keepdims=True)
    sub = lax.broadcasted_iota(jnp.int32, (SUBLANES, tm), 0)
    idx_o = jnp.zeros((SUBLANES, tm), jnp.int32)
    rank_o = jnp.zeros((SUBLANES, tm), jnp.int32)
    gate_o = jnp.zeros((SUBLANES, tm), F32)
    for k in range(TOP_K):
        rk = jnp.sum(jnp.where(onehots[k], before, 0.0), axis=0, keepdims=True)
        idx_o = jnp.where(sub == k, idxs[k], idx_o)
        rank_o = jnp.where(sub == k, rk.astype(jnp.int32), rank_o)
        gate_o = jnp.where(sub == k, es[k] / den, gate_o)
    idx_ref[...] = idx_o
    rank_ref[...] = rank_o
    gate_ref[...] = gate_o
    cnt_ref[...] = jnp.broadcast_to(carry[...], (N_EXPERTS, LANES))


def post_call(x, sof, sob, z, hof, hob, hg, snw, hnw, wo, pnw, gm, fnw, shf, scf, rwh, rwl, rb):
    bsz, t_len, _ = x.shape
    tm = TOK_TILE
    nt = t_len // tm
    n_tok = bsz * t_len
    tok = lambda c: pl.BlockSpec((None, tm, c), lambda b, i: (b, i, 0))
    full = lambda shape: pl.BlockSpec(shape, lambda b, i: (0,) * len(shape))
    per_b = pl.BlockSpec((None, 1, D_MODEL), lambda b, i: (b, 0, 0))
    flat = lambda c: pl.BlockSpec((tm, c), lambda b, i: (b * nt + i, 0))
    rout = pl.BlockSpec((SUBLANES, tm), lambda b, i: (b * nt + i, 0))
    n_tiles = n_tok // tm
    ii = np.arange(tm)
    tri = jnp.asarray(ii[:, None] < ii[None, :], BF16)
    return pl.pallas_call(
        functools.partial(_post_kernel, tm=tm),
        grid=(bsz, nt),
        in_specs=[tok(D_MODEL), tok(SSD_INNER), tok(SSD_INNER), tok(SSD_INNER), tok(HG_INNER), tok(HG_INNER),
                  tok(HG_INNER), full((1, SSD_INNER)), full((1, HG_INNER)), full((D_MODEL, D_MODEL)),
                  full((1, D_MODEL)), per_b, full((1, D_MODEL)), per_b, per_b,
                  full((D_MODEL, LANES)), full((D_MODEL, LANES)), full((1, LANES)), full((tm, tm))],
        out_specs=[flat(D_MODEL), pl.BlockSpec((tm * ROW_LINES, LANES), lambda b, i: (b * nt + i, 0)),
                   rout, rout, rout, pl.BlockSpec((N_EXPERTS, LANES), lambda b, i: (0, 0))],
        out_shape=[jax.ShapeDtypeStruct((n_tok, D_MODEL), F32),
                   jax.ShapeDtypeStruct((n_tok * ROW_LINES, LANES), F32),
                   jax.ShapeDtypeStruct((n_tiles * SUBLANES, tm), jnp.int32),
                   jax.ShapeDtypeStruct((n_tiles * SUBLANES, tm), jnp.int32),
                   jax.ShapeDtypeStruct((n_tiles * SUBLANES, tm), F32),
                   jax.ShapeDtypeStruct((N_EXPERTS, LANES), F32)],
        scratch_shapes=[pltpu.VMEM((N_EXPERTS, 1), F32)],
        compiler_params=_cparams(("arbitrary", "arbitrary")),
        name="post",
    )(x, sof, sob, z, hof, hob, hg, snw, hnw, wo, pnw, gm, fnw, shf, scf, rwh, rwl, rb, tri)


def _wait_rows(hbm_ref, n_rows, sem):
    n = n_rows * ROW_LINES
    pltpu.make_async_copy(hbm_ref.at[pl.ds(0, n)], hbm_ref.at[pl.ds(0, n)], sem).wait()


PAD_RUNS = tuple(2 ** j for j in range(int(math.log2(MOE_ROWS))))
DISPATCH_TILES = 2


def _dispatch_kernel(zstart_ref, zpad_ref, dest_ref, hx_ref, buf_ref, zrows, sem, zsem, *, tm):
    i = pl.program_id(0)

    def zero_rows(first, n):
        first = pl.multiple_of(first * ROW_LINES, ROW_LINES)
        return pltpu.make_async_copy(zrows.at[pl.ds(0, n * ROW_LINES)], buf_ref.at[pl.ds(first, n * ROW_LINES)], zsem)

    def for_pad_runs(act):
        def per_expert(e, carry):
            pad = zpad_ref[e]
            off = zstart_ref[e]
            for run in PAD_RUNS:
                @pl.when((pad & run) != 0)
                def _():
                    act(zero_rows(off, run))
                off = off + (pad & run)
            return carry

        lax.fori_loop(0, N_EXPERTS, per_expert, 0)

        def tail(j, carry):
            act(zero_rows(j * PAD_RUNS[-1], PAD_RUNS[-1]))
            return carry

        n_rows = buf_ref.shape[0] // ROW_LINES
        lax.fori_loop(zstart_ref[N_EXPERTS] // PAD_RUNS[-1], n_rows // PAD_RUNS[-1], tail, 0)

    @pl.when(i == 0)
    def _():
        zrows[...] = jnp.zeros_like(zrows)
        for_pad_runs(lambda cp: cp.start())

    def issue(t, carry):
        for sub in range(DISPATCH_TILES):
            for k in range(TOP_K):
                pltpu.make_async_copy(_tile_of(hx_ref, sub * tm + t),
                                      _tile_of(buf_ref, dest_ref[(sub * TOP_K + k) * tm + t]),
                                      sem).start(priority=k % 2)
        return carry

    lax.fori_loop(0, tm, issue, 0, unroll=2)
    _wait_rows(buf_ref, DISPATCH_TILES * tm * TOP_K, sem)

    @pl.when(i == 0)
    def _():
        for_pad_runs(lambda cp: cp.wait())


def dispatch_call(zstart, zpad, dest_flat, hx, n_rows):
    n_tok = hx.shape[0] // ROW_LINES
    tm = TOK_TILE
    step_tok = DISPATCH_TILES * tm
    return pl.pallas_call(
        functools.partial(_dispatch_kernel, tm=tm),
        grid=(n_tok // step_tok,),
        in_specs=[pl.BlockSpec(memory_space=pltpu.SMEM), pl.BlockSpec(memory_space=pltpu.SMEM),
                  pl.BlockSpec((step_tok * TOP_K,), lambda i: (i,), memory_space=pltpu.SMEM),
                  pl.BlockSpec((step_tok * ROW_LINES, LANES), lambda i: (i, 0))],
        out_specs=pl.BlockSpec(memory_space=pl.ANY),
        out_shape=jax.ShapeDtypeStruct((n_rows * ROW_LINES, LANES), F32),
        scratch_shapes=[pltpu.VMEM((PAD_RUNS[-1] * ROW_LINES, LANES), F32), pltpu.SemaphoreType.DMA(()),
                        pltpu.SemaphoreType.DMA(())],
        compiler_params=_cparams(("arbitrary",)),
        name="dispatch",
    )(zstart, zpad, dest_flat, hx)


W1_TCOLS = 256


def _experts_kernel(be_ref, nu_ref, nxt_ref, x_ref, w1_hbm, b1_ref, w2_hbm, b2_ref, y_ref,
                    w1f, w2f, w1t, w2s, tbuf, wsem):
    i = pl.program_id(0)
    live = i < nu_ref[0]
    new_expert = (i == 0) | (be_ref[i] != be_ref[jnp.maximum(i - 1, 0)])

    def weight_copies(e):
        return (pltpu.make_async_copy(w1_hbm.at[e], w1f, wsem.at[0]),
                pltpu.make_async_copy(w2_hbm.at[e], w2f, wsem.at[1]))

    @pl.when(live & new_expert)
    def _():
        @pl.when(i == 0)
        def _():
            for cp in weight_copies(be_ref[0]):
                cp.start()

        for cp in weight_copies(be_ref[i]):
            cp.wait()
        half = W1_TCOLS // 2
        for c in range(2 * D_FF // W1_TCOLS):
            for j in range(D_MODEL // LANES):
                ks = slice(j * LANES, (j + 1) * LANES)
                tbuf[j] = w1f[ks, c * W1_TCOLS:(c + 1) * W1_TCOLS].T
                w1t[c * half:(c + 1) * half, ks] = tbuf[j, pl.ds(0, half, stride=2), :].astype(BF16)
                w1t[D_FF + c * half:D_FF + (c + 1) * half, ks] = tbuf[j, pl.ds(1, half, stride=2), :].astype(BF16)
        w2s[...] = w2f[...].astype(BF16)

        @pl.when(nxt_ref[i] >= 0)
        def _():
            for cp in weight_copies(nxt_ref[i]):
                cp.start(priority=1)

    @pl.when(live)
    def _():
        xb = _from_token_tiles(x_ref, MOE_ROWS).astype(BF16)
        u = lax.dot_general(xb, w1t[...], (((1,), (1,)), ((), ())), preferred_element_type=F32) + b1_ref[...]
        glu = jnp.minimum(u[:, :D_FF], SWIGLU_LIMIT)
        lin = jnp.clip(u[:, D_FF:], -SWIGLU_LIMIT, SWIGLU_LIMIT)
        a = glu * _sigmoid(SWIGLU_ALPHA * glu) * (lin + 1.0)
        y = jnp.dot(a.astype(BF16), w2s[...], preferred_element_type=F32) + b2_ref[...]
        _to_token_tiles(y_ref, y, MOE_ROWS)

    @pl.when(jnp.logical_not(live))
    def _():
        y_ref[...] = jnp.zeros_like(y_ref)


def experts_call(block_e, n_used, next_e, xs, w1, b1p, w2, b2):
    rows = xs.shape[0] // ROW_LINES
    r = MOE_ROWS
    nb = rows // r
    blk = (r * ROW_LINES, LANES)
    grid_spec = pltpu.PrefetchScalarGridSpec(
        num_scalar_prefetch=3,
        grid=(nb,),
        in_specs=[pl.BlockSpec(blk, lambda i, be, nu, nx: (jnp.maximum(jnp.minimum(i, nu[0] - 1), 0), 0)),
                  pl.BlockSpec(memory_space=pl.ANY),
                  pl.BlockSpec((None, 1, 2 * D_FF), lambda i, be, nu, nx: (be[i], 0, 0)),
                  pl.BlockSpec(memory_space=pl.ANY),
                  pl.BlockSpec((None, 1, D_MODEL), lambda i, be, nu, nx: (be[i], 0, 0))],
        out_specs=pl.BlockSpec(blk, lambda i, be, nu, nx: (i, 0)),
        scratch_shapes=[pltpu.VMEM((D_MODEL, 2 * D_FF), F32), pltpu.VMEM((D_FF, D_MODEL), F32),
                        pltpu.VMEM((2 * D_FF, D_MODEL), BF16), pltpu.VMEM((D_FF, D_MODEL), BF16),
                        pltpu.VMEM((D_MODEL // LANES, W1_TCOLS, LANES), F32),
                        pltpu.SemaphoreType.DMA((2,))],
    )
    return pl.pallas_call(
        _experts_kernel,
        grid_spec=grid_spec,
        out_shape=jax.ShapeDtypeStruct((rows * ROW_LINES, LANES), F32),
        compiler_params=_cparams(("arbitrary",)),
        name="experts",
    )(block_e, n_used, next_e, xs, w1, b1p, w2, b2)


def _combine_kernel(dest_ref, dest_next_ref, y_ref, gate_ref, x1_ref, gf_ref, nw_ref, o_ref, buf, sem, *, tm):
    i = pl.program_id(0)
    slot = i % 2

    def gather_tile(dref, into):
        def issue(t, carry):
            for k in range(TOP_K):
                pltpu.make_async_copy(_tile_of(y_ref, dref[k * tm + t]), _tile_of(buf.at[into], k * tm + t),
                                      sem.at[into]).start(priority=k % 2)
            return carry

        lax.fori_loop(0, tm, issue, 0, unroll=4)

    @pl.when(i == 0)
    def _():
        gather_tile(dest_ref, 0)

    @pl.when(i + 1 < pl.num_programs(0))
    def _():
        gather_tile(dest_next_ref, 1 - slot)

    _wait_rows(y_ref, tm * TOP_K, sem.at[slot])
    gate = gate_ref[...].T
    fx = None
    for k in range(TOP_K):
        yk = jnp.concatenate([buf[slot, pl.ds(k * tm * ROW_LINES + c, tm, stride=ROW_LINES), :]
                              for c in range(ROW_LINES)], axis=1)
        fx = gate[:, k:k + 1] * yk if fx is None else fx + gate[:, k:k + 1] * yk
    o_ref[...] = x1_ref[...] + gf_ref[...] * _rms(fx, nw_ref[...])


def combine_call(dest_flat, ys, gates, x1, gf, nw, bsz):
    n_tok = x1.shape[0]
    tm = TOK_TILE
    nt = n_tok // bsz // tm
    n_steps = n_tok // tm
    return pl.pallas_call(
        functools.partial(_combine_kernel, tm=tm),
        grid=(n_steps,),
        in_specs=[pl.BlockSpec((tm * TOP_K,), lambda i: (i,), memory_space=pltpu.SMEM),
                  pl.BlockSpec((tm * TOP_K,), lambda i: (jnp.minimum(i + 1, n_steps - 1),),
                               memory_space=pltpu.SMEM),
                  pl.BlockSpec(memory_space=pl.ANY),
                  pl.BlockSpec((SUBLANES, tm), lambda i: (i, 0)),
                  pl.BlockSpec((tm, D_MODEL), lambda i: (i, 0)),
                  pl.BlockSpec((None, 1, D_MODEL), lambda i: (i // nt, 0, 0)),
                  pl.BlockSpec((1, D_MODEL), lambda i: (0, 0))],
        out_specs=pl.BlockSpec((tm, D_MODEL), lambda i: (i, 0)),
        out_shape=jax.ShapeDtypeStruct((n_tok, D_MODEL), F32),
        scratch_shapes=[pltpu.VMEM((2, TOP_K * tm * ROW_LINES, LANES), F32), pltpu.SemaphoreType.DMA((2,))],
        compiler_params=_cparams(("arbitrary",)),
        name="combine",
    )(dest_flat, dest_flat, ys, gates, x1, gf, nw)


def kernel(x, c, ctx, c_ctx, ada_w, ada_b, mix_pre_norm, mix_post_norm, w_in, w_out, ssd_conv_w, ssd_conv_b,
           ssd_dt_bias, ssd_a_log, ssd_d, ssd_norm, hg_lb, hg_norm, ffn_pre_norm, ffn_post_norm, router_w,
           router_b, moe_w1, moe_b1, moe_w2, moe_b2):
    bsz, t_len, d = x.shape
    assert ada_w.shape[0] == 1 and d == D_MODEL and bsz <= SUBLANES - 1
    n_tok = bsz * t_len

    cc = jnp.zeros((SUBLANES, d), F32).at[:bsz].set(c).at[bsz].set(c_ctx)
    mod = ada_call(cc, ada_w[0], ada_b[0][None, :])
    sh_m, sc_m, g_m, sh_f, sc_f, g_f = [m[:bsz, None, :] for m in jnp.split(mod, 6, axis=-1)]
    csh_m, csc_m = [jnp.broadcast_to(m[bsz][None, None, :], (bsz, 1, d)) for m in jnp.split(mod, 6, axis=-1)[:2]]

    w = w_in[0]
    wa = jnp.concatenate([w[:, :SSD_INNER + SSD_XBC], w[:, SSD_COLS + 4 * HG_INNER:]], axis=1).astype(BF16)
    wdt = jnp.zeros((d, LANES), F32).at[:, :2 * SSD_HEADS].set(w[:, SSD_INNER + SSD_XBC:SSD_COLS]).astype(BF16)
    wb = w[:, SSD_COLS:SSD_COLS + 4 * HG_INNER].astype(BF16)
    dtb = jnp.zeros((1, LANES), F32).at[0, :2 * SSD_HEADS].set(ssd_dt_bias[0].reshape(-1))
    nw = mix_pre_norm[0][None, :]
    cw, cb = ssd_conv_w[0], ssd_conv_b[0][None, :]
    neg_a = -jnp.exp(ssd_a_log[0].astype(F32)).reshape(1, 2 * SSD_HEADS)
    dskip = jnp.repeat(ssd_d[0], SSD_HEAD_DIM)[None, :]
    lbp = hg_lb.astype(F32).reshape(2, 2 * HG_INNER)

    _, cxbc, _, cdt, cdtT = inproj_a_call(ctx, csh_m, csc_m, nw, wa, wdt, cw, cb, dtb, tm=ctx.shape[1])
    cu = inproj_b_call(ctx, csh_m, csc_m, nw, wb, col_major=False)
    z_ssd = jnp.zeros((bsz, N_PAIRS, SSD_STATE, LANES), F32)
    z_hg = jnp.zeros((bsz, HG_HEADS, HG_DK, HG_DK), F32)
    _, _, ssf, ssb = ssd_scan_call(cxbc, cdt, cdtT, z_ssd, z_ssd, neg_a, neg_a.reshape(-1, 1), dskip)
    _, _, hsf, hsb = hg_scan_call(cu, lbp, z_hg, z_hg, row_major_out=False)

    zg, xbc, hgate, dt, dtT = inproj_a_call(x, sh_m, sc_m, nw, wa, wdt, cw, cb, dtb, tm=512)
    ub = inproj_b_call(x, sh_m, sc_m, nw, wb, col_major=True)
    sof, sob, _, _ = ssd_scan_call(xbc, dt, dtT, ssf, ssb, neg_a, neg_a.reshape(-1, 1), dskip)
    hof, hob, _, _ = hg_scan_call(ub, lbp, hsf, hsb, row_major_out=True)

    rw = jnp.zeros((d, LANES), F32).at[:, :N_EXPERTS].set(router_w[0])
    rwh = rw.astype(BF16)
    rwl = (rw - rwh.astype(F32)).astype(BF16)
    rb = jnp.zeros((1, LANES), F32).at[0, :N_EXPERTS].set(router_b[0])
    x1, hx, idx, rank, gates, cnt = post_call(
        x, sof, sob, zg, hof, hob, hgate, ssd_norm[0][None, :], hg_norm[0][None, :], w_out[0].astype(BF16),
        mix_post_norm[0][None, :], g_m, ffn_pre_norm[0][None, :], sh_f, sc_f, rwh, rwl, rb)

    r = MOE_ROWS
    n_blocks = (n_tok * TOP_K + N_EXPERTS * (r - 1) + r - 1) // r
    n_tiles = n_tok // TOK_TILE
    counts = cnt[:, 0].astype(jnp.int32)
    padded = (counts + r - 1) // r * r
    pend = jnp.cumsum(padded)
    pstart = pend - padded
    idx3 = idx.reshape(n_tiles, SUBLANES, TOK_TILE)[:, :TOP_K, :]
    rank3 = rank.reshape(n_tiles, SUBLANES, TOK_TILE)[:, :TOP_K, :]
    dest = rank3
    for e in range(N_EXPERTS):
        dest = dest + jnp.where(idx3 == e, pstart[e], 0)
    dest = dest.reshape(-1)
    starts = jnp.arange(n_blocks, dtype=jnp.int32) * r
    block_e = jnp.minimum(jnp.sum((pend[None, :] <= starts[:, None]).astype(jnp.int32), axis=1), N_EXPERTS - 1)
    n_used = (pend[-1:] // r).astype(jnp.int32)
    run_end = pend[block_e] // r
    next_e = jnp.where(run_end < n_used[0], block_e[jnp.minimum(run_end, n_blocks - 1)], -1).astype(jnp.int32)

    zstart = jnp.concatenate([pstart + counts, pend[-1:]])
    xs = dispatch_call(zstart, padded - counts, dest, hx, n_blocks * r)
    b1p = jnp.concatenate([moe_b1[0][:, 0::2], moe_b1[0][:, 1::2]], axis=-1)[:, None, :]
    ys = experts_call(block_e, n_used, next_e, xs, moe_w1[0], b1p, moe_w2[0], moe_b2[0][:, None, :])
    out = combine_call(dest, ys, gates, x1, g_f, ffn_post_norm[0][None, :], bsz)
    return out.reshape(bsz, t_len, d)
```

```python
import functools
import math

import numpy as np
import jax
import jax.numpy as jnp
from jax import lax
from jax.experimental import pallas as pl
from jax.experimental.pallas import tpu as pltpu

F32 = jnp.float32
BF16 = jnp.bfloat16
HIGHEST = lax.Precision.HIGHEST

D_MODEL = 1024
GRID_W = 64
SSD_HEADS = 8
SSD_HEAD_DIM = 64
SSD_INNER = 512
SSD_STATE = 64
SSD_GROUPS = 2
SSD_CONV = 5
SSD_XBC = 768
SSD_COLS = 1296
HG_HEADS = 4
HG_DK = 128
HG_INNER = 512
N_EXPERTS = 32
TOP_K = 4
D_FF = 1024
SWIGLU_ALPHA = 1.702
SWIGLU_LIMIT = 7.0
EPS = 1e-6

LANES = 128
SUBLANES = 8
VMEM_LIMIT = 56 * 1024 * 1024

SSD_CHUNK = 256
HG_CHUNK = 128
HG_LEVELS = 7
MOE_ROWS = 256
TOK_TILE = 256
LOG2E = 1.4426950408889634


def _cparams(sem):
    return pltpu.CompilerParams(dimension_semantics=sem, vmem_limit_bytes=VMEM_LIMIT)


def _sigmoid(x):
    return 1.0 / (1.0 + jnp.exp(-x))


def _silu(x):
    return x * _sigmoid(x)


def _rms(x, w):
    return x * lax.rsqrt(jnp.mean(x * x, axis=-1, keepdims=True) + EPS) * w


def _split3(v):
    hi = v.astype(BF16)
    r1 = v - hi.astype(F32)
    mid = r1.astype(BF16)
    lo = (r1 - mid.astype(F32)).astype(BF16)
    return hi, mid, lo


def _ada_kernel(c_ref, w_ref, b_ref, o_ref):
    s = _silu(c_ref[...])
    o_ref[...] = jnp.dot(s, w_ref[...], precision=HIGHEST, preferred_element_type=F32) + b_ref[...]


def ada_call(cc, w, b):
    n = w.shape[1]
    tn = 1536
    return pl.pallas_call(
        _ada_kernel,
        grid=(n // tn,),
        in_specs=[pl.BlockSpec((SUBLANES, D_MODEL), lambda j: (0, 0)),
                  pl.BlockSpec((D_MODEL, tn), lambda j: (0, j)),
                  pl.BlockSpec((1, tn), lambda j: (0, j))],
        out_specs=pl.BlockSpec((SUBLANES, tn), lambda j: (0, j)),
        out_shape=jax.ShapeDtypeStruct((SUBLANES, n), F32),
        compiler_params=_cparams(("arbitrary",)),
        name="ada",
    )(cc, w, b)


A_COLS = SSD_INNER + SSD_XBC + HG_INNER


def _prep(xt, nw, sh, sc):
    return (_rms(xt, nw) * (1.0 + sc) + sh).astype(BF16)


def _inproj_a_kernel(x_ref, xp_ref, xn_ref, sh_ref, sc_ref, nw_ref, wa_ref, wdt_ref, cw_ref, cb_ref, dtb_ref,
                     z_ref, xbc_ref, g_ref, dt_ref, dtT_ref, scr, *, tm):
    i = pl.program_id(1)
    last = pl.num_programs(1) - 1
    nw, sh, sc = nw_ref[...], sh_ref[...], sc_ref[...]
    h = _prep(x_ref[...], nw, sh, sc)
    ua = jnp.dot(h, wa_ref[...], preferred_element_type=F32)
    z_ref[...] = ua[:, :SSD_INNER]
    g_ref[...] = ua[:, SSD_INNER + SSD_XBC:]
    wx = wa_ref[:, SSD_INNER:SSD_INNER + SSD_XBC]
    up = jnp.dot(_prep(xp_ref[...], nw, sh, sc), wx, preferred_element_type=F32)
    un = jnp.dot(_prep(xn_ref[...], nw, sh, sc), wx, preferred_element_type=F32)
    scr[0:SUBLANES, :] = jnp.where(i > 0, up, 0.0)
    scr[SUBLANES:SUBLANES + tm, :] = ua[:, SSD_INNER:SSD_INNER + SSD_XBC]
    scr[SUBLANES + tm:, :] = jnp.where(i < last, un, 0.0)
    acc = jnp.broadcast_to(cb_ref[...], (tm, SSD_XBC))
    pad = SSD_CONV // 2
    for k in range(SSD_CONV):
        off = SUBLANES - pad + k
        acc = acc + cw_ref[k:k + 1, :] * scr[off:off + tm, :]
    xbc_ref[...] = _silu(acc)
    draw = jnp.dot(h, wdt_ref[...], preferred_element_type=F32) + dtb_ref[...]
    dt = jnp.maximum(draw, 0.0) + jnp.log(1.0 + jnp.exp(-jnp.abs(draw)))
    dt_ref[...] = dt[:, :2 * SSD_HEADS]
    dtT_ref[...] = dt.T[:2 * SSD_HEADS, :]


def inproj_a_call(x, sh, sc, nw, wa, wdt, cw, cb, dtb, tm):
    bsz, t_len, _ = x.shape
    nt = t_len // tm
    r8 = tm // SUBLANES
    n8 = t_len // SUBLANES
    full = lambda shape: pl.BlockSpec(shape, lambda b, i: (0,) * len(shape))
    tok = lambda c: pl.BlockSpec((None, tm, c), lambda b, i: (b, i, 0))
    return pl.pallas_call(
        functools.partial(_inproj_a_kernel, tm=tm),
        grid=(bsz, nt),
        in_specs=[tok(D_MODEL),
                  pl.BlockSpec((None, SUBLANES, D_MODEL), lambda b, i: (b, jnp.maximum(i * r8 - 1, 0), 0)),
                  pl.BlockSpec((None, SUBLANES, D_MODEL), lambda b, i: (b, jnp.minimum((i + 1) * r8, n8 - 1), 0)),
                  pl.BlockSpec((None, 1, D_MODEL), lambda b, i: (b, 0, 0)),
                  pl.BlockSpec((None, 1, D_MODEL), lambda b, i: (b, 0, 0)),
                  full((1, D_MODEL)), full((D_MODEL, A_COLS)), full((D_MODEL, LANES)),
                  full((SSD_CONV, SSD_XBC)), full((1, SSD_XBC)), full((1, LANES))],
        out_specs=[tok(SSD_INNER), tok(SSD_XBC), tok(HG_INNER), tok(2 * SSD_HEADS),
                   pl.BlockSpec((None, 2 * SSD_HEADS, tm), lambda b, i: (b, 0, i))],
        out_shape=[jax.ShapeDtypeStruct((bsz, t_len, SSD_INNER), F32),
                   jax.ShapeDtypeStruct((bsz, t_len, SSD_XBC), F32),
                   jax.ShapeDtypeStruct((bsz, t_len, HG_INNER), F32),
                   jax.ShapeDtypeStruct((bsz, t_len, 2 * SSD_HEADS), F32),
                   jax.ShapeDtypeStruct((bsz, 2 * SSD_HEADS, t_len), F32)],
        scratch_shapes=[pltpu.VMEM((tm + 2 * SUBLANES, SSD_XBC), F32)],
        compiler_params=_cparams(("arbitrary", "arbitrary")),
        name="inproj_a",
    )(x, x, x, sh, sc, nw, wa, wdt, cw, cb, dtb)


B_COLS = 4 * HG_INNER


def _inproj_b_kernel(x_ref, sh_ref, sc_ref, nw_ref, w_ref, o_ref, *, ncol):
    if ncol:
        xt = jnp.concatenate([x_ref[:, w * D_MODEL:(w + 1) * D_MODEL] for w in range(ncol)], axis=0)
    else:
        xt = x_ref[...]
    h = _prep(xt, nw_ref[...], sh_ref[...], sc_ref[...])
    o_ref[...] = jnp.dot(h, w_ref[...], preferred_element_type=F32)


def inproj_b_call(x, sh, sc, nw, wb, col_major):
    bsz, t_len, _ = x.shape
    if col_major:
        ncol = 8
        tm = ncol * GRID_W
        rows = t_len // GRID_W
        assert rows == GRID_W
        xin = x.reshape(bsz, rows, GRID_W * D_MODEL)
        x_spec = pl.BlockSpec((None, rows, ncol * D_MODEL), lambda b, i: (b, 0, i))
    else:
        ncol = 0
        tm = t_len
        xin = x
        x_spec = pl.BlockSpec((None, tm, D_MODEL), lambda b, i: (b, i, 0))
    full = lambda shape: pl.BlockSpec(shape, lambda b, i: (0,) * len(shape))
    return pl.pallas_call(
        functools.partial(_inproj_b_kernel, ncol=ncol),
        grid=(bsz, t_len // tm),
        in_specs=[x_spec,
                  pl.BlockSpec((None, 1, D_MODEL), lambda b, i: (b, 0, 0)),
                  pl.BlockSpec((None, 1, D_MODEL), lambda b, i: (b, 0, 0)),
                  full((1, D_MODEL)), full((D_MODEL, B_COLS))],
        out_specs=pl.BlockSpec((None, tm, B_COLS), lambda b, i: (b, i, 0)),
        out_shape=jax.ShapeDtypeStruct((bsz, t_len, B_COLS), F32),
        compiler_params=_cparams(("arbitrary", "arbitrary")),
        name="inproj_b",
    )(xin, sh, sc, nw, wb)


N_PAIRS = SSD_HEADS // 2


def _ssd_dir(xbc, dt, dtT, s_ref, tri, triT, na_row, na_col, fwd, need_out=True):
    c = SSD_CHUNK
    xs = xbc[:, :SSD_INNER]
    bm = xbc[:, SSD_INNER:SSD_INNER + LANES]
    cm = xbc[:, SSD_INNER + LANES:]
    col0 = 0 if fwd else SSD_HEADS
    la = dt[:, col0:col0 + SSD_HEADS] * na_row[:, col0:col0 + SSD_HEADS]
    dtr = dtT[col0:col0 + SSD_HEADS, :]
    laT = dtr * na_col[col0:col0 + SSD_HEADS, :]
    g = sum(jnp.dot(tri, p, preferred_element_type=F32) for p in _split3(la))
    gT = sum(jnp.dot(p, triT, preferred_element_type=F32) for p in _split3(laT))
    end = c - 1 if fwd else 0
    bmT = bm.T
    ii = lax.broadcasted_iota(jnp.int32, (c, c), 0)
    jj = lax.broadcasted_iota(jnp.int32, (c, c), 1)
    causal = (jj <= ii) if fwd else (jj >= ii)
    lane = lax.broadcasted_iota(jnp.int32, (c, LANES), 1)
    lo_half = lane < SSD_HEAD_DIM
    lane_s = lax.broadcasted_iota(jnp.int32, (SSD_STATE, LANES), 1) < SSD_HEAD_DIM
    outs = []
    for grp in range(SSD_GROUPS):
        if need_out:
            in_grp = (lane >= grp * SSD_STATE) & (lane < (grp + 1) * SSD_STATE)
            cm_g = jnp.where(in_grp, cm, 0.0).astype(BF16)
            gmat = jnp.dot(cm_g, bmT.astype(BF16), preferred_element_type=F32)
        bmT_g = bmT[grp * SSD_STATE:(grp + 1) * SSD_STATE, :]
        for pp in range(N_PAIRS // SSD_GROUPS):
            pair = grp * (N_PAIRS // SSD_GROUPS) + pp
            heads = (2 * pair, 2 * pair + 1)
            xs_p = xs[:, pair * LANES:(pair + 1) * LANES]
            xbd = jnp.concatenate([jnp.where(lo_half, xs_p, 0.0), jnp.where(lo_half, 0.0, xs_p)],
                                  axis=0).astype(BF16)
            ms, bws, ecols, arows = [], [], [], []
            for hd in heads:
                gcol = g[:, hd:hd + 1]
                grow = gT[hd:hd + 1, :]
                glast = grow[:, end:end + 1]
                bws.append(bmT_g * (dtr[hd:hd + 1, :] * jnp.exp(glast - grow)))
                arows.append(jnp.exp(glast))
                if need_out:
                    dec = jnp.exp(jnp.where(causal, gcol - grow, -jnp.inf))
                    ms.append(gmat * dec * dtr[hd:hd + 1, :])
                    ecols.append(jnp.exp(gcol))
            s_old = s_ref[pair]
            if need_out:
                mcat = jnp.concatenate(ms, axis=1).astype(BF16)
                zeros = jnp.zeros_like(s_old)
                s_pad = jnp.concatenate([s_old, zeros] if grp == 0 else [zeros, s_old], axis=0).astype(BF16)
                o_inter = jnp.dot(cm.astype(BF16), s_pad, preferred_element_type=F32)
                o_inter = o_inter * jnp.where(lo_half, ecols[0], ecols[1])
                outs.append(jnp.dot(mcat, xbd, preferred_element_type=F32) + o_inter)
            bw = jnp.concatenate(bws, axis=1).astype(BF16)
            s_ref[pair] = (s_old * jnp.where(lane_s, arows[0], arows[1])
                           + jnp.dot(bw, xbd, preferred_element_type=F32))
    return outs


def _ssd_scan_kernel(xf_ref, dtf_ref, dtTf_ref, xb_ref, dtb_ref, dtTb_ref, s0f_ref, s0b_ref,
                     trif_ref, trifT_ref, trib_ref, tribT_ref, nar_ref, nac_ref, dsk_ref, *rest, need_out):
    if need_out:
        of_ref, ob_ref, sfo_ref, sbo_ref, sf, sb = rest
    else:
        sfo_ref, sbo_ref, sf, sb = rest
    n = pl.program_id(1)

    @pl.when(n == 0)
    def _():
        sf[...] = s0f_ref[...]
        sb[...] = s0b_ref[...]

    xf = xf_ref[...]
    outs = _ssd_dir(xf, dtf_ref[...], dtTf_ref[...], sf, trif_ref[...], trifT_ref[...],
                    nar_ref[...], nac_ref[...], True, need_out)
    if need_out:
        of_ref[...] = jnp.concatenate(outs, axis=1) + dsk_ref[...] * xf[:, :SSD_INNER]
    outs = _ssd_dir(xb_ref[...], dtb_ref[...], dtTb_ref[...], sb, trib_ref[...], tribT_ref[...],
                    nar_ref[...], nac_ref[...], False, need_out)
    if need_out:
        ob_ref[...] = jnp.concatenate(outs, axis=1)

    @pl.when(n == pl.num_programs(1) - 1)
    def _():
        sfo_ref[...] = sf[...]
        sbo_ref[...] = sb[...]


def _ssd_consts():
    c = SSD_CHUNK
    i = np.arange(c)
    trif = (i[:, None] >= i[None, :]).astype(np.float32)
    trib = (i[:, None] <= i[None, :]).astype(np.float32)
    return [jnp.asarray(a, BF16) for a in (trif, trif.T, trib, trib.T)]


def ssd_scan_call(xbc, dt, dtT, s0f, s0b, na_row, na_col, dskip, need_out=True):
    bsz, t_len, _ = xbc.shape
    c = SSD_CHUNK
    nc = t_len // c
    fw = lambda w: pl.BlockSpec((None, c, w), lambda b, n: (b, n, 0))
    bw = lambda w: pl.BlockSpec((None, c, w), lambda b, n: (b, nc - 1 - n, 0))
    full = lambda shape: pl.BlockSpec(shape, lambda b, n: (0,) * len(shape))
    st = pl.BlockSpec((None, N_PAIRS, SSD_STATE, LANES), lambda b, n: (b, 0, 0, 0))
    st_shape = jax.ShapeDtypeStruct((bsz, N_PAIRS, SSD_STATE, LANES), F32)
    o_shape = jax.ShapeDtypeStruct((bsz, t_len, SSD_INNER), F32)
    res = pl.pallas_call(
        functools.partial(_ssd_scan_kernel, need_out=need_out),
        grid=(bsz, nc),
        in_specs=[fw(SSD_XBC), fw(2 * SSD_HEADS),
                  pl.BlockSpec((None, 2 * SSD_HEADS, c), lambda b, n: (b, 0, n)),
                  bw(SSD_XBC), bw(2 * SSD_HEADS),
                  pl.BlockSpec((None, 2 * SSD_HEADS, c), lambda b, n: (b, 0, nc - 1 - n)),
                  st, st, full((c, c)), full((c, c)), full((c, c)), full((c, c)),
                  full((1, 2 * SSD_HEADS)), full((2 * SSD_HEADS, 1)), full((1, SSD_INNER))],
        out_specs=([fw(SSD_INNER), bw(SSD_INNER)] if need_out else []) + [st, st],
        out_shape=([o_shape, o_shape] if need_out else []) + [st_shape, st_shape],
        scratch_shapes=[pltpu.VMEM((N_PAIRS, SSD_STATE, LANES), F32),
                        pltpu.VMEM((N_PAIRS, SSD_STATE, LANES), F32)],
        compiler_params=_cparams(("arbitrary", "arbitrary")),
        name="ssd_scan",
    )(xbc, dt, dtT, xbc, dt, dtT, s0f, s0b, *_ssd_consts(), na_row, na_col, dskip)
    return tuple(res) if need_out else (None, None) + tuple(res)


def _hg_consts():
    c = HG_CHUNK
    t = np.arange(c)
    m_f = np.zeros((HG_LEVELS + 1, c, c), np.float32)
    m_f[0] = np.eye(c)
    for lv in range(1, HG_LEVELS + 1):
        m = 2 ** lv
        blk = t // m
        right = (t % m) >= m // 2
        m_f[lv] = (blk[:, None] == blk[None, :]) & right[:, None] & (~right[None, :])
    m_b = np.transpose(m_f, (0, 2, 1))
    tri_f = (t[:, None] >= t[None, :]).astype(np.float32)
    tri_b = (t[:, None] <= t[None, :]).astype(np.float32)
    return jnp.asarray(tri_f, BF16), jnp.asarray(tri_b, BF16), jnp.asarray(m_f, F32), jnp.asarray(m_b, F32)


def _hg_dir(u, lb_row, st_ref, tri_ref, mask_ref, fwd, need_out=True):
    c = HG_CHUNK
    end = c - 1 if fwd else 0
    outs = []
    fcol = HG_INNER if fwd else 2 * HG_INNER
    nt = (((1,), (1,)), ((), ()))
    row = lax.broadcasted_iota(jnp.int32, (c, HG_DK), 0)

    def halves(lo, hi, half):
        if half % SUBLANES == 0:
            return jnp.concatenate([(hi if (s // half) % 2 else lo)[s:s + half] for s in range(0, c, half)], axis=0)
        return jnp.where(((row // half) % 2) == 1, hi, lo)

    def shift(x, s):
        s = s % c
        if s % SUBLANES == 0:
            return jnp.concatenate([x[c - s:], x[:c - s]], axis=0)
        return pltpu.roll(x, s, 0)

    def by_side(query_side, key_side, half):
        return halves(key_side, query_side, half) if fwd else halves(query_side, key_side, half)

    for hd in range(HG_HEADS):
        sl = slice(hd * HG_DK, (hd + 1) * HG_DK)
        lb = lb_row[:, sl]
        f = lb + (1.0 - lb) * _sigmoid(u[:, fcol + hd * HG_DK:fcol + (hd + 1) * HG_DK])
        k = 1.0 - f
        la = jnp.log(f)
        v = u[:, 3 * HG_INNER + hd * HG_DK:3 * HG_INNER + (hd + 1) * HG_DK]
        la_hi = la.astype(BF16)
        la_lo = (la - la_hi.astype(F32)).astype(BF16)
        g2 = jnp.dot(tri_ref[...], jnp.concatenate([la_hi, la_lo], axis=1), preferred_element_type=F32)
        g = g2[:, :HG_DK] + g2[:, HG_DK:]
        st = st_ref[hd]
        if not need_out:
            g_end = g[end:end + 1, :]
            st_ref[hd] = (st * jnp.exp(g_end)
                          + jnp.dot(v.T.astype(BF16), (k * jnp.exp(g_end - g)).astype(BF16),
                                    preferred_element_type=F32))
            continue
        q = _silu(u[:, sl]) * (HG_DK ** -0.5)
        scores = mask_ref[0] * lax.dot_general(q.astype(BF16), k.astype(BF16), nt, preferred_element_type=F32)
        fill = g
        for lv in range(1, HG_LEVELS + 1):
            half = 2 ** (lv - 1)
            if fwd:
                ref = halves(fill, shift(fill, half), half)
            else:
                ref = halves(shift(fill, -half), fill, half)
            decay = jnp.exp2(jnp.abs(g - ref) * (-LOG2E))
            y = (by_side(q, k, half) * decay).astype(BF16)
            scores = scores + mask_ref[lv] * lax.dot_general(y, y, nt, preferred_element_type=F32)
            if fwd:
                fill = halves(shift(fill, -half), fill, half)
            else:
                fill = halves(fill, shift(fill, half), half)
        x_cum = jnp.exp(g)
        x_rem = jnp.exp(fill - g)
        o = jnp.dot(scores.astype(BF16), v.astype(BF16), preferred_element_type=F32)
        o = o + lax.dot_general((q * x_cum).astype(BF16), st.astype(BF16), nt, preferred_element_type=F32)
        outs.append(o)
        st_ref[hd] = (st * x_cum[end:end + 1, :]
                      + jnp.dot(v.T.astype(BF16), (k * x_rem).astype(BF16), preferred_element_type=F32))
    return jnp.concatenate(outs, axis=1) if need_out else None


def _hg_scan_kernel(uf_ref, ub_ref, lbp_ref, s0f_ref, s0b_ref, trif_ref, trib_ref, mf_ref, mb_ref,
                    *rest, img_rows, need_out):
    if need_out:
        of_ref, ob_ref, sfo_ref, sbo_ref, sf, sb = rest
    else:
        sfo_ref, sbo_ref, sf, sb = rest
    n = pl.program_id(1)

    def put(o_ref, o):
        if img_rows:
            for j in range(HG_CHUNK // img_rows):
                o_ref[:, j * HG_INNER:(j + 1) * HG_INNER] = o[j * img_rows:(j + 1) * img_rows]
        else:
            o_ref[...] = o

    @pl.when(n == 0)
    def _():
        sf[...] = s0f_ref[...]
        sb[...] = s0b_ref[...]

    p = lbp_ref[...]
    mx = jnp.max(p, axis=0, keepdims=True)
    e = jnp.exp(p - mx)
    lb = e[0:1, :] / jnp.sum(e, axis=0, keepdims=True)
    o_f = _hg_dir(uf_ref[...], lb[:, :HG_INNER], sf, trif_ref, mf_ref, True, need_out)
    o_b = _hg_dir(ub_ref[...], lb[:, HG_INNER:], sb, trib_ref, mb_ref, False, need_out)
    if need_out:
        put(of_ref, o_f)
        put(ob_ref, o_b)

    @pl.when(n == pl.num_programs(1) - 1)
    def _():
        sfo_ref[...] = sf[...]
        sbo_ref[...] = sb[...]


def hg_scan_call(u, lbp, s0f, s0b, row_major_out, need_out=True):
    bsz, t_len, _ = u.shape
    c = HG_CHUNK
    nc = t_len // c
    full = lambda shape: pl.BlockSpec(shape, lambda b, n: (0,) * len(shape))
    st = pl.BlockSpec((None, HG_HEADS, HG_DK, HG_DK), lambda b, n: (b, 0, 0, 0))
    st_shape = jax.ShapeDtypeStruct((bsz, HG_HEADS, HG_DK, HG_DK), F32)
    if row_major_out:
        img_rows = t_len // GRID_W
        cols = c // img_rows
        assert cols * img_rows == c
        o_shape = jax.ShapeDtypeStruct((bsz, img_rows, GRID_W * HG_INNER), F32)
        of_spec = pl.BlockSpec((None, img_rows, cols * HG_INNER), lambda b, n: (b, 0, n))
        ob_spec = pl.BlockSpec((None, img_rows, cols * HG_INNER), lambda b, n: (b, 0, nc - 1 - n))
    else:
        img_rows = 0
        o_shape = jax.ShapeDtypeStruct((bsz, t_len, HG_INNER), F32)
        of_spec = pl.BlockSpec((None, c, HG_INNER), lambda b, n: (b, n, 0))
        ob_spec = pl.BlockSpec((None, c, HG_INNER), lambda b, n: (b, nc - 1 - n, 0))
    tri_f, tri_b, m_f, m_b = _hg_consts()
    res = pl.pallas_call(
        functools.partial(_hg_scan_kernel, img_rows=img_rows, need_out=need_out),
        grid=(bsz, nc),
        in_specs=[pl.BlockSpec((None, c, B_COLS), lambda b, n: (b, n, 0)),
                  pl.BlockSpec((None, c, B_COLS), lambda b, n: (b, nc - 1 - n, 0)),
                  full((2, 2 * HG_INNER)), st, st,
                  full(tri_f.shape), full(tri_b.shape), full(m_f.shape), full(m_b.shape)],
        out_specs=([of_spec, ob_spec] if need_out else []) + [st, st],
        out_shape=([o_shape, o_shape] if need_out else []) + [st_shape, st_shape],
        scratch_shapes=[pltpu.VMEM((HG_HEADS, HG_DK, HG_DK), F32), pltpu.VMEM((HG_HEADS, HG_DK, HG_DK), F32)],
        compiler_params=_cparams(("arbitrary", "arbitrary")),
        name="hg_scan",
    )(u, u, lbp, s0f, s0b, tri_f, tri_b, m_f, m_b)
    if not need_out:
        return (None, None) + tuple(res)
    of, ob, sfo, sbo = res
    if row_major_out:
        of = of.reshape(bsz, t_len, HG_INNER)
        ob = ob.reshape(bsz, t_len, HG_INNER)
    return of, ob, sfo, sbo


ROW_LINES = D_MODEL // LANES


def _to_token_tiles(ref, val, n_rows):
    for c in range(ROW_LINES):
        ref[pl.ds(c, n_rows, stride=ROW_LINES), :] = val[:, c * LANES:(c + 1) * LANES]


def _from_token_tiles(ref, n_rows, first_row=0):
    return jnp.concatenate([ref[pl.ds(first_row * ROW_LINES + c, n_rows, stride=ROW_LINES), :]
                            for c in range(ROW_LINES)], axis=1)


def _tile_of(ref, row):
    return ref.at[pl.ds(pl.multiple_of(row * ROW_LINES, ROW_LINES), ROW_LINES)]


def _post_kernel(x_ref, sof_ref, sob_ref, z_ref, hof_ref, hob_ref, hg_ref,
                 snw_ref, hnw_ref, wo_ref, pnw_ref, gm_ref, fnw_ref, shf_ref, scf_ref, rwh_ref, rwl_ref, rb_ref,
                 tri_ref, x1_ref, hx_ref, idx_ref, rank_ref, gate_ref, cnt_ref, carry, *, tm):
    first = (pl.program_id(0) == 0) & (pl.program_id(1) == 0)

    @pl.when(first)
    def _():
        carry[...] = jnp.zeros_like(carry)

    y = (sof_ref[...] + sob_ref[...]) * _silu(z_ref[...])
    y = _rms(y, snw_ref[...])
    o = hof_ref[...] + hob_ref[...]
    hnw = hnw_ref[...]
    o = jnp.concatenate([_rms(o[:, h * HG_DK:(h + 1) * HG_DK], hnw[:, h * HG_DK:(h + 1) * HG_DK])
                         for h in range(HG_HEADS)], axis=1)
    o = o * _silu(hg_ref[...])
    mix = (jnp.dot(y.astype(BF16), wo_ref[:SSD_INNER, :], preferred_element_type=F32)
           + jnp.dot(o.astype(BF16), wo_ref[SSD_INNER:, :], preferred_element_type=F32))
    x1 = x_ref[...] + gm_ref[...] * _rms(mix, pnw_ref[...])
    x1_ref[...] = x1
    hx = _rms(x1, fnw_ref[...]) * (1.0 + scf_ref[...]) + shf_ref[...]
    _to_token_tiles(hx_ref, hx, tm)
    hx_hi = hx.astype(BF16)
    hx_lo = (hx - hx_hi.astype(F32)).astype(BF16)
    rwh = rwh_ref[...]
    logits = (jnp.dot(hx_hi, rwh, preferred_element_type=F32) + jnp.dot(hx_lo, rwh, preferred_element_type=F32)
              + jnp.dot(hx_hi, rwl_ref[...], preferred_element_type=F32)) + rb_ref[...]
    work = logits.T[:N_EXPERTS, :]
    erow = lax.broadcasted_iota(jnp.int32, (N_EXPERTS, tm), 0)
    vals, idxs = [], []
    for _ in range(TOP_K):
        m = jnp.max(work, axis=0, keepdims=True)
        ix = jnp.min(jnp.where(work == m, erow, N_EXPERTS), axis=0, keepdims=True)
        vals.append(m)
        idxs.append(ix)
        work = jnp.where(erow == ix, -jnp.inf, work)
    es = [jnp.exp(v - vals[0]) for v in vals]
    den = es[0] + es[1] + es[2] + es[3]
    onehots = [(erow == ix) for ix in idxs]
    multi = sum(oh.astype(F32) for oh in onehots)
    before = jnp.dot(multi.astype(BF16), tri_ref[...], preferred_element_type=F32) + carry[...]
    carry[...] = carry[...] + jnp.sum(multi, axis=1, keepdims=True)
    sub = lax.broadcasted_iota(jnp.int32, (SUBLANES, tm), 0)
    idx_o = jnp.zeros((SUBLANES, tm), jnp.int32)
    rank_o = jnp.zeros((SUBLANES, tm), jnp.int32)
    gate_o = jnp.zeros((SUBLANES, tm), F32)
    for k in range(TOP_K):
        rk = jnp.sum(jnp.where(onehots[k], before, 0.0), axis=0, keepdims=True)
        idx_o = jnp.where(sub == k, idxs[k], idx_o)
        rank_o = jnp.where(sub == k, rk.astype(jnp.int32), rank_o)
        gate_o = jnp.where(sub == k, es[k] / den, gate_o)
    idx_ref[...] = idx_o
    rank_ref[...] = rank_o
    gate_ref[...] = gate_o
    cnt_ref[...] = jnp.broadcast_to(carry[...], (N_EXPERTS, LANES))


def post_call(x, sof, sob, z, hof, hob, hg, snw, hnw, wo, pnw, gm, fnw, shf, scf, rwh, rwl, rb):
    bsz, t_len, _ = x.shape
    tm = TOK_TILE
    nt = t_len // tm
    n_tok = bsz * t_len
    tok = lambda c: pl.BlockSpec((None, tm, c), lambda b, i: (b, i, 0))
    full = lambda shape: pl.BlockSpec(shape, lambda b, i: (0,) * len(shape))
    per_b = pl.BlockSpec((None, 1, D_MODEL), lambda b, i: (b, 0, 0))
    flat = lambda c: pl.BlockSpec((tm, c), lambda b, i: (b * nt + i, 0))
    rout = pl.BlockSpec((SUBLANES, tm), lambda b, i: (b * nt + i, 0))
    n_tiles = n_tok // tm
    ii = np.arange(tm)
    tri = jnp.asarray(ii[:, None] < ii[None, :], BF16)
    return pl.pallas_call(
        functools.partial(_post_kernel, tm=tm),
        grid=(bsz, nt),
        in_specs=[tok(D_MODEL), tok(SSD_INNER), tok(SSD_INNER), tok(SSD_INNER), tok(HG_INNER), tok(HG_INNER),
                  tok(HG_INNER), full((1, SSD_INNER)), full((1, HG_INNER)), full((D_MODEL, D_MODEL)),
                  full((1, D_MODEL)), per_b, full((1, D_MODEL)), per_b, per_b,
                  full((D_MODEL, LANES)), full((D_MODEL, LANES)), full((1, LANES)), full((tm, tm))],
        out_specs=[flat(D_MODEL), pl.BlockSpec((tm * ROW_LINES, LANES), lambda b, i: (b * nt + i, 0)),
                   rout, rout, rout, pl.BlockSpec((N_EXPERTS, LANES), lambda b, i: (0, 0))],
        out_shape=[jax.ShapeDtypeStruct((n_tok, D_MODEL), F32),
                   jax.ShapeDtypeStruct((n_tok * ROW_LINES, LANES), F32),
                   jax.ShapeDtypeStruct((n_tiles * SUBLANES, tm), jnp.int32),
                   jax.ShapeDtypeStruct((n_tiles * SUBLANES, tm), jnp.int32),
                   jax.ShapeDtypeStruct((n_tiles * SUBLANES, tm), F32),
                   jax.ShapeDtypeStruct((N_EXPERTS, LANES), F32)],
        scratch_shapes=[pltpu.VMEM((N_EXPERTS, 1), F32)],
        compiler_params=_cparams(("arbitrary", "arbitrary")),
        name="post",
    )(x, sof, sob, z, hof, hob, hg, snw, hnw, wo, pnw, gm, fnw, shf, scf, rwh, rwl, rb, tri)


def _wait_rows(hbm_ref, n_rows, sem):
    n = n_rows * ROW_LINES
    pltpu.make_async_copy(hbm_ref.at[pl.ds(0, n)], hbm_ref.at[pl.ds(0, n)], sem).wait()


PAD_RUNS = tuple(2 ** j for j in range(int(math.log2(MOE_ROWS))))
DISPATCH_TILES = 4


def _dispatch_kernel(zstart_ref, zpad_ref, dest_ref, hx_ref, buf_ref, zrows, sem, zsem, *, tm):
    i = pl.program_id(0)

    def zero_rows(first, n):
        first = pl.multiple_of(first * ROW_LINES, ROW_LINES)
        return pltpu.make_async_copy(zrows.at[pl.ds(0, n * ROW_LINES)], buf_ref.at[pl.ds(first, n * ROW_LINES)], zsem)

    def for_pad_runs(act):
        def per_expert(e, carry):
            pad = zpad_ref[e]
            off = zstart_ref[e]
            for run in PAD_RUNS:
                @pl.when((pad & run) != 0)
                def _():
                    act(zero_rows(off, run))
                off = off + (pad & run)
            return carry

        lax.fori_loop(0, N_EXPERTS, per_expert, 0)

        def tail(j, carry):
            act(zero_rows(j * PAD_RUNS[-1], PAD_RUNS[-1]))
            return carry

        n_rows = buf_ref.shape[0] // ROW_LINES
        lax.fori_loop(zstart_ref[N_EXPERTS] // PAD_RUNS[-1], n_rows // PAD_RUNS[-1], tail, 0)

    @pl.when(i == 0)
    def _():
        zrows[...] = jnp.zeros_like(zrows)
        for_pad_runs(lambda cp: cp.start())

    def issue(t, carry):
        for sub in range(DISPATCH_TILES):
            for k in range(TOP_K):
                pltpu.make_async_copy(_tile_of(hx_ref, sub * tm + t),
                                      _tile_of(buf_ref, dest_ref[(sub * TOP_K + k) * tm + t]),
                                      sem).start(priority=k % 2)
        return carry

    lax.fori_loop(0, tm, issue, 0, unroll=2)
    _wait_rows(buf_ref, DISPATCH_TILES * tm * TOP_K, sem)

    @pl.when(i == 0)
    def _():
        for_pad_runs(lambda cp: cp.wait())


def dispatch_call(zstart, zpad, dest_flat, hx, n_rows):
    n_tok = hx.shape[0] // ROW_LINES
    tm = TOK_TILE
    step_tok = DISPATCH_TILES * tm
    return pl.pallas_call(
        functools.partial(_dispatch_kernel, tm=tm),
        grid=(n_tok // step_tok,),
        in_specs=[pl.BlockSpec(memory_space=pltpu.SMEM), pl.BlockSpec(memory_space=pltpu.SMEM),
                  pl.BlockSpec((step_tok * TOP_K,), lambda i: (i,), memory_space=pltpu.SMEM),
                  pl.BlockSpec((step_tok * ROW_LINES, LANES), lambda i: (i, 0))],
        out_specs=pl.BlockSpec(memory_space=pl.ANY),
        out_shape=jax.ShapeDtypeStruct((n_rows * ROW_LINES, LANES), F32),
        scratch_shapes=[pltpu.VMEM((PAD_RUNS[-1] * ROW_LINES, LANES), F32), pltpu.SemaphoreType.DMA(()),
                        pltpu.SemaphoreType.DMA(())],
        compiler_params=_cparams(("arbitrary",)),
        name="dispatch",
    )(zstart, zpad, dest_flat, hx)


W1_TCOLS = 256


def _experts_kernel(be_ref, nu_ref, nxt_ref, x_ref, w1_hbm, b1_ref, w2_hbm, b2_ref, y_ref,
                    w1f, w2f, w1t, w2s, tbuf, wsem):
    i = pl.program_id(0)
    live = i < nu_ref[0]
    new_expert = (i == 0) | (be_ref[i] != be_ref[jnp.maximum(i - 1, 0)])

    def weight_copies(e):
        return (pltpu.make_async_copy(w1_hbm.at[e], w1f, wsem.at[0]),
                pltpu.make_async_copy(w2_hbm.at[e], w2f, wsem.at[1]))

    @pl.when(live & new_expert)
    def _():
        @pl.when(i == 0)
        def _():
            for cp in weight_copies(be_ref[0]):
                cp.start()

        for cp in weight_copies(be_ref[i]):
            cp.wait()
        half = W1_TCOLS // 2
        for c in range(2 * D_FF // W1_TCOLS):
            for j in range(D_MODEL // LANES):
                ks = slice(j * LANES, (j + 1) * LANES)
                tbuf[j] = w1f[ks, c * W1_TCOLS:(c + 1) * W1_TCOLS].T
                w1t[c * half:(c + 1) * half, ks] = tbuf[j, pl.ds(0, half, stride=2), :].astype(BF16)
                w1t[D_FF + c * half:D_FF + (c + 1) * half, ks] = tbuf[j, pl.ds(1, half, stride=2), :].astype(BF16)
        w2s[...] = w2f[...].astype(BF16)

        @pl.when(nxt_ref[i] >= 0)
        def _():
            for cp in weight_copies(nxt_ref[i]):
                cp.start(priority=1)

    @pl.when(live)
    def _():
        xb = _from_token_tiles(x_ref, MOE_ROWS).astype(BF16)
        u = lax.dot_general(xb, w1t[...], (((1,), (1,)), ((), ())), preferred_element_type=F32) + b1_ref[...]
        glu = jnp.minimum(u[:, :D_FF], SWIGLU_LIMIT)
        lin = jnp.clip(u[:, D_FF:], -SWIGLU_LIMIT, SWIGLU_LIMIT)
        a = glu * _sigmoid(SWIGLU_ALPHA * glu) * (lin + 1.0)
        y = jnp.dot(a.astype(BF16), w2s[...], preferred_element_type=F32) + b2_ref[...]
        _to_token_tiles(y_ref, y, MOE_ROWS)

    @pl.when(jnp.logical_not(live))
    def _():
        y_ref[...] = jnp.zeros_like(y_ref)


def experts_call(block_e, n_used, next_e, xs, w1, b1p, w2, b2):
    rows = xs.shape[0] // ROW_LINES
    r = MOE_ROWS
    nb = rows // r
    blk = (r * ROW_LINES, LANES)
    grid_spec = pltpu.PrefetchScalarGridSpec(
        num_scalar_prefetch=3,
        grid=(nb,),
        in_specs=[pl.BlockSpec(blk, lambda i, be, nu, nx: (jnp.maximum(jnp.minimum(i, nu[0] - 1), 0), 0)),
                  pl.BlockSpec(memory_space=pl.ANY),
                  pl.BlockSpec((None, 1, 2 * D_FF), lambda i, be, nu, nx: (be[i], 0, 0)),
                  pl.BlockSpec(memory_space=pl.ANY),
                  pl.BlockSpec((None, 1, D_MODEL), lambda i, be, nu, nx: (be[i], 0, 0))],
        out_specs=pl.BlockSpec(blk, lambda i, be, nu, nx: (i, 0)),
        scratch_shapes=[pltpu.VMEM((D_MODEL, 2 * D_FF), F32), pltpu.VMEM((D_FF, D_MODEL), F32),
                        pltpu.VMEM((2 * D_FF, D_MODEL), BF16), pltpu.VMEM((D_FF, D_MODEL), BF16),
                        pltpu.VMEM((D_MODEL // LANES, W1_TCOLS, LANES), F32),
                        pltpu.SemaphoreType.DMA((2,))],
    )
    return pl.pallas_call(
        _experts_kernel,
        grid_spec=grid_spec,
        out_shape=jax.ShapeDtypeStruct((rows * ROW_LINES, LANES), F32),
        compiler_params=_cparams(("arbitrary",)),
        name="experts",
    )(block_e, n_used, next_e, xs, w1, b1p, w2, b2)


def _combine_kernel(dest_ref, dest_next_ref, y_ref, gate_ref, x1_ref, gf_ref, nw_ref, o_ref, buf, sem, *, tm):
    i = pl.program_id(0)
    slot = i % 2

    def gather_tile(dref, into):
        def issue(t, carry):
            for k in range(TOP_K):
                pltpu.make_async_copy(_tile_of(y_ref, dref[k * tm + t]), _tile_of(buf.at[into], k * tm + t),
                                      sem.at[into]).start(priority=k % 2)
            return carry

        lax.fori_loop(0, tm, issue, 0, unroll=4)

    @pl.when(i == 0)
    def _():
        gather_tile(dest_ref, 0)

    @pl.when(i + 1 < pl.num_programs(0))
    def _():
        gather_tile(dest_next_ref, 1 - slot)

    _wait_rows(y_ref, tm * TOP_K, sem.at[slot])
    gate = gate_ref[...].T
    fx = None
    for k in range(TOP_K):
        yk = jnp.concatenate([buf[slot, pl.ds(k * tm * ROW_LINES + c, tm, stride=ROW_LINES), :]
                              for c in range(ROW_LINES)], axis=1)
        fx = gate[:, k:k + 1] * yk if fx is None else fx + gate[:, k:k + 1] * yk
    o_ref[...] = x1_ref[...] + gf_ref[...] * _rms(fx, nw_ref[...])


def combine_call(dest_flat, ys, gates, x1, gf, nw, bsz):
    n_tok = x1.shape[0]
    tm = TOK_TILE
    nt = n_tok // bsz // tm
    n_steps = n_tok // tm
    return pl.pallas_call(
        functools.partial(_combine_kernel, tm=tm),
        grid=(n_steps,),
        in_specs=[pl.BlockSpec((tm * TOP_K,), lambda i: (i,), memory_space=pltpu.SMEM),
                  pl.BlockSpec((tm * TOP_K,), lambda i: (jnp.minimum(i + 1, n_steps - 1),),
                               memory_space=pltpu.SMEM),
                  pl.BlockSpec(memory_space=pl.ANY),
                  pl.BlockSpec((SUBLANES, tm), lambda i: (i, 0)),
                  pl.BlockSpec((tm, D_MODEL), lambda i: (i, 0)),
                  pl.BlockSpec((None, 1, D_MODEL), lambda i: (i // nt, 0, 0)),
                  pl.BlockSpec((1, D_MODEL), lambda i: (0, 0))],
        out_specs=pl.BlockSpec((tm, D_MODEL), lambda i: (i, 0)),
        out_shape=jax.ShapeDtypeStruct((n_tok, D_MODEL), F32),
        scratch_shapes=[pltpu.VMEM((2, TOP_K * tm * ROW_LINES, LANES), F32), pltpu.SemaphoreType.DMA((2,))],
        compiler_params=_cparams(("arbitrary",)),
        name="combine",
    )(dest_flat, dest_flat, ys, gates, x1, gf, nw)


def kernel(x, c, ctx, c_ctx, ada_w, ada_b, mix_pre_norm, mix_post_norm, w_in, w_out, ssd_conv_w, ssd_conv_b,
           ssd_dt_bias, ssd_a_log, ssd_d, ssd_norm, hg_lb, hg_norm, ffn_pre_norm, ffn_post_norm, router_w,
           router_b, moe_w1, moe_b1, moe_w2, moe_b2):
    bsz, t_len, d = x.shape
    assert ada_w.shape[0] == 1 and d == D_MODEL and bsz <= SUBLANES - 1
    n_tok = bsz * t_len

    cc = jnp.zeros((SUBLANES, d), F32).at[:bsz].set(c).at[bsz].set(c_ctx)
    mod = ada_call(cc, ada_w[0], ada_b[0][None, :])
    sh_m, sc_m, g_m, sh_f, sc_f, g_f = [m[:bsz, None, :] for m in jnp.split(mod, 6, axis=-1)]
    csh_m, csc_m = [jnp.broadcast_to(m[bsz][None, None, :], (bsz, 1, d)) for m in jnp.split(mod, 6, axis=-1)[:2]]

    w = w_in[0]
    wa = jnp.concatenate([w[:, :SSD_INNER + SSD_XBC], w[:, SSD_COLS + 4 * HG_INNER:]], axis=1).astype(BF16)
    wdt = jnp.zeros((d, LANES), F32).at[:, :2 * SSD_HEADS].set(w[:, SSD_INNER + SSD_XBC:SSD_COLS]).astype(BF16)
    wb = w[:, SSD_COLS:SSD_COLS + 4 * HG_INNER].astype(BF16)
    dtb = jnp.zeros((1, LANES), F32).at[0, :2 * SSD_HEADS].set(ssd_dt_bias[0].reshape(-1))
    nw = mix_pre_norm[0][None, :]
    cw, cb = ssd_conv_w[0], ssd_conv_b[0][None, :]
    neg_a = -jnp.exp(ssd_a_log[0].astype(F32)).reshape(1, 2 * SSD_HEADS)
    dskip = jnp.repeat(ssd_d[0], SSD_HEAD_DIM)[None, :]
    lbp = hg_lb.astype(F32).reshape(2, 2 * HG_INNER)

    _, cxbc, _, cdt, cdtT = inproj_a_call(ctx, csh_m, csc_m, nw, wa, wdt, cw, cb, dtb, tm=ctx.shape[1])
    cu = inproj_b_call(ctx, csh_m, csc_m, nw, wb, col_major=False)
    z_ssd = jnp.zeros((bsz, N_PAIRS, SSD_STATE, LANES), F32)
    z_hg = jnp.zeros((bsz, HG_HEADS, HG_DK, HG_DK), F32)
    _, _, ssf, ssb = ssd_scan_call(cxbc, cdt, cdtT, z_ssd, z_ssd, neg_a, neg_a.reshape(-1, 1), dskip,
                                   need_out=False)
    _, _, hsf, hsb = hg_scan_call(cu, lbp, z_hg, z_hg, row_major_out=False, need_out=False)

    zg, xbc, hgate, dt, dtT = inproj_a_call(x, sh_m, sc_m, nw, wa, wdt, cw, cb, dtb, tm=512)
    ub = inproj_b_call(x, sh_m, sc_m, nw, wb, col_major=True)
    sof, sob, _, _ = ssd_scan_call(xbc, dt, dtT, ssf, ssb, neg_a, neg_a.reshape(-1, 1), dskip)
    hof, hob, _, _ = hg_scan_call(ub, lbp, hsf, hsb, row_major_out=True)

    rw = jnp.zeros((d, LANES), F32).at[:, :N_EXPERTS].set(router_w[0])
    rwh = rw.astype(BF16)
    rwl = (rw - rwh.astype(F32)).astype(BF16)
    rb = jnp.zeros((1, LANES), F32).at[0, :N_EXPERTS].set(router_b[0])
    x1, hx, idx, rank, gates, cnt = post_call(
        x, sof, sob, zg, hof, hob, hgate, ssd_norm[0][None, :], hg_norm[0][None, :], w_out[0].astype(BF16),
        mix_post_norm[0][None, :], g_m, ffn_pre_norm[0][None, :], sh_f, sc_f, rwh, rwl, rb)

    r = MOE_ROWS
    n_blocks = (n_tok * TOP_K + N_EXPERTS * (r - 1) + r - 1) // r
    n_tiles = n_tok // TOK_TILE
    counts = cnt[:, 0].astype(jnp.int32)
    padded = (counts + r - 1) // r * r
    pend = jnp.cumsum(padded)
    pstart = pend - padded
    idx3 = idx.reshape(n_tiles, SUBLANES, TOK_TILE)[:, :TOP_K, :]
    rank3 = rank.reshape(n_tiles, SUBLANES, TOK_TILE)[:, :TOP_K, :]
    dest = rank3
    for e in range(N_EXPERTS):
        dest = dest + jnp.where(idx3 == e, pstart[e], 0)
    dest = dest.reshape(-1)
    starts = jnp.arange(n_blocks, dtype=jnp.int32) * r
    block_e = jnp.minimum(jnp.sum((pend[None, :] <= starts[:, None]).astype(jnp.int32), axis=1), N_EXPERTS - 1)
    n_used = (pend[-1:] // r).astype(jnp.int32)
    run_end = pend[block_e] // r
    next_e = jnp.where(run_end < n_used[0], block_e[jnp.minimum(run_end, n_blocks - 1)], -1).astype(jnp.int32)

    zstart = jnp.concatenate([pstart + counts, pend[-1:]])
    xs = dispatch_call(zstart, padded - counts, dest, hx, n_blocks * r)
    b1p = jnp.concatenate([moe_b1[0][:, 0::2], moe_b1[0][:, 1::2]], axis=-1)[:, None, :]
    ys = experts_call(block_e, n_used, next_e, xs, moe_w1[0], b1p, moe_w2[0], moe_b2[0][:, None, :])
    out = combine_call(dest, ys, gates, x1, g_f, ffn_post_norm[0][None, :], bsz)
    return out.reshape(bsz, t_len, d)
```

```python
import functools
import math

import numpy as np
import jax
import jax.numpy as jnp
from jax import lax
from jax.experimental import pallas as pl
from jax.experimental.pallas import tpu as pltpu

F32 = jnp.float32
BF16 = jnp.bfloat16
HIGHEST = lax.Precision.HIGHEST

D_MODEL = 1024
GRID_W = 64
SSD_HEADS = 8
SSD_HEAD_DIM = 64
SSD_INNER = 512
SSD_STATE = 64
SSD_GROUPS = 2
SSD_CONV = 5
SSD_XBC = 768
SSD_COLS = 1296
HG_HEADS = 4
HG_DK = 128
HG_INNER = 512
N_EXPERTS = 32
TOP_K = 4
D_FF = 1024
SWIGLU_ALPHA = 1.702
SWIGLU_LIMIT = 7.0
EPS = 1e-6

LANES = 128
SUBLANES = 8
VMEM_LIMIT = 56 * 1024 * 1024

SSD_CHUNK = 256
HG_CHUNK = 128
HG_LEVELS = 7
MOE_ROWS = 256
TOK_TILE = 512
LOG2E = 1.4426950408889634


def _cparams(sem):
    return pltpu.CompilerParams(dimension_semantics=sem, vmem_limit_bytes=VMEM_LIMIT)


def _sigmoid(x):
    return 1.0 / (1.0 + jnp.exp(-x))


def _silu(x):
    return x * _sigmoid(x)


def _rms(x, w):
    return x * lax.rsqrt(jnp.mean(x * x, axis=-1, keepdims=True) + EPS) * w


def _split3(v):
    hi = v.astype(BF16)
    r1 = v - hi.astype(F32)
    mid = r1.astype(BF16)
    lo = (r1 - mid.astype(F32)).astype(BF16)
    return hi, mid, lo


def _ada_kernel(c_ref, w_ref, b_ref, o_ref):
    s = _silu(c_ref[...])
    o_ref[...] = jnp.dot(s, w_ref[...], precision=HIGHEST, preferred_element_type=F32) + b_ref[...]


def ada_call(cc, w, b):
    n = w.shape[1]
    tn = 1536
    return pl.pallas_call(
        _ada_kernel,
        grid=(n // tn,),
        in_specs=[pl.BlockSpec((SUBLANES, D_MODEL), lambda j: (0, 0)),
                  pl.BlockSpec((D_MODEL, tn), lambda j: (0, j)),
                  pl.BlockSpec((1, tn), lambda j: (0, j))],
        out_specs=pl.BlockSpec((SUBLANES, tn), lambda j: (0, j)),
        out_shape=jax.ShapeDtypeStruct((SUBLANES, n), F32),
        compiler_params=_cparams(("arbitrary",)),
        name="ada",
    )(cc, w, b)


A_COLS = SSD_INNER + SSD_XBC + HG_INNER


def _prep(xt, nw, sh, sc):
    return (_rms(xt, nw) * (1.0 + sc) + sh).astype(BF16)


def _inproj_a_kernel(x_ref, xp_ref, xn_ref, sh_ref, sc_ref, nw_ref, wa_ref, wdt_ref, cw_ref, cb_ref, dtb_ref,
                     z_ref, xbc_ref, g_ref, dt_ref, dtT_ref, scr, *, tm):
    i = pl.program_id(1)
    last = pl.num_programs(1) - 1
    nw, sh, sc = nw_ref[...], sh_ref[...], sc_ref[...]
    h = _prep(x_ref[...], nw, sh, sc)
    ua = jnp.dot(h, wa_ref[...], preferred_element_type=F32)
    z_ref[...] = ua[:, :SSD_INNER]
    g_ref[...] = ua[:, SSD_INNER + SSD_XBC:]
    wx = wa_ref[:, SSD_INNER:SSD_INNER + SSD_XBC]
    up = jnp.dot(_prep(xp_ref[...], nw, sh, sc), wx, preferred_element_type=F32)
    un = jnp.dot(_prep(xn_ref[...], nw, sh, sc), wx, preferred_element_type=F32)
    scr[0:SUBLANES, :] = jnp.where(i > 0, up, 0.0)
    scr[SUBLANES:SUBLANES + tm, :] = ua[:, SSD_INNER:SSD_INNER + SSD_XBC]
    scr[SUBLANES + tm:, :] = jnp.where(i < last, un, 0.0)
    acc = jnp.broadcast_to(cb_ref[...], (tm, SSD_XBC))
    pad = SSD_CONV // 2
    for k in range(SSD_CONV):
        off = SUBLANES - pad + k
        acc = acc + cw_ref[k:k + 1, :] * scr[off:off + tm, :]
    xbc_ref[...] = _silu(acc)
    draw = jnp.dot(h, wdt_ref[...], preferred_element_type=F32) + dtb_ref[...]
    dt = jnp.maximum(draw, 0.0) + jnp.log(1.0 + jnp.exp(-jnp.abs(draw)))
    dt_ref[...] = dt[:, :2 * SSD_HEADS]
    dtT_ref[...] = dt.T[:2 * SSD_HEADS, :]


def inproj_a_call(x, sh, sc, nw, wa, wdt, cw, cb, dtb, tm):
    bsz, t_len, _ = x.shape
    nt = t_len // tm
    r8 = tm // SUBLANES
    n8 = t_len // SUBLANES
    full = lambda shape: pl.BlockSpec(shape, lambda b, i: (0,) * len(shape))
    tok = lambda c: pl.BlockSpec((None, tm, c), lambda b, i: (b, i, 0))
    return pl.pallas_call(
        functools.partial(_inproj_a_kernel, tm=tm),
        grid=(bsz, nt),
        in_specs=[tok(D_MODEL),
                  pl.BlockSpec((None, SUBLANES, D_MODEL), lambda b, i: (b, jnp.maximum(i * r8 - 1, 0), 0)),
                  pl.BlockSpec((None, SUBLANES, D_MODEL), lambda b, i: (b, jnp.minimum((i + 1) * r8, n8 - 1), 0)),
                  pl.BlockSpec((None, 1, D_MODEL), lambda b, i: (b, 0, 0)),
                  pl.BlockSpec((None, 1, D_MODEL), lambda b, i: (b, 0, 0)),
                  full((1, D_MODEL)), full((D_MODEL, A_COLS)), full((D_MODEL, LANES)),
                  full((SSD_CONV, SSD_XBC)), full((1, SSD_XBC)), full((1, LANES))],
        out_specs=[tok(SSD_INNER), tok(SSD_XBC), tok(HG_INNER), tok(2 * SSD_HEADS),
                   pl.BlockSpec((None, 2 * SSD_HEADS, tm), lambda b, i: (b, 0, i))],
        out_shape=[jax.ShapeDtypeStruct((bsz, t_len, SSD_INNER), F32),
                   jax.ShapeDtypeStruct((bsz, t_len, SSD_XBC), F32),
                   jax.ShapeDtypeStruct((bsz, t_len, HG_INNER), F32),
                   jax.ShapeDtypeStruct((bsz, t_len, 2 * SSD_HEADS), F32),
                   jax.ShapeDtypeStruct((bsz, 2 * SSD_HEADS, t_len), F32)],
        scratch_shapes=[pltpu.VMEM((tm + 2 * SUBLANES, SSD_XBC), F32)],
        compiler_params=_cparams(("arbitrary", "arbitrary")),
        name="inproj_a",
    )(x, x, x, sh, sc, nw, wa, wdt, cw, cb, dtb)


B_COLS = 4 * HG_INNER


def _inproj_b_kernel(x_ref, sh_ref, sc_ref, nw_ref, w_ref, o_ref, *, ncol):
    if ncol:
        xt = jnp.concatenate([x_ref[:, w * D_MODEL:(w + 1) * D_MODEL] for w in range(ncol)], axis=0)
    else:
        xt = x_ref[...]
    h = _prep(xt, nw_ref[...], sh_ref[...], sc_ref[...])
    o_ref[...] = jnp.dot(h, w_ref[...], preferred_element_type=F32)


def inproj_b_call(x, sh, sc, nw, wb, col_major):
    bsz, t_len, _ = x.shape
    if col_major:
        ncol = 8
        tm = ncol * GRID_W
        rows = t_len // GRID_W
        assert rows == GRID_W
        xin = x.reshape(bsz, rows, GRID_W * D_MODEL)
        x_spec = pl.BlockSpec((None, rows, ncol * D_MODEL), lambda b, i: (b, 0, i))
    else:
        ncol = 0
        tm = t_len
        xin = x
        x_spec = pl.BlockSpec((None, tm, D_MODEL), lambda b, i: (b, i, 0))
    full = lambda shape: pl.BlockSpec(shape, lambda b, i: (0,) * len(shape))
    return pl.pallas_call(
        functools.partial(_inproj_b_kernel, ncol=ncol),
        grid=(bsz, t_len // tm),
        in_specs=[x_spec,
                  pl.BlockSpec((None, 1, D_MODEL), lambda b, i: (b, 0, 0)),
                  pl.BlockSpec((None, 1, D_MODEL), lambda b, i: (b, 0, 0)),
                  full((1, D_MODEL)), full((D_MODEL, B_COLS))],
        out_specs=pl.BlockSpec((None, tm, B_COLS), lambda b, i: (b, i, 0)),
        out_shape=jax.ShapeDtypeStruct((bsz, t_len, B_COLS), F32),
        compiler_params=_cparams(("arbitrary", "arbitrary")),
        name="inproj_b",
    )(xin, sh, sc, nw, wb)


N_PAIRS = SSD_HEADS // 2


def _ssd_dir(xbc, dt, dtT, s_ref, tri, triT, na_row, na_col, fwd, need_out=True):
    c = SSD_CHUNK
    xs = xbc[:, :SSD_INNER]
    bm = xbc[:, SSD_INNER:SSD_INNER + LANES]
    cm = xbc[:, SSD_INNER + LANES:]
    col0 = 0 if fwd else SSD_HEADS
    la = dt[:, col0:col0 + SSD_HEADS] * na_row[:, col0:col0 + SSD_HEADS]
    dtr = dtT[col0:col0 + SSD_HEADS, :]
    laT = dtr * na_col[col0:col0 + SSD_HEADS, :]
    g = sum(jnp.dot(tri, p, preferred_element_type=F32) for p in _split3(la))
    gT = sum(jnp.dot(p, triT, preferred_element_type=F32) for p in _split3(laT))
    end = c - 1 if fwd else 0
    bmT = bm.T
    ii = lax.broadcasted_iota(jnp.int32, (c, c), 0)
    jj = lax.broadcasted_iota(jnp.int32, (c, c), 1)
    causal = (jj <= ii) if fwd else (jj >= ii)
    lane = lax.broadcasted_iota(jnp.int32, (c, LANES), 1)
    lo_half = lane < SSD_HEAD_DIM
    lane_s = lax.broadcasted_iota(jnp.int32, (SSD_STATE, LANES), 1) < SSD_HEAD_DIM
    outs = []
    for grp in range(SSD_GROUPS):
        if need_out:
            in_grp = (lane >= grp * SSD_STATE) & (lane < (grp + 1) * SSD_STATE)
            cm_g = jnp.where(in_grp, cm, 0.0).astype(BF16)
            gmat = jnp.dot(cm_g, bmT.astype(BF16), preferred_element_type=F32)
        bmT_g = bmT[grp * SSD_STATE:(grp + 1) * SSD_STATE, :]
        for pp in range(N_PAIRS // SSD_GROUPS):
            pair = grp * (N_PAIRS // SSD_GROUPS) + pp
            heads = (2 * pair, 2 * pair + 1)
            xs_p = xs[:, pair * LANES:(pair + 1) * LANES]
            xbd = jnp.concatenate([jnp.where(lo_half, xs_p, 0.0), jnp.where(lo_half, 0.0, xs_p)],
                                  axis=0).astype(BF16)
            ms, bws, ecols, arows = [], [], [], []
            for hd in heads:
                gcol = g[:, hd:hd + 1]
                grow = gT[hd:hd + 1, :]
                glast = grow[:, end:end + 1]
                bws.append(bmT_g * (dtr[hd:hd + 1, :] * jnp.exp(glast - grow)))
                arows.append(jnp.exp(glast))
                if need_out:
                    dec = jnp.exp(jnp.where(causal, gcol - grow, -jnp.inf))
                    ms.append(gmat * dec * dtr[hd:hd + 1, :])
                    ecols.append(jnp.exp(gcol))
            s_old = s_ref[pair]
            if need_out:
                mcat = jnp.concatenate(ms, axis=1).astype(BF16)
                zeros = jnp.zeros_like(s_old)
                s_pad = jnp.concatenate([s_old, zeros] if grp == 0 else [zeros, s_old], axis=0).astype(BF16)
                o_inter = jnp.dot(cm.astype(BF16), s_pad, preferred_element_type=F32)
                o_inter = o_inter * jnp.where(lo_half, ecols[0], ecols[1])
                outs.append(jnp.dot(mcat, xbd, preferred_element_type=F32) + o_inter)
            bw = jnp.concatenate(bws, axis=1).astype(BF16)
            s_ref[pair] = (s_old * jnp.where(lane_s, arows[0], arows[1])
                           + jnp.dot(bw, xbd, preferred_element_type=F32))
    return outs


def _ssd_scan_kernel(xf_ref, dtf_ref, dtTf_ref, xb_ref, dtb_ref, dtTb_ref, s0f_ref, s0b_ref,
                     trif_ref, trifT_ref, trib_ref, tribT_ref, nar_ref, nac_ref, dsk_ref, *rest, need_out):
    if need_out:
        of_ref, ob_ref, sfo_ref, sbo_ref, sf, sb = rest
    else:
        sfo_ref, sbo_ref, sf, sb = rest
    n = pl.program_id(1)

    @pl.when(n == 0)
    def _():
        sf[...] = s0f_ref[...]
        sb[...] = s0b_ref[...]

    xf = xf_ref[...]
    outs = _ssd_dir(xf, dtf_ref[...], dtTf_ref[...], sf, trif_ref[...], trifT_ref[...],
                    nar_ref[...], nac_ref[...], True, need_out)
    if need_out:
        of_ref[...] = jnp.concatenate(outs, axis=1) + dsk_ref[...] * xf[:, :SSD_INNER]
    outs = _ssd_dir(xb_ref[...], dtb_ref[...], dtTb_ref[...], sb, trib_ref[...], tribT_ref[...],
                    nar_ref[...], nac_ref[...], False, need_out)
    if need_out:
        ob_ref[...] = jnp.concatenate(outs, axis=1)

    @pl.when(n == pl.num_programs(1) - 1)
    def _():
        sfo_ref[...] = sf[...]
        sbo_ref[...] = sb[...]


def _ssd_consts():
    c = SSD_CHUNK
    i = np.arange(c)
    trif = (i[:, None] >= i[None, :]).astype(np.float32)
    trib = (i[:, None] <= i[None, :]).astype(np.float32)
    return [jnp.asarray(a, BF16) for a in (trif, trif.T, trib, trib.T)]


def ssd_scan_call(xbc, dt, dtT, s0f, s0b, na_row, na_col, dskip, need_out=True):
    bsz, t_len, _ = xbc.shape
    c = SSD_CHUNK
    nc = t_len // c
    fw = lambda w: pl.BlockSpec((None, c, w), lambda b, n: (b, n, 0))
    bw = lambda w: pl.BlockSpec((None, c, w), lambda b, n: (b, nc - 1 - n, 0))
    full = lambda shape: pl.BlockSpec(shape, lambda b, n: (0,) * len(shape))
    st = pl.BlockSpec((None, N_PAIRS, SSD_STATE, LANES), lambda b, n: (b, 0, 0, 0))
    st_shape = jax.ShapeDtypeStruct((bsz, N_PAIRS, SSD_STATE, LANES), F32)
    o_shape = jax.ShapeDtypeStruct((bsz, t_len, SSD_INNER), F32)
    res = pl.pallas_call(
        functools.partial(_ssd_scan_kernel, need_out=need_out),
        grid=(bsz, nc),
        in_specs=[fw(SSD_XBC), fw(2 * SSD_HEADS),
                  pl.BlockSpec((None, 2 * SSD_HEADS, c), lambda b, n: (b, 0, n)),
                  bw(SSD_XBC), bw(2 * SSD_HEADS),
                  pl.BlockSpec((None, 2 * SSD_HEADS, c), lambda b, n: (b, 0, nc - 1 - n)),
                  st, st, full((c, c)), full((c, c)), full((c, c)), full((c, c)),
                  full((1, 2 * SSD_HEADS)), full((2 * SSD_HEADS, 1)), full((1, SSD_INNER))],
        out_specs=([fw(SSD_INNER), bw(SSD_INNER)] if need_out else []) + [st, st],
        out_shape=([o_shape, o_shape] if need_out else []) + [st_shape, st_shape],
        scratch_shapes=[pltpu.VMEM((N_PAIRS, SSD_STATE, LANES), F32),
                        pltpu.VMEM((N_PAIRS, SSD_STATE, LANES), F32)],
        compiler_params=_cparams(("arbitrary", "arbitrary")),
        name="ssd_scan",
    )(xbc, dt, dtT, xbc, dt, dtT, s0f, s0b, *_ssd_consts(), na_row, na_col, dskip)
    return tuple(res) if need_out else (None, None) + tuple(res)


def _hg_consts():
    c = HG_CHUNK
    t = np.arange(c)
    m_f = np.zeros((HG_LEVELS + 1, c, c), np.float32)
    m_f[0] = np.eye(c)
    for lv in range(1, HG_LEVELS + 1):
        m = 2 ** lv
        blk = t // m
        right = (t % m) >= m // 2
        m_f[lv] = (blk[:, None] == blk[None, :]) & right[:, None] & (~right[None, :])
    m_b = np.transpose(m_f, (0, 2, 1))
    tri_f = (t[:, None] >= t[None, :]).astype(np.float32)
    tri_b = (t[:, None] <= t[None, :]).astype(np.float32)
    return jnp.asarray(tri_f, BF16), jnp.asarray(tri_b, BF16), jnp.asarray(m_f, F32), jnp.asarray(m_b, F32)


def _hg_dir(u, lb_row, st_ref, tri_ref, mask_ref, fwd, need_out=True):
    c = HG_CHUNK
    end = c - 1 if fwd else 0
    outs = []
    fcol = HG_INNER if fwd else 2 * HG_INNER
    nt = (((1,), (1,)), ((), ()))
    row = lax.broadcasted_iota(jnp.int32, (c, HG_DK), 0)

    def halves(lo, hi, half):
        if half % SUBLANES == 0:
            return jnp.concatenate([(hi if (s // half) % 2 else lo)[s:s + half] for s in range(0, c, half)], axis=0)
        return jnp.where(((row // half) % 2) == 1, hi, lo)

    def shift(x, s):
        s = s % c
        if s % SUBLANES == 0:
            return jnp.concatenate([x[c - s:], x[:c - s]], axis=0)
        return pltpu.roll(x, s, 0)

    def by_side(query_side, key_side, half):
        return halves(key_side, query_side, half) if fwd else halves(query_side, key_side, half)

    for hd in range(HG_HEADS):
        sl = slice(hd * HG_DK, (hd + 1) * HG_DK)
        lb = lb_row[:, sl]
        f = lb + (1.0 - lb) * _sigmoid(u[:, fcol + hd * HG_DK:fcol + (hd + 1) * HG_DK])
        k = 1.0 - f
        la = jnp.log(f)
        v = u[:, 3 * HG_INNER + hd * HG_DK:3 * HG_INNER + (hd + 1) * HG_DK]
        la_hi = la.astype(BF16)
        la_lo = (la - la_hi.astype(F32)).astype(BF16)
        g2 = jnp.dot(tri_ref[...], jnp.concatenate([la_hi, la_lo], axis=1), preferred_element_type=F32)
        g = g2[:, :HG_DK] + g2[:, HG_DK:]
        st = st_ref[hd]
        if not need_out:
            g_end = g[end:end + 1, :]
            st_ref[hd] = (st * jnp.exp(g_end)
                          + jnp.dot(v.T.astype(BF16), (k * jnp.exp(g_end - g)).astype(BF16),
                                    preferred_element_type=F32))
            continue
        q = _silu(u[:, sl]) * (HG_DK ** -0.5)
        scores = mask_ref[0] * lax.dot_general(q.astype(BF16), k.astype(BF16), nt, preferred_element_type=F32)
        fill = g
        for lv in range(1, HG_LEVELS + 1):
            half = 2 ** (lv - 1)
            if fwd:
                ref = halves(fill, shift(fill, half), half)
            else:
                ref = halves(shift(fill, -half), fill, half)
            decay = jnp.exp2(jnp.abs(g - ref) * (-LOG2E))
            y = (by_side(q, k, half) * decay).astype(BF16)
            scores = scores + mask_ref[lv] * lax.dot_general(y, y, nt, preferred_element_type=F32)
            if fwd:
                fill = halves(shift(fill, -half), fill, half)
            else:
                fill = halves(fill, shift(fill, half), half)
        x_cum = jnp.exp(g)
        x_rem = jnp.exp(fill - g)
        o = jnp.dot(scores.astype(BF16), v.astype(BF16), preferred_element_type=F32)
        o = o + lax.dot_general((q * x_cum).astype(BF16), st.astype(BF16), nt, preferred_element_type=F32)
        outs.append(o)
        st_ref[hd] = (st * x_cum[end:end + 1, :]
                      + jnp.dot(v.T.astype(BF16), (k * x_rem).astype(BF16), preferred_element_type=F32))
    return jnp.concatenate(outs, axis=1) if need_out else None


def _hg_scan_kernel(uf_ref, ub_ref, lbp_ref, s0f_ref, s0b_ref, trif_ref, trib_ref, mf_ref, mb_ref,
                    *rest, img_rows, need_out):
    if need_out:
        of_ref, ob_ref, sfo_ref, sbo_ref, sf, sb = rest
    else:
        sfo_ref, sbo_ref, sf, sb = rest
    n = pl.program_id(1)

    def put(o_ref, o):
        if img_rows:
            for j in range(HG_CHUNK // img_rows):
                o_ref[:, j * HG_INNER:(j + 1) * HG_INNER] = o[j * img_rows:(j + 1) * img_rows]
        else:
            o_ref[...] = o

    @pl.when(n == 0)
    def _():
        sf[...] = s0f_ref[...]
        sb[...] = s0b_ref[...]

    p = lbp_ref[...]
    mx = jnp.max(p, axis=0, keepdims=True)
    e = jnp.exp(p - mx)
    lb = e[0:1, :] / jnp.sum(e, axis=0, keepdims=True)
    o_f = _hg_dir(uf_ref[...], lb[:, :HG_INNER], sf, trif_ref, mf_ref, True, need_out)
    o_b = _hg_dir(ub_ref[...], lb[:, HG_INNER:], sb, trib_ref, mb_ref, False, need_out)
    if need_out:
        put(of_ref, o_f)
        put(ob_ref, o_b)

    @pl.when(n == pl.num_programs(1) - 1)
    def _():
        sfo_ref[...] = sf[...]
        sbo_ref[...] = sb[...]


def hg_scan_call(u, lbp, s0f, s0b, row_major_out, need_out=True):
    bsz, t_len, _ = u.shape
    c = HG_CHUNK
    nc = t_len // c
    full = lambda shape: pl.BlockSpec(shape, lambda b, n: (0,) * len(shape))
    st = pl.BlockSpec((None, HG_HEADS, HG_DK, HG_DK), lambda b, n: (b, 0, 0, 0))
    st_shape = jax.ShapeDtypeStruct((bsz, HG_HEADS, HG_DK, HG_DK), F32)
    if row_major_out:
        img_rows = t_len // GRID_W
        cols = c // img_rows
        assert cols * img_rows == c
        o_shape = jax.ShapeDtypeStruct((bsz, img_rows, GRID_W * HG_INNER), F32)
        of_spec = pl.BlockSpec((None, img_rows, cols * HG_INNER), lambda b, n: (b, 0, n))
        ob_spec = pl.BlockSpec((None, img_rows, cols * HG_INNER), lambda b, n: (b, 0, nc - 1 - n))
    else:
        img_rows = 0
        o_shape = jax.ShapeDtypeStruct((bsz, t_len, HG_INNER), F32)
        of_spec = pl.BlockSpec((None, c, HG_INNER), lambda b, n: (b, n, 0))
        ob_spec = pl.BlockSpec((None, c, HG_INNER), lambda b, n: (b, nc - 1 - n, 0))
    tri_f, tri_b, m_f, m_b = _hg_consts()
    res = pl.pallas_call(
        functools.partial(_hg_scan_kernel, img_rows=img_rows, need_out=need_out),
        grid=(bsz, nc),
        in_specs=[pl.BlockSpec((None, c, B_COLS), lambda b, n: (b, n, 0)),
                  pl.BlockSpec((None, c, B_COLS), lambda b, n: (b, nc - 1 - n, 0)),
                  full((2, 2 * HG_INNER)), st, st,
                  full(tri_f.shape), full(tri_b.shape), full(m_f.shape), full(m_b.shape)],
        out_specs=([of_spec, ob_spec] if need_out else []) + [st, st],
        out_shape=([o_shape, o_shape] if need_out else []) + [st_shape, st_shape],
        scratch_shapes=[pltpu.VMEM((HG_HEADS, HG_DK, HG_DK), F32), pltpu.VMEM((HG_HEADS, HG_DK, HG_DK), F32)],
        compiler_params=_cparams(("arbitrary", "arbitrary")),
        name="hg_scan",
    )(u, u, lbp, s0f, s0b, tri_f, tri_b, m_f, m_b)
    if not need_out:
        return (None, None) + tuple(res)
    return tuple(res)


ROW_LINES = D_MODEL // LANES


def _to_token_tiles(ref, val, n_rows):
    for c in range(ROW_LINES):
        ref[pl.ds(c, n_rows, stride=ROW_LINES), :] = val[:, c * LANES:(c + 1) * LANES]


def _from_token_tiles(ref, n_rows, first_row=0):
    return jnp.concatenate([ref[pl.ds(first_row * ROW_LINES + c, n_rows, stride=ROW_LINES), :]
                            for c in range(ROW_LINES)], axis=1)


def _tile_of(ref, row):
    return ref.at[pl.ds(pl.multiple_of(row * ROW_LINES, ROW_LINES), ROW_LINES)]


def _post_kernel(x_ref, sof_ref, sob_ref, z_ref, hof_ref, hob_ref, hg_ref,
                 snw_ref, hnw_ref, wo_ref, pnw_ref, gm_ref, fnw_ref, shf_ref, scf_ref, rwh_ref, rwl_ref, rb_ref,
                 tri_ref, x1_ref, hx_ref, idx_ref, rank_ref, gate_ref, cnt_ref, carry, hscr, *, tm):
    first = (pl.program_id(0) == 0) & (pl.program_id(1) == 0)

    @pl.when(first)
    def _():
        carry[...] = jnp.zeros_like(carry)

    y = (sof_ref[...] + sob_ref[...]) * _silu(z_ref[...])
    y = _rms(y, snw_ref[...])
    ho = hof_ref[...] + hob_ref[...]
    n_img = tm // GRID_W
    for w in range(GRID_W):
        for cc in range(HG_INNER // LANES):
            lo = w * HG_INNER + cc * LANES
            hscr[cc, pl.ds(w, n_img, stride=GRID_W), :] = ho[:, lo:lo + LANES]
    o = jnp.concatenate([hscr[cc] for cc in range(HG_INNER // LANES)], axis=1)
    hnw = hnw_ref[...]
    o = jnp.concatenate([_rms(o[:, h * HG_DK:(h + 1) * HG_DK], hnw[:, h * HG_DK:(h + 1) * HG_DK])
                         for h in range(HG_HEADS)], axis=1)
    o = o * _silu(hg_ref[...])
    mix = (jnp.dot(y.astype(BF16), wo_ref[:SSD_INNER, :], preferred_element_type=F32)
           + jnp.dot(o.astype(BF16), wo_ref[SSD_INNER:, :], preferred_element_type=F32))
    x1 = x_ref[...] + gm_ref[...] * _rms(mix, pnw_ref[...])
    x1_ref[...] = x1
    hx = _rms(x1, fnw_ref[...]) * (1.0 + scf_ref[...]) + shf_ref[...]
    _to_token_tiles(hx_ref, hx, tm)
    hx_hi = hx.astype(BF16)
    hx_lo = (hx - hx_hi.astype(F32)).astype(BF16)
    rwh = rwh_ref[...]
    logits = (jnp.dot(hx_hi, rwh, preferred_element_type=F32) + jnp.dot(hx_lo, rwh, preferred_element_type=F32)
              + jnp.dot(hx_hi, rwl_ref[...], preferred_element_type=F32)) + rb_ref[...]
    work = logits.T[:N_EXPERTS, :]
    erow = lax.broadcasted_iota(jnp.int32, (N_EXPERTS, tm), 0)
    vals, idxs = [], []
    for _ in range(TOP_K):
        m = jnp.max(work, axis=0, keepdims=True)
        ix = jnp.min(jnp.where(work == m, erow, N_EXPERTS), axis=0, keepdims=True)
        vals.append(m)
        idxs.append(ix)
        work = jnp.where(erow == ix, -jnp.inf, work)
    es = [jnp.exp(v - vals[0]) for v in vals]
    den = es[0] + es[1] + es[2] + es[3]
    onehots = [(erow == ix) for ix in idxs]
    multi = sum(oh.astype(F32) for oh in onehots)
    before = jnp.dot(multi.astype(BF16), tri_ref[...], preferred_element_type=F32) + carry[...]
    carry[...] = carry[...] + jnp.sum(multi, axis=1, keepdims=True)
    sub = lax.broadcasted_iota(jnp.int32, (SUBLANES, tm), 0)
    idx_o = jnp.zeros((SUBLANES, tm), jnp.int32)
    rank_o = jnp.zeros((SUBLANES, tm), jnp.int32)
    gate_o = jnp.zeros((SUBLANES, tm), F32)
    for k in range(TOP_K):
        rk = jnp.sum(jnp.where(onehots[k], before, 0.0), axis=0, keepdims=True)
        idx_o = jnp.where(sub == k, idxs[k], idx_o)
        rank_o = jnp.where(sub == k, rk.astype(jnp.int32), rank_o)
        gate_o = jnp.where(sub == k, es[k] / den, gate_o)
    idx_ref[...] = idx_o
    rank_ref[...] = rank_o
    gate_ref[...] = gate_o
    cnt_ref[...] = jnp.broadcast_to(carry[...], (N_EXPERTS, LANES))


def post_call(x, sof, sob, z, hof, hob, hg, snw, hnw, wo, pnw, gm, fnw, shf, scf, rwh, rwl, rb):
    bsz, t_len, _ = x.shape
    tm = TOK_TILE
    nt = t_len // tm
    n_tok = bsz * t_len
    tok = lambda c: pl.BlockSpec((None, tm, c), lambda b, i: (b, i, 0))
    full = lambda shape: pl.BlockSpec(shape, lambda b, i: (0,) * len(shape))
    per_b = pl.BlockSpec((None, 1, D_MODEL), lambda b, i: (b, 0, 0))
    flat = lambda c: pl.BlockSpec((tm, c), lambda b, i: (b * nt + i, 0))
    rout = pl.BlockSpec((SUBLANES, tm), lambda b, i: (b * nt + i, 0))
    n_tiles = n_tok // tm
    ii = np.arange(tm)
    tri = jnp.asarray(ii[:, None] < ii[None, :], BF16)
    assert tm % (GRID_W * SUBLANES) == 0 and hof.shape == (bsz, t_len // GRID_W, GRID_W * HG_INNER)
    img = pl.BlockSpec((None, tm // GRID_W, GRID_W * HG_INNER), lambda b, i: (b, i, 0))
    return pl.pallas_call(
        functools.partial(_post_kernel, tm=tm),
        grid=(bsz, nt),
        in_specs=[tok(D_MODEL), tok(SSD_INNER), tok(SSD_INNER), tok(SSD_INNER), img, img,
                  tok(HG_INNER), full((1, SSD_INNER)), full((1, HG_INNER)), full((D_MODEL, D_MODEL)),
                  full((1, D_MODEL)), per_b, full((1, D_MODEL)), per_b, per_b,
                  full((D_MODEL, LANES)), full((D_MODEL, LANES)), full((1, LANES)), full((tm, tm))],
        out_specs=[flat(D_MODEL), pl.BlockSpec((tm * ROW_LINES, LANES), lambda b, i: (b * nt + i, 0)),
                   rout, rout, rout, pl.BlockSpec((N_EXPERTS, LANES), lambda b, i: (0, 0))],
        out_shape=[jax.ShapeDtypeStruct((n_tok, D_MODEL), F32),
                   jax.ShapeDtypeStruct((n_tok * ROW_LINES, LANES), F32),
                   jax.ShapeDtypeStruct((n_tiles * SUBLANES, tm), jnp.int32),
                   jax.ShapeDtypeStruct((n_tiles * SUBLANES, tm), jnp.int32),
                   jax.ShapeDtypeStruct((n_tiles * SUBLANES, tm), F32),
                   jax.ShapeDtypeStruct((N_EXPERTS, LANES), F32)],
        scratch_shapes=[pltpu.VMEM((N_EXPERTS, 1), F32), pltpu.VMEM((HG_INNER // LANES, tm, LANES), F32)],
        compiler_params=_cparams(("arbitrary", "arbitrary")),
        name="post",
    )(x, sof, sob, z, hof, hob, hg, snw, hnw, wo, pnw, gm, fnw, shf, scf, rwh, rwl, rb, tri)


def _wait_rows(hbm_ref, n_rows, sem):
    n = n_rows * ROW_LINES
    pltpu.make_async_copy(hbm_ref.at[pl.ds(0, n)], hbm_ref.at[pl.ds(0, n)], sem).wait()


PAD_RUNS = tuple(2 ** j for j in range(int(math.log2(MOE_ROWS))))
DISPATCH_TILES = 2


def _dispatch_kernel(zstart_ref, zpad_ref, dest_ref, hx_ref, buf_ref, zrows, sem, zsem, *, tm):
    i = pl.program_id(0)

    def zero_rows(first, n):
        first = pl.multiple_of(first * ROW_LINES, ROW_LINES)
        return pltpu.make_async_copy(zrows.at[pl.ds(0, n * ROW_LINES)], buf_ref.at[pl.ds(first, n * ROW_LINES)], zsem)

    def for_pad_runs(act):
        def per_expert(e, carry):
            pad = zpad_ref[e]
            off = zstart_ref[e]
            for run in PAD_RUNS:
                @pl.when((pad & run) != 0)
                def _():
                    act(zero_rows(off, run))
                off = off + (pad & run)
            return carry

        lax.fori_loop(0, N_EXPERTS, per_expert, 0)

        def tail(j, carry):
            act(zero_rows(j * PAD_RUNS[-1], PAD_RUNS[-1]))
            return carry

        n_rows = buf_ref.shape[0] // ROW_LINES
        lax.fori_loop(zstart_ref[N_EXPERTS] // PAD_RUNS[-1], n_rows // PAD_RUNS[-1], tail, 0)

    @pl.when(i == 0)
    def _():
        zrows[...] = jnp.zeros_like(zrows)
        for_pad_runs(lambda cp: cp.start())

    def issue(t, carry):
        for sub in range(DISPATCH_TILES):
            for k in range(TOP_K):
                pltpu.make_async_copy(_tile_of(hx_ref, sub * tm + t),
                                      _tile_of(buf_ref, dest_ref[(sub * TOP_K + k) * tm + t]),
                                      sem).start(priority=k % 2)
        return carry

    lax.fori_loop(0, tm, issue, 0, unroll=2)
    _wait_rows(buf_ref, DISPATCH_TILES * tm * TOP_K, sem)

    @pl.when(i == 0)
    def _():
        for_pad_runs(lambda cp: cp.wait())


def dispatch_call(zstart, zpad, dest_flat, hx, n_rows):
    n_tok = hx.shape[0] // ROW_LINES
    tm = TOK_TILE
    step_tok = DISPATCH_TILES * tm
    return pl.pallas_call(
        functools.partial(_dispatch_kernel, tm=tm),
        grid=(n_tok // step_tok,),
        in_specs=[pl.BlockSpec(memory_space=pltpu.SMEM), pl.BlockSpec(memory_space=pltpu.SMEM),
                  pl.BlockSpec((step_tok * TOP_K,), lambda i: (i,), memory_space=pltpu.SMEM),
                  pl.BlockSpec((step_tok * ROW_LINES, LANES), lambda i: (i, 0))],
        out_specs=pl.BlockSpec(memory_space=pl.ANY),
        out_shape=jax.ShapeDtypeStruct((n_rows * ROW_LINES, LANES), F32),
        scratch_shapes=[pltpu.VMEM((PAD_RUNS[-1] * ROW_LINES, LANES), F32), pltpu.SemaphoreType.DMA(()),
                        pltpu.SemaphoreType.DMA(())],
        compiler_params=_cparams(("arbitrary",)),
        name="dispatch",
    )(zstart, zpad, dest_flat, hx)


W1_TCOLS = 256


def _experts_kernel(be_ref, nu_ref, nxt_ref, x_ref, w1_hbm, b1_ref, w2_hbm, b2_ref, y_ref,
                    w1f, w2f, w1t, w2s, tbuf, wsem):
    i = pl.program_id(0)
    live = i < nu_ref[0]
    new_expert = (i == 0) | (be_ref[i] != be_ref[jnp.maximum(i - 1, 0)])

    def weight_copies(e):
        return (pltpu.make_async_copy(w1_hbm.at[e], w1f, wsem.at[0]),
                pltpu.make_async_copy(w2_hbm.at[e], w2f, wsem.at[1]))

    @pl.when(live & new_expert)
    def _():
        @pl.when(i == 0)
        def _():
            for cp in weight_copies(be_ref[0]):
                cp.start()

        for cp in weight_copies(be_ref[i]):
            cp.wait()
        half = W1_TCOLS // 2
        for c in range(2 * D_FF // W1_TCOLS):
            for j in range(D_MODEL // LANES):
                ks = slice(j * LANES, (j + 1) * LANES)
                tbuf[j] = w1f[ks, c * W1_TCOLS:(c + 1) * W1_TCOLS].T
                w1t[c * half:(c + 1) * half, ks] = tbuf[j, pl.ds(0, half, stride=2), :].astype(BF16)
                w1t[D_FF + c * half:D_FF + (c + 1) * half, ks] = tbuf[j, pl.ds(1, half, stride=2), :].astype(BF16)
        w2s[...] = w2f[...].astype(BF16)

        @pl.when(nxt_ref[i] >= 0)
        def _():
            for cp in weight_copies(nxt_ref[i]):
                cp.start(priority=1)

    @pl.when(live)
    def _():
        xb = _from_token_tiles(x_ref, MOE_ROWS).astype(BF16)
        u = lax.dot_general(xb, w1t[...], (((1,), (1,)), ((), ())), preferred_element_type=F32) + b1_ref[...]
        glu = jnp.minimum(u[:, :D_FF], SWIGLU_LIMIT)
        lin = jnp.clip(u[:, D_FF:], -SWIGLU_LIMIT, SWIGLU_LIMIT)
        a = glu * _sigmoid(SWIGLU_ALPHA * glu) * (lin + 1.0)
        y = jnp.dot(a.astype(BF16), w2s[...], preferred_element_type=F32) + b2_ref[...]
        _to_token_tiles(y_ref, y, MOE_ROWS)

    @pl.when(jnp.logical_not(live))
    def _():
        y_ref[...] = jnp.zeros_like(y_ref)


def experts_call(block_e, n_used, next_e, xs, w1, b1p, w2, b2):
    rows = xs.shape[0] // ROW_LINES
    r = MOE_ROWS
    nb = rows // r
    blk = (r * ROW_LINES, LANES)
    grid_spec = pltpu.PrefetchScalarGridSpec(
        num_scalar_prefetch=3,
        grid=(nb,),
        in_specs=[pl.BlockSpec(blk, lambda i, be, nu, nx: (jnp.maximum(jnp.minimum(i, nu[0] - 1), 0), 0)),
                  pl.BlockSpec(memory_space=pl.ANY),
                  pl.BlockSpec((None, 1, 2 * D_FF), lambda i, be, nu, nx: (be[i], 0, 0)),
                  pl.BlockSpec(memory_space=pl.ANY),
                  pl.BlockSpec((None, 1, D_MODEL), lambda i, be, nu, nx: (be[i], 0, 0))],
        out_specs=pl.BlockSpec(blk, lambda i, be, nu, nx: (i, 0)),
        scratch_shapes=[pltpu.VMEM((D_MODEL, 2 * D_FF), F32), pltpu.VMEM((D_FF, D_MODEL), F32),
                        pltpu.VMEM((2 * D_FF, D_MODEL), BF16), pltpu.VMEM((D_FF, D_MODEL), BF16),
                        pltpu.VMEM((D_MODEL // LANES, W1_TCOLS, LANES), F32),
                        pltpu.SemaphoreType.DMA((2,))],
    )
    return pl.pallas_call(
        _experts_kernel,
        grid_spec=grid_spec,
        out_shape=jax.ShapeDtypeStruct((rows * ROW_LINES, LANES), F32),
        compiler_params=_cparams(("arbitrary",)),
        name="experts",
    )(block_e, n_used, next_e, xs, w1, b1p, w2, b2)


def _combine_kernel(dest_ref, dest_next_ref, y_ref, gate_ref, x1_ref, gf_ref, nw_ref, o_ref, buf, sem, *, tm):
    i = pl.program_id(0)
    slot = i % 2

    def gather_tile(dref, into):
        def issue(t, carry):
            for k in range(TOP_K):
                pltpu.make_async_copy(_tile_of(y_ref, dref[k * tm + t]), _tile_of(buf.at[into], k * tm + t),
                                      sem.at[into]).start(priority=k % 2)
            return carry

        lax.fori_loop(0, tm, issue, 0, unroll=4)

    @pl.when(i == 0)
    def _():
        gather_tile(dest_ref, 0)

    @pl.when(i + 1 < pl.num_programs(0))
    def _():
        gather_tile(dest_next_ref, 1 - slot)

    _wait_rows(y_ref, tm * TOP_K, sem.at[slot])
    gate = gate_ref[...].T
    fx = None
    for k in range(TOP_K):
        yk = jnp.concatenate([buf[slot, pl.ds(k * tm * ROW_LINES + c, tm, stride=ROW_LINES), :]
                              for c in range(ROW_LINES)], axis=1)
        fx = gate[:, k:k + 1] * yk if fx is None else fx + gate[:, k:k + 1] * yk
    o_ref[...] = x1_ref[...] + gf_ref[...] * _rms(fx, nw_ref[...])


def combine_call(dest_flat, ys, gates, x1, gf, nw, bsz):
    n_tok = x1.shape[0]
    tm = TOK_TILE
    nt = n_tok // bsz // tm
    n_steps = n_tok // tm
    return pl.pallas_call(
        functools.partial(_combine_kernel, tm=tm),
        grid=(n_steps,),
        in_specs=[pl.BlockSpec((tm * TOP_K,), lambda i: (i,), memory_space=pltpu.SMEM),
                  pl.BlockSpec((tm * TOP_K,), lambda i: (jnp.minimum(i + 1, n_steps - 1),),
                               memory_space=pltpu.SMEM),
                  pl.BlockSpec(memory_space=pl.ANY),
                  pl.BlockSpec((SUBLANES, tm), lambda i: (i, 0)),
                  pl.BlockSpec((tm, D_MODEL), lambda i: (i, 0)),
                  pl.BlockSpec((None, 1, D_MODEL), lambda i: (i // nt, 0, 0)),
                  pl.BlockSpec((1, D_MODEL), lambda i: (0, 0))],
        out_specs=pl.BlockSpec((tm, D_MODEL), lambda i: (i, 0)),
        out_shape=jax.ShapeDtypeStruct((n_tok, D_MODEL), F32),
        scratch_shapes=[pltpu.VMEM((2, TOP_K * tm * ROW_LINES, LANES), F32), pltpu.SemaphoreType.DMA((2,))],
        compiler_params=_cparams(("arbitrary",)),
        name="combine",
    )(dest_flat, dest_flat, ys, gates, x1, gf, nw)


def kernel(x, c, ctx, c_ctx, ada_w, ada_b, mix_pre_norm, mix_post_norm, w_in, w_out, ssd_conv_w, ssd_conv_b,
           ssd_dt_bias, ssd_a_log, ssd_d, ssd_norm, hg_lb, hg_norm, ffn_pre_norm, ffn_post_norm, router_w,
           router_b, moe_w1, moe_b1, moe_w2, moe_b2):
    bsz, t_len, d = x.shape
    assert ada_w.shape[0] == 1 and d == D_MODEL and bsz <= SUBLANES - 1
    n_tok = bsz * t_len

    cc = jnp.zeros((SUBLANES, d), F32).at[:bsz].set(c).at[bsz].set(c_ctx)
    mod = ada_call(cc, ada_w[0], ada_b[0][None, :])
    sh_m, sc_m, g_m, sh_f, sc_f, g_f = [m[:bsz, None, :] for m in jnp.split(mod, 6, axis=-1)]
    csh_m, csc_m = [jnp.broadcast_to(m[bsz][None, None, :], (bsz, 1, d)) for m in jnp.split(mod, 6, axis=-1)[:2]]

    w = w_in[0]
    wa = jnp.concatenate([w[:, :SSD_INNER + SSD_XBC], w[:, SSD_COLS + 4 * HG_INNER:]], axis=1).astype(BF16)
    wdt = jnp.zeros((d, LANES), F32).at[:, :2 * SSD_HEADS].set(w[:, SSD_INNER + SSD_XBC:SSD_COLS]).astype(BF16)
    wb = w[:, SSD_COLS:SSD_COLS + 4 * HG_INNER].astype(BF16)
    dtb = jnp.zeros((1, LANES), F32).at[0, :2 * SSD_HEADS].set(ssd_dt_bias[0].reshape(-1))
    nw = mix_pre_norm[0][None, :]
    cw, cb = ssd_conv_w[0], ssd_conv_b[0][None, :]
    neg_a = -jnp.exp(ssd_a_log[0].astype(F32)).reshape(1, 2 * SSD_HEADS)
    dskip = jnp.repeat(ssd_d[0], SSD_HEAD_DIM)[None, :]
    lbp = hg_lb.astype(F32).reshape(2, 2 * HG_INNER)

    _, cxbc, _, cdt, cdtT = inproj_a_call(ctx, csh_m, csc_m, nw, wa, wdt, cw, cb, dtb, tm=ctx.shape[1])
    cu = inproj_b_call(ctx, csh_m, csc_m, nw, wb, col_major=False)
    z_ssd = jnp.zeros((bsz, N_PAIRS, SSD_STATE, LANES), F32)
    z_hg = jnp.zeros((bsz, HG_HEADS, HG_DK, HG_DK), F32)
    _, _, ssf, ssb = ssd_scan_call(cxbc, cdt, cdtT, z_ssd, z_ssd, neg_a, neg_a.reshape(-1, 1), dskip,
                                   need_out=False)
    _, _, hsf, hsb = hg_scan_call(cu, lbp, z_hg, z_hg, row_major_out=False, need_out=False)

    zg, xbc, hgate, dt, dtT = inproj_a_call(x, sh_m, sc_m, nw, wa, wdt, cw, cb, dtb, tm=512)
    ub = inproj_b_call(x, sh_m, sc_m, nw, wb, col_major=True)
    sof, sob, _, _ = ssd_scan_call(xbc, dt, dtT, ssf, ssb, neg_a, neg_a.reshape(-1, 1), dskip)
    hof, hob, _, _ = hg_scan_call(ub, lbp, hsf, hsb, row_major_out=True)

    rw = jnp.zeros((d, LANES), F32).at[:, :N_EXPERTS].set(router_w[0])
    rwh = rw.astype(BF16)
    rwl = (rw - rwh.astype(F32)).astype(BF16)
    rb = jnp.zeros((1, LANES), F32).at[0, :N_EXPERTS].set(router_b[0])
    x1, hx, idx, rank, gates, cnt = post_call(
        x, sof, sob, zg, hof, hob, hgate, ssd_norm[0][None, :], hg_norm[0][None, :], w_out[0].astype(BF16),
        mix_post_norm[0][None, :], g_m, ffn_pre_norm[0][None, :], sh_f, sc_f, rwh, rwl, rb)

    r = MOE_ROWS
    n_blocks = (n_tok * TOP_K + N_EXPERTS * (r - 1) + r - 1) // r
    n_tiles = n_tok // TOK_TILE
    counts = cnt[:, 0].astype(jnp.int32)
    padded = (counts + r - 1) // r * r
    pend = jnp.cumsum(padded)
    pstart = pend - padded
    idx3 = idx.reshape(n_tiles, SUBLANES, TOK_TILE)[:, :TOP_K, :]
    rank3 = rank.reshape(n_tiles, SUBLANES, TOK_TILE)[:, :TOP_K, :]
    dest = rank3
    for e in range(N_EXPERTS):
        dest = dest + jnp.where(idx3 == e, pstart[e], 0)
    dest = dest.reshape(-1)
    starts = jnp.arange(n_blocks, dtype=jnp.int32) * r
    block_e = jnp.minimum(jnp.sum((pend[None, :] <= starts[:, None]).astype(jnp.int32), axis=1), N_EXPERTS - 1)
    n_used = (pend[-1:] // r).astype(jnp.int32)
    run_end = pend[block_e] // r
    next_e = jnp.where(run_end < n_used[0], block_e[jnp.minimum(run_end, n_blocks - 1)], -1).astype(jnp.int32)

    zstart = jnp.concatenate([pstart + counts, pend[-1:]])
    xs = dispatch_call(zstart, padded - counts, dest, hx, n_blocks * r)
    b1p = jnp.concatenate([moe_b1[0][:, 0::2], moe_b1[0][:, 1::2]], axis=-1)[:, None, :]
    ys = experts_call(block_e, n_used, next_e, xs, moe_w1[0], b1p, moe_w2[0], moe_b2[0][:, None, :])
    out = combine_call(dest, ys, gates, x1, g_f, ffn_post_norm[0][None, :], bsz)
    return out.reshape(bsz, t_len, d)
```

```python
import functools
import math

import numpy as np
import jax
import jax.numpy as jnp
from jax import lax
from jax.experimental import pallas as pl
from jax.experimental.pallas import tpu as pltpu

F32 = jnp.float32
BF16 = jnp.bfloat16
HIGHEST = lax.Precision.HIGHEST

D_MODEL = 1024
GRID_W = 64
SSD_HEADS = 8
SSD_HEAD_DIM = 64
SSD_INNER = 512
SSD_STATE = 64
SSD_GROUPS = 2
SSD_CONV = 5
SSD_XBC = 768
SSD_COLS = 1296
HG_HEADS = 4
HG_DK = 128
HG_INNER = 512
N_EXPERTS = 32
TOP_K = 4
D_FF = 1024
SWIGLU_ALPHA = 1.702
SWIGLU_LIMIT = 7.0
EPS = 1e-6

LANES = 128
SUBLANES = 8
VMEM_LIMIT = 56 * 1024 * 1024

SSD_CHUNK = 256
HG_CHUNK = 128
HG_LEVELS = 7
HG_STEP_CHUNKS = 2
MOE_ROWS = 256
TOK_TILE = 512
LOG2E = 1.4426950408889634


def _cparams(sem):
    return pltpu.CompilerParams(dimension_semantics=sem, vmem_limit_bytes=VMEM_LIMIT)


def _sigmoid(x):
    return 1.0 / (1.0 + jnp.exp(-x))


def _silu(x):
    return x * _sigmoid(x)


def _rms(x, w):
    return x * lax.rsqrt(jnp.mean(x * x, axis=-1, keepdims=True) + EPS) * w


def _split3(v):
    hi = v.astype(BF16)
    r1 = v - hi.astype(F32)
    mid = r1.astype(BF16)
    lo = (r1 - mid.astype(F32)).astype(BF16)
    return hi, mid, lo


def _ada_kernel(c_ref, w_ref, b_ref, o_ref):
    s = _silu(c_ref[...])
    o_ref[...] = jnp.dot(s, w_ref[...], precision=HIGHEST, preferred_element_type=F32) + b_ref[...]


def ada_call(cc, w, b):
    n = w.shape[1]
    tn = 1536
    return pl.pallas_call(
        _ada_kernel,
        grid=(n // tn,),
        in_specs=[pl.BlockSpec((SUBLANES, D_MODEL), lambda j: (0, 0)),
                  pl.BlockSpec((D_MODEL, tn), lambda j: (0, j)),
                  pl.BlockSpec((1, tn), lambda j: (0, j))],
        out_specs=pl.BlockSpec((SUBLANES, tn), lambda j: (0, j)),
        out_shape=jax.ShapeDtypeStruct((SUBLANES, n), F32),
        compiler_params=_cparams(("arbitrary",)),
        name="ada",
    )(cc, w, b)


A_COLS = SSD_INNER + SSD_XBC + HG_INNER


def _prep(xt, nw, sh, sc):
    return (_rms(xt, nw) * (1.0 + sc) + sh).astype(BF16)


def _inproj_a_kernel(x_ref, xp_ref, xn_ref, sh_ref, sc_ref, nw_ref, wa_ref, wdt_ref, cw_ref, cb_ref, dtb_ref,
                     z_ref, xbc_ref, g_ref, dt_ref, dtT_ref, scr, *, tm):
    i = pl.program_id(1)
    last = pl.num_programs(1) - 1
    nw, sh, sc = nw_ref[...], sh_ref[...], sc_ref[...]
    h = _prep(x_ref[...], nw, sh, sc)
    ua = jnp.dot(h, wa_ref[...], preferred_element_type=F32)
    z_ref[...] = ua[:, :SSD_INNER]
    g_ref[...] = ua[:, SSD_INNER + SSD_XBC:]
    wx = wa_ref[:, SSD_INNER:SSD_INNER + SSD_XBC]
    up = jnp.dot(_prep(xp_ref[...], nw, sh, sc), wx, preferred_element_type=F32)
    un = jnp.dot(_prep(xn_ref[...], nw, sh, sc), wx, preferred_element_type=F32)
    scr[0:SUBLANES, :] = jnp.where(i > 0, up, 0.0)
    scr[SUBLANES:SUBLANES + tm, :] = ua[:, SSD_INNER:SSD_INNER + SSD_XBC]
    scr[SUBLANES + tm:, :] = jnp.where(i < last, un, 0.0)
    acc = jnp.broadcast_to(cb_ref[...], (tm, SSD_XBC))
    pad = SSD_CONV // 2
    for k in range(SSD_CONV):
        off = SUBLANES - pad + k
        acc = acc + cw_ref[k:k + 1, :] * scr[off:off + tm, :]
    xbc_ref[...] = _silu(acc)
    draw = jnp.dot(h, wdt_ref[...], preferred_element_type=F32) + dtb_ref[...]
    dt = jnp.maximum(draw, 0.0) + jnp.log(1.0 + jnp.exp(-jnp.abs(draw)))
    dt_ref[...] = dt[:, :2 * SSD_HEADS]
    dtT_ref[...] = dt.T[:2 * SSD_HEADS, :]


def inproj_a_call(x, sh, sc, nw, wa, wdt, cw, cb, dtb, tm):
    bsz, t_len, _ = x.shape
    nt = t_len // tm
    r8 = tm // SUBLANES
    n8 = t_len // SUBLANES
    full = lambda shape: pl.BlockSpec(shape, lambda b, i: (0,) * len(shape))
    tok = lambda c: pl.BlockSpec((None, tm, c), lambda b, i: (b, i, 0))
    return pl.pallas_call(
        functools.partial(_inproj_a_kernel, tm=tm),
        grid=(bsz, nt),
        in_specs=[tok(D_MODEL),
                  pl.BlockSpec((None, SUBLANES, D_MODEL), lambda b, i: (b, jnp.maximum(i * r8 - 1, 0), 0)),
                  pl.BlockSpec((None, SUBLANES, D_MODEL), lambda b, i: (b, jnp.minimum((i + 1) * r8, n8 - 1), 0)),
                  pl.BlockSpec((None, 1, D_MODEL), lambda b, i: (b, 0, 0)),
                  pl.BlockSpec((None, 1, D_MODEL), lambda b, i: (b, 0, 0)),
                  full((1, D_MODEL)), full((D_MODEL, A_COLS)), full((D_MODEL, LANES)),
                  full((SSD_CONV, SSD_XBC)), full((1, SSD_XBC)), full((1, LANES))],
        out_specs=[tok(SSD_INNER), tok(SSD_XBC), tok(HG_INNER), tok(2 * SSD_HEADS),
                   pl.BlockSpec((None, 2 * SSD_HEADS, tm), lambda b, i: (b, 0, i))],
        out_shape=[jax.ShapeDtypeStruct((bsz, t_len, SSD_INNER), F32),
                   jax.ShapeDtypeStruct((bsz, t_len, SSD_XBC), F32),
                   jax.ShapeDtypeStruct((bsz, t_len, HG_INNER), F32),
                   jax.ShapeDtypeStruct((bsz, t_len, 2 * SSD_HEADS), F32),
                   jax.ShapeDtypeStruct((bsz, 2 * SSD_HEADS, t_len), F32)],
        scratch_shapes=[pltpu.VMEM((tm + 2 * SUBLANES, SSD_XBC), F32)],
        compiler_params=_cparams(("arbitrary", "arbitrary")),
        name="inproj_a",
    )(x, x, x, sh, sc, nw, wa, wdt, cw, cb, dtb)


B_COLS = 4 * HG_INNER


def _inproj_b_kernel(x_ref, sh_ref, sc_ref, nw_ref, w_ref, o_ref, *, ncol):
    if ncol:
        xt = jnp.concatenate([x_ref[:, w * D_MODEL:(w + 1) * D_MODEL] for w in range(ncol)], axis=0)
    else:
        xt = x_ref[...]
    h = _prep(xt, nw_ref[...], sh_ref[...], sc_ref[...])
    o_ref[...] = jnp.dot(h, w_ref[...], preferred_element_type=F32)


def inproj_b_call(x, sh, sc, nw, wb, col_major):
    bsz, t_len, _ = x.shape
    if col_major:
        ncol = 8
        tm = ncol * GRID_W
        rows = t_len // GRID_W
        assert rows == GRID_W
        xin = x.reshape(bsz, rows, GRID_W * D_MODEL)
        x_spec = pl.BlockSpec((None, rows, ncol * D_MODEL), lambda b, i: (b, 0, i))
    else:
        ncol = 0
        tm = t_len
        xin = x
        x_spec = pl.BlockSpec((None, tm, D_MODEL), lambda b, i: (b, i, 0))
    full = lambda shape: pl.BlockSpec(shape, lambda b, i: (0,) * len(shape))
    return pl.pallas_call(
        functools.partial(_inproj_b_kernel, ncol=ncol),
        grid=(bsz, t_len // tm),
        in_specs=[x_spec,
                  pl.BlockSpec((None, 1, D_MODEL), lambda b, i: (b, 0, 0)),
                  pl.BlockSpec((None, 1, D_MODEL), lambda b, i: (b, 0, 0)),
                  full((1, D_MODEL)), full((D_MODEL, B_COLS))],
        out_specs=pl.BlockSpec((None, tm, B_COLS), lambda b, i: (b, i, 0)),
        out_shape=jax.ShapeDtypeStruct((bsz, t_len, B_COLS), F32),
        compiler_params=_cparams(("arbitrary", "arbitrary")),
        name="inproj_b",
    )(xin, sh, sc, nw, wb)


N_PAIRS = SSD_HEADS // 2


def _ssd_dir(xbc, dt, dtT, s_ref, tri, triT, na_row, na_col, fwd, need_out=True):
    c = SSD_CHUNK
    xs = xbc[:, :SSD_INNER]
    bm = xbc[:, SSD_INNER:SSD_INNER + LANES]
    cm = xbc[:, SSD_INNER + LANES:]
    col0 = 0 if fwd else SSD_HEADS
    la = dt[:, col0:col0 + SSD_HEADS] * na_row[:, col0:col0 + SSD_HEADS]
    dtr = dtT[col0:col0 + SSD_HEADS, :]
    laT = dtr * na_col[col0:col0 + SSD_HEADS, :]
    g = sum(jnp.dot(tri, p, preferred_element_type=F32) for p in _split3(la))
    gT = sum(jnp.dot(p, triT, preferred_element_type=F32) for p in _split3(laT))
    end = c - 1 if fwd else 0
    bmT = bm.T
    ii = lax.broadcasted_iota(jnp.int32, (c, c), 0)
    jj = lax.broadcasted_iota(jnp.int32, (c, c), 1)
    causal = (jj <= ii) if fwd else (jj >= ii)
    lane = lax.broadcasted_iota(jnp.int32, (c, LANES), 1)
    lo_half = lane < SSD_HEAD_DIM
    lane_s = lax.broadcasted_iota(jnp.int32, (SSD_STATE, LANES), 1) < SSD_HEAD_DIM
    outs = []
    for grp in range(SSD_GROUPS):
        if need_out:
            in_grp = (lane >= grp * SSD_STATE) & (lane < (grp + 1) * SSD_STATE)
            cm_g = jnp.where(in_grp, cm, 0.0).astype(BF16)
            gmat = jnp.dot(cm_g, bmT.astype(BF16), preferred_element_type=F32)
        bmT_g = bmT[grp * SSD_STATE:(grp + 1) * SSD_STATE, :]
        for pp in range(N_PAIRS // SSD_GROUPS):
            pair = grp * (N_PAIRS // SSD_GROUPS) + pp
            heads = (2 * pair, 2 * pair + 1)
            xs_p = xs[:, pair * LANES:(pair + 1) * LANES]
            xbd = jnp.concatenate([jnp.where(lo_half, xs_p, 0.0), jnp.where(lo_half, 0.0, xs_p)],
                                  axis=0).astype(BF16)
            ms, bws, ecols, arows = [], [], [], []
            for hd in heads:
                gcol = g[:, hd:hd + 1]
                grow = gT[hd:hd + 1, :]
                glast = grow[:, end:end + 1]
                bws.append(bmT_g * (dtr[hd:hd + 1, :] * jnp.exp(glast - grow)))
                arows.append(jnp.exp(glast))
                if need_out:
                    dec = jnp.exp(jnp.where(causal, gcol - grow, -jnp.inf))
                    ms.append(gmat * dec * dtr[hd:hd + 1, :])
                    ecols.append(jnp.exp(gcol))
            s_old = s_ref[pair]
            if need_out:
                mcat = jnp.concatenate(ms, axis=1).astype(BF16)
                zeros = jnp.zeros_like(s_old)
                s_pad = jnp.concatenate([s_old, zeros] if grp == 0 else [zeros, s_old], axis=0).astype(BF16)
                o_inter = jnp.dot(cm.astype(BF16), s_pad, preferred_element_type=F32)
                o_inter = o_inter * jnp.where(lo_half, ecols[0], ecols[1])
                outs.append(jnp.dot(mcat, xbd, preferred_element_type=F32) + o_inter)
            bw = jnp.concatenate(bws, axis=1).astype(BF16)
            s_ref[pair] = (s_old * jnp.where(lane_s, arows[0], arows[1])
                           + jnp.dot(bw, xbd, preferred_element_type=F32))
    return outs


def _ssd_scan_kernel(xf_ref, dtf_ref, dtTf_ref, xb_ref, dtb_ref, dtTb_ref, s0f_ref, s0b_ref,
                     trif_ref, trifT_ref, trib_ref, tribT_ref, nar_ref, nac_ref, dsk_ref, *rest, need_out):
    if need_out:
        of_ref, ob_ref, sfo_ref, sbo_ref, sf, sb = rest
    else:
        sfo_ref, sbo_ref, sf, sb = rest
    n = pl.program_id(1)

    @pl.when(n == 0)
    def _():
        sf[...] = s0f_ref[...]
        sb[...] = s0b_ref[...]

    xf = xf_ref[...]
    outs = _ssd_dir(xf, dtf_ref[...], dtTf_ref[...], sf, trif_ref[...], trifT_ref[...],
                    nar_ref[...], nac_ref[...], True, need_out)
    if need_out:
        of_ref[...] = jnp.concatenate(outs, axis=1) + dsk_ref[...] * xf[:, :SSD_INNER]
    outs = _ssd_dir(xb_ref[...], dtb_ref[...], dtTb_ref[...], sb, trib_ref[...], tribT_ref[...],
                    nar_ref[...], nac_ref[...], False, need_out)
    if need_out:
        ob_ref[...] = jnp.concatenate(outs, axis=1)

    @pl.when(n == pl.num_programs(1) - 1)
    def _():
        sfo_ref[...] = sf[...]
        sbo_ref[...] = sb[...]


def _ssd_consts():
    c = SSD_CHUNK
    i = np.arange(c)
    trif = (i[:, None] >= i[None, :]).astype(np.float32)
    trib = (i[:, None] <= i[None, :]).astype(np.float32)
    return [jnp.asarray(a, BF16) for a in (trif, trif.T, trib, trib.T)]


def ssd_scan_call(xbc, dt, dtT, s0f, s0b, na_row, na_col, dskip, need_out=True):
    bsz, t_len, _ = xbc.shape
    c = SSD_CHUNK
    nc = t_len // c
    fw = lambda w: pl.BlockSpec((None, c, w), lambda b, n: (b, n, 0))
    bw = lambda w: pl.BlockSpec((None, c, w), lambda b, n: (b, nc - 1 - n, 0))
    full = lambda shape: pl.BlockSpec(shape, lambda b, n: (0,) * len(shape))
    st = pl.BlockSpec((None, N_PAIRS, SSD_STATE, LANES), lambda b, n: (b, 0, 0, 0))
    st_shape = jax.ShapeDtypeStruct((bsz, N_PAIRS, SSD_STATE, LANES), F32)
    o_shape = jax.ShapeDtypeStruct((bsz, t_len, SSD_INNER), F32)
    res = pl.pallas_call(
        functools.partial(_ssd_scan_kernel, need_out=need_out),
        grid=(bsz, nc),
        in_specs=[fw(SSD_XBC), fw(2 * SSD_HEADS),
                  pl.BlockSpec((None, 2 * SSD_HEADS, c), lambda b, n: (b, 0, n)),
                  bw(SSD_XBC), bw(2 * SSD_HEADS),
                  pl.BlockSpec((None, 2 * SSD_HEADS, c), lambda b, n: (b, 0, nc - 1 - n)),
                  st, st, full((c, c)), full((c, c)), full((c, c)), full((c, c)),
                  full((1, 2 * SSD_HEADS)), full((2 * SSD_HEADS, 1)), full((1, SSD_INNER))],
        out_specs=([fw(SSD_INNER), bw(SSD_INNER)] if need_out else []) + [st, st],
        out_shape=([o_shape, o_shape] if need_out else []) + [st_shape, st_shape],
        scratch_shapes=[pltpu.VMEM((N_PAIRS, SSD_STATE, LANES), F32),
                        pltpu.VMEM((N_PAIRS, SSD_STATE, LANES), F32)],
        compiler_params=_cparams(("arbitrary", "arbitrary")),
        name="ssd_scan",
    )(xbc, dt, dtT, xbc, dt, dtT, s0f, s0b, *_ssd_consts(), na_row, na_col, dskip)
    return tuple(res) if need_out else (None, None) + tuple(res)


def _hg_consts():
    c = HG_CHUNK
    t = np.arange(c)
    m_f = np.zeros((HG_LEVELS + 1, c, c), np.float32)
    m_f[0] = np.eye(c)
    for lv in range(1, HG_LEVELS + 1):
        m = 2 ** lv
        blk = t // m
        right = (t % m) >= m // 2
        m_f[lv] = (blk[:, None] == blk[None, :]) & right[:, None] & (~right[None, :])
    m_b = np.transpose(m_f, (0, 2, 1))
    tri_f = (t[:, None] >= t[None, :]).astype(np.float32)
    tri_b = (t[:, None] <= t[None, :]).astype(np.float32)
    return jnp.asarray(tri_f, BF16), jnp.asarray(tri_b, BF16), jnp.asarray(m_f, F32), jnp.asarray(m_b, F32)


def _hg_dir(u, lb_row, st_ref, tri_ref, mask_ref, fwd, need_out=True):
    c = HG_CHUNK
    end = c - 1 if fwd else 0
    outs = []
    fcol = HG_INNER if fwd else 2 * HG_INNER
    nt = (((1,), (1,)), ((), ()))
    row = lax.broadcasted_iota(jnp.int32, (c, HG_DK), 0)

    def halves(lo, hi, half):
        if half % SUBLANES == 0:
            return jnp.concatenate([(hi if (s // half) % 2 else lo)[s:s + half] for s in range(0, c, half)], axis=0)
        return jnp.where(((row // half) % 2) == 1, hi, lo)

    def shift(x, s):
        s = s % c
        if s % SUBLANES == 0:
            return jnp.concatenate([x[c - s:], x[:c - s]], axis=0)
        return pltpu.roll(x, s, 0)

    def by_side(query_side, key_side, half):
        return halves(key_side, query_side, half) if fwd else halves(query_side, key_side, half)

    for hd in range(HG_HEADS):
        sl = slice(hd * HG_DK, (hd + 1) * HG_DK)
        lb = lb_row[:, sl]
        f = lb + (1.0 - lb) * _sigmoid(u[:, fcol + hd * HG_DK:fcol + (hd + 1) * HG_DK])
        k = 1.0 - f
        la = jnp.log(f)
        v = u[:, 3 * HG_INNER + hd * HG_DK:3 * HG_INNER + (hd + 1) * HG_DK]
        la_hi = la.astype(BF16)
        la_lo = (la - la_hi.astype(F32)).astype(BF16)
        g2 = jnp.dot(tri_ref[...], jnp.concatenate([la_hi, la_lo], axis=1), preferred_element_type=F32)
        g = g2[:, :HG_DK] + g2[:, HG_DK:]
        st = st_ref[hd]
        if not need_out:
            g_end = g[end:end + 1, :]
            st_ref[hd] = (st * jnp.exp(g_end)
                          + jnp.dot(v.T.astype(BF16), (k * jnp.exp(g_end - g)).astype(BF16),
                                    preferred_element_type=F32))
            continue
        q = _silu(u[:, sl]) * (HG_DK ** -0.5)
        scores = mask_ref[0] * lax.dot_general(q.astype(BF16), k.astype(BF16), nt, preferred_element_type=F32)
        fill = g
        for lv in range(1, HG_LEVELS + 1):
            half = 2 ** (lv - 1)
            if fwd:
                ref = halves(fill, shift(fill, half), half)
            else:
                ref = halves(shift(fill, -half), fill, half)
            decay = jnp.exp2(jnp.abs(g - ref) * (-LOG2E))
            y = (by_side(q, k, half) * decay).astype(BF16)
            scores = scores + mask_ref[lv] * lax.dot_general(y, y, nt, preferred_element_type=F32)
            if fwd:
                fill = halves(shift(fill, -half), fill, half)
            else:
                fill = halves(fill, shift(fill, half), half)
        x_cum = jnp.exp(g)
        x_rem = jnp.exp(fill - g)
        o = jnp.dot(scores.astype(BF16), v.astype(BF16), preferred_element_type=F32)
        o = o + lax.dot_general((q * x_cum).astype(BF16), st.astype(BF16), nt, preferred_element_type=F32)
        outs.append(o)
        st_ref[hd] = (st * x_cum[end:end + 1, :]
                      + jnp.dot(v.T.astype(BF16), (k * x_rem).astype(BF16), preferred_element_type=F32))
    return jnp.concatenate(outs, axis=1) if need_out else None


def _hg_scan_kernel(uf_ref, ub_ref, lbp_ref, s0f_ref, s0b_ref, trif_ref, trib_ref, mf_ref, mb_ref,
                    *rest, img_rows, need_out):
    if need_out:
        of_ref, ob_ref, sfo_ref, sbo_ref, sf, sb = rest
    else:
        sfo_ref, sbo_ref, sf, sb = rest
    n = pl.program_id(1)
    c = HG_CHUNK

    def put(o_ref, o, sub):
        if img_rows:
            cols = c // img_rows
            for j in range(cols):
                lo = (sub * cols + j) * HG_INNER
                o_ref[:, lo:lo + HG_INNER] = o[j * img_rows:(j + 1) * img_rows]
        else:
            o_ref[sub * c:(sub + 1) * c, :] = o

    @pl.when(n == 0)
    def _():
        sf[...] = s0f_ref[...]
        sb[...] = s0b_ref[...]

    p = lbp_ref[...]
    mx = jnp.max(p, axis=0, keepdims=True)
    e = jnp.exp(p - mx)
    lb = e[0:1, :] / jnp.sum(e, axis=0, keepdims=True)
    for sub in range(HG_STEP_CHUNKS):
        rev = HG_STEP_CHUNKS - 1 - sub
        o_f = _hg_dir(uf_ref[sub * c:(sub + 1) * c, :], lb[:, :HG_INNER], sf, trif_ref, mf_ref, True, need_out)
        o_b = _hg_dir(ub_ref[rev * c:(rev + 1) * c, :], lb[:, HG_INNER:], sb, trib_ref, mb_ref, False, need_out)
        if need_out:
            put(of_ref, o_f, sub)
            put(ob_ref, o_b, rev)

    @pl.when(n == pl.num_programs(1) - 1)
    def _():
        sfo_ref[...] = sf[...]
        sbo_ref[...] = sb[...]


def hg_scan_call(u, lbp, s0f, s0b, row_major_out, need_out=True):
    bsz, t_len, _ = u.shape
    c = HG_CHUNK * HG_STEP_CHUNKS
    nc = t_len // c
    assert nc * c == t_len
    full = lambda shape: pl.BlockSpec(shape, lambda b, n: (0,) * len(shape))
    st = pl.BlockSpec((None, HG_HEADS, HG_DK, HG_DK), lambda b, n: (b, 0, 0, 0))
    st_shape = jax.ShapeDtypeStruct((bsz, HG_HEADS, HG_DK, HG_DK), F32)
    if row_major_out:
        img_rows = t_len // GRID_W
        cols = c // img_rows
        assert cols * img_rows == c and HG_CHUNK % img_rows == 0
        o_shape = jax.ShapeDtypeStruct((bsz, img_rows, GRID_W * HG_INNER), F32)
        of_spec = pl.BlockSpec((None, img_rows, cols * HG_INNER), lambda b, n: (b, 0, n))
        ob_spec = pl.BlockSpec((None, img_rows, cols * HG_INNER), lambda b, n: (b, 0, nc - 1 - n))
    else:
        img_rows = 0
        o_shape = jax.ShapeDtypeStruct((bsz, t_len, HG_INNER), F32)
        of_spec = pl.BlockSpec((None, c, HG_INNER), lambda b, n: (b, n, 0))
        ob_spec = pl.BlockSpec((None, c, HG_INNER), lambda b, n: (b, nc - 1 - n, 0))
    tri_f, tri_b, m_f, m_b = _hg_consts()
    res = pl.pallas_call(
        functools.partial(_hg_scan_kernel, img_rows=img_rows, need_out=need_out),
        grid=(bsz, nc),
        in_specs=[pl.BlockSpec((None, c, B_COLS), lambda b, n: (b, n, 0)),
                  pl.BlockSpec((None, c, B_COLS), lambda b, n: (b, nc - 1 - n, 0)),
                  full((2, 2 * HG_INNER)), st, st,
                  full(tri_f.shape), full(tri_b.shape), full(m_f.shape), full(m_b.shape)],
        out_specs=([of_spec, ob_spec] if need_out else []) + [st, st],
        out_shape=([o_shape, o_shape] if need_out else []) + [st_shape, st_shape],
        scratch_shapes=[pltpu.VMEM((HG_HEADS, HG_DK, HG_DK), F32), pltpu.VMEM((HG_HEADS, HG_DK, HG_DK), F32)],
        compiler_params=_cparams(("arbitrary", "arbitrary")),
        name="hg_scan",
    )(u, u, lbp, s0f, s0b, tri_f, tri_b, m_f, m_b)
    if not need_out:
        return (None, None) + tuple(res)
    return tuple(res)


ROW_LINES = D_MODEL // LANES


def _to_token_tiles(ref, val, n_rows):
    for c in range(ROW_LINES):
        ref[pl.ds(c, n_rows, stride=ROW_LINES), :] = val[:, c * LANES:(c + 1) * LANES]


def _from_token_tiles(ref, n_rows, first_row=0):
    return jnp.concatenate([ref[pl.ds(first_row * ROW_LINES + c, n_rows, stride=ROW_LINES), :]
                            for c in range(ROW_LINES)], axis=1)


def _tile_of(ref, row):
    return ref.at[pl.ds(pl.multiple_of(row * ROW_LINES, ROW_LINES), ROW_LINES)]


def _post_kernel(x_ref, sof_ref, sob_ref, z_ref, hof_ref, hob_ref, hg_ref,
                 snw_ref, hnw_ref, wo_ref, pnw_ref, gm_ref, fnw_ref, shf_ref, scf_ref, rwh_ref, rwl_ref, rb_ref,
                 tri_ref, x1_ref, hx_ref, idx_ref, rank_ref, gate_ref, cnt_ref, carry, hscr, *, tm):
    first = (pl.program_id(0) == 0) & (pl.program_id(1) == 0)

    @pl.when(first)
    def _():
        carry[...] = jnp.zeros_like(carry)

    y = (sof_ref[...] + sob_ref[...]) * _silu(z_ref[...])
    y = _rms(y, snw_ref[...])
    ho = hof_ref[...] + hob_ref[...]
    n_img = tm // GRID_W
    for w in range(GRID_W):
        for cc in range(HG_INNER // LANES):
            lo = w * HG_INNER + cc * LANES
            hscr[cc, pl.ds(w, n_img, stride=GRID_W), :] = ho[:, lo:lo + LANES]
    o = jnp.concatenate([hscr[cc] for cc in range(HG_INNER // LANES)], axis=1)
    hnw = hnw_ref[...]
    o = jnp.concatenate([_rms(o[:, h * HG_DK:(h + 1) * HG_DK], hnw[:, h * HG_DK:(h + 1) * HG_DK])
                         for h in range(HG_HEADS)], axis=1)
    o = o * _silu(hg_ref[...])
    mix = (jnp.dot(y.astype(BF16), wo_ref[:SSD_INNER, :], preferred_element_type=F32)
           + jnp.dot(o.astype(BF16), wo_ref[SSD_INNER:, :], preferred_element_type=F32))
    x1 = x_ref[...] + gm_ref[...] * _rms(mix, pnw_ref[...])
    x1_ref[...] = x1
    hx = _rms(x1, fnw_ref[...]) * (1.0 + scf_ref[...]) + shf_ref[...]
    _to_token_tiles(hx_ref, hx, tm)
    hx_hi = hx.astype(BF16)
    hx_lo = (hx - hx_hi.astype(F32)).astype(BF16)
    rwh = rwh_ref[...]
    logits = (jnp.dot(hx_hi, rwh, preferred_element_type=F32) + jnp.dot(hx_lo, rwh, preferred_element_type=F32)
              + jnp.dot(hx_hi, rwl_ref[...], preferred_element_type=F32)) + rb_ref[...]
    work = logits.T[:N_EXPERTS, :]
    erow = lax.broadcasted_iota(jnp.int32, (N_EXPERTS, tm), 0)
    vals, idxs = [], []
    for _ in range(TOP_K):
        m = jnp.max(work, axis=0, keepdims=True)
        ix = jnp.min(jnp.where(work == m, erow, N_EXPERTS), axis=0, keepdims=True)
        vals.append(m)
        idxs.append(ix)
        work = jnp.where(erow == ix, -jnp.inf, work)
    es = [jnp.exp(v - vals[0]) for v in vals]
    den = es[0] + es[1] + es[2] + es[3]
    onehots = [(erow == ix) for ix in idxs]
    multi = sum(oh.astype(F32) for oh in onehots)
    before = jnp.dot(multi.astype(BF16), tri_ref[...], preferred_element_type=F32) + carry[...]
    carry[...] = carry[...] + jnp.sum(multi, axis=1, keepdims=True)
    sub = lax.broadcasted_iota(jnp.int32, (SUBLANES, tm), 0)
    idx_o = jnp.zeros((SUBLANES, tm), jnp.int32)
    rank_o = jnp.zeros((SUBLANES, tm), jnp.int32)
    gate_o = jnp.zeros((SUBLANES, tm), F32)
    for k in range(TOP_K):
        rk = jnp.sum(jnp.where(onehots[k], before, 0.0), axis=0, keepdims=True)
        idx_o = jnp.where(sub == k, idxs[k], idx_o)
        rank_o = jnp.where(sub == k, rk.astype(jnp.int32), rank_o)
        gate_o = jnp.where(sub == k, es[k] / den, gate_o)
    idx_ref[...] = idx_o
    rank_ref[...] = rank_o
    gate_ref[...] = gate_o
    cnt_ref[...] = jnp.broadcast_to(carry[...], (N_EXPERTS, LANES))


def post_call(x, sof, sob, z, hof, hob, hg, snw, hnw, wo, pnw, gm, fnw, shf, scf, rwh, rwl, rb):
    bsz, t_len, _ = x.shape
    tm = TOK_TILE
    nt = t_len // tm
    n_tok = bsz * t_len
    tok = lambda c: pl.BlockSpec((None, tm, c), lambda b, i: (b, i, 0))
    full = lambda shape: pl.BlockSpec(shape, lambda b, i: (0,) * len(shape))
    per_b = pl.BlockSpec((None, 1, D_MODEL), lambda b, i: (b, 0, 0))
    flat = lambda c: pl.BlockSpec((tm, c), lambda b, i: (b * nt + i, 0))
    rout = pl.BlockSpec((SUBLANES, tm), lambda b, i: (b * nt + i, 0))
    n_tiles = n_tok // tm
    ii = np.arange(tm)
    tri = jnp.asarray(ii[:, None] < ii[None, :], BF16)
    assert tm % (GRID_W * SUBLANES) == 0 and hof.shape == (bsz, t_len // GRID_W, GRID_W * HG_INNER)
    img = pl.BlockSpec((None, tm // GRID_W, GRID_W * HG_INNER), lambda b, i: (b, i, 0))
    return pl.pallas_call(
        functools.partial(_post_kernel, tm=tm),
        grid=(bsz, nt),
        in_specs=[tok(D_MODEL), tok(SSD_INNER), tok(SSD_INNER), tok(SSD_INNER), img, img,
                  tok(HG_INNER), full((1, SSD_INNER)), full((1, HG_INNER)), full((D_MODEL, D_MODEL)),
                  full((1, D_MODEL)), per_b, full((1, D_MODEL)), per_b, per_b,
                  full((D_MODEL, LANES)), full((D_MODEL, LANES)), full((1, LANES)), full((tm, tm))],
        out_specs=[flat(D_MODEL), pl.BlockSpec((tm * ROW_LINES, LANES), lambda b, i: (b * nt + i, 0)),
                   rout, rout, rout, pl.BlockSpec((N_EXPERTS, LANES), lambda b, i: (0, 0))],
        out_shape=[jax.ShapeDtypeStruct((n_tok, D_MODEL), F32),
                   jax.ShapeDtypeStruct((n_tok * ROW_LINES, LANES), F32),
                   jax.ShapeDtypeStruct((n_tiles * SUBLANES, tm), jnp.int32),
                   jax.ShapeDtypeStruct((n_tiles * SUBLANES, tm), jnp.int32),
                   jax.ShapeDtypeStruct((n_tiles * SUBLANES, tm), F32),
                   jax.ShapeDtypeStruct((N_EXPERTS, LANES), F32)],
        scratch_shapes=[pltpu.VMEM((N_EXPERTS, 1), F32), pltpu.VMEM((HG_INNER // LANES, tm, LANES), F32)],
        compiler_params=_cparams(("arbitrary", "arbitrary")),
        name="post",
    )(x, sof, sob, z, hof, hob, hg, snw, hnw, wo, pnw, gm, fnw, shf, scf, rwh, rwl, rb, tri)


def _wait_rows(hbm_ref, n_rows, sem):
    n = n_rows * ROW_LINES
    pltpu.make_async_copy(hbm_ref.at[pl.ds(0, n)], hbm_ref.at[pl.ds(0, n)], sem).wait()


PAD_RUNS = tuple(2 ** j for j in range(int(math.log2(MOE_ROWS))))
DISPATCH_TILES = 2


def _dispatch_kernel(zstart_ref, zpad_ref, dest_ref, hx_ref, buf_ref, zrows, sem, zsem, *, tm):
    i = pl.program_id(0)

    def zero_rows(first, n):
        first = pl.multiple_of(first * ROW_LINES, ROW_LINES)
        return pltpu.make_async_copy(zrows.at[pl.ds(0, n * ROW_LINES)], buf_ref.at[pl.ds(first, n * ROW_LINES)], zsem)

    def for_pad_runs(act):
        def per_expert(e, carry):
            pad = zpad_ref[e]
            off = zstart_ref[e]
            for run in PAD_RUNS:
                @pl.when((pad & run) != 0)
                def _():
                    act(zero_rows(off, run))
                off = off + (pad & run)
            return carry

        lax.fori_loop(0, N_EXPERTS, per_expert, 0)

        def tail(j, carry):
            act(zero_rows(j * PAD_RUNS[-1], PAD_RUNS[-1]))
            return carry

        n_rows = buf_ref.shape[0] // ROW_LINES
        lax.fori_loop(zstart_ref[N_EXPERTS] // PAD_RUNS[-1], n_rows // PAD_RUNS[-1], tail, 0)

    @pl.when(i == 0)
    def _():
        zrows[...] = jnp.zeros_like(zrows)
        for_pad_runs(lambda cp: cp.start())

    def issue(t, carry):
        for sub in range(DISPATCH_TILES):
            for k in range(TOP_K):
                pltpu.make_async_copy(_tile_of(hx_ref, sub * tm + t),
                                      _tile_of(buf_ref, dest_ref[(sub * TOP_K + k) * tm + t]),
                                      sem).start(priority=k % 2)
        return carry

    lax.fori_loop(0, tm, issue, 0, unroll=2)
    _wait_rows(buf_ref, DISPATCH_TILES * tm * TOP_K, sem)

    @pl.when(i == 0)
    def _():
        for_pad_runs(lambda cp: cp.wait())


def dispatch_call(zstart, zpad, dest_flat, hx, n_rows):
    n_tok = hx.shape[0] // ROW_LINES
    tm = TOK_TILE
    step_tok = DISPATCH_TILES * tm
    return pl.pallas_call(
        functools.partial(_dispatch_kernel, tm=tm),
        grid=(n_tok // step_tok,),
        in_specs=[pl.BlockSpec(memory_space=pltpu.SMEM), pl.BlockSpec(memory_space=pltpu.SMEM),
                  pl.BlockSpec((step_tok * TOP_K,), lambda i: (i,), memory_space=pltpu.SMEM),
                  pl.BlockSpec((step_tok * ROW_LINES, LANES), lambda i: (i, 0))],
        out_specs=pl.BlockSpec(memory_space=pl.ANY),
        out_shape=jax.ShapeDtypeStruct((n_rows * ROW_LINES, LANES), F32),
        scratch_shapes=[pltpu.VMEM((PAD_RUNS[-1] * ROW_LINES, LANES), F32), pltpu.SemaphoreType.DMA(()),
                        pltpu.SemaphoreType.DMA(())],
        compiler_params=_cparams(("arbitrary",)),
        name="dispatch",
    )(zstart, zpad, dest_flat, hx)


W1_TCOLS = 256


def _experts_kernel(be_ref, nu_ref, nxt_ref, x_ref, w1_hbm, b1_ref, w2_hbm, b2_ref, y_ref,
                    w1f, w2f, w1t, w2s, tbuf, wsem):
    i = pl.program_id(0)
    live = i < nu_ref[0]
    new_expert = (i == 0) | (be_ref[i] != be_ref[jnp.maximum(i - 1, 0)])

    def weight_copies(e):
        return (pltpu.make_async_copy(w1_hbm.at[e], w1f, wsem.at[0]),
                pltpu.make_async_copy(w2_hbm.at[e], w2f, wsem.at[1]))

    @pl.when(live & new_expert)
    def _():
        @pl.when(i == 0)
        def _():
            for cp in weight_copies(be_ref[0]):
                cp.start()

        for cp in weight_copies(be_ref[i]):
            cp.wait()
        half = W1_TCOLS // 2
        for c in range(2 * D_FF // W1_TCOLS):
            for j in range(D_MODEL // LANES):
                ks = slice(j * LANES, (j + 1) * LANES)
                tbuf[j] = w1f[ks, c * W1_TCOLS:(c + 1) * W1_TCOLS].T
                w1t[c * half:(c + 1) * half, ks] = tbuf[j, pl.ds(0, half, stride=2), :].astype(BF16)
                w1t[D_FF + c * half:D_FF + (c + 1) * half, ks] = tbuf[j, pl.ds(1, half, stride=2), :].astype(BF16)
        w2s[...] = w2f[...].astype(BF16)

        @pl.when(nxt_ref[i] >= 0)
        def _():
            for cp in weight_copies(nxt_ref[i]):
                cp.start(priority=1)

    @pl.when(live)
    def _():
        xb = _from_token_tiles(x_ref, MOE_ROWS).astype(BF16)
        u = lax.dot_general(xb, w1t[...], (((1,), (1,)), ((), ())), preferred_element_type=F32) + b1_ref[...]
        glu = jnp.minimum(u[:, :D_FF], SWIGLU_LIMIT)
        lin = jnp.clip(u[:, D_FF:], -SWIGLU_LIMIT, SWIGLU_LIMIT)
        a = glu * _sigmoid(SWIGLU_ALPHA * glu) * (lin + 1.0)
        y = jnp.dot(a.astype(BF16), w2s[...], preferred_element_type=F32) + b2_ref[...]
        _to_token_tiles(y_ref, y, MOE_ROWS)

    @pl.when(jnp.logical_not(live))
    def _():
        y_ref[...] = jnp.zeros_like(y_ref)


def experts_call(block_e, n_used, next_e, xs, w1, b1p, w2, b2):
    rows = xs.shape[0] // ROW_LINES
    r = MOE_ROWS
    nb = rows // r
    blk = (r * ROW_LINES, LANES)
    grid_spec = pltpu.PrefetchScalarGridSpec(
        num_scalar_prefetch=3,
        grid=(nb,),
        in_specs=[pl.BlockSpec(blk, lambda i, be, nu, nx: (jnp.maximum(jnp.minimum(i, nu[0] - 1), 0), 0)),
                  pl.BlockSpec(memory_space=pl.ANY),
                  pl.BlockSpec((None, 1, 2 * D_FF), lambda i, be, nu, nx: (be[i], 0, 0)),
                  pl.BlockSpec(memory_space=pl.ANY),
                  pl.BlockSpec((None, 1, D_MODEL), lambda i, be, nu, nx: (be[i], 0, 0))],
        out_specs=pl.BlockSpec(blk, lambda i, be, nu, nx: (i, 0)),
        scratch_shapes=[pltpu.VMEM((D_MODEL, 2 * D_FF), F32), pltpu.VMEM((D_FF, D_MODEL), F32),
                        pltpu.VMEM((2 * D_FF, D_MODEL), BF16), pltpu.VMEM((D_FF, D_MODEL), BF16),
                        pltpu.VMEM((D_MODEL // LANES, W1_TCOLS, LANES), F32),
                        pltpu.SemaphoreType.DMA((2,))],
    )
    return pl.pallas_call(
        _experts_kernel,
        grid_spec=grid_spec,
        out_shape=jax.ShapeDtypeStruct((rows * ROW_LINES, LANES), F32),
        compiler_params=_cparams(("arbitrary",)),
        name="experts",
    )(block_e, n_used, next_e, xs, w1, b1p, w2, b2)


def _combine_kernel(dest_ref, dest_next_ref, y_ref, gate_ref, x1_ref, gf_ref, nw_ref, o_ref, buf, sem, *, tm):
    i = pl.program_id(0)
    slot = i % 2

    def gather_tile(dref, into):
        def issue(t, carry):
            for k in range(TOP_K):
                pltpu.make_async_copy(_tile_of(y_ref, dref[k * tm + t]), _tile_of(buf.at[into], k * tm + t),
                                      sem.at[into]).start(priority=k % 2)
            return carry

        lax.fori_loop(0, tm, issue, 0, unroll=4)

    @pl.when(i == 0)
    def _():
        gather_tile(dest_ref, 0)

    @pl.when(i + 1 < pl.num_programs(0))
    def _():
        gather_tile(dest_next_ref, 1 - slot)

    _wait_rows(y_ref, tm * TOP_K, sem.at[slot])
    gate = gate_ref[...].T
    fx = None
    for k in range(TOP_K):
        yk = jnp.concatenate([buf[slot, pl.ds(k * tm * ROW_LINES + c, tm, stride=ROW_LINES), :]
                              for c in range(ROW_LINES)], axis=1)
        fx = gate[:, k:k + 1] * yk if fx is None else fx + gate[:, k:k + 1] * yk
    o_ref[...] = x1_ref[...] + gf_ref[...] * _rms(fx, nw_ref[...])


def combine_call(dest_flat, ys, gates, x1, gf, nw, bsz):
    n_tok = x1.shape[0]
    tm = TOK_TILE
    nt = n_tok // bsz // tm
    n_steps = n_tok // tm
    return pl.pallas_call(
        functools.partial(_combine_kernel, tm=tm),
        grid=(n_steps,),
        in_specs=[pl.BlockSpec((tm * TOP_K,), lambda i: (i,), memory_space=pltpu.SMEM),
                  pl.BlockSpec((tm * TOP_K,), lambda i: (jnp.minimum(i + 1, n_steps - 1),),
                               memory_space=pltpu.SMEM),
                  pl.BlockSpec(memory_space=pl.ANY),
                  pl.BlockSpec((SUBLANES, tm), lambda i: (i, 0)),
                  pl.BlockSpec((tm, D_MODEL), lambda i: (i, 0)),
                  pl.BlockSpec((None, 1, D_MODEL), lambda i: (i // nt, 0, 0)),
                  pl.BlockSpec((1, D_MODEL), lambda i: (0, 0))],
        out_specs=pl.BlockSpec((tm, D_MODEL), lambda i: (i, 0)),
        out_shape=jax.ShapeDtypeStruct((n_tok, D_MODEL), F32),
        scratch_shapes=[pltpu.VMEM((2, TOP_K * tm * ROW_LINES, LANES), F32), pltpu.SemaphoreType.DMA((2,))],
        compiler_params=_cparams(("arbitrary",)),
        name="combine",
    )(dest_flat, dest_flat, ys, gates, x1, gf, nw)


def kernel(x, c, ctx, c_ctx, ada_w, ada_b, mix_pre_norm, mix_post_norm, w_in, w_out, ssd_conv_w, ssd_conv_b,
           ssd_dt_bias, ssd_a_log, ssd_d, ssd_norm, hg_lb, hg_norm, ffn_pre_norm, ffn_post_norm, router_w,
           router_b, moe_w1, moe_b1, moe_w2, moe_b2):
    bsz, t_len, d = x.shape
    assert ada_w.shape[0] == 1 and d == D_MODEL and bsz <= SUBLANES - 1
    n_tok = bsz * t_len

    cc = jnp.zeros((SUBLANES, d), F32).at[:bsz].set(c).at[bsz].set(c_ctx)
    mod = ada_call(cc, ada_w[0], ada_b[0][None, :])
    sh_m, sc_m, g_m, sh_f, sc_f, g_f = [m[:bsz, None, :] for m in jnp.split(mod, 6, axis=-1)]
    csh_m, csc_m = [jnp.broadcast_to(m[bsz][None, None, :], (bsz, 1, d)) for m in jnp.split(mod, 6, axis=-1)[:2]]

    w = w_in[0]
    wa = jnp.concatenate([w[:, :SSD_INNER + SSD_XBC], w[:, SSD_COLS + 4 * HG_INNER:]], axis=1).astype(BF16)
    wdt = jnp.zeros((d, LANES), F32).at[:, :2 * SSD_HEADS].set(w[:, SSD_INNER + SSD_XBC:SSD_COLS]).astype(BF16)
    wb = w[:, SSD_COLS:SSD_COLS + 4 * HG_INNER].astype(BF16)
    dtb = jnp.zeros((1, LANES), F32).at[0, :2 * SSD_HEADS].set(ssd_dt_bias[0].reshape(-1))
    nw = mix_pre_norm[0][None, :]
    cw, cb = ssd_conv_w[0], ssd_conv_b[0][None, :]
    neg_a = -jnp.exp(ssd_a_log[0].astype(F32)).reshape(1, 2 * SSD_HEADS)
    dskip = jnp.repeat(ssd_d[0], SSD_HEAD_DIM)[None, :]
    lbp = hg_lb.astype(F32).reshape(2, 2 * HG_INNER)

    _, cxbc, _, cdt, cdtT = inproj_a_call(ctx, csh_m, csc_m, nw, wa, wdt, cw, cb, dtb, tm=ctx.shape[1])
    cu = inproj_b_call(ctx, csh_m, csc_m, nw, wb, col_major=False)
    z_ssd = jnp.zeros((bsz, N_PAIRS, SSD_STATE, LANES), F32)
    z_hg = jnp.zeros((bsz, HG_HEADS, HG_DK, HG_DK), F32)
    _, _, ssf, ssb = ssd_scan_call(cxbc, cdt, cdtT, z_ssd, z_ssd, neg_a, neg_a.reshape(-1, 1), dskip,
                                   need_out=False)
    _, _, hsf, hsb = hg_scan_call(cu, lbp, z_hg, z_hg, row_major_out=False, need_out=False)

    zg, xbc, hgate, dt, dtT = inproj_a_call(x, sh_m, sc_m, nw, wa, wdt, cw, cb, dtb, tm=512)
    ub = inproj_b_call(x, sh_m, sc_m, nw, wb, col_major=True)
    sof, sob, _, _ = ssd_scan_call(xbc, dt, dtT, ssf, ssb, neg_a, neg_a.reshape(-1, 1), dskip)
    hof, hob, _, _ = hg_scan_call(ub, lbp, hsf, hsb, row_major_out=True)

    rw = jnp.zeros((d, LANES), F32).at[:, :N_EXPERTS].set(router_w[0])
    rwh = rw.astype(BF16)
    rwl = (rw - rwh.astype(F32)).astype(BF16)
    rb = jnp.zeros((1, LANES), F32).at[0, :N_EXPERTS].set(router_b[0])
    x1, hx, idx, rank, gates, cnt = post_call(
        x, sof, sob, zg, hof, hob, hgate, ssd_norm[0][None, :], hg_norm[0][None, :], w_out[0].astype(BF16),
        mix_post_norm[0][None, :], g_m, ffn_pre_norm[0][None, :], sh_f, sc_f, rwh, rwl, rb)

    r = MOE_ROWS
    n_blocks = (n_tok * TOP_K + N_EXPERTS * (r - 1) + r - 1) // r
    n_tiles = n_tok // TOK_TILE
    counts = cnt[:, 0].astype(jnp.int32)
    padded = (counts + r - 1) // r * r
    pend = jnp.cumsum(padded)
    pstart = pend - padded
    idx3 = idx.reshape(n_tiles, SUBLANES, TOK_TILE)[:, :TOP_K, :]
    rank3 = rank.reshape(n_tiles, SUBLANES, TOK_TILE)[:, :TOP_K, :]
    dest = rank3
    for e in range(N_EXPERTS):
        dest = dest + jnp.where(idx3 == e, pstart[e], 0)
    dest = dest.reshape(-1)
    starts = jnp.arange(n_blocks, dtype=jnp.int32) * r
    block_e = jnp.minimum(jnp.sum((pend[None, :] <= starts[:, None]).astype(jnp.int32), axis=1), N_EXPERTS - 1)
    n_used = (pend[-1:] // r).astype(jnp.int32)
    run_end = pend[block_e] // r
    next_e = jnp.where(run_end < n_used[0], block_e[jnp.minimum(run_end, n_blocks - 1)], -1).astype(jnp.int32)

    zstart = jnp.concatenate([pstart + counts, pend[-1:]])
    xs = dispatch_call(zstart, padded - counts, dest, hx, n_blocks * r)
    b1p = jnp.concatenate([moe_b1[0][:, 0::2], moe_b1[0][:, 1::2]], axis=-1)[:, None, :]
    ys = experts_call(block_e, n_used, next_e, xs, moe_w1[0], b1p, moe_w2[0], moe_b2[0][:, None, :])
    out = combine_call(dest, ys, gates, x1, g_f, ffn_post_norm[0][None, :], bsz)
    return out.reshape(bsz, t_len, d)
```

```python
import functools
import math

import numpy as np
import jax
import jax.numpy as jnp
from jax import lax
from jax.experimental import pallas as pl
from jax.experimental.pallas import tpu as pltpu

F32 = jnp.float32
BF16 = jnp.bfloat16
HIGHEST = lax.Precision.HIGHEST

D_MODEL = 1024
GRID_W = 64
SSD_HEADS = 8
SSD_HEAD_DIM = 64
SSD_INNER = 512
SSD_STATE = 64
SSD_GROUPS = 2
SSD_CONV = 5
SSD_XBC = 768
SSD_COLS = 1296
HG_HEADS = 4
HG_DK = 128
HG_INNER = 512
N_EXPERTS = 32
TOP_K = 4
D_FF = 1024
SWIGLU_ALPHA = 1.702
SWIGLU_LIMIT = 7.0
EPS = 1e-6

LANES = 128
SUBLANES = 8
VMEM_LIMIT = 56 * 1024 * 1024

SSD_CHUNK = 256
SSD_STEP_CHUNKS = 2
HG_CHUNK = 128
HG_LEVELS = 7
HG_STEP_CHUNKS = 2
MOE_ROWS = 256
TOK_TILE = 512
LOG2E = 1.4426950408889634


def _cparams(sem):
    return pltpu.CompilerParams(dimension_semantics=sem, vmem_limit_bytes=VMEM_LIMIT)


def _sigmoid(x):
    return 1.0 / (1.0 + jnp.exp(-x))


def _silu(x):
    return x * _sigmoid(x)


def _rms(x, w):
    return x * lax.rsqrt(jnp.mean(x * x, axis=-1, keepdims=True) + EPS) * w


def _split3(v):
    hi = v.astype(BF16)
    r1 = v - hi.astype(F32)
    mid = r1.astype(BF16)
    lo = (r1 - mid.astype(F32)).astype(BF16)
    return hi, mid, lo


def _ada_kernel(c_ref, w_ref, b_ref, o_ref):
    s = _silu(c_ref[...])
    o_ref[...] = jnp.dot(s, w_ref[...], precision=HIGHEST, preferred_element_type=F32) + b_ref[...]


def ada_call(cc, w, b):
    n = w.shape[1]
    tn = 1536
    return pl.pallas_call(
        _ada_kernel,
        grid=(n // tn,),
        in_specs=[pl.BlockSpec((SUBLANES, D_MODEL), lambda j: (0, 0)),
                  pl.BlockSpec((D_MODEL, tn), lambda j: (0, j)),
                  pl.BlockSpec((1, tn), lambda j: (0, j))],
        out_specs=pl.BlockSpec((SUBLANES, tn), lambda j: (0, j)),
        out_shape=jax.ShapeDtypeStruct((SUBLANES, n), F32),
        compiler_params=_cparams(("arbitrary",)),
        name="ada",
    )(cc, w, b)


A_COLS = SSD_INNER + SSD_XBC + HG_INNER


def _prep(xt, nw, sh, sc):
    return (_rms(xt, nw) * (1.0 + sc) + sh).astype(BF16)


def _inproj_a_kernel(x_ref, xp_ref, xn_ref, sh_ref, sc_ref, nw_ref, wa_ref, wdt_ref, cw_ref, cb_ref, dtb_ref,
                     z_ref, xbc_ref, g_ref, dt_ref, dtT_ref, scr, *, tm):
    i = pl.program_id(1)
    last = pl.num_programs(1) - 1
    nw, sh, sc = nw_ref[...], sh_ref[...], sc_ref[...]
    h = _prep(x_ref[...], nw, sh, sc)
    ua = jnp.dot(h, wa_ref[...], preferred_element_type=F32)
    z_ref[...] = ua[:, :SSD_INNER]
    g_ref[...] = ua[:, SSD_INNER + SSD_XBC:]
    wx = wa_ref[:, SSD_INNER:SSD_INNER + SSD_XBC]
    up = jnp.dot(_prep(xp_ref[...], nw, sh, sc), wx, preferred_element_type=F32)
    un = jnp.dot(_prep(xn_ref[...], nw, sh, sc), wx, preferred_element_type=F32)
    scr[0:SUBLANES, :] = jnp.where(i > 0, up, 0.0)
    scr[SUBLANES:SUBLANES + tm, :] = ua[:, SSD_INNER:SSD_INNER + SSD_XBC]
    scr[SUBLANES + tm:, :] = jnp.where(i < last, un, 0.0)
    acc = jnp.broadcast_to(cb_ref[...], (tm, SSD_XBC))
    pad = SSD_CONV // 2
    for k in range(SSD_CONV):
        off = SUBLANES - pad + k
        acc = acc + cw_ref[k:k + 1, :] * scr[off:off + tm, :]
    xbc_ref[...] = _silu(acc)
    draw = jnp.dot(h, wdt_ref[...], preferred_element_type=F32) + dtb_ref[...]
    dt = jnp.maximum(draw, 0.0) + jnp.log(1.0 + jnp.exp(-jnp.abs(draw)))
    dt_ref[...] = dt[:, :2 * SSD_HEADS]
    dtT_ref[...] = dt.T[:2 * SSD_HEADS, :]


def inproj_a_call(x, sh, sc, nw, wa, wdt, cw, cb, dtb, tm):
    bsz, t_len, _ = x.shape
    nt = t_len // tm
    r8 = tm // SUBLANES
    n8 = t_len // SUBLANES
    full = lambda shape: pl.BlockSpec(shape, lambda b, i: (0,) * len(shape))
    tok = lambda c: pl.BlockSpec((None, tm, c), lambda b, i: (b, i, 0))
    return pl.pallas_call(
        functools.partial(_inproj_a_kernel, tm=tm),
        grid=(bsz, nt),
        in_specs=[tok(D_MODEL),
                  pl.BlockSpec((None, SUBLANES, D_MODEL), lambda b, i: (b, jnp.maximum(i * r8 - 1, 0), 0)),
                  pl.BlockSpec((None, SUBLANES, D_MODEL), lambda b, i: (b, jnp.minimum((i + 1) * r8, n8 - 1), 0)),
                  pl.BlockSpec((None, 1, D_MODEL), lambda b, i: (b, 0, 0)),
                  pl.BlockSpec((None, 1, D_MODEL), lambda b, i: (b, 0, 0)),
                  full((1, D_MODEL)), full((D_MODEL, A_COLS)), full((D_MODEL, LANES)),
                  full((SSD_CONV, SSD_XBC)), full((1, SSD_XBC)), full((1, LANES))],
        out_specs=[tok(SSD_INNER), tok(SSD_XBC), tok(HG_INNER), tok(2 * SSD_HEADS),
                   pl.BlockSpec((None, 2 * SSD_HEADS, tm), lambda b, i: (b, 0, i))],
        out_shape=[jax.ShapeDtypeStruct((bsz, t_len, SSD_INNER), F32),
                   jax.ShapeDtypeStruct((bsz, t_len, SSD_XBC), F32),
                   jax.ShapeDtypeStruct((bsz, t_len, HG_INNER), F32),
                   jax.ShapeDtypeStruct((bsz, t_len, 2 * SSD_HEADS), F32),
                   jax.ShapeDtypeStruct((bsz, 2 * SSD_HEADS, t_len), F32)],
        scratch_shapes=[pltpu.VMEM((tm + 2 * SUBLANES, SSD_XBC), F32)],
        compiler_params=_cparams(("arbitrary", "arbitrary")),
        name="inproj_a",
    )(x, x, x, sh, sc, nw, wa, wdt, cw, cb, dtb)


B_COLS = 4 * HG_INNER


def _inproj_b_kernel(x_ref, sh_ref, sc_ref, nw_ref, w_ref, o_ref, *, ncol):
    if ncol:
        xt = jnp.concatenate([x_ref[:, w * D_MODEL:(w + 1) * D_MODEL] for w in range(ncol)], axis=0)
    else:
        xt = x_ref[...]
    h = _prep(xt, nw_ref[...], sh_ref[...], sc_ref[...])
    o_ref[...] = jnp.dot(h, w_ref[...], preferred_element_type=F32)


def inproj_b_call(x, sh, sc, nw, wb, col_major):
    bsz, t_len, _ = x.shape
    if col_major:
        ncol = 8
        tm = ncol * GRID_W
        rows = t_len // GRID_W
        assert rows == GRID_W
        xin = x.reshape(bsz, rows, GRID_W * D_MODEL)
        x_spec = pl.BlockSpec((None, rows, ncol * D_MODEL), lambda b, i: (b, 0, i))
    else:
        ncol = 0
        tm = t_len
        xin = x
        x_spec = pl.BlockSpec((None, tm, D_MODEL), lambda b, i: (b, i, 0))
    full = lambda shape: pl.BlockSpec(shape, lambda b, i: (0,) * len(shape))
    return pl.pallas_call(
        functools.partial(_inproj_b_kernel, ncol=ncol),
        grid=(bsz, t_len // tm),
        in_specs=[x_spec,
                  pl.BlockSpec((None, 1, D_MODEL), lambda b, i: (b, 0, 0)),
                  pl.BlockSpec((None, 1, D_MODEL), lambda b, i: (b, 0, 0)),
                  full((1, D_MODEL)), full((D_MODEL, B_COLS))],
        out_specs=pl.BlockSpec((None, tm, B_COLS), lambda b, i: (b, i, 0)),
        out_shape=jax.ShapeDtypeStruct((bsz, t_len, B_COLS), F32),
        compiler_params=_cparams(("arbitrary", "arbitrary")),
        name="inproj_b",
    )(xin, sh, sc, nw, wb)


N_PAIRS = SSD_HEADS // 2


def _ssd_dir(xbc, dt, dtT, s_ref, tri, triT, na_row, na_col, fwd, need_out=True):
    c = SSD_CHUNK
    xs = xbc[:, :SSD_INNER]
    bm = xbc[:, SSD_INNER:SSD_INNER + LANES]
    cm = xbc[:, SSD_INNER + LANES:]
    col0 = 0 if fwd else SSD_HEADS
    la = dt[:, col0:col0 + SSD_HEADS] * na_row[:, col0:col0 + SSD_HEADS]
    dtr = dtT[col0:col0 + SSD_HEADS, :]
    laT = dtr * na_col[col0:col0 + SSD_HEADS, :]
    g = sum(jnp.dot(tri, p, preferred_element_type=F32) for p in _split3(la))
    gT = sum(jnp.dot(p, triT, preferred_element_type=F32) for p in _split3(laT))
    end = c - 1 if fwd else 0
    bmT = bm.T
    ii = lax.broadcasted_iota(jnp.int32, (c, c), 0)
    jj = lax.broadcasted_iota(jnp.int32, (c, c), 1)
    causal = (jj <= ii) if fwd else (jj >= ii)
    lane = lax.broadcasted_iota(jnp.int32, (c, LANES), 1)
    lo_half = lane < SSD_HEAD_DIM
    lane_s = lax.broadcasted_iota(jnp.int32, (SSD_STATE, LANES), 1) < SSD_HEAD_DIM
    outs = []
    for grp in range(SSD_GROUPS):
        if need_out:
            in_grp = (lane >= grp * SSD_STATE) & (lane < (grp + 1) * SSD_STATE)
            cm_g = jnp.where(in_grp, cm, 0.0).astype(BF16)
            gmat = jnp.dot(cm_g, bmT.astype(BF16), preferred_element_type=F32)
        bmT_g = bmT[grp * SSD_STATE:(grp + 1) * SSD_STATE, :]
        for pp in range(N_PAIRS // SSD_GROUPS):
            pair = grp * (N_PAIRS // SSD_GROUPS) + pp
            heads = (2 * pair, 2 * pair + 1)
            xs_p = xs[:, pair * LANES:(pair + 1) * LANES]
            xbd = jnp.concatenate([jnp.where(lo_half, xs_p, 0.0), jnp.where(lo_half, 0.0, xs_p)],
                                  axis=0).astype(BF16)
            ms, bws, ecols, arows = [], [], [], []
            for hd in heads:
                gcol = g[:, hd:hd + 1]
                grow = gT[hd:hd + 1, :]
                glast = grow[:, end:end + 1]
                bws.append(bmT_g * (dtr[hd:hd + 1, :] * jnp.exp(glast - grow)))
                arows.append(jnp.exp(glast))
                if need_out:
                    dec = jnp.exp(jnp.where(causal, gcol - grow, -jnp.inf))
                    ms.append(gmat * dec * dtr[hd:hd + 1, :])
                    ecols.append(jnp.exp(gcol))
            s_old = s_ref[pair]
            if need_out:
                mcat = jnp.concatenate(ms, axis=1).astype(BF16)
                zeros = jnp.zeros_like(s_old)
                s_pad = jnp.concatenate([s_old, zeros] if grp == 0 else [zeros, s_old], axis=0).astype(BF16)
                o_inter = jnp.dot(cm.astype(BF16), s_pad, preferred_element_type=F32)
                o_inter = o_inter * jnp.where(lo_half, ecols[0], ecols[1])
                outs.append(jnp.dot(mcat, xbd, preferred_element_type=F32) + o_inter)
            bw = jnp.concatenate(bws, axis=1).astype(BF16)
            s_ref[pair] = (s_old * jnp.where(lane_s, arows[0], arows[1])
                           + jnp.dot(bw, xbd, preferred_element_type=F32))
    return outs


def _ssd_scan_kernel(xf_ref, dtf_ref, dtTf_ref, xb_ref, dtb_ref, dtTb_ref, s0f_ref, s0b_ref,
                     trif_ref, trifT_ref, trib_ref, tribT_ref, nar_ref, nac_ref, dsk_ref, *rest, need_out,
                     step_chunks):
    if need_out:
        of_ref, ob_ref, sfo_ref, sbo_ref, sf, sb = rest
    else:
        sfo_ref, sbo_ref, sf, sb = rest
    n = pl.program_id(1)

    @pl.when(n == 0)
    def _():
        sf[...] = s0f_ref[...]
        sb[...] = s0b_ref[...]

    c = SSD_CHUNK
    for sub in range(step_chunks):
        rev = step_chunks - 1 - sub
        fs, bs = slice(sub * c, (sub + 1) * c), slice(rev * c, (rev + 1) * c)
        xf = xf_ref[fs, :]
        outs = _ssd_dir(xf, dtf_ref[fs, :], dtTf_ref[:, fs], sf, trif_ref[...], trifT_ref[...],
                        nar_ref[...], nac_ref[...], True, need_out)
        if need_out:
            of_ref[fs, :] = jnp.concatenate(outs, axis=1) + dsk_ref[...] * xf[:, :SSD_INNER]
        outs = _ssd_dir(xb_ref[bs, :], dtb_ref[bs, :], dtTb_ref[:, bs], sb, trib_ref[...], tribT_ref[...],
                        nar_ref[...], nac_ref[...], False, need_out)
        if need_out:
            ob_ref[bs, :] = jnp.concatenate(outs, axis=1)

    @pl.when(n == pl.num_programs(1) - 1)
    def _():
        sfo_ref[...] = sf[...]
        sbo_ref[...] = sb[...]


def _ssd_consts():
    c = SSD_CHUNK
    i = np.arange(c)
    trif = (i[:, None] >= i[None, :]).astype(np.float32)
    trib = (i[:, None] <= i[None, :]).astype(np.float32)
    return [jnp.asarray(a, BF16) for a in (trif, trif.T, trib, trib.T)]


def ssd_scan_call(xbc, dt, dtT, s0f, s0b, na_row, na_col, dskip, need_out=True):
    bsz, t_len, _ = xbc.shape
    step_chunks = SSD_STEP_CHUNKS if (t_len // SSD_CHUNK) % SSD_STEP_CHUNKS == 0 else 1
    c = SSD_CHUNK * step_chunks
    nc = t_len // c
    assert nc * c == t_len
    cc = SSD_CHUNK
    fw = lambda w: pl.BlockSpec((None, c, w), lambda b, n: (b, n, 0))
    bw = lambda w: pl.BlockSpec((None, c, w), lambda b, n: (b, nc - 1 - n, 0))
    full = lambda shape: pl.BlockSpec(shape, lambda b, n: (0,) * len(shape))
    st = pl.BlockSpec((None, N_PAIRS, SSD_STATE, LANES), lambda b, n: (b, 0, 0, 0))
    st_shape = jax.ShapeDtypeStruct((bsz, N_PAIRS, SSD_STATE, LANES), F32)
    o_shape = jax.ShapeDtypeStruct((bsz, t_len, SSD_INNER), F32)
    res = pl.pallas_call(
        functools.partial(_ssd_scan_kernel, need_out=need_out, step_chunks=step_chunks),
        grid=(bsz, nc),
        in_specs=[fw(SSD_XBC), fw(2 * SSD_HEADS),
                  pl.BlockSpec((None, 2 * SSD_HEADS, c), lambda b, n: (b, 0, n)),
                  bw(SSD_XBC), bw(2 * SSD_HEADS),
                  pl.BlockSpec((None, 2 * SSD_HEADS, c), lambda b, n: (b, 0, nc - 1 - n)),
                  st, st, full((cc, cc)), full((cc, cc)), full((cc, cc)), full((cc, cc)),
                  full((1, 2 * SSD_HEADS)), full((2 * SSD_HEADS, 1)), full((1, SSD_INNER))],
        out_specs=([fw(SSD_INNER), bw(SSD_INNER)] if need_out else []) + [st, st],
        out_shape=([o_shape, o_shape] if need_out else []) + [st_shape, st_shape],
        scratch_shapes=[pltpu.VMEM((N_PAIRS, SSD_STATE, LANES), F32),
                        pltpu.VMEM((N_PAIRS, SSD_STATE, LANES), F32)],
        compiler_params=_cparams(("arbitrary", "arbitrary")),
        name="ssd_scan",
    )(xbc, dt, dtT, xbc, dt, dtT, s0f, s0b, *_ssd_consts(), na_row, na_col, dskip)
    return tuple(res) if need_out else (None, None) + tuple(res)


def _hg_consts():
    c = HG_CHUNK
    t = np.arange(c)
    m_f = np.zeros((HG_LEVELS + 1, c, c), np.float32)
    m_f[0] = np.eye(c)
    for lv in range(1, HG_LEVELS + 1):
        m = 2 ** lv
        blk = t // m
        right = (t % m) >= m // 2
        m_f[lv] = (blk[:, None] == blk[None, :]) & right[:, None] & (~right[None, :])
    m_b = np.transpose(m_f, (0, 2, 1))
    tri_f = (t[:, None] >= t[None, :]).astype(np.float32)
    tri_b = (t[:, None] <= t[None, :]).astype(np.float32)
    return jnp.asarray(tri_f, BF16), jnp.asarray(tri_b, BF16), jnp.asarray(m_f, F32), jnp.asarray(m_b, F32)


def _hg_dir(u, lb_row, st_ref, tri_ref, mask_ref, fwd, need_out=True):
    c = HG_CHUNK
    end = c - 1 if fwd else 0
    outs = []
    fcol = HG_INNER if fwd else 2 * HG_INNER
    nt = (((1,), (1,)), ((), ()))
    row = lax.broadcasted_iota(jnp.int32, (c, HG_DK), 0)

    def halves(lo, hi, half):
        if half % SUBLANES == 0:
            return jnp.concatenate([(hi if (s // half) % 2 else lo)[s:s + half] for s in range(0, c, half)], axis=0)
        return jnp.where(((row // half) % 2) == 1, hi, lo)

    def shift(x, s):
        s = s % c
        if s % SUBLANES == 0:
            return jnp.concatenate([x[c - s:], x[:c - s]], axis=0)
        return pltpu.roll(x, s, 0)

    def by_side(query_side, key_side, half):
        return halves(key_side, query_side, half) if fwd else halves(query_side, key_side, half)

    for hd in range(HG_HEADS):
        sl = slice(hd * HG_DK, (hd + 1) * HG_DK)
        lb = lb_row[:, sl]
        f = lb + (1.0 - lb) * _sigmoid(u[:, fcol + hd * HG_DK:fcol + (hd + 1) * HG_DK])
        k = 1.0 - f
        la = jnp.log(f)
        v = u[:, 3 * HG_INNER + hd * HG_DK:3 * HG_INNER + (hd + 1) * HG_DK]
        la_hi = la.astype(BF16)
        la_lo = (la - la_hi.astype(F32)).astype(BF16)
        g2 = jnp.dot(tri_ref[...], jnp.concatenate([la_hi, la_lo], axis=1), preferred_element_type=F32)
        g = g2[:, :HG_DK] + g2[:, HG_DK:]
        st = st_ref[hd]
        if not need_out:
            g_end = g[end:end + 1, :]
            st_ref[hd] = (st * jnp.exp(g_end)
                          + jnp.dot(v.T.astype(BF16), (k * jnp.exp(g_end - g)).astype(BF16),
                                    preferred_element_type=F32))
            continue
        q = _silu(u[:, sl]) * (HG_DK ** -0.5)
        scores = mask_ref[0] * lax.dot_general(q.astype(BF16), k.astype(BF16), nt, preferred_element_type=F32)
        fill = g
        for lv in range(1, HG_LEVELS + 1):
            half = 2 ** (lv - 1)
            if fwd:
                ref = halves(fill, shift(fill, half), half)
            else:
                ref = halves(shift(fill, -half), fill, half)
            decay = jnp.exp2(jnp.abs(g - ref) * (-LOG2E))
            y = (by_side(q, k, half) * decay).astype(BF16)
            scores = scores + mask_ref[lv] * lax.dot_general(y, y, nt, preferred_element_type=F32)
            if fwd:
                fill = halves(shift(fill, -half), fill, half)
            else:
                fill = halves(fill, shift(fill, half), half)
        x_cum = jnp.exp(g)
        x_rem = jnp.exp(fill - g)
        o = jnp.dot(scores.astype(BF16), v.astype(BF16), preferred_element_type=F32)
        o = o + lax.dot_general((q * x_cum).astype(BF16), st.astype(BF16), nt, preferred_element_type=F32)
        outs.append(o)
        st_ref[hd] = (st * x_cum[end:end + 1, :]
                      + jnp.dot(v.T.astype(BF16), (k * x_rem).astype(BF16), preferred_element_type=F32))
    return jnp.concatenate(outs, axis=1) if need_out else None


def _hg_scan_kernel(uf_ref, ub_ref, lbp_ref, s0f_ref, s0b_ref, trif_ref, trib_ref, mf_ref, mb_ref,
                    *rest, img_rows, need_out):
    if need_out:
        of_ref, ob_ref, sfo_ref, sbo_ref, sf, sb = rest
    else:
        sfo_ref, sbo_ref, sf, sb = rest
    n = pl.program_id(1)
    c = HG_CHUNK

    def put(o_ref, o, sub):
        if img_rows:
            cols = c // img_rows
            for j in range(cols):
                lo = (sub * cols + j) * HG_INNER
                o_ref[:, lo:lo + HG_INNER] = o[j * img_rows:(j + 1) * img_rows]
        else:
            o_ref[sub * c:(sub + 1) * c, :] = o

    @pl.when(n == 0)
    def _():
        sf[...] = s0f_ref[...]
        sb[...] = s0b_ref[...]

    p = lbp_ref[...]
    mx = jnp.max(p, axis=0, keepdims=True)
    e = jnp.exp(p - mx)
    lb = e[0:1, :] / jnp.sum(e, axis=0, keepdims=True)
    for sub in range(HG_STEP_CHUNKS):
        rev = HG_STEP_CHUNKS - 1 - sub
        o_f = _hg_dir(uf_ref[sub * c:(sub + 1) * c, :], lb[:, :HG_INNER], sf, trif_ref, mf_ref, True, need_out)
        o_b = _hg_dir(ub_ref[rev * c:(rev + 1) * c, :], lb[:, HG_INNER:], sb, trib_ref, mb_ref, False, need_out)
        if need_out:
            put(of_ref, o_f, sub)
            put(ob_ref, o_b, rev)

    @pl.when(n == pl.num_programs(1) - 1)
    def _():
        sfo_ref[...] = sf[...]
        sbo_ref[...] = sb[...]


def hg_scan_call(u, lbp, s0f, s0b, row_major_out, need_out=True):
    bsz, t_len, _ = u.shape
    c = HG_CHUNK * HG_STEP_CHUNKS
    nc = t_len // c
    assert nc * c == t_len
    full = lambda shape: pl.BlockSpec(shape, lambda b, n: (0,) * len(shape))
    st = pl.BlockSpec((None, HG_HEADS, HG_DK, HG_DK), lambda b, n: (b, 0, 0, 0))
    st_shape = jax.ShapeDtypeStruct((bsz, HG_HEADS, HG_DK, HG_DK), F32)
    if row_major_out:
        img_rows = t_len // GRID_W
        cols = c // img_rows
        assert cols * img_rows == c and HG_CHUNK % img_rows == 0
        o_shape = jax.ShapeDtypeStruct((bsz, img_rows, GRID_W * HG_INNER), F32)
        of_spec = pl.BlockSpec((None, img_rows, cols * HG_INNER), lambda b, n: (b, 0, n))
        ob_spec = pl.BlockSpec((None, img_rows, cols * HG_INNER), lambda b, n: (b, 0, nc - 1 - n))
    else:
        img_rows = 0
        o_shape = jax.ShapeDtypeStruct((bsz, t_len, HG_INNER), F32)
        of_spec = pl.BlockSpec((None, c, HG_INNER), lambda b, n: (b, n, 0))
        ob_spec = pl.BlockSpec((None, c, HG_INNER), lambda b, n: (b, nc - 1 - n, 0))
    tri_f, tri_b, m_f, m_b = _hg_consts()
    res = pl.pallas_call(
        functools.partial(_hg_scan_kernel, img_rows=img_rows, need_out=need_out),
        grid=(bsz, nc),
        in_specs=[pl.BlockSpec((None, c, B_COLS), lambda b, n: (b, n, 0)),
                  pl.BlockSpec((None, c, B_COLS), lambda b, n: (b, nc - 1 - n, 0)),
                  full((2, 2 * HG_INNER)), st, st,
                  full(tri_f.shape), full(tri_b.shape), full(m_f.shape), full(m_b.shape)],
        out_specs=([of_spec, ob_spec] if need_out else []) + [st, st],
        out_shape=([o_shape, o_shape] if need_out else []) + [st_shape, st_shape],
        scratch_shapes=[pltpu.VMEM((HG_HEADS, HG_DK, HG_DK), F32), pltpu.VMEM((HG_HEADS, HG_DK, HG_DK), F32)],
        compiler_params=_cparams(("arbitrary", "arbitrary")),
        name="hg_scan",
    )(u, u, lbp, s0f, s0b, tri_f, tri_b, m_f, m_b)
    if not need_out:
        return (None, None) + tuple(res)
    return tuple(res)


ROW_LINES = D_MODEL // LANES


def _to_token_tiles(ref, val, n_rows):
    for c in range(ROW_LINES):
        ref[pl.ds(c, n_rows, stride=ROW_LINES), :] = val[:, c * LANES:(c + 1) * LANES]


def _from_token_tiles(ref, n_rows, first_row=0):
    return jnp.concatenate([ref[pl.ds(first_row * ROW_LINES + c, n_rows, stride=ROW_LINES), :]
                            for c in range(ROW_LINES)], axis=1)


def _tile_of(ref, row):
    return ref.at[pl.ds(pl.multiple_of(row * ROW_LINES, ROW_LINES), ROW_LINES)]


def _post_kernel(x_ref, sof_ref, sob_ref, z_ref, hof_ref, hob_ref, hg_ref,
                 snw_ref, hnw_ref, wo_ref, pnw_ref, gm_ref, fnw_ref, shf_ref, scf_ref, rwh_ref, rwl_ref, rb_ref,
                 tri_ref, x1_ref, hx_ref, idx_ref, rank_ref, gate_ref, cnt_ref, carry, hscr, *, tm):
    first = (pl.program_id(0) == 0) & (pl.program_id(1) == 0)

    @pl.when(first)
    def _():
        carry[...] = jnp.zeros_like(carry)

    y = (sof_ref[...] + sob_ref[...]) * _silu(z_ref[...])
    y = _rms(y, snw_ref[...])
    ho = hof_ref[...] + hob_ref[...]
    n_img = tm // GRID_W
    for w in range(GRID_W):
        for cc in range(HG_INNER // LANES):
            lo = w * HG_INNER + cc * LANES
            hscr[cc, pl.ds(w, n_img, stride=GRID_W), :] = ho[:, lo:lo + LANES]
    o = jnp.concatenate([hscr[cc] for cc in range(HG_INNER // LANES)], axis=1)
    hnw = hnw_ref[...]
    o = jnp.concatenate([_rms(o[:, h * HG_DK:(h + 1) * HG_DK], hnw[:, h * HG_DK:(h + 1) * HG_DK])
                         for h in range(HG_HEADS)], axis=1)
    o = o * _silu(hg_ref[...])
    mix = (jnp.dot(y.astype(BF16), wo_ref[:SSD_INNER, :], preferred_element_type=F32)
           + jnp.dot(o.astype(BF16), wo_ref[SSD_INNER:, :], preferred_element_type=F32))
    x1 = x_ref[...] + gm_ref[...] * _rms(mix, pnw_ref[...])
    x1_ref[...] = x1
    hx = _rms(x1, fnw_ref[...]) * (1.0 + scf_ref[...]) + shf_ref[...]
    _to_token_tiles(hx_ref, hx, tm)
    hx_hi = hx.astype(BF16)
    hx_lo = (hx - hx_hi.astype(F32)).astype(BF16)
    rwh = rwh_ref[...]
    logits = (jnp.dot(hx_hi, rwh, preferred_element_type=F32) + jnp.dot(hx_lo, rwh, preferred_element_type=F32)
              + jnp.dot(hx_hi, rwl_ref[...], preferred_element_type=F32)) + rb_ref[...]
    work = logits.T[:N_EXPERTS, :]
    erow = lax.broadcasted_iota(jnp.int32, (N_EXPERTS, tm), 0)
    vals, idxs = [], []
    for _ in range(TOP_K):
        m = jnp.max(work, axis=0, keepdims=True)
        ix = jnp.min(jnp.where(work == m, erow, N_EXPERTS), axis=0, keepdims=True)
        vals.append(m)
        idxs.append(ix)
        work = jnp.where(erow == ix, -jnp.inf, work)
    es = [jnp.exp(v - vals[0]) for v in vals]
    den = es[0] + es[1] + es[2] + es[3]
    onehots = [(erow == ix) for ix in idxs]
    multi = sum(oh.astype(F32) for oh in onehots)
    before = jnp.dot(multi.astype(BF16), tri_ref[...], preferred_element_type=F32) + carry[...]
    carry[...] = carry[...] + jnp.sum(multi, axis=1, keepdims=True)
    sub = lax.broadcasted_iota(jnp.int32, (SUBLANES, tm), 0)
    idx_o = jnp.zeros((SUBLANES, tm), jnp.int32)
    rank_o = jnp.zeros((SUBLANES, tm), jnp.int32)
    gate_o = jnp.zeros((SUBLANES, tm), F32)
    for k in range(TOP_K):
        rk = jnp.sum(jnp.where(onehots[k], before, 0.0), axis=0, keepdims=True)
        idx_o = jnp.where(sub == k, idxs[k], idx_o)
        rank_o = jnp.where(sub == k, rk.astype(jnp.int32), rank_o)
        gate_o = jnp.where(sub == k, es[k] / den, gate_o)
    idx_ref[...] = idx_o
    rank_ref[...] = rank_o
    gate_ref[...] = gate_o
    cnt_ref[...] = jnp.broadcast_to(carry[...], (N_EXPERTS, LANES))


def post_call(x, sof, sob, z, hof, hob, hg, snw, hnw, wo, pnw, gm, fnw, shf, scf, rwh, rwl, rb):
    bsz, t_len, _ = x.shape
    tm = TOK_TILE
    nt = t_len // tm
    n_tok = bsz * t_len
    tok = lambda c: pl.BlockSpec((None, tm, c), lambda b, i: (b, i, 0))
    full = lambda shape: pl.BlockSpec(shape, lambda b, i: (0,) * len(shape))
    per_b = pl.BlockSpec((None, 1, D_MODEL), lambda b, i: (b, 0, 0))
    flat = lambda c: pl.BlockSpec((tm, c), lambda b, i: (b * nt + i, 0))
    rout = pl.BlockSpec((SUBLANES, tm), lambda b, i: (b * nt + i, 0))
    n_tiles = n_tok // tm
    ii = np.arange(tm)
    tri = jnp.asarray(ii[:, None] < ii[None, :], BF16)
    assert tm % (GRID_W * SUBLANES) == 0 and hof.shape == (bsz, t_len // GRID_W, GRID_W * HG_INNER)
    img = pl.BlockSpec((None, tm // GRID_W, GRID_W * HG_INNER), lambda b, i: (b, i, 0))
    return pl.pallas_call(
        functools.partial(_post_kernel, tm=tm),
        grid=(bsz, nt),
        in_specs=[tok(D_MODEL), tok(SSD_INNER), tok(SSD_INNER), tok(SSD_INNER), img, img,
                  tok(HG_INNER), full((1, SSD_INNER)), full((1, HG_INNER)), full((D_MODEL, D_MODEL)),
                  full((1, D_MODEL)), per_b, full((1, D_MODEL)), per_b, per_b,
                  full((D_MODEL, LANES)), full((D_MODEL, LANES)), full((1, LANES)), full((tm, tm))],
        out_specs=[flat(D_MODEL), pl.BlockSpec((tm * ROW_LINES, LANES), lambda b, i: (b * nt + i, 0)),
                   rout, rout, rout, pl.BlockSpec((N_EXPERTS, LANES), lambda b, i: (0, 0))],
        out_shape=[jax.ShapeDtypeStruct((n_tok, D_MODEL), F32),
                   jax.ShapeDtypeStruct((n_tok * ROW_LINES, LANES), F32),
                   jax.ShapeDtypeStruct((n_tiles * SUBLANES, tm), jnp.int32),
                   jax.ShapeDtypeStruct((n_tiles * SUBLANES, tm), jnp.int32),
                   jax.ShapeDtypeStruct((n_tiles * SUBLANES, tm), F32),
                   jax.ShapeDtypeStruct((N_EXPERTS, LANES), F32)],
        scratch_shapes=[pltpu.VMEM((N_EXPERTS, 1), F32), pltpu.VMEM((HG_INNER // LANES, tm, LANES), F32)],
        compiler_params=_cparams(("arbitrary", "arbitrary")),
        name="post",
    )(x, sof, sob, z, hof, hob, hg, snw, hnw, wo, pnw, gm, fnw, shf, scf, rwh, rwl, rb, tri)


def _wait_rows(hbm_ref, n_rows, sem):
    n = n_rows * ROW_LINES
    pltpu.make_async_copy(hbm_ref.at[pl.ds(0, n)], hbm_ref.at[pl.ds(0, n)], sem).wait()


PAD_RUNS = tuple(2 ** j for j in range(int(math.log2(MOE_ROWS))))
DISPATCH_TILES = 2


def _dispatch_kernel(zstart_ref, zpad_ref, dest_ref, hx_ref, buf_ref, zrows, sem, zsem, *, tm):
    i = pl.program_id(0)

    def zero_rows(first, n):
        first = pl.multiple_of(first * ROW_LINES, ROW_LINES)
        return pltpu.make_async_copy(zrows.at[pl.ds(0, n * ROW_LINES)], buf_ref.at[pl.ds(first, n * ROW_LINES)], zsem)

    def for_pad_runs(act):
        def per_expert(e, carry):
            pad = zpad_ref[e]
            off = zstart_ref[e]
            for run in PAD_RUNS:
                @pl.when((pad & run) != 0)
                def _():
                    act(zero_rows(off, run))
                off = off + (pad & run)
            return carry

        lax.fori_loop(0, N_EXPERTS, per_expert, 0)

        def tail(j, carry):
            act(zero_rows(j * PAD_RUNS[-1], PAD_RUNS[-1]))
            return carry

        n_rows = buf_ref.shape[0] // ROW_LINES
        lax.fori_loop(zstart_ref[N_EXPERTS] // PAD_RUNS[-1], n_rows // PAD_RUNS[-1], tail, 0)

    @pl.when(i == 0)
    def _():
        zrows[...] = jnp.zeros_like(zrows)
        for_pad_runs(lambda cp: cp.start())

    def issue(t, carry):
        for sub in range(DISPATCH_TILES):
            for k in range(TOP_K):
                pltpu.make_async_copy(_tile_of(hx_ref, sub * tm + t),
                                      _tile_of(buf_ref, dest_ref[(sub * TOP_K + k) * tm + t]),
                                      sem).start(priority=k % 2)
        return carry

    lax.fori_loop(0, tm, issue, 0, unroll=2)
    _wait_rows(buf_ref, DISPATCH_TILES * tm * TOP_K, sem)

    @pl.when(i == 0)
    def _():
        for_pad_runs(lambda cp: cp.wait())


def dispatch_call(zstart, zpad, dest_flat, hx, n_rows):
    n_tok = hx.shape[0] // ROW_LINES
    tm = TOK_TILE
    step_tok = DISPATCH_TILES * tm
    return pl.pallas_call(
        functools.partial(_dispatch_kernel, tm=tm),
        grid=(n_tok // step_tok,),
        in_specs=[pl.BlockSpec(memory_space=pltpu.SMEM), pl.BlockSpec(memory_space=pltpu.SMEM),
                  pl.BlockSpec((step_tok * TOP_K,), lambda i: (i,), memory_space=pltpu.SMEM),
                  pl.BlockSpec((step_tok * ROW_LINES, LANES), lambda i: (i, 0))],
        out_specs=pl.BlockSpec(memory_space=pl.ANY),
        out_shape=jax.ShapeDtypeStruct((n_rows * ROW_LINES, LANES), F32),
        scratch_shapes=[pltpu.VMEM((PAD_RUNS[-1] * ROW_LINES, LANES), F32), pltpu.SemaphoreType.DMA(()),
                        pltpu.SemaphoreType.DMA(())],
        compiler_params=_cparams(("arbitrary",)),
        name="dispatch",
    )(zstart, zpad, dest_flat, hx)


W1_TCOLS = 256


def _experts_kernel(be_ref, nu_ref, nxt_ref, x_ref, w1_hbm, b1_ref, w2_hbm, b2_ref, y_ref,
                    w1f, w2f, w1t, w2s, tbuf, wsem):
    i = pl.program_id(0)
    live = i < nu_ref[0]
    new_expert = (i == 0) | (be_ref[i] != be_ref[jnp.maximum(i - 1, 0)])

    def weight_copies(e):
        return (pltpu.make_async_copy(w1_hbm.at[e], w1f, wsem.at[0]),
                pltpu.make_async_copy(w2_hbm.at[e], w2f, wsem.at[1]))

    @pl.when(live & new_expert)
    def _():
        @pl.when(i == 0)
        def _():
            for cp in weight_copies(be_ref[0]):
                cp.start()

        for cp in weight_copies(be_ref[i]):
            cp.wait()
        half = W1_TCOLS // 2
        for c in range(2 * D_FF // W1_TCOLS):
            for j in range(D_MODEL // LANES):
                ks = slice(j * LANES, (j + 1) * LANES)
                tbuf[j] = w1f[ks, c * W1_TCOLS:(c + 1) * W1_TCOLS].T
                w1t[c * half:(c + 1) * half, ks] = tbuf[j, pl.ds(0, half, stride=2), :].astype(BF16)
                w1t[D_FF + c * half:D_FF + (c + 1) * half, ks] = tbuf[j, pl.ds(1, half, stride=2), :].astype(BF16)
        w2s[...] = w2f[...].astype(BF16)

        @pl.when(nxt_ref[i] >= 0)
        def _():
            for cp in weight_copies(nxt_ref[i]):
                cp.start(priority=1)

    @pl.when(live)
    def _():
        xb = _from_token_tiles(x_ref, MOE_ROWS).astype(BF16)
        u = lax.dot_general(xb, w1t[...], (((1,), (1,)), ((), ())), preferred_element_type=F32) + b1_ref[...]
        glu = jnp.minimum(u[:, :D_FF], SWIGLU_LIMIT)
        lin = jnp.clip(u[:, D_FF:], -SWIGLU_LIMIT, SWIGLU_LIMIT)
        a = glu * _sigmoid(SWIGLU_ALPHA * glu) * (lin + 1.0)
        y = jnp.dot(a.astype(BF16), w2s[...], preferred_element_type=F32) + b2_ref[...]
        _to_token_tiles(y_ref, y, MOE_ROWS)

    @pl.when(jnp.logical_not(live))
    def _():
        y_ref[...] = jnp.zeros_like(y_ref)


def experts_call(block_e, n_used, next_e, xs, w1, b1p, w2, b2):
    rows = xs.shape[0] // ROW_LINES
    r = MOE_ROWS
    nb = rows // r
    blk = (r * ROW_LINES, LANES)
    grid_spec = pltpu.PrefetchScalarGridSpec(
        num_scalar_prefetch=3,
        grid=(nb,),
        in_specs=[pl.BlockSpec(blk, lambda i, be, nu, nx: (jnp.maximum(jnp.minimum(i, nu[0] - 1), 0), 0)),
                  pl.BlockSpec(memory_space=pl.ANY),
                  pl.BlockSpec((None, 1, 2 * D_FF), lambda i, be, nu, nx: (be[i], 0, 0)),
                  pl.BlockSpec(memory_space=pl.ANY),
                  pl.BlockSpec((None, 1, D_MODEL), lambda i, be, nu, nx: (be[i], 0, 0))],
        out_specs=pl.BlockSpec(blk, lambda i, be, nu, nx: (i, 0)),
        scratch_shapes=[pltpu.VMEM((D_MODEL, 2 * D_FF), F32), pltpu.VMEM((D_FF, D_MODEL), F32),
                        pltpu.VMEM((2 * D_FF, D_MODEL), BF16), pltpu.VMEM((D_FF, D_MODEL), BF16),
                        pltpu.VMEM((D_MODEL // LANES, W1_TCOLS, LANES), F32),
                        pltpu.SemaphoreType.DMA((2,))],
    )
    return pl.pallas_call(
        _experts_kernel,
        grid_spec=grid_spec,
        out_shape=jax.ShapeDtypeStruct((rows * ROW_LINES, LANES), F32),
        compiler_params=_cparams(("arbitrary",)),
        name="experts",
    )(block_e, n_used, next_e, xs, w1, b1p, w2, b2)


def _combine_kernel(dest_ref, dest_next_ref, y_ref, gate_ref, x1_ref, gf_ref, nw_ref, o_ref, buf, sem, *, tm):
    i = pl.program_id(0)
    slot = i % 2

    def gather_tile(dref, into):
        def issue(t, carry):
            for k in range(TOP_K):
                pltpu.make_async_copy(_tile_of(y_ref, dref[k * tm + t]), _tile_of(buf.at[into], k * tm + t),
                                      sem.at[into]).start(priority=k % 2)
            return carry

        lax.fori_loop(0, tm, issue, 0, unroll=4)

    @pl.when(i == 0)
    def _():
        gather_tile(dest_ref, 0)

    @pl.when(i + 1 < pl.num_programs(0))
    def _():
        gather_tile(dest_next_ref, 1 - slot)

    _wait_rows(y_ref, tm * TOP_K, sem.at[slot])
    gate = gate_ref[...].T
    fx = None
    for k in range(TOP_K):
        yk = jnp.concatenate([buf[slot, pl.ds(k * tm * ROW_LINES + c, tm, stride=ROW_LINES), :]
                              for c in range(ROW_LINES)], axis=1)
        fx = gate[:, k:k + 1] * yk if fx is None else fx + gate[:, k:k + 1] * yk
    o_ref[...] = x1_ref[...] + gf_ref[...] * _rms(fx, nw_ref[...])


def combine_call(dest_flat, ys, gates, x1, gf, nw, bsz):
    n_tok = x1.shape[0]
    tm = TOK_TILE
    nt = n_tok // bsz // tm
    n_steps = n_tok // tm
    return pl.pallas_call(
        functools.partial(_combine_kernel, tm=tm),
        grid=(n_steps,),
        in_specs=[pl.BlockSpec((tm * TOP_K,), lambda i: (i,), memory_space=pltpu.SMEM),
                  pl.BlockSpec((tm * TOP_K,), lambda i: (jnp.minimum(i + 1, n_steps - 1),),
                               memory_space=pltpu.SMEM),
                  pl.BlockSpec(memory_space=pl.ANY),
                  pl.BlockSpec((SUBLANES, tm), lambda i: (i, 0)),
                  pl.BlockSpec((tm, D_MODEL), lambda i: (i, 0)),
                  pl.BlockSpec((None, 1, D_MODEL), lambda i: (i // nt, 0, 0)),
                  pl.BlockSpec((1, D_MODEL), lambda i: (0, 0))],
        out_specs=pl.BlockSpec((tm, D_MODEL), lambda i: (i, 0)),
        out_shape=jax.ShapeDtypeStruct((n_tok, D_MODEL), F32),
        scratch_shapes=[pltpu.VMEM((2, TOP_K * tm * ROW_LINES, LANES), F32), pltpu.SemaphoreType.DMA((2,))],
        compiler_params=_cparams(("arbitrary",)),
        name="combine",
    )(dest_flat, dest_flat, ys, gates, x1, gf, nw)


def kernel(x, c, ctx, c_ctx, ada_w, ada_b, mix_pre_norm, mix_post_norm, w_in, w_out, ssd_conv_w, ssd_conv_b,
           ssd_dt_bias, ssd_a_log, ssd_d, ssd_norm, hg_lb, hg_norm, ffn_pre_norm, ffn_post_norm, router_w,
           router_b, moe_w1, moe_b1, moe_w2, moe_b2):
    bsz, t_len, d = x.shape
    assert ada_w.shape[0] == 1 and d == D_MODEL and bsz <= SUBLANES - 1
    n_tok = bsz * t_len

    cc = jnp.zeros((SUBLANES, d), F32).at[:bsz].set(c).at[bsz].set(c_ctx)
    mod = ada_call(cc, ada_w[0], ada_b[0][None, :])
    sh_m, sc_m, g_m, sh_f, sc_f, g_f = [m[:bsz, None, :] for m in jnp.split(mod, 6, axis=-1)]
    csh_m, csc_m = [jnp.broadcast_to(m[bsz][None, None, :], (bsz, 1, d)) for m in jnp.split(mod, 6, axis=-1)[:2]]

    w = w_in[0]
    wa = jnp.concatenate([w[:, :SSD_INNER + SSD_XBC], w[:, SSD_COLS + 4 * HG_INNER:]], axis=1).astype(BF16)
    wdt = jnp.zeros((d, LANES), F32).at[:, :2 * SSD_HEADS].set(w[:, SSD_INNER + SSD_XBC:SSD_COLS]).astype(BF16)
    wb = w[:, SSD_COLS:SSD_COLS + 4 * HG_INNER].astype(BF16)
    dtb = jnp.zeros((1, LANES), F32).at[0, :2 * SSD_HEADS].set(ssd_dt_bias[0].reshape(-1))
    nw = mix_pre_norm[0][None, :]
    cw, cb = ssd_conv_w[0], ssd_conv_b[0][None, :]
    neg_a = -jnp.exp(ssd_a_log[0].astype(F32)).reshape(1, 2 * SSD_HEADS)
    dskip = jnp.repeat(ssd_d[0], SSD_HEAD_DIM)[None, :]
    lbp = hg_lb.astype(F32).reshape(2, 2 * HG_INNER)

    _, cxbc, _, cdt, cdtT = inproj_a_call(ctx, csh_m, csc_m, nw, wa, wdt, cw, cb, dtb, tm=ctx.shape[1])
    cu = inproj_b_call(ctx, csh_m, csc_m, nw, wb, col_major=False)
    z_ssd = jnp.zeros((bsz, N_PAIRS, SSD_STATE, LANES), F32)
    z_hg = jnp.zeros((bsz, HG_HEADS, HG_DK, HG_DK), F32)
    _, _, ssf, ssb = ssd_scan_call(cxbc, cdt, cdtT, z_ssd, z_ssd, neg_a, neg_a.reshape(-1, 1), dskip,
                                   need_out=False)
    _, _, hsf, hsb = hg_scan_call(cu, lbp, z_hg, z_hg, row_major_out=False, need_out=False)

    zg, xbc, hgate, dt, dtT = inproj_a_call(x, sh_m, sc_m, nw, wa, wdt, cw, cb, dtb, tm=512)
    ub = inproj_b_call(x, sh_m, sc_m, nw, wb, col_major=True)
    sof, sob, _, _ = ssd_scan_call(xbc, dt, dtT, ssf, ssb, neg_a, neg_a.reshape(-1, 1), dskip)
    hof, hob, _, _ = hg_scan_call(ub, lbp, hsf, hsb, row_major_out=True)

    rw = jnp.zeros((d, LANES), F32).at[:, :N_EXPERTS].set(router_w[0])
    rwh = rw.astype(BF16)
    rwl = (rw - rwh.astype(F32)).astype(BF16)
    rb = jnp.zeros((1, LANES), F32).at[0, :N_EXPERTS].set(router_b[0])
    x1, hx, idx, rank, gates, cnt = post_call(
        x, sof, sob, zg, hof, hob, hgate, ssd_norm[0][None, :], hg_norm[0][None, :], w_out[0].astype(BF16),
        mix_post_norm[0][None, :], g_m, ffn_pre_norm[0][None, :], sh_f, sc_f, rwh, rwl, rb)

    r = MOE_ROWS
    n_blocks = (n_tok * TOP_K + N_EXPERTS * (r - 1) + r - 1) // r
    n_tiles = n_tok // TOK_TILE
    counts = cnt[:, 0].astype(jnp.int32)
    padded = (counts + r - 1) // r * r
    pend = jnp.cumsum(padded)
    pstart = pend - padded
    idx3 = idx.reshape(n_tiles, SUBLANES, TOK_TILE)[:, :TOP_K, :]
    rank3 = rank.reshape(n_tiles, SUBLANES, TOK_TILE)[:, :TOP_K, :]
    dest = rank3
    for e in range(N_EXPERTS):
        dest = dest + jnp.where(idx3 == e, pstart[e], 0)
    dest = dest.reshape(-1)
    starts = jnp.arange(n_blocks, dtype=jnp.int32) * r
    block_e = jnp.minimum(jnp.sum((pend[None, :] <= starts[:, None]).astype(jnp.int32), axis=1), N_EXPERTS - 1)
    n_used = (pend[-1:] // r).astype(jnp.int32)
    run_end = pend[block_e] // r
    next_e = jnp.where(run_end < n_used[0], block_e[jnp.minimum(run_end, n_blocks - 1)], -1).astype(jnp.int32)

    zstart = jnp.concatenate([pstart + counts, pend[-1:]])
    xs = dispatch_call(zstart, padded - counts, dest, hx, n_blocks * r)
    b1p = jnp.concatenate([moe_b1[0][:, 0::2], moe_b1[0][:, 1::2]], axis=-1)[:, None, :]
    ys = experts_call(block_e, n_used, next_e, xs, moe_w1[0], b1p, moe_w2[0], moe_b2[0][:, None, :])
    out = combine_call(dest, ys, gates, x1, g_f, ffn_post_norm[0][None, :], bsz)
    return out.reshape(bsz, t_len, d)
```

```python
import functools
import math

import numpy as np
import jax
import jax.numpy as jnp
from jax import lax
from jax.experimental import pallas as pl
from jax.experimental.pallas import tpu as pltpu

F32 = jnp.float32
BF16 = jnp.bfloat16
HIGHEST = lax.Precision.HIGHEST

D_MODEL = 1024
GRID_W = 64
SSD_HEADS = 8
SSD_HEAD_DIM = 64
SSD_INNER = 512
SSD_STATE = 64
SSD_GROUPS = 2
SSD_CONV = 5
SSD_XBC = 768
SSD_COLS = 1296
HG_HEADS = 4
HG_DK = 128
HG_INNER = 512
N_EXPERTS = 32
TOP_K = 4
D_FF = 1024
SWIGLU_ALPHA = 1.702
SWIGLU_LIMIT = 7.0
EPS = 1e-6

LANES = 128
SUBLANES = 8
VMEM_LIMIT = 56 * 1024 * 1024

SSD_CHUNK = 256
SSD_STEP_CHUNKS = 2
HG_CHUNK = 128
HG_LEVELS = 7
HG_STEP_CHUNKS = (1, 2, 4)
MOE_ROWS = 256
TOK_TILE = 512
LOG2E = 1.4426950408889634


def _cparams(sem):
    return pltpu.CompilerParams(dimension_semantics=sem, vmem_limit_bytes=VMEM_LIMIT)


def _sigmoid(x):
    return 1.0 / (1.0 + jnp.exp(-x))


def _silu(x):
    return x * _sigmoid(x)


def _rms(x, w):
    return x * lax.rsqrt(jnp.mean(x * x, axis=-1, keepdims=True) + EPS) * w


def _split3(v):
    hi = v.astype(BF16)
    r1 = v - hi.astype(F32)
    mid = r1.astype(BF16)
    lo = (r1 - mid.astype(F32)).astype(BF16)
    return hi, mid, lo


def _ada_kernel(c_ref, w_ref, b_ref, o_ref):
    s = _silu(c_ref[...])
    o_ref[...] = jnp.dot(s, w_ref[...], precision=HIGHEST, preferred_element_type=F32) + b_ref[...]


def ada_call(cc, w, b):
    n = w.shape[1]
    tn = 1536
    return pl.pallas_call(
        _ada_kernel,
        grid=(n // tn,),
        in_specs=[pl.BlockSpec((SUBLANES, D_MODEL), lambda j: (0, 0)),
                  pl.BlockSpec((D_MODEL, tn), lambda j: (0, j)),
                  pl.BlockSpec((1, tn), lambda j: (0, j))],
        out_specs=pl.BlockSpec((SUBLANES, tn), lambda j: (0, j)),
        out_shape=jax.ShapeDtypeStruct((SUBLANES, n), F32),
        compiler_params=_cparams(("arbitrary",)),
        name="ada",
    )(cc, w, b)


A_COLS = SSD_INNER + SSD_XBC + HG_INNER


def _prep(xt, nw, sh, sc):
    return (_rms(xt, nw) * (1.0 + sc) + sh).astype(BF16)


def _inproj_a_kernel(x_ref, xp_ref, xn_ref, sh_ref, sc_ref, nw_ref, wa_ref, wdt_ref, cw_ref, cb_ref, dtb_ref,
                     z_ref, xbc_ref, g_ref, dt_ref, dtT_ref, scr, *, tm):
    i = pl.program_id(1)
    last = pl.num_programs(1) - 1
    nw, sh, sc = nw_ref[...], sh_ref[...], sc_ref[...]
    h = _prep(x_ref[...], nw, sh, sc)
    ua = jnp.dot(h, wa_ref[...], preferred_element_type=F32)
    z_ref[...] = ua[:, :SSD_INNER]
    g_ref[...] = ua[:, SSD_INNER + SSD_XBC:]
    wx = wa_ref[:, SSD_INNER:SSD_INNER + SSD_XBC]
    up = jnp.dot(_prep(xp_ref[...], nw, sh, sc), wx, preferred_element_type=F32)
    un = jnp.dot(_prep(xn_ref[...], nw, sh, sc), wx, preferred_element_type=F32)
    scr[0:SUBLANES, :] = jnp.where(i > 0, up, 0.0)
    scr[SUBLANES:SUBLANES + tm, :] = ua[:, SSD_INNER:SSD_INNER + SSD_XBC]
    scr[SUBLANES + tm:, :] = jnp.where(i < last, un, 0.0)
    acc = jnp.broadcast_to(cb_ref[...], (tm, SSD_XBC))
    pad = SSD_CONV // 2
    for k in range(SSD_CONV):
        off = SUBLANES - pad + k
        acc = acc + cw_ref[k:k + 1, :] * scr[off:off + tm, :]
    xbc_ref[...] = _silu(acc)
    draw = jnp.dot(h, wdt_ref[...], preferred_element_type=F32) + dtb_ref[...]
    dt = jnp.maximum(draw, 0.0) + jnp.log(1.0 + jnp.exp(-jnp.abs(draw)))
    dt_ref[...] = dt[:, :2 * SSD_HEADS]
    dtT_ref[...] = dt.T[:2 * SSD_HEADS, :]


def inproj_a_call(x, sh, sc, nw, wa, wdt, cw, cb, dtb, tm):
    bsz, t_len, _ = x.shape
    nt = t_len // tm
    r8 = tm // SUBLANES
    n8 = t_len // SUBLANES
    full = lambda shape: pl.BlockSpec(shape, lambda b, i: (0,) * len(shape))
    tok = lambda c: pl.BlockSpec((None, tm, c), lambda b, i: (b, i, 0))
    return pl.pallas_call(
        functools.partial(_inproj_a_kernel, tm=tm),
        grid=(bsz, nt),
        in_specs=[tok(D_MODEL),
                  pl.BlockSpec((None, SUBLANES, D_MODEL), lambda b, i: (b, jnp.maximum(i * r8 - 1, 0), 0)),
                  pl.BlockSpec((None, SUBLANES, D_MODEL), lambda b, i: (b, jnp.minimum((i + 1) * r8, n8 - 1), 0)),
                  pl.BlockSpec((None, 1, D_MODEL), lambda b, i: (b, 0, 0)),
                  pl.BlockSpec((None, 1, D_MODEL), lambda b, i: (b, 0, 0)),
                  full((1, D_MODEL)), full((D_MODEL, A_COLS)), full((D_MODEL, LANES)),
                  full((SSD_CONV, SSD_XBC)), full((1, SSD_XBC)), full((1, LANES))],
        out_specs=[tok(SSD_INNER), tok(SSD_XBC), tok(HG_INNER), tok(2 * SSD_HEADS),
                   pl.BlockSpec((None, 2 * SSD_HEADS, tm), lambda b, i: (b, 0, i))],
        out_shape=[jax.ShapeDtypeStruct((bsz, t_len, SSD_INNER), F32),
                   jax.ShapeDtypeStruct((bsz, t_len, SSD_XBC), F32),
                   jax.ShapeDtypeStruct((bsz, t_len, HG_INNER), F32),
                   jax.ShapeDtypeStruct((bsz, t_len, 2 * SSD_HEADS), F32),
                   jax.ShapeDtypeStruct((bsz, 2 * SSD_HEADS, t_len), F32)],
        scratch_shapes=[pltpu.VMEM((tm + 2 * SUBLANES, SSD_XBC), F32)],
        compiler_params=_cparams(("arbitrary", "arbitrary")),
        name="inproj_a",
    )(x, x, x, sh, sc, nw, wa, wdt, cw, cb, dtb)


B_COLS = 4 * HG_INNER


def _inproj_b_kernel(x_ref, sh_ref, sc_ref, nw_ref, w_ref, o_ref, *, ncol):
    if ncol:
        xt = jnp.concatenate([x_ref[:, w * D_MODEL:(w + 1) * D_MODEL] for w in range(ncol)], axis=0)
    else:
        xt = x_ref[...]
    h = _prep(xt, nw_ref[...], sh_ref[...], sc_ref[...])
    o_ref[...] = jnp.dot(h, w_ref[...], preferred_element_type=F32)


def inproj_b_call(x, sh, sc, nw, wb, col_major):
    bsz, t_len, _ = x.shape
    if col_major:
        ncol = 8
        tm = ncol * GRID_W
        rows = t_len // GRID_W
        assert rows == GRID_W
        xin = x.reshape(bsz, rows, GRID_W * D_MODEL)
        x_spec = pl.BlockSpec((None, rows, ncol * D_MODEL), lambda b, i: (b, 0, i))
    else:
        ncol = 0
        tm = t_len
        xin = x
        x_spec = pl.BlockSpec((None, tm, D_MODEL), lambda b, i: (b, i, 0))
    full = lambda shape: pl.BlockSpec(shape, lambda b, i: (0,) * len(shape))
    return pl.pallas_call(
        functools.partial(_inproj_b_kernel, ncol=ncol),
        grid=(bsz, t_len // tm),
        in_specs=[x_spec,
                  pl.BlockSpec((None, 1, D_MODEL), lambda b, i: (b, 0, 0)),
                  pl.BlockSpec((None, 1, D_MODEL), lambda b, i: (b, 0, 0)),
                  full((1, D_MODEL)), full((D_MODEL, B_COLS))],
        out_specs=pl.BlockSpec((None, tm, B_COLS), lambda b, i: (b, i, 0)),
        out_shape=jax.ShapeDtypeStruct((bsz, t_len, B_COLS), F32),
        compiler_params=_cparams(("arbitrary", "arbitrary")),
        name="inproj_b",
    )(xin, sh, sc, nw, wb)


N_PAIRS = SSD_HEADS // 2


def _ssd_dir(xbc, dt, dtT, s_ref, tri, triT, na_row, na_col, fwd, need_out=True):
    c = SSD_CHUNK
    xs = xbc[:, :SSD_INNER]
    bm = xbc[:, SSD_INNER:SSD_INNER + LANES]
    cm = xbc[:, SSD_INNER + LANES:]
    col0 = 0 if fwd else SSD_HEADS
    la = dt[:, col0:col0 + SSD_HEADS] * na_row[:, col0:col0 + SSD_HEADS]
    dtr = dtT[col0:col0 + SSD_HEADS, :]
    laT = dtr * na_col[col0:col0 + SSD_HEADS, :]
    g = sum(jnp.dot(tri, p, preferred_element_type=F32) for p in _split3(la))
    gT = sum(jnp.dot(p, triT, preferred_element_type=F32) for p in _split3(laT))
    end = c - 1 if fwd else 0
    bmT = bm.T
    ii = lax.broadcasted_iota(jnp.int32, (c, c), 0)
    jj = lax.broadcasted_iota(jnp.int32, (c, c), 1)
    causal = (jj <= ii) if fwd else (jj >= ii)
    lane = lax.broadcasted_iota(jnp.int32, (c, LANES), 1)
    lo_half = lane < SSD_HEAD_DIM
    lane_s = lax.broadcasted_iota(jnp.int32, (SSD_STATE, LANES), 1) < SSD_HEAD_DIM
    outs = []
    for grp in range(SSD_GROUPS):
        if need_out:
            in_grp = (lane >= grp * SSD_STATE) & (lane < (grp + 1) * SSD_STATE)
            cm_g = jnp.where(in_grp, cm, 0.0).astype(BF16)
            gmat = jnp.dot(cm_g, bmT.astype(BF16), preferred_element_type=F32)
        bmT_g = bmT[grp * SSD_STATE:(grp + 1) * SSD_STATE, :]
        for pp in range(N_PAIRS // SSD_GROUPS):
            pair = grp * (N_PAIRS // SSD_GROUPS) + pp
            heads = (2 * pair, 2 * pair + 1)
            xs_p = xs[:, pair * LANES:(pair + 1) * LANES]
            xbd = jnp.concatenate([jnp.where(lo_half, xs_p, 0.0), jnp.where(lo_half, 0.0, xs_p)],
                                  axis=0).astype(BF16)
            ms, bws, ecols, arows = [], [], [], []
            for hd in heads:
                gcol = g[:, hd:hd + 1]
                grow = gT[hd:hd + 1, :]
                glast = grow[:, end:end + 1]
                bws.append(bmT_g * (dtr[hd:hd + 1, :] * jnp.exp(glast - grow)))
                arows.append(jnp.exp(glast))
                if need_out:
                    dec = jnp.exp(jnp.where(causal, gcol - grow, -jnp.inf))
                    ms.append(gmat * dec * dtr[hd:hd + 1, :])
                    ecols.append(jnp.exp(gcol))
            s_old = s_ref[pair]
            if need_out:
                mcat = jnp.concatenate(ms, axis=1).astype(BF16)
                zeros = jnp.zeros_like(s_old)
                s_pad = jnp.concatenate([s_old, zeros] if grp == 0 else [zeros, s_old], axis=0).astype(BF16)
                o_inter = jnp.dot(cm.astype(BF16), s_pad, preferred_element_type=F32)
                o_inter = o_inter * jnp.where(lo_half, ecols[0], ecols[1])
                outs.append(jnp.dot(mcat, xbd, preferred_element_type=F32) + o_inter)
            bw = jnp.concatenate(bws, axis=1).astype(BF16)
            s_ref[pair] = (s_old * jnp.where(lane_s, arows[0], arows[1])
                           + jnp.dot(bw, xbd, preferred_element_type=F32))
    return outs


def _ssd_scan_kernel(xf_ref, dtf_ref, dtTf_ref, xb_ref, dtb_ref, dtTb_ref, s0f_ref, s0b_ref,
                     trif_ref, trifT_ref, trib_ref, tribT_ref, nar_ref, nac_ref, dsk_ref, *rest, need_out,
                     step_chunks):
    if need_out:
        of_ref, ob_ref, sfo_ref, sbo_ref, sf, sb = rest
    else:
        sfo_ref, sbo_ref, sf, sb = rest
    n = pl.program_id(1)

    @pl.when(n == 0)
    def _():
        sf[...] = s0f_ref[...]
        sb[...] = s0b_ref[...]

    c = SSD_CHUNK
    for sub in range(step_chunks):
        rev = step_chunks - 1 - sub
        fs, bs = slice(sub * c, (sub + 1) * c), slice(rev * c, (rev + 1) * c)
        xf = xf_ref[fs, :]
        outs = _ssd_dir(xf, dtf_ref[fs, :], dtTf_ref[:, fs], sf, trif_ref[...], trifT_ref[...],
                        nar_ref[...], nac_ref[...], True, need_out)
        if need_out:
            of_ref[fs, :] = jnp.concatenate(outs, axis=1) + dsk_ref[...] * xf[:, :SSD_INNER]
        outs = _ssd_dir(xb_ref[bs, :], dtb_ref[bs, :], dtTb_ref[:, bs], sb, trib_ref[...], tribT_ref[...],
                        nar_ref[...], nac_ref[...], False, need_out)
        if need_out:
            ob_ref[bs, :] = jnp.concatenate(outs, axis=1)

    @pl.when(n == pl.num_programs(1) - 1)
    def _():
        sfo_ref[...] = sf[...]
        sbo_ref[...] = sb[...]


def _ssd_consts():
    c = SSD_CHUNK
    i = np.arange(c)
    trif = (i[:, None] >= i[None, :]).astype(np.float32)
    trib = (i[:, None] <= i[None, :]).astype(np.float32)
    return [jnp.asarray(a, BF16) for a in (trif, trif.T, trib, trib.T)]


def ssd_scan_call(xbc, dt, dtT, s0f, s0b, na_row, na_col, dskip, need_out=True):
    bsz, t_len, _ = xbc.shape
    step_chunks = SSD_STEP_CHUNKS if (t_len // SSD_CHUNK) % SSD_STEP_CHUNKS == 0 else 1
    c = SSD_CHUNK * step_chunks
    nc = t_len // c
    assert nc * c == t_len
    cc = SSD_CHUNK
    fw = lambda w: pl.BlockSpec((None, c, w), lambda b, n: (b, n, 0))
    bw = lambda w: pl.BlockSpec((None, c, w), lambda b, n: (b, nc - 1 - n, 0))
    full = lambda shape: pl.BlockSpec(shape, lambda b, n: (0,) * len(shape))
    st = pl.BlockSpec((None, N_PAIRS, SSD_STATE, LANES), lambda b, n: (b, 0, 0, 0))
    st_shape = jax.ShapeDtypeStruct((bsz, N_PAIRS, SSD_STATE, LANES), F32)
    o_shape = jax.ShapeDtypeStruct((bsz, t_len, SSD_INNER), F32)
    res = pl.pallas_call(
        functools.partial(_ssd_scan_kernel, need_out=need_out, step_chunks=step_chunks),
        grid=(bsz, nc),
        in_specs=[fw(SSD_XBC), fw(2 * SSD_HEADS),
                  pl.BlockSpec((None, 2 * SSD_HEADS, c), lambda b, n: (b, 0, n)),
                  bw(SSD_XBC), bw(2 * SSD_HEADS),
                  pl.BlockSpec((None, 2 * SSD_HEADS, c), lambda b, n: (b, 0, nc - 1 - n)),
                  st, st, full((cc, cc)), full((cc, cc)), full((cc, cc)), full((cc, cc)),
                  full((1, 2 * SSD_HEADS)), full((2 * SSD_HEADS, 1)), full((1, SSD_INNER))],
        out_specs=([fw(SSD_INNER), bw(SSD_INNER)] if need_out else []) + [st, st],
        out_shape=([o_shape, o_shape] if need_out else []) + [st_shape, st_shape],
        scratch_shapes=[pltpu.VMEM((N_PAIRS, SSD_STATE, LANES), F32),
                        pltpu.VMEM((N_PAIRS, SSD_STATE, LANES), F32)],
        compiler_params=_cparams(("arbitrary", "arbitrary")),
        name="ssd_scan",
    )(xbc, dt, dtT, xbc, dt, dtT, s0f, s0b, *_ssd_consts(), na_row, na_col, dskip)
    return tuple(res) if need_out else (None, None) + tuple(res)


def _hg_consts():
    c = HG_CHUNK
    t = np.arange(c)
    m_f = np.zeros((HG_LEVELS + 1, c, c), np.float32)
    m_f[0] = np.eye(c)
    for lv in range(1, HG_LEVELS + 1):
        m = 2 ** lv
        blk = t // m
        right = (t % m) >= m // 2
        m_f[lv] = (blk[:, None] == blk[None, :]) & right[:, None] & (~right[None, :])
    m_b = np.transpose(m_f, (0, 2, 1))
    tri_f = (t[:, None] >= t[None, :]).astype(np.float32)
    tri_b = (t[:, None] <= t[None, :]).astype(np.float32)
    return jnp.asarray(tri_f, BF16), jnp.asarray(tri_b, BF16), jnp.asarray(m_f, F32), jnp.asarray(m_b, F32)


def _hg_dir(u, lb_row, st_ref, tri_ref, mask_ref, fwd, need_out=True):
    c = HG_CHUNK
    end = c - 1 if fwd else 0
    outs = []
    fcol = HG_INNER if fwd else 2 * HG_INNER
    nt = (((1,), (1,)), ((), ()))
    row = lax.broadcasted_iota(jnp.int32, (c, HG_DK), 0)

    def halves(lo, hi, half):
        if half % SUBLANES == 0:
            return jnp.concatenate([(hi if (s // half) % 2 else lo)[s:s + half] for s in range(0, c, half)], axis=0)
        return jnp.where(((row // half) % 2) == 1, hi, lo)

    def shift(x, s):
        s = s % c
        if s % SUBLANES == 0:
            return jnp.concatenate([x[c - s:], x[:c - s]], axis=0)
        return pltpu.roll(x, s, 0)

    def by_side(query_side, key_side, half):
        return halves(key_side, query_side, half) if fwd else halves(query_side, key_side, half)

    for hd in range(HG_HEADS):
        sl = slice(hd * HG_DK, (hd + 1) * HG_DK)
        lb = lb_row[:, sl]
        f = lb + (1.0 - lb) * _sigmoid(u[:, fcol + hd * HG_DK:fcol + (hd + 1) * HG_DK])
        k = 1.0 - f
        la = jnp.log(f)
        v = u[:, 3 * HG_INNER + hd * HG_DK:3 * HG_INNER + (hd + 1) * HG_DK]
        la_hi = la.astype(BF16)
        la_lo = (la - la_hi.astype(F32)).astype(BF16)
        g2 = jnp.dot(tri_ref[...], jnp.concatenate([la_hi, la_lo], axis=1), preferred_element_type=F32)
        g = g2[:, :HG_DK] + g2[:, HG_DK:]
        st = st_ref[hd]
        if not need_out:
            g_end = g[end:end + 1, :]
            st_ref[hd] = (st * jnp.exp(g_end)
                          + jnp.dot(v.T.astype(BF16), (k * jnp.exp(g_end - g)).astype(BF16),
                                    preferred_element_type=F32))
            continue
        q = _silu(u[:, sl]) * (HG_DK ** -0.5)
        scores = mask_ref[0] * lax.dot_general(q.astype(BF16), k.astype(BF16), nt, preferred_element_type=F32)
        fill = g
        for lv in range(1, HG_LEVELS + 1):
            half = 2 ** (lv - 1)
            if fwd:
                ref = halves(fill, shift(fill, half), half)
            else:
                ref = halves(shift(fill, -half), fill, half)
            decay = jnp.exp2(jnp.abs(g - ref) * (-LOG2E))
            y = (by_side(q, k, half) * decay).astype(BF16)
            scores = scores + mask_ref[lv] * lax.dot_general(y, y, nt, preferred_element_type=F32)
            if fwd:
                fill = halves(shift(fill, -half), fill, half)
            else:
                fill = halves(fill, shift(fill, half), half)
        x_cum = jnp.exp(g)
        x_rem = jnp.exp(fill - g)
        o = jnp.dot(scores.astype(BF16), v.astype(BF16), preferred_element_type=F32)
        o = o + lax.dot_general((q * x_cum).astype(BF16), st.astype(BF16), nt, preferred_element_type=F32)
        outs.append(o)
        st_ref[hd] = (st * x_cum[end:end + 1, :]
                      + jnp.dot(v.T.astype(BF16), (k * x_rem).astype(BF16), preferred_element_type=F32))
    return jnp.concatenate(outs, axis=1) if need_out else None


def _hg_scan_kernel(uf_ref, ub_ref, lbp_ref, s0f_ref, s0b_ref, trif_ref, trib_ref, mf_ref, mb_ref,
                    *rest, img_rows, need_out, step_chunks):
    if need_out:
        of_ref, ob_ref, sfo_ref, sbo_ref, sf, sb = rest
    else:
        sfo_ref, sbo_ref, sf, sb = rest
    n = pl.program_id(1)
    c = HG_CHUNK

    def put(o_ref, o, sub):
        if img_rows:
            cols = c // img_rows
            for j in range(cols):
                lo = (sub * cols + j) * HG_INNER
                o_ref[:, lo:lo + HG_INNER] = o[j * img_rows:(j + 1) * img_rows]
        else:
            o_ref[sub * c:(sub + 1) * c, :] = o

    @pl.when(n == 0)
    def _():
        sf[...] = s0f_ref[...]
        sb[...] = s0b_ref[...]

    p = lbp_ref[...]
    mx = jnp.max(p, axis=0, keepdims=True)
    e = jnp.exp(p - mx)
    lb = e[0:1, :] / jnp.sum(e, axis=0, keepdims=True)
    for sub in range(step_chunks):
        rev = step_chunks - 1 - sub
        o_f = _hg_dir(uf_ref[sub * c:(sub + 1) * c, :], lb[:, :HG_INNER], sf, trif_ref, mf_ref, True, need_out)
        o_b = _hg_dir(ub_ref[rev * c:(rev + 1) * c, :], lb[:, HG_INNER:], sb, trib_ref, mb_ref, False, need_out)
        if need_out:
            put(of_ref, o_f, sub)
            put(ob_ref, o_b, rev)

    @pl.when(n == pl.num_programs(1) - 1)
    def _():
        sfo_ref[...] = sf[...]
        sbo_ref[...] = sb[...]


def hg_scan_call(u, lbp, s0f, s0b, row_major_out, need_out=True):
    bsz, t_len, _ = u.shape
    n_chunks = t_len // HG_CHUNK
    step_chunks = max(s for s in HG_STEP_CHUNKS if n_chunks % s == 0)
    c = HG_CHUNK * step_chunks
    nc = t_len // c
    assert nc * c == t_len
    full = lambda shape: pl.BlockSpec(shape, lambda b, n: (0,) * len(shape))
    st = pl.BlockSpec((None, HG_HEADS, HG_DK, HG_DK), lambda b, n: (b, 0, 0, 0))
    st_shape = jax.ShapeDtypeStruct((bsz, HG_HEADS, HG_DK, HG_DK), F32)
    if row_major_out:
        img_rows = t_len // GRID_W
        cols = c // img_rows
        assert cols * img_rows == c and HG_CHUNK % img_rows == 0
        o_shape = jax.ShapeDtypeStruct((bsz, img_rows, GRID_W * HG_INNER), F32)
        of_spec = pl.BlockSpec((None, img_rows, cols * HG_INNER), lambda b, n: (b, 0, n))
        ob_spec = pl.BlockSpec((None, img_rows, cols * HG_INNER), lambda b, n: (b, 0, nc - 1 - n))
    else:
        img_rows = 0
        o_shape = jax.ShapeDtypeStruct((bsz, t_len, HG_INNER), F32)
        of_spec = pl.BlockSpec((None, c, HG_INNER), lambda b, n: (b, n, 0))
        ob_spec = pl.BlockSpec((None, c, HG_INNER), lambda b, n: (b, nc - 1 - n, 0))
    tri_f, tri_b, m_f, m_b = _hg_consts()
    res = pl.pallas_call(
        functools.partial(_hg_scan_kernel, img_rows=img_rows, need_out=need_out, step_chunks=step_chunks),
        grid=(bsz, nc),
        in_specs=[pl.BlockSpec((None, c, B_COLS), lambda b, n: (b, n, 0)),
                  pl.BlockSpec((None, c, B_COLS), lambda b, n: (b, nc - 1 - n, 0)),
                  full((2, 2 * HG_INNER)), st, st,
                  full(tri_f.shape), full(tri_b.shape), full(m_f.shape), full(m_b.shape)],
        out_specs=([of_spec, ob_spec] if need_out else []) + [st, st],
        out_shape=([o_shape, o_shape] if need_out else []) + [st_shape, st_shape],
        scratch_shapes=[pltpu.VMEM((HG_HEADS, HG_DK, HG_DK), F32), pltpu.VMEM((HG_HEADS, HG_DK, HG_DK), F32)],
        compiler_params=_cparams(("arbitrary", "arbitrary")),
        name="hg_scan",
    )(u, u, lbp, s0f, s0b, tri_f, tri_b, m_f, m_b)
    if not need_out:
        return (None, None) + tuple(res)
    return tuple(res)


ROW_LINES = D_MODEL // LANES


def _to_token_tiles(ref, val, n_rows):
    for c in range(ROW_LINES):
        ref[pl.ds(c, n_rows, stride=ROW_LINES), :] = val[:, c * LANES:(c + 1) * LANES]


def _from_token_tiles(ref, n_rows, first_row=0):
    return jnp.concatenate([ref[pl.ds(first_row * ROW_LINES + c, n_rows, stride=ROW_LINES), :]
                            for c in range(ROW_LINES)], axis=1)


def _tile_of(ref, row):
    return ref.at[pl.ds(pl.multiple_of(row * ROW_LINES, ROW_LINES), ROW_LINES)]


def _post_kernel(x_ref, sof_ref, sob_ref, z_ref, hof_ref, hob_ref, hg_ref,
                 snw_ref, hnw_ref, wo_ref, pnw_ref, gm_ref, fnw_ref, shf_ref, scf_ref, rwh_ref, rwl_ref, rb_ref,
                 tri_ref, x1_ref, hx_ref, idx_ref, rank_ref, gate_ref, cnt_ref, carry, hscr, *, tm):
    first = (pl.program_id(0) == 0) & (pl.program_id(1) == 0)

    @pl.when(first)
    def _():
        carry[...] = jnp.zeros_like(carry)

    y = (sof_ref[...] + sob_ref[...]) * _silu(z_ref[...])
    y = _rms(y, snw_ref[...])
    ho = hof_ref[...] + hob_ref[...]
    n_img = tm // GRID_W
    for w in range(GRID_W):
        for cc in range(HG_INNER // LANES):
            lo = w * HG_INNER + cc * LANES
            hscr[cc, pl.ds(w, n_img, stride=GRID_W), :] = ho[:, lo:lo + LANES]
    o = jnp.concatenate([hscr[cc] for cc in range(HG_INNER // LANES)], axis=1)
    hnw = hnw_ref[...]
    o = jnp.concatenate([_rms(o[:, h * HG_DK:(h + 1) * HG_DK], hnw[:, h * HG_DK:(h + 1) * HG_DK])
                         for h in range(HG_HEADS)], axis=1)
    o = o * _silu(hg_ref[...])
    mix = (jnp.dot(y.astype(BF16), wo_ref[:SSD_INNER, :], preferred_element_type=F32)
           + jnp.dot(o.astype(BF16), wo_ref[SSD_INNER:, :], preferred_element_type=F32))
    x1 = x_ref[...] + gm_ref[...] * _rms(mix, pnw_ref[...])
    x1_ref[...] = x1
    hx = _rms(x1, fnw_ref[...]) * (1.0 + scf_ref[...]) + shf_ref[...]
    _to_token_tiles(hx_ref, hx, tm)
    hx_hi = hx.astype(BF16)
    hx_lo = (hx - hx_hi.astype(F32)).astype(BF16)
    rwh = rwh_ref[...]
    logits = (jnp.dot(hx_hi, rwh, preferred_element_type=F32) + jnp.dot(hx_lo, rwh, preferred_element_type=F32)
              + jnp.dot(hx_hi, rwl_ref[...], preferred_element_type=F32)) + rb_ref[...]
    work = logits.T[:N_EXPERTS, :]
    erow = lax.broadcasted_iota(jnp.int32, (N_EXPERTS, tm), 0)
    vals, idxs = [], []
    for _ in range(TOP_K):
        m = jnp.max(work, axis=0, keepdims=True)
        ix = jnp.min(jnp.where(work == m, erow, N_EXPERTS), axis=0, keepdims=True)
        vals.append(m)
        idxs.append(ix)
        work = jnp.where(erow == ix, -jnp.inf, work)
    es = [jnp.exp(v - vals[0]) for v in vals]
    den = es[0] + es[1] + es[2] + es[3]
    onehots = [(erow == ix) for ix in idxs]
    multi = sum(oh.astype(F32) for oh in onehots)
    before = jnp.dot(multi.astype(BF16), tri_ref[...], preferred_element_type=F32) + carry[...]
    carry[...] = carry[...] + jnp.sum(multi, axis=1, keepdims=True)
    sub = lax.broadcasted_iota(jnp.int32, (SUBLANES, tm), 0)
    idx_o = jnp.zeros((SUBLANES, tm), jnp.int32)
    rank_o = jnp.zeros((SUBLANES, tm), jnp.int32)
    gate_o = jnp.zeros((SUBLANES, tm), F32)
    for k in range(TOP_K):
        rk = jnp.sum(jnp.where(onehots[k], before, 0.0), axis=0, keepdims=True)
        idx_o = jnp.where(sub == k, idxs[k], idx_o)
        rank_o = jnp.where(sub == k, rk.astype(jnp.int32), rank_o)
        gate_o = jnp.where(sub == k, es[k] / den, gate_o)
    idx_ref[...] = idx_o
    rank_ref[...] = rank_o
    gate_ref[...] = gate_o
    cnt_ref[...] = jnp.broadcast_to(carry[...], (N_EXPERTS, LANES))


def post_call(x, sof, sob, z, hof, hob, hg, snw, hnw, wo, pnw, gm, fnw, shf, scf, rwh, rwl, rb):
    bsz, t_len, _ = x.shape
    tm = TOK_TILE
    nt = t_len // tm
    n_tok = bsz * t_len
    tok = lambda c: pl.BlockSpec((None, tm, c), lambda b, i: (b, i, 0))
    full = lambda shape: pl.BlockSpec(shape, lambda b, i: (0,) * len(shape))
    per_b = pl.BlockSpec((None, 1, D_MODEL), lambda b, i: (b, 0, 0))
    flat = lambda c: pl.BlockSpec((tm, c), lambda b, i: (b * nt + i, 0))
    rout = pl.BlockSpec((SUBLANES, tm), lambda b, i: (b * nt + i, 0))
    n_tiles = n_tok // tm
    ii = np.arange(tm)
    tri = jnp.asarray(ii[:, None] < ii[None, :], BF16)
    assert tm % (GRID_W * SUBLANES) == 0 and hof.shape == (bsz, t_len // GRID_W, GRID_W * HG_INNER)
    img = pl.BlockSpec((None, tm // GRID_W, GRID_W * HG_INNER), lambda b, i: (b, i, 0))
    return pl.pallas_call(
        functools.partial(_post_kernel, tm=tm),
        grid=(bsz, nt),
        in_specs=[tok(D_MODEL), tok(SSD_INNER), tok(SSD_INNER), tok(SSD_INNER), img, img,
                  tok(HG_INNER), full((1, SSD_INNER)), full((1, HG_INNER)), full((D_MODEL, D_MODEL)),
                  full((1, D_MODEL)), per_b, full((1, D_MODEL)), per_b, per_b,
                  full((D_MODEL, LANES)), full((D_MODEL, LANES)), full((1, LANES)), full((tm, tm))],
        out_specs=[flat(D_MODEL), pl.BlockSpec((tm * ROW_LINES, LANES), lambda b, i: (b * nt + i, 0)),
                   rout, rout, rout, pl.BlockSpec((N_EXPERTS, LANES), lambda b, i: (0, 0))],
        out_shape=[jax.ShapeDtypeStruct((n_tok, D_MODEL), F32),
                   jax.ShapeDtypeStruct((n_tok * ROW_LINES, LANES), F32),
                   jax.ShapeDtypeStruct((n_tiles * SUBLANES, tm), jnp.int32),
                   jax.ShapeDtypeStruct((n_tiles * SUBLANES, tm), jnp.int32),
                   jax.ShapeDtypeStruct((n_tiles * SUBLANES, tm), F32),
                   jax.ShapeDtypeStruct((N_EXPERTS, LANES), F32)],
        scratch_shapes=[pltpu.VMEM((N_EXPERTS, 1), F32), pltpu.VMEM((HG_INNER // LANES, tm, LANES), F32)],
        compiler_params=_cparams(("arbitrary", "arbitrary")),
        name="post",
    )(x, sof, sob, z, hof, hob, hg, snw, hnw, wo, pnw, gm, fnw, shf, scf, rwh, rwl, rb, tri)


def _wait_rows(hbm_ref, n_rows, sem):
    n = n_rows * ROW_LINES
    pltpu.make_async_copy(hbm_ref.at[pl.ds(0, n)], hbm_ref.at[pl.ds(0, n)], sem).wait()


PAD_RUNS = tuple(2 ** j for j in range(int(math.log2(MOE_ROWS))))
DISPATCH_TILES = 2


def _dispatch_kernel(zstart_ref, zpad_ref, dest_ref, hx_ref, buf_ref, zrows, sem, zsem, *, tm):
    i = pl.program_id(0)

    def zero_rows(first, n):
        first = pl.multiple_of(first * ROW_LINES, ROW_LINES)
        return pltpu.make_async_copy(zrows.at[pl.ds(0, n * ROW_LINES)], buf_ref.at[pl.ds(first, n * ROW_LINES)], zsem)

    def for_pad_runs(act):
        def per_expert(e, carry):
            pad = zpad_ref[e]
            off = zstart_ref[e]
            for run in PAD_RUNS:
                @pl.when((pad & run) != 0)
                def _():
                    act(zero_rows(off, run))
                off = off + (pad & run)
            return carry

        lax.fori_loop(0, N_EXPERTS, per_expert, 0)

        def tail(j, carry):
            act(zero_rows(j * PAD_RUNS[-1], PAD_RUNS[-1]))
            return carry

        n_rows = buf_ref.shape[0] // ROW_LINES
        lax.fori_loop(zstart_ref[N_EXPERTS] // PAD_RUNS[-1], n_rows // PAD_RUNS[-1], tail, 0)

    @pl.when(i == 0)
    def _():
        zrows[...] = jnp.zeros_like(zrows)
        for_pad_runs(lambda cp: cp.start())

    def issue(t, carry):
        for sub in range(DISPATCH_TILES):
            for k in range(TOP_K):
                pltpu.make_async_copy(_tile_of(hx_ref, sub * tm + t),
                                      _tile_of(buf_ref, dest_ref[(sub * TOP_K + k) * tm + t]),
                                      sem).start(priority=k % 2)
        return carry

    lax.fori_loop(0, tm, issue, 0, unroll=2)
    _wait_rows(buf_ref, DISPATCH_TILES * tm * TOP_K, sem)

    @pl.when(i == 0)
    def _():
        for_pad_runs(lambda cp: cp.wait())


def dispatch_call(zstart, zpad, dest_flat, hx, n_rows):
    n_tok = hx.shape[0] // ROW_LINES
    tm = TOK_TILE
    step_tok = DISPATCH_TILES * tm
    return pl.pallas_call(
        functools.partial(_dispatch_kernel, tm=tm),
        grid=(n_tok // step_tok,),
        in_specs=[pl.BlockSpec(memory_space=pltpu.SMEM), pl.BlockSpec(memory_space=pltpu.SMEM),
                  pl.BlockSpec((step_tok * TOP_K,), lambda i: (i,), memory_space=pltpu.SMEM),
                  pl.BlockSpec((step_tok * ROW_LINES, LANES), lambda i: (i, 0))],
        out_specs=pl.BlockSpec(memory_space=pl.ANY),
        out_shape=jax.ShapeDtypeStruct((n_rows * ROW_LINES, LANES), F32),
        scratch_shapes=[pltpu.VMEM((PAD_RUNS[-1] * ROW_LINES, LANES), F32), pltpu.SemaphoreType.DMA(()),
                        pltpu.SemaphoreType.DMA(())],
        compiler_params=_cparams(("arbitrary",)),
        name="dispatch",
    )(zstart, zpad, dest_flat, hx)


W1_TCOLS = 256


def _experts_kernel(be_ref, nu_ref, nxt_ref, x_ref, w1_hbm, b1_ref, w2_hbm, b2_ref, y_ref,
                    w1f, w2f, w1t, w2s, tbuf, wsem):
    i = pl.program_id(0)
    live = i < nu_ref[0]
    new_expert = (i == 0) | (be_ref[i] != be_ref[jnp.maximum(i - 1, 0)])

    def weight_copies(e):
        return (pltpu.make_async_copy(w1_hbm.at[e], w1f, wsem.at[0]),
                pltpu.make_async_copy(w2_hbm.at[e], w2f, wsem.at[1]))

    @pl.when(live & new_expert)
    def _():
        @pl.when(i == 0)
        def _():
            for cp in weight_copies(be_ref[0]):
                cp.start()

        for cp in weight_copies(be_ref[i]):
            cp.wait()
        half = W1_TCOLS // 2
        for c in range(2 * D_FF // W1_TCOLS):
            for j in range(D_MODEL // LANES):
                ks = slice(j * LANES, (j + 1) * LANES)
                tbuf[j] = w1f[ks, c * W1_TCOLS:(c + 1) * W1_TCOLS].T
                w1t[c * half:(c + 1) * half, ks] = tbuf[j, pl.ds(0, half, stride=2), :].astype(BF16)
                w1t[D_FF + c * half:D_FF + (c + 1) * half, ks] = tbuf[j, pl.ds(1, half, stride=2), :].astype(BF16)
        w2s[...] = w2f[...].astype(BF16)

        @pl.when(nxt_ref[i] >= 0)
        def _():
            for cp in weight_copies(nxt_ref[i]):
                cp.start(priority=1)

    @pl.when(live)
    def _():
        xb = _from_token_tiles(x_ref, MOE_ROWS).astype(BF16)
        u = lax.dot_general(xb, w1t[...], (((1,), (1,)), ((), ())), preferred_element_type=F32) + b1_ref[...]
        glu = jnp.minimum(u[:, :D_FF], SWIGLU_LIMIT)
        lin = jnp.clip(u[:, D_FF:], -SWIGLU_LIMIT, SWIGLU_LIMIT)
        a = glu * _sigmoid(SWIGLU_ALPHA * glu) * (lin + 1.0)
        y = jnp.dot(a.astype(BF16), w2s[...], preferred_element_type=F32) + b2_ref[...]
        _to_token_tiles(y_ref, y, MOE_ROWS)

    @pl.when(jnp.logical_not(live))
    def _():
        y_ref[...] = jnp.zeros_like(y_ref)


def experts_call(block_e, n_used, next_e, xs, w1, b1p, w2, b2):
    rows = xs.shape[0] // ROW_LINES
    r = MOE_ROWS
    nb = rows // r
    blk = (r * ROW_LINES, LANES)
    grid_spec = pltpu.PrefetchScalarGridSpec(
        num_scalar_prefetch=3,
        grid=(nb,),
        in_specs=[pl.BlockSpec(blk, lambda i, be, nu, nx: (jnp.maximum(jnp.minimum(i, nu[0] - 1), 0), 0)),
                  pl.BlockSpec(memory_space=pl.ANY),
                  pl.BlockSpec((None, 1, 2 * D_FF), lambda i, be, nu, nx: (be[i], 0, 0)),
                  pl.BlockSpec(memory_space=pl.ANY),
                  pl.BlockSpec((None, 1, D_MODEL), lambda i, be, nu, nx: (be[i], 0, 0))],
        out_specs=pl.BlockSpec(blk, lambda i, be, nu, nx: (i, 0)),
        scratch_shapes=[pltpu.VMEM((D_MODEL, 2 * D_FF), F32), pltpu.VMEM((D_FF, D_MODEL), F32),
                        pltpu.VMEM((2 * D_FF, D_MODEL), BF16), pltpu.VMEM((D_FF, D_MODEL), BF16),
                        pltpu.VMEM((D_MODEL // LANES, W1_TCOLS, LANES), F32),
                        pltpu.SemaphoreType.DMA((2,))],
    )
    return pl.pallas_call(
        _experts_kernel,
        grid_spec=grid_spec,
        out_shape=jax.ShapeDtypeStruct((rows * ROW_LINES, LANES), F32),
        compiler_params=_cparams(("arbitrary",)),
        name="experts",
    )(block_e, n_used, next_e, xs, w1, b1p, w2, b2)


def _combine_kernel(dest_ref, dest_next_ref, y_ref, gate_ref, x1_ref, gf_ref, nw_ref, o_ref, buf, sem, *, tm):
    i = pl.program_id(0)
    slot = i % 2

    def gather_tile(dref, into):
        def issue(t, carry):
            for k in range(TOP_K):
                pltpu.make_async_copy(_tile_of(y_ref, dref[k * tm + t]), _tile_of(buf.at[into], k * tm + t),
                                      sem.at[into]).start(priority=k % 2)
            return carry

        lax.fori_loop(0, tm, issue, 0, unroll=4)

    @pl.when(i == 0)
    def _():
        gather_tile(dest_ref, 0)

    @pl.when(i + 1 < pl.num_programs(0))
    def _():
        gather_tile(dest_next_ref, 1 - slot)

    _wait_rows(y_ref, tm * TOP_K, sem.at[slot])
    gate = gate_ref[...].T
    fx = None
    for k in range(TOP_K):
        yk = jnp.concatenate([buf[slot, pl.ds(k * tm * ROW_LINES + c, tm, stride=ROW_LINES), :]
                              for c in range(ROW_LINES)], axis=1)
        fx = gate[:, k:k + 1] * yk if fx is None else fx + gate[:, k:k + 1] * yk
    o_ref[...] = x1_ref[...] + gf_ref[...] * _rms(fx, nw_ref[...])


def combine_call(dest_flat, ys, gates, x1, gf, nw, bsz):
    n_tok = x1.shape[0]
    tm = TOK_TILE
    nt = n_tok // bsz // tm
    n_steps = n_tok // tm
    return pl.pallas_call(
        functools.partial(_combine_kernel, tm=tm),
        grid=(n_steps,),
        in_specs=[pl.BlockSpec((tm * TOP_K,), lambda i: (i,), memory_space=pltpu.SMEM),
                  pl.BlockSpec((tm * TOP_K,), lambda i: (jnp.minimum(i + 1, n_steps - 1),),
                               memory_space=pltpu.SMEM),
                  pl.BlockSpec(memory_space=pl.ANY),
                  pl.BlockSpec((SUBLANES, tm), lambda i: (i, 0)),
                  pl.BlockSpec((tm, D_MODEL), lambda i: (i, 0)),
                  pl.BlockSpec((None, 1, D_MODEL), lambda i: (i // nt, 0, 0)),
                  pl.BlockSpec((1, D_MODEL), lambda i: (0, 0))],
        out_specs=pl.BlockSpec((tm, D_MODEL), lambda i: (i, 0)),
        out_shape=jax.ShapeDtypeStruct((n_tok, D_MODEL), F32),
        scratch_shapes=[pltpu.VMEM((2, TOP_K * tm * ROW_LINES, LANES), F32), pltpu.SemaphoreType.DMA((2,))],
        compiler_params=_cparams(("arbitrary",)),
        name="combine",
    )(dest_flat, dest_flat, ys, gates, x1, gf, nw)


def kernel(x, c, ctx, c_ctx, ada_w, ada_b, mix_pre_norm, mix_post_norm, w_in, w_out, ssd_conv_w, ssd_conv_b,
           ssd_dt_bias, ssd_a_log, ssd_d, ssd_norm, hg_lb, hg_norm, ffn_pre_norm, ffn_post_norm, router_w,
           router_b, moe_w1, moe_b1, moe_w2, moe_b2):
    bsz, t_len, d = x.shape
    assert ada_w.shape[0] == 1 and d == D_MODEL and bsz <= SUBLANES - 1
    n_tok = bsz * t_len

    cc = jnp.zeros((SUBLANES, d), F32).at[:bsz].set(c).at[bsz].set(c_ctx)
    mod = ada_call(cc, ada_w[0], ada_b[0][None, :])
    sh_m, sc_m, g_m, sh_f, sc_f, g_f = [m[:bsz, None, :] for m in jnp.split(mod, 6, axis=-1)]
    csh_m, csc_m = [jnp.broadcast_to(m[bsz][None, None, :], (bsz, 1, d)) for m in jnp.split(mod, 6, axis=-1)[:2]]

    w = w_in[0]
    wa = jnp.concatenate([w[:, :SSD_INNER + SSD_XBC], w[:, SSD_COLS + 4 * HG_INNER:]], axis=1).astype(BF16)
    wdt = jnp.zeros((d, LANES), F32).at[:, :2 * SSD_HEADS].set(w[:, SSD_INNER + SSD_XBC:SSD_COLS]).astype(BF16)
    wb = w[:, SSD_COLS:SSD_COLS + 4 * HG_INNER].astype(BF16)
    dtb = jnp.zeros((1, LANES), F32).at[0, :2 * SSD_HEADS].set(ssd_dt_bias[0].reshape(-1))
    nw = mix_pre_norm[0][None, :]
    cw, cb = ssd_conv_w[0], ssd_conv_b[0][None, :]
    neg_a = -jnp.exp(ssd_a_log[0].astype(F32)).reshape(1, 2 * SSD_HEADS)
    dskip = jnp.repeat(ssd_d[0], SSD_HEAD_DIM)[None, :]
    lbp = hg_lb.astype(F32).reshape(2, 2 * HG_INNER)

    _, cxbc, _, cdt, cdtT = inproj_a_call(ctx, csh_m, csc_m, nw, wa, wdt, cw, cb, dtb, tm=ctx.shape[1])
    cu = inproj_b_call(ctx, csh_m, csc_m, nw, wb, col_major=False)
    z_ssd = jnp.zeros((bsz, N_PAIRS, SSD_STATE, LANES), F32)
    z_hg = jnp.zeros((bsz, HG_HEADS, HG_DK, HG_DK), F32)
    _, _, ssf, ssb = ssd_scan_call(cxbc, cdt, cdtT, z_ssd, z_ssd, neg_a, neg_a.reshape(-1, 1), dskip,
                                   need_out=False)
    _, _, hsf, hsb = hg_scan_call(cu, lbp, z_hg, z_hg, row_major_out=False, need_out=False)

    zg, xbc, hgate, dt, dtT = inproj_a_call(x, sh_m, sc_m, nw, wa, wdt, cw, cb, dtb, tm=512)
    ub = inproj_b_call(x, sh_m, sc_m, nw, wb, col_major=True)
    sof, sob, _, _ = ssd_scan_call(xbc, dt, dtT, ssf, ssb, neg_a, neg_a.reshape(-1, 1), dskip)
    hof, hob, _, _ = hg_scan_call(ub, lbp, hsf, hsb, row_major_out=True)

    rw = jnp.zeros((d, LANES), F32).at[:, :N_EXPERTS].set(router_w[0])
    rwh = rw.astype(BF16)
    rwl = (rw - rwh.astype(F32)).astype(BF16)
    rb = jnp.zeros((1, LANES), F32).at[0, :N_EXPERTS].set(router_b[0])
    x1, hx, idx, rank, gates, cnt = post_call(
        x, sof, sob, zg, hof, hob, hgate, ssd_norm[0][None, :], hg_norm[0][None, :], w_out[0].astype(BF16),
        mix_post_norm[0][None, :], g_m, ffn_pre_norm[0][None, :], sh_f, sc_f, rwh, rwl, rb)

    r = MOE_ROWS
    n_blocks = (n_tok * TOP_K + N_EXPERTS * (r - 1) + r - 1) // r
    n_tiles = n_tok // TOK_TILE
    counts = cnt[:, 0].astype(jnp.int32)
    padded = (counts + r - 1) // r * r
    pend = jnp.cumsum(padded)
    pstart = pend - padded
    idx3 = idx.reshape(n_tiles, SUBLANES, TOK_TILE)[:, :TOP_K, :]
    rank3 = rank.reshape(n_tiles, SUBLANES, TOK_TILE)[:, :TOP_K, :]
    dest = rank3
    for e in range(N_EXPERTS):
        dest = dest + jnp.where(idx3 == e, pstart[e], 0)
    dest = dest.reshape(-1)
    starts = jnp.arange(n_blocks, dtype=jnp.int32) * r
    block_e = jnp.minimum(jnp.sum((pend[None, :] <= starts[:, None]).astype(jnp.int32), axis=1), N_EXPERTS - 1)
    n_used = (pend[-1:] // r).astype(jnp.int32)
    run_end = pend[block_e] // r
    next_e = jnp.where(run_end < n_used[0], block_e[jnp.minimum(run_end, n_blocks - 1)], -1).astype(jnp.int32)

    zstart = jnp.concatenate([pstart + counts, pend[-1:]])
    xs = dispatch_call(zstart, padded - counts, dest, hx, n_blocks * r)
    b1p = jnp.concatenate([moe_b1[0][:, 0::2], moe_b1[0][:, 1::2]], axis=-1)[:, None, :]
    ys = experts_call(block_e, n_used, next_e, xs, moe_w1[0], b1p, moe_w2[0], moe_b2[0][:, None, :])
    out = combine_call(dest, ys, gates, x1, g_f, ffn_post_norm[0][None, :], bsz)
    return out.reshape(bsz, t_len, d)
```

```python
import functools
import math

import numpy as np
import jax
import jax.numpy as jnp
from jax import lax
from jax.experimental import pallas as pl
from jax.experimental.pallas import tpu as pltpu

F32 = jnp.float32
BF16 = jnp.bfloat16
HIGHEST = lax.Precision.HIGHEST

D_MODEL = 1024
GRID_W = 64
SSD_HEADS = 8
SSD_HEAD_DIM = 64
SSD_INNER = 512
SSD_STATE = 64
SSD_GROUPS = 2
SSD_CONV = 5
SSD_XBC = 768
SSD_COLS = 1296
HG_HEADS = 4
HG_DK = 128
HG_INNER = 512
N_EXPERTS = 32
TOP_K = 4
D_FF = 1024
SWIGLU_ALPHA = 1.702
SWIGLU_LIMIT = 7.0
EPS = 1e-6

LANES = 128
SUBLANES = 8
VMEM_LIMIT = 56 * 1024 * 1024

SSD_CHUNK = 256
SSD_STEP_CHUNKS = 2
HG_CHUNK = 128
HG_LEVELS = 7
HG_STEP_CHUNKS = (1, 2, 4)
MOE_ROWS = 256
TOK_TILE = 512
LOG2E = 1.4426950408889634


def _cparams(sem):
    return pltpu.CompilerParams(dimension_semantics=sem, vmem_limit_bytes=VMEM_LIMIT)


def _sigmoid(x):
    return 1.0 / (1.0 + jnp.exp(-x))


def _silu(x):
    return x * _sigmoid(x)


def _rms(x, w):
    return x * lax.rsqrt(jnp.mean(x * x, axis=-1, keepdims=True) + EPS) * w


def _split3(v):
    hi = v.astype(BF16)
    r1 = v - hi.astype(F32)
    mid = r1.astype(BF16)
    lo = (r1 - mid.astype(F32)).astype(BF16)
    return hi, mid, lo


def _ada_kernel(c_ref, w_ref, b_ref, o_ref):
    s = _silu(c_ref[...])
    o_ref[...] = jnp.dot(s, w_ref[...], precision=HIGHEST, preferred_element_type=F32) + b_ref[...]


def ada_call(cc, w, b):
    n = w.shape[1]
    tn = 1536
    return pl.pallas_call(
        _ada_kernel,
        grid=(n // tn,),
        in_specs=[pl.BlockSpec((SUBLANES, D_MODEL), lambda j: (0, 0)),
                  pl.BlockSpec((D_MODEL, tn), lambda j: (0, j)),
                  pl.BlockSpec((1, tn), lambda j: (0, j))],
        out_specs=pl.BlockSpec((SUBLANES, tn), lambda j: (0, j)),
        out_shape=jax.ShapeDtypeStruct((SUBLANES, n), F32),
        compiler_params=_cparams(("arbitrary",)),
        name="ada",
    )(cc, w, b)


A_COLS = SSD_INNER + SSD_XBC + HG_INNER


def _prep(xt, nw, sh, sc):
    return (_rms(xt, nw) * (1.0 + sc) + sh).astype(BF16)


def _inproj_a_kernel(x_ref, xp_ref, xn_ref, sh_ref, sc_ref, nw_ref, wa_ref, wdt_ref, cw_ref, cb_ref, dtb_ref,
                     z_ref, xbc_ref, g_ref, dt_ref, dtT_ref, scr, *, tm):
    i = pl.program_id(1)
    last = pl.num_programs(1) - 1
    nw, sh, sc = nw_ref[...], sh_ref[...], sc_ref[...]
    h = _prep(x_ref[...], nw, sh, sc)
    ua = jnp.dot(h, wa_ref[...], preferred_element_type=F32)
    z_ref[...] = ua[:, :SSD_INNER]
    g_ref[...] = ua[:, SSD_INNER + SSD_XBC:]
    wx = wa_ref[:, SSD_INNER:SSD_INNER + SSD_XBC]
    up = jnp.dot(_prep(xp_ref[...], nw, sh, sc), wx, preferred_element_type=F32)
    un = jnp.dot(_prep(xn_ref[...], nw, sh, sc), wx, preferred_element_type=F32)
    scr[0:SUBLANES, :] = jnp.where(i > 0, up, 0.0)
    scr[SUBLANES:SUBLANES + tm, :] = ua[:, SSD_INNER:SSD_INNER + SSD_XBC]
    scr[SUBLANES + tm:, :] = jnp.where(i < last, un, 0.0)
    acc = jnp.broadcast_to(cb_ref[...], (tm, SSD_XBC))
    pad = SSD_CONV // 2
    for k in range(SSD_CONV):
        off = SUBLANES - pad + k
        acc = acc + cw_ref[k:k + 1, :] * scr[off:off + tm, :]
    xbc_ref[...] = _silu(acc)
    draw = jnp.dot(h, wdt_ref[...], preferred_element_type=F32) + dtb_ref[...]
    dt = jnp.maximum(draw, 0.0) + jnp.log(1.0 + jnp.exp(-jnp.abs(draw)))
    dt_ref[...] = dt[:, :2 * SSD_HEADS]
    dtT_ref[...] = dt.T[:2 * SSD_HEADS, :]


def inproj_a_call(x, sh, sc, nw, wa, wdt, cw, cb, dtb, tm):
    bsz, t_len, _ = x.shape
    nt = t_len // tm
    r8 = tm // SUBLANES
    n8 = t_len // SUBLANES
    full = lambda shape: pl.BlockSpec(shape, lambda b, i: (0,) * len(shape))
    tok = lambda c: pl.BlockSpec((None, tm, c), lambda b, i: (b, i, 0))
    return pl.pallas_call(
        functools.partial(_inproj_a_kernel, tm=tm),
        grid=(bsz, nt),
        in_specs=[tok(D_MODEL),
                  pl.BlockSpec((None, SUBLANES, D_MODEL), lambda b, i: (b, jnp.maximum(i * r8 - 1, 0), 0)),
                  pl.BlockSpec((None, SUBLANES, D_MODEL), lambda b, i: (b, jnp.minimum((i + 1) * r8, n8 - 1), 0)),
                  pl.BlockSpec((None, 1, D_MODEL), lambda b, i: (b, 0, 0)),
                  pl.BlockSpec((None, 1, D_MODEL), lambda b, i: (b, 0, 0)),
                  full((1, D_MODEL)), full((D_MODEL, A_COLS)), full((D_MODEL, LANES)),
                  full((SSD_CONV, SSD_XBC)), full((1, SSD_XBC)), full((1, LANES))],
        out_specs=[tok(SSD_INNER), tok(SSD_XBC), tok(HG_INNER), tok(2 * SSD_HEADS),
                   pl.BlockSpec((None, 2 * SSD_HEADS, tm), lambda b, i: (b, 0, i))],
        out_shape=[jax.ShapeDtypeStruct((bsz, t_len, SSD_INNER), F32),
                   jax.ShapeDtypeStruct((bsz, t_len, SSD_XBC), F32),
                   jax.ShapeDtypeStruct((bsz, t_len, HG_INNER), F32),
                   jax.ShapeDtypeStruct((bsz, t_len, 2 * SSD_HEADS), F32),
                   jax.ShapeDtypeStruct((bsz, 2 * SSD_HEADS, t_len), F32)],
        scratch_shapes=[pltpu.VMEM((tm + 2 * SUBLANES, SSD_XBC), F32)],
        compiler_params=_cparams(("arbitrary", "arbitrary")),
        name="inproj_a",
    )(x, x, x, sh, sc, nw, wa, wdt, cw, cb, dtb)


B_COLS = 4 * HG_INNER


def _inproj_b_kernel(x_ref, sh_ref, sc_ref, nw_ref, w_ref, o_ref, *, ncol):
    if ncol:
        xt = jnp.concatenate([x_ref[:, w * D_MODEL:(w + 1) * D_MODEL] for w in range(ncol)], axis=0)
    else:
        xt = x_ref[...]
    h = _prep(xt, nw_ref[...], sh_ref[...], sc_ref[...])
    o_ref[...] = jnp.dot(h, w_ref[...], preferred_element_type=F32)


def inproj_b_call(x, sh, sc, nw, wb, col_major):
    bsz, t_len, _ = x.shape
    if col_major:
        ncol = 8
        tm = ncol * GRID_W
        rows = t_len // GRID_W
        assert rows == GRID_W
        xin = x.reshape(bsz, rows, GRID_W * D_MODEL)
        x_spec = pl.BlockSpec((None, rows, ncol * D_MODEL), lambda b, i: (b, 0, i))
    else:
        ncol = 0
        tm = t_len
        xin = x
        x_spec = pl.BlockSpec((None, tm, D_MODEL), lambda b, i: (b, i, 0))
    full = lambda shape: pl.BlockSpec(shape, lambda b, i: (0,) * len(shape))
    return pl.pallas_call(
        functools.partial(_inproj_b_kernel, ncol=ncol),
        grid=(bsz, t_len // tm),
        in_specs=[x_spec,
                  pl.BlockSpec((None, 1, D_MODEL), lambda b, i: (b, 0, 0)),
                  pl.BlockSpec((None, 1, D_MODEL), lambda b, i: (b, 0, 0)),
                  full((1, D_MODEL)), full((D_MODEL, B_COLS))],
        out_specs=pl.BlockSpec((None, tm, B_COLS), lambda b, i: (b, i, 0)),
        out_shape=jax.ShapeDtypeStruct((bsz, t_len, B_COLS), F32),
        compiler_params=_cparams(("arbitrary", "arbitrary")),
        name="inproj_b",
    )(xin, sh, sc, nw, wb)


N_PAIRS = SSD_HEADS // 2


def _ssd_dir(xbc, dt, dtT, s_ref, tri, triT, na_row, na_col, fwd, need_out=True):
    c = SSD_CHUNK
    xs = xbc[:, :SSD_INNER]
    bm = xbc[:, SSD_INNER:SSD_INNER + LANES]
    cm = xbc[:, SSD_INNER + LANES:]
    col0 = 0 if fwd else SSD_HEADS
    la = dt[:, col0:col0 + SSD_HEADS] * na_row[:, col0:col0 + SSD_HEADS]
    dtr = dtT[col0:col0 + SSD_HEADS, :]
    laT = dtr * na_col[col0:col0 + SSD_HEADS, :]
    g = sum(jnp.dot(tri, p, preferred_element_type=F32) for p in _split3(la))
    gT = sum(jnp.dot(p, triT, preferred_element_type=F32) for p in _split3(laT))
    end = c - 1 if fwd else 0
    bmT = bm.T
    ii = lax.broadcasted_iota(jnp.int32, (c, c), 0)
    jj = lax.broadcasted_iota(jnp.int32, (c, c), 1)
    causal = (jj <= ii) if fwd else (jj >= ii)
    lane = lax.broadcasted_iota(jnp.int32, (c, LANES), 1)
    lo_half = lane < SSD_HEAD_DIM
    lane_s = lax.broadcasted_iota(jnp.int32, (SSD_STATE, LANES), 1) < SSD_HEAD_DIM
    outs = []
    for grp in range(SSD_GROUPS):
        if need_out:
            in_grp = (lane >= grp * SSD_STATE) & (lane < (grp + 1) * SSD_STATE)
            cm_g = jnp.where(in_grp, cm, 0.0).astype(BF16)
            gmat = jnp.dot(cm_g, bmT.astype(BF16), preferred_element_type=F32)
        bmT_g = bmT[grp * SSD_STATE:(grp + 1) * SSD_STATE, :]
        for pp in range(N_PAIRS // SSD_GROUPS):
            pair = grp * (N_PAIRS // SSD_GROUPS) + pp
            heads = (2 * pair, 2 * pair + 1)
            xs_p = xs[:, pair * LANES:(pair + 1) * LANES]
            xbd = jnp.concatenate([jnp.where(lo_half, xs_p, 0.0), jnp.where(lo_half, 0.0, xs_p)],
                                  axis=0).astype(BF16)
            ms, bws, ecols, arows = [], [], [], []
            for hd in heads:
                gcol = g[:, hd:hd + 1]
                grow = gT[hd:hd + 1, :]
                glast = grow[:, end:end + 1]
                bws.append(bmT_g * (dtr[hd:hd + 1, :] * jnp.exp(glast - grow)))
                arows.append(jnp.exp(glast))
                if need_out:
                    dec = jnp.exp(jnp.where(causal, gcol - grow, -jnp.inf))
                    ms.append(gmat * dec * dtr[hd:hd + 1, :])
                    ecols.append(jnp.exp(gcol))
            s_old = s_ref[pair]
            if need_out:
                mcat = jnp.concatenate(ms, axis=1).astype(BF16)
                zeros = jnp.zeros_like(s_old)
                s_pad = jnp.concatenate([s_old, zeros] if grp == 0 else [zeros, s_old], axis=0).astype(BF16)
                o_inter = jnp.dot(cm.astype(BF16), s_pad, preferred_element_type=F32)
                o_inter = o_inter * jnp.where(lo_half, ecols[0], ecols[1])
                outs.append(jnp.dot(mcat, xbd, preferred_element_type=F32) + o_inter)
            bw = jnp.concatenate(bws, axis=1).astype(BF16)
            s_ref[pair] = (s_old * jnp.where(lane_s, arows[0], arows[1])
                           + jnp.dot(bw, xbd, preferred_element_type=F32))
    return outs


def _ssd_scan_kernel(xf_ref, dtf_ref, dtTf_ref, xb_ref, dtb_ref, dtTb_ref, s0f_ref, s0b_ref,
                     trif_ref, trifT_ref, trib_ref, tribT_ref, nar_ref, nac_ref, dsk_ref, *rest, need_out,
                     step_chunks):
    if need_out:
        of_ref, ob_ref, sfo_ref, sbo_ref, sf, sb = rest
    else:
        sfo_ref, sbo_ref, sf, sb = rest
    n = pl.program_id(1)

    @pl.when(n == 0)
    def _():
        sf[...] = s0f_ref[...]
        sb[...] = s0b_ref[...]

    c = SSD_CHUNK
    for sub in range(step_chunks):
        rev = step_chunks - 1 - sub
        fs, bs = slice(sub * c, (sub + 1) * c), slice(rev * c, (rev + 1) * c)
        xf = xf_ref[fs, :]
        outs = _ssd_dir(xf, dtf_ref[fs, :], dtTf_ref[:, fs], sf, trif_ref[...], trifT_ref[...],
                        nar_ref[...], nac_ref[...], True, need_out)
        if need_out:
            of_ref[fs, :] = jnp.concatenate(outs, axis=1) + dsk_ref[...] * xf[:, :SSD_INNER]
        outs = _ssd_dir(xb_ref[bs, :], dtb_ref[bs, :], dtTb_ref[:, bs], sb, trib_ref[...], tribT_ref[...],
                        nar_ref[...], nac_ref[...], False, need_out)
        if need_out:
            ob_ref[bs, :] = jnp.concatenate(outs, axis=1)

    @pl.when(n == pl.num_programs(1) - 1)
    def _():
        sfo_ref[...] = sf[...]
        sbo_ref[...] = sb[...]


def _ssd_consts():
    c = SSD_CHUNK
    i = np.arange(c)
    trif = (i[:, None] >= i[None, :]).astype(np.float32)
    trib = (i[:, None] <= i[None, :]).astype(np.float32)
    return [jnp.asarray(a, BF16) for a in (trif, trif.T, trib, trib.T)]


def ssd_scan_call(xbc, dt, dtT, s0f, s0b, na_row, na_col, dskip, need_out=True):
    bsz, t_len, _ = xbc.shape
    step_chunks = SSD_STEP_CHUNKS if (t_len // SSD_CHUNK) % SSD_STEP_CHUNKS == 0 else 1
    c = SSD_CHUNK * step_chunks
    nc = t_len // c
    assert nc * c == t_len
    cc = SSD_CHUNK
    fw = lambda w: pl.BlockSpec((None, c, w), lambda b, n: (b, n, 0))
    bw = lambda w: pl.BlockSpec((None, c, w), lambda b, n: (b, nc - 1 - n, 0))
    full = lambda shape: pl.BlockSpec(shape, lambda b, n: (0,) * len(shape))
    st = pl.BlockSpec((None, N_PAIRS, SSD_STATE, LANES), lambda b, n: (b, 0, 0, 0))
    st_shape = jax.ShapeDtypeStruct((bsz, N_PAIRS, SSD_STATE, LANES), F32)
    o_shape = jax.ShapeDtypeStruct((bsz, t_len, SSD_INNER), F32)
    res = pl.pallas_call(
        functools.partial(_ssd_scan_kernel, need_out=need_out, step_chunks=step_chunks),
        grid=(bsz, nc),
        in_specs=[fw(SSD_XBC), fw(2 * SSD_HEADS),
                  pl.BlockSpec((None, 2 * SSD_HEADS, c), lambda b, n: (b, 0, n)),
                  bw(SSD_XBC), bw(2 * SSD_HEADS),
                  pl.BlockSpec((None, 2 * SSD_HEADS, c), lambda b, n: (b, 0, nc - 1 - n)),
                  st, st, full((cc, cc)), full((cc, cc)), full((cc, cc)), full((cc, cc)),
                  full((1, 2 * SSD_HEADS)), full((2 * SSD_HEADS, 1)), full((1, SSD_INNER))],
        out_specs=([fw(SSD_INNER), bw(SSD_INNER)] if need_out else []) + [st, st],
        out_shape=([o_shape, o_shape] if need_out else []) + [st_shape, st_shape],
        scratch_shapes=[pltpu.VMEM((N_PAIRS, SSD_STATE, LANES), F32),
                        pltpu.VMEM((N_PAIRS, SSD_STATE, LANES), F32)],
        compiler_params=_cparams(("arbitrary", "arbitrary")),
        name="ssd_scan",
    )(xbc, dt, dtT, xbc, dt, dtT, s0f, s0b, *_ssd_consts(), na_row, na_col, dskip)
    return tuple(res) if need_out else (None, None) + tuple(res)


def _hg_consts():
    c = HG_CHUNK
    t = np.arange(c)
    m_f = np.zeros((HG_LEVELS + 1, c, c), np.float32)
    m_f[0] = np.eye(c)
    for lv in range(1, HG_LEVELS + 1):
        m = 2 ** lv
        blk = t // m
        right = (t % m) >= m // 2
        m_f[lv] = (blk[:, None] == blk[None, :]) & right[:, None] & (~right[None, :])
    m_b = np.transpose(m_f, (0, 2, 1))
    tri_f = (t[:, None] >= t[None, :]).astype(np.float32)
    tri_b = (t[:, None] <= t[None, :]).astype(np.float32)
    return jnp.asarray(tri_f, BF16), jnp.asarray(tri_b, BF16), jnp.asarray(m_f, F32), jnp.asarray(m_b, F32)


def _hg_dir(u, lb_row, st_ref, tri_ref, mask_ref, fwd, need_out=True):
    c = HG_CHUNK
    end = c - 1 if fwd else 0
    outs = []
    fcol = HG_INNER if fwd else 2 * HG_INNER
    nt = (((1,), (1,)), ((), ()))
    row = lax.broadcasted_iota(jnp.int32, (c, HG_DK), 0)

    def halves(lo, hi, half):
        if half % SUBLANES == 0:
            return jnp.concatenate([(hi if (s // half) % 2 else lo)[s:s + half] for s in range(0, c, half)], axis=0)
        return jnp.where(((row // half) % 2) == 1, hi, lo)

    def shift(x, s):
        s = s % c
        if s % SUBLANES == 0:
            return jnp.concatenate([x[c - s:], x[:c - s]], axis=0)
        return pltpu.roll(x, s, 0)

    def by_side(query_side, key_side, half):
        return halves(key_side, query_side, half) if fwd else halves(query_side, key_side, half)

    for hd in range(HG_HEADS):
        sl = slice(hd * HG_DK, (hd + 1) * HG_DK)
        lb = lb_row[:, sl]
        f = lb + (1.0 - lb) * _sigmoid(u[:, fcol + hd * HG_DK:fcol + (hd + 1) * HG_DK])
        k = 1.0 - f
        la = jnp.log(f)
        v = u[:, 3 * HG_INNER + hd * HG_DK:3 * HG_INNER + (hd + 1) * HG_DK]
        la_hi = la.astype(BF16)
        la_lo = (la - la_hi.astype(F32)).astype(BF16)
        g2 = jnp.dot(tri_ref[...], jnp.concatenate([la_hi, la_lo], axis=1), preferred_element_type=F32)
        g = g2[:, :HG_DK] + g2[:, HG_DK:]
        st = st_ref[hd]
        if not need_out:
            g_end = g[end:end + 1, :]
            st_ref[hd] = (st * jnp.exp(g_end)
                          + jnp.dot(v.T.astype(BF16), (k * jnp.exp(g_end - g)).astype(BF16),
                                    preferred_element_type=F32))
            continue
        q = _silu(u[:, sl]) * (HG_DK ** -0.5)
        scores = mask_ref[0] * lax.dot_general(q.astype(BF16), k.astype(BF16), nt, preferred_element_type=F32)
        fill = g
        for lv in range(1, HG_LEVELS + 1):
            half = 2 ** (lv - 1)
            if fwd:
                ref = halves(fill, shift(fill, half), half)
            else:
                ref = halves(shift(fill, -half), fill, half)
            decay = jnp.exp2(jnp.abs(g - ref) * (-LOG2E))
            y = (by_side(q, k, half) * decay).astype(BF16)
            scores = scores + mask_ref[lv] * lax.dot_general(y, y, nt, preferred_element_type=F32)
            if fwd:
                fill = halves(shift(fill, -half), fill, half)
            else:
                fill = halves(fill, shift(fill, half), half)
        x_cum = jnp.exp(g)
        x_rem = jnp.exp(fill - g)
        o = jnp.dot(scores.astype(BF16), v.astype(BF16), preferred_element_type=F32)
        o = o + lax.dot_general((q * x_cum).astype(BF16), st.astype(BF16), nt, preferred_element_type=F32)
        outs.append(o)
        st_ref[hd] = (st * x_cum[end:end + 1, :]
                      + jnp.dot(v.T.astype(BF16), (k * x_rem).astype(BF16), preferred_element_type=F32))
    return jnp.concatenate(outs, axis=1) if need_out else None


def _hg_scan_kernel(uf_ref, ub_ref, lbp_ref, s0f_ref, s0b_ref, trif_ref, trib_ref, mf_ref, mb_ref,
                    *rest, img_rows, need_out, step_chunks):
    if need_out:
        of_ref, ob_ref, sfo_ref, sbo_ref, sf, sb = rest
    else:
        sfo_ref, sbo_ref, sf, sb = rest
    n = pl.program_id(1)
    c = HG_CHUNK

    def put(o_ref, o, sub):
        if img_rows:
            cols = c // img_rows
            for j in range(cols):
                lo = (sub * cols + j) * HG_INNER
                o_ref[:, lo:lo + HG_INNER] = o[j * img_rows:(j + 1) * img_rows]
        else:
            o_ref[sub * c:(sub + 1) * c, :] = o

    @pl.when(n == 0)
    def _():
        sf[...] = s0f_ref[...]
        sb[...] = s0b_ref[...]

    p = lbp_ref[...]
    mx = jnp.max(p, axis=0, keepdims=True)
    e = jnp.exp(p - mx)
    lb = e[0:1, :] / jnp.sum(e, axis=0, keepdims=True)
    for sub in range(step_chunks):
        rev = step_chunks - 1 - sub
        o_f = _hg_dir(uf_ref[sub * c:(sub + 1) * c, :], lb[:, :HG_INNER], sf, trif_ref, mf_ref, True, need_out)
        o_b = _hg_dir(ub_ref[rev * c:(rev + 1) * c, :], lb[:, HG_INNER:], sb, trib_ref, mb_ref, False, need_out)
        if need_out:
            put(of_ref, o_f, sub)
            put(ob_ref, o_b, rev)

    @pl.when(n == pl.num_programs(1) - 1)
    def _():
        sfo_ref[...] = sf[...]
        sbo_ref[...] = sb[...]


def hg_scan_call(u, lbp, s0f, s0b, row_major_out, need_out=True):
    bsz, t_len, _ = u.shape
    n_chunks = t_len // HG_CHUNK
    step_chunks = max(s for s in HG_STEP_CHUNKS if n_chunks % s == 0)
    c = HG_CHUNK * step_chunks
    nc = t_len // c
    assert nc * c == t_len
    full = lambda shape: pl.BlockSpec(shape, lambda b, n: (0,) * len(shape))
    st = pl.BlockSpec((None, HG_HEADS, HG_DK, HG_DK), lambda b, n: (b, 0, 0, 0))
    st_shape = jax.ShapeDtypeStruct((bsz, HG_HEADS, HG_DK, HG_DK), F32)
    if row_major_out:
        img_rows = t_len // GRID_W
        cols = c // img_rows
        assert cols * img_rows == c and HG_CHUNK % img_rows == 0
        o_shape = jax.ShapeDtypeStruct((bsz, img_rows, GRID_W * HG_INNER), F32)
        of_spec = pl.BlockSpec((None, img_rows, cols * HG_INNER), lambda b, n: (b, 0, n))
        ob_spec = pl.BlockSpec((None, img_rows, cols * HG_INNER), lambda b, n: (b, 0, nc - 1 - n))
    else:
        img_rows = 0
        o_shape = jax.ShapeDtypeStruct((bsz, t_len, HG_INNER), F32)
        of_spec = pl.BlockSpec((None, c, HG_INNER), lambda b, n: (b, n, 0))
        ob_spec = pl.BlockSpec((None, c, HG_INNER), lambda b, n: (b, nc - 1 - n, 0))
    tri_f, tri_b, m_f, m_b = _hg_consts()
    res = pl.pallas_call(
        functools.partial(_hg_scan_kernel, img_rows=img_rows, need_out=need_out, step_chunks=step_chunks),
        grid=(bsz, nc),
        in_specs=[pl.BlockSpec((None, c, B_COLS), lambda b, n: (b, n, 0)),
                  pl.BlockSpec((None, c, B_COLS), lambda b, n: (b, nc - 1 - n, 0)),
                  full((2, 2 * HG_INNER)), st, st,
                  full(tri_f.shape), full(tri_b.shape), full(m_f.shape), full(m_b.shape)],
        out_specs=([of_spec, ob_spec] if need_out else []) + [st, st],
        out_shape=([o_shape, o_shape] if need_out else []) + [st_shape, st_shape],
        scratch_shapes=[pltpu.VMEM((HG_HEADS, HG_DK, HG_DK), F32), pltpu.VMEM((HG_HEADS, HG_DK, HG_DK), F32)],
        compiler_params=_cparams(("arbitrary", "arbitrary")),
        name="hg_scan",
    )(u, u, lbp, s0f, s0b, tri_f, tri_b, m_f, m_b)
    if not need_out:
        return (None, None) + tuple(res)
    return tuple(res)


ROW_LINES = D_MODEL // LANES


def _to_token_tiles(ref, val, n_rows):
    for c in range(ROW_LINES):
        ref[pl.ds(c, n_rows, stride=ROW_LINES), :] = val[:, c * LANES:(c + 1) * LANES]


def _from_token_tiles(ref, n_rows, first_row=0):
    return jnp.concatenate([ref[pl.ds(first_row * ROW_LINES + c, n_rows, stride=ROW_LINES), :]
                            for c in range(ROW_LINES)], axis=1)


def _tile_of(ref, row):
    return ref.at[pl.ds(pl.multiple_of(row * ROW_LINES, ROW_LINES), ROW_LINES)]


def _post_kernel(x_ref, sof_ref, sob_ref, z_ref, hof_ref, hob_ref, hg_ref,
                 snw_ref, hnw_ref, wo_ref, pnw_ref, gm_ref, fnw_ref, shf_ref, scf_ref, rwh_ref, rwl_ref, rb_ref,
                 tri_ref, x1_ref, hx_ref, idx_ref, rank_ref, gate_ref, cnt_ref, carry, hscr, *, tm):
    first = (pl.program_id(0) == 0) & (pl.program_id(1) == 0)

    @pl.when(first)
    def _():
        carry[...] = jnp.zeros_like(carry)

    y = (sof_ref[...] + sob_ref[...]) * _silu(z_ref[...])
    y = _rms(y, snw_ref[...])
    ho = hof_ref[...] + hob_ref[...]
    n_img = tm // GRID_W
    for w in range(GRID_W):
        for cc in range(HG_INNER // LANES):
            lo = w * HG_INNER + cc * LANES
            hscr[cc, pl.ds(w, n_img, stride=GRID_W), :] = ho[:, lo:lo + LANES]
    o = jnp.concatenate([hscr[cc] for cc in range(HG_INNER // LANES)], axis=1)
    hnw = hnw_ref[...]
    o = jnp.concatenate([_rms(o[:, h * HG_DK:(h + 1) * HG_DK], hnw[:, h * HG_DK:(h + 1) * HG_DK])
                         for h in range(HG_HEADS)], axis=1)
    o = o * _silu(hg_ref[...])
    mix = (jnp.dot(y.astype(BF16), wo_ref[:SSD_INNER, :], preferred_element_type=F32)
           + jnp.dot(o.astype(BF16), wo_ref[SSD_INNER:, :], preferred_element_type=F32))
    x1 = x_ref[...] + gm_ref[...] * _rms(mix, pnw_ref[...])
    x1_ref[...] = x1
    hx = _rms(x1, fnw_ref[...]) * (1.0 + scf_ref[...]) + shf_ref[...]
    _to_token_tiles(hx_ref, hx, tm)
    hx_hi = hx.astype(BF16)
    hx_lo = (hx - hx_hi.astype(F32)).astype(BF16)
    rwh = rwh_ref[...]
    logits = (jnp.dot(hx_hi, rwh, preferred_element_type=F32) + jnp.dot(hx_lo, rwh, preferred_element_type=F32)
              + jnp.dot(hx_hi, rwl_ref[...], preferred_element_type=F32)) + rb_ref[...]
    work = logits.T[:N_EXPERTS, :]
    erow = lax.broadcasted_iota(jnp.int32, (N_EXPERTS, tm), 0)
    vals, idxs = [], []
    for _ in range(TOP_K):
        m = jnp.max(work, axis=0, keepdims=True)
        ix = jnp.min(jnp.where(work == m, erow, N_EXPERTS), axis=0, keepdims=True)
        vals.append(m)
        idxs.append(ix)
        work = jnp.where(erow == ix, -jnp.inf, work)
    es = [jnp.exp(v - vals[0]) for v in vals]
    den = es[0] + es[1] + es[2] + es[3]
    onehots = [(erow == ix) for ix in idxs]
    multi = sum(oh.astype(F32) for oh in onehots)
    before = jnp.dot(multi.astype(BF16), tri_ref[...], preferred_element_type=F32) + carry[...]
    carry[...] = carry[...] + jnp.sum(multi, axis=1, keepdims=True)
    sub = lax.broadcasted_iota(jnp.int32, (SUBLANES, tm), 0)
    idx_o = jnp.zeros((SUBLANES, tm), jnp.int32)
    rank_o = jnp.zeros((SUBLANES, tm), jnp.int32)
    gate_o = jnp.zeros((SUBLANES, tm), F32)
    for k in range(TOP_K):
        rk = jnp.sum(jnp.where(onehots[k], before, 0.0), axis=0, keepdims=True)
        idx_o = jnp.where(sub == k, idxs[k], idx_o)
        rank_o = jnp.where(sub == k, rk.astype(jnp.int32), rank_o)
        gate_o = jnp.where(sub == k, es[k] / den, gate_o)
    idx_ref[...] = idx_o
    rank_ref[...] = rank_o
    gate_ref[...] = gate_o
    cnt_ref[...] = jnp.broadcast_to(carry[...], (N_EXPERTS, LANES))


def post_call(x, sof, sob, z, hof, hob, hg, snw, hnw, wo, pnw, gm, fnw, shf, scf, rwh, rwl, rb):
    bsz, t_len, _ = x.shape
    tm = TOK_TILE
    nt = t_len // tm
    n_tok = bsz * t_len
    tok = lambda c: pl.BlockSpec((None, tm, c), lambda b, i: (b, i, 0))
    full = lambda shape: pl.BlockSpec(shape, lambda b, i: (0,) * len(shape))
    per_b = pl.BlockSpec((None, 1, D_MODEL), lambda b, i: (b, 0, 0))
    flat = lambda c: pl.BlockSpec((tm, c), lambda b, i: (b * nt + i, 0))
    rout = pl.BlockSpec((SUBLANES, tm), lambda b, i: (b * nt + i, 0))
    n_tiles = n_tok // tm
    ii = np.arange(tm)
    tri = jnp.asarray(ii[:, None] < ii[None, :], BF16)
    assert tm % (GRID_W * SUBLANES) == 0 and hof.shape == (bsz, t_len // GRID_W, GRID_W * HG_INNER)
    img = pl.BlockSpec((None, tm // GRID_W, GRID_W * HG_INNER), lambda b, i: (b, i, 0))
    return pl.pallas_call(
        functools.partial(_post_kernel, tm=tm),
        grid=(bsz, nt),
        in_specs=[tok(D_MODEL), tok(SSD_INNER), tok(SSD_INNER), tok(SSD_INNER), img, img,
                  tok(HG_INNER), full((1, SSD_INNER)), full((1, HG_INNER)), full((D_MODEL, D_MODEL)),
                  full((1, D_MODEL)), per_b, full((1, D_MODEL)), per_b, per_b,
                  full((D_MODEL, LANES)), full((D_MODEL, LANES)), full((1, LANES)), full((tm, tm))],
        out_specs=[flat(D_MODEL), pl.BlockSpec((tm * ROW_LINES, LANES), lambda b, i: (b * nt + i, 0)),
                   rout, rout, rout, pl.BlockSpec((N_EXPERTS, LANES), lambda b, i: (0, 0))],
        out_shape=[jax.ShapeDtypeStruct((n_tok, D_MODEL), F32),
                   jax.ShapeDtypeStruct((n_tok * ROW_LINES, LANES), F32),
                   jax.ShapeDtypeStruct((n_tiles * SUBLANES, tm), jnp.int32),
                   jax.ShapeDtypeStruct((n_tiles * SUBLANES, tm), jnp.int32),
                   jax.ShapeDtypeStruct((n_tiles * SUBLANES, tm), F32),
                   jax.ShapeDtypeStruct((N_EXPERTS, LANES), F32)],
        scratch_shapes=[pltpu.VMEM((N_EXPERTS, 1), F32), pltpu.VMEM((HG_INNER // LANES, tm, LANES), F32)],
        compiler_params=_cparams(("arbitrary", "arbitrary")),
        name="post",
    )(x, sof, sob, z, hof, hob, hg, snw, hnw, wo, pnw, gm, fnw, shf, scf, rwh, rwl, rb, tri)


def _wait_rows(hbm_ref, n_rows, sem):
    n = n_rows * ROW_LINES
    pltpu.make_async_copy(hbm_ref.at[pl.ds(0, n)], hbm_ref.at[pl.ds(0, n)], sem).wait()


PAD_RUNS = tuple(2 ** j for j in range(int(math.log2(MOE_ROWS))))
DISPATCH_TILES = 2


def _dispatch_kernel(zstart_ref, zpad_ref, dest_ref, hx_ref, buf_ref, zrows, sem, zsem, *, tm):
    i = pl.program_id(0)

    def zero_rows(first, n):
        first = pl.multiple_of(first * ROW_LINES, ROW_LINES)
        return pltpu.make_async_copy(zrows.at[pl.ds(0, n * ROW_LINES)], buf_ref.at[pl.ds(first, n * ROW_LINES)], zsem)

    def for_pad_runs(act):
        def per_expert(e, carry):
            pad = zpad_ref[e]
            off = zstart_ref[e]
            for run in PAD_RUNS:
                @pl.when((pad & run) != 0)
                def _():
                    act(zero_rows(off, run))
                off = off + (pad & run)
            return carry

        lax.fori_loop(0, N_EXPERTS, per_expert, 0)

        def tail(j, carry):
            act(zero_rows(j * PAD_RUNS[-1], PAD_RUNS[-1]))
            return carry

        n_rows = buf_ref.shape[0] // ROW_LINES
        lax.fori_loop(zstart_ref[N_EXPERTS] // PAD_RUNS[-1], n_rows // PAD_RUNS[-1], tail, 0)

    @pl.when(i == 0)
    def _():
        zrows[...] = jnp.zeros_like(zrows)
        for_pad_runs(lambda cp: cp.start())

    def issue(t, carry):
        for sub in range(DISPATCH_TILES):
            for k in range(TOP_K):
                pltpu.make_async_copy(_tile_of(hx_ref, sub * tm + t),
                                      _tile_of(buf_ref, dest_ref[(sub * TOP_K + k) * tm + t]),
                                      sem).start(priority=k % 2)
        return carry

    lax.fori_loop(0, tm, issue, 0, unroll=2)
    _wait_rows(buf_ref, DISPATCH_TILES * tm * TOP_K, sem)

    @pl.when(i == 0)
    def _():
        for_pad_runs(lambda cp: cp.wait())


def dispatch_call(zstart, zpad, dest_flat, hx, n_rows):
    n_tok = hx.shape[0] // ROW_LINES
    tm = TOK_TILE
    step_tok = DISPATCH_TILES * tm
    return pl.pallas_call(
        functools.partial(_dispatch_kernel, tm=tm),
        grid=(n_tok // step_tok,),
        in_specs=[pl.BlockSpec(memory_space=pltpu.SMEM), pl.BlockSpec(memory_space=pltpu.SMEM),
                  pl.BlockSpec((step_tok * TOP_K,), lambda i: (i,), memory_space=pltpu.SMEM),
                  pl.BlockSpec((step_tok * ROW_LINES, LANES), lambda i: (i, 0))],
        out_specs=pl.BlockSpec(memory_space=pl.ANY),
        out_shape=jax.ShapeDtypeStruct((n_rows * ROW_LINES, LANES), F32),
        scratch_shapes=[pltpu.VMEM((PAD_RUNS[-1] * ROW_LINES, LANES), F32), pltpu.SemaphoreType.DMA(()),
                        pltpu.SemaphoreType.DMA(())],
        compiler_params=_cparams(("arbitrary",)),
        name="dispatch",
    )(zstart, zpad, dest_flat, hx)


W1_TCOLS = 256


def _experts_kernel(be_ref, nu_ref, nxt_ref, x_ref, w1_hbm, b1_ref, w2_hbm, b2_ref, y_ref,
                    w1f, w2f, w1t, w2s, tbuf, wsem):
    i = pl.program_id(0)
    live = i < nu_ref[0]
    new_expert = (i == 0) | (be_ref[i] != be_ref[jnp.maximum(i - 1, 0)])

    def weight_copies(e):
        return (pltpu.make_async_copy(w1_hbm.at[e], w1f, wsem.at[0]),
                pltpu.make_async_copy(w2_hbm.at[e], w2f, wsem.at[1]))

    @pl.when(live & new_expert)
    def _():
        @pl.when(i == 0)
        def _():
            for cp in weight_copies(be_ref[0]):
                cp.start()

        for cp in weight_copies(be_ref[i]):
            cp.wait()
        half = W1_TCOLS // 2
        for c in range(2 * D_FF // W1_TCOLS):
            for j in range(D_MODEL // LANES):
                ks = slice(j * LANES, (j + 1) * LANES)
                tbuf[j] = w1f[ks, c * W1_TCOLS:(c + 1) * W1_TCOLS].T
                w1t[c * half:(c + 1) * half, ks] = tbuf[j, pl.ds(0, half, stride=2), :].astype(BF16)
                w1t[D_FF + c * half:D_FF + (c + 1) * half, ks] = tbuf[j, pl.ds(1, half, stride=2), :].astype(BF16)
        w2s[...] = w2f[...].astype(BF16)

        @pl.when(nxt_ref[i] >= 0)
        def _():
            for cp in weight_copies(nxt_ref[i]):
                cp.start(priority=1)

    @pl.when(live)
    def _():
        xb = _from_token_tiles(x_ref, MOE_ROWS).astype(BF16)
        u = lax.dot_general(xb, w1t[...], (((1,), (1,)), ((), ())), preferred_element_type=F32) + b1_ref[...]
        glu = jnp.minimum(u[:, :D_FF], SWIGLU_LIMIT)
        lin = jnp.clip(u[:, D_FF:], -SWIGLU_LIMIT, SWIGLU_LIMIT)
        a = glu * _sigmoid(SWIGLU_ALPHA * glu) * (lin + 1.0)
        y = jnp.dot(a.astype(BF16), w2s[...], preferred_element_type=F32) + b2_ref[...]
        _to_token_tiles(y_ref, y, MOE_ROWS)

    @pl.when(jnp.logical_not(live))
    def _():
        y_ref[...] = jnp.zeros_like(y_ref)


def experts_call(block_e, n_used, next_e, xs, w1, b1p, w2, b2):
    rows = xs.shape[0] // ROW_LINES
    r = MOE_ROWS
    nb = rows // r
    blk = (r * ROW_LINES, LANES)
    grid_spec = pltpu.PrefetchScalarGridSpec(
        num_scalar_prefetch=3,
        grid=(nb,),
        in_specs=[pl.BlockSpec(blk, lambda i, be, nu, nx: (jnp.maximum(jnp.minimum(i, nu[0] - 1), 0), 0)),
                  pl.BlockSpec(memory_space=pl.ANY),
                  pl.BlockSpec((None, 1, 2 * D_FF), lambda i, be, nu, nx: (be[i], 0, 0)),
                  pl.BlockSpec(memory_space=pl.ANY),
                  pl.BlockSpec((None, 1, D_MODEL), lambda i, be, nu, nx: (be[i], 0, 0))],
        out_specs=pl.BlockSpec(blk, lambda i, be, nu, nx: (i, 0)),
        scratch_shapes=[pltpu.VMEM((D_MODEL, 2 * D_FF), F32), pltpu.VMEM((D_FF, D_MODEL), F32),
                        pltpu.VMEM((2 * D_FF, D_MODEL), BF16), pltpu.VMEM((D_FF, D_MODEL), BF16),
                        pltpu.VMEM((D_MODEL // LANES, W1_TCOLS, LANES), F32),
                        pltpu.SemaphoreType.DMA((2,))],
    )
    return pl.pallas_call(
        _experts_kernel,
        grid_spec=grid_spec,
        out_shape=jax.ShapeDtypeStruct((rows * ROW_LINES, LANES), F32),
        compiler_params=_cparams(("arbitrary",)),
        name="experts",
    )(block_e, n_used, next_e, xs, w1, b1p, w2, b2)


def _combine_kernel(dest_ref, dest_next_ref, y_ref, gate_ref, x1_ref, gf_ref, nw_ref, o_ref, buf, sem, *, tm):
    i = pl.program_id(0)
    slot = i % 2

    def gather_tile(dref, into):
        def issue(t, carry):
            for k in range(TOP_K):
                pltpu.make_async_copy(_tile_of(y_ref, dref[k * tm + t]), _tile_of(buf.at[into], k * tm + t),
                                      sem.at[into]).start(priority=k % 2)
            return carry

        lax.fori_loop(0, tm, issue, 0, unroll=4)

    @pl.when(i == 0)
    def _():
        gather_tile(dest_ref, 0)

    @pl.when(i + 1 < pl.num_programs(0))
    def _():
        gather_tile(dest_next_ref, 1 - slot)

    _wait_rows(y_ref, tm * TOP_K, sem.at[slot])
    gate = gate_ref[...].T
    fx = None
    for k in range(TOP_K):
        yk = jnp.concatenate([buf[slot, pl.ds(k * tm * ROW_LINES + c, tm, stride=ROW_LINES), :]
                              for c in range(ROW_LINES)], axis=1)
        fx = gate[:, k:k + 1] * yk if fx is None else fx + gate[:, k:k + 1] * yk
    o_ref[...] = x1_ref[...] + gf_ref[...] * _rms(fx, nw_ref[...])


def combine_call(dest_flat, ys, gates, x1, gf, nw, bsz):
    n_tok = x1.shape[0]
    tm = TOK_TILE
    nt = n_tok // bsz // tm
    n_steps = n_tok // tm
    return pl.pallas_call(
        functools.partial(_combine_kernel, tm=tm),
        grid=(n_steps,),
        in_specs=[pl.BlockSpec((tm * TOP_K,), lambda i: (i,), memory_space=pltpu.SMEM),
                  pl.BlockSpec((tm * TOP_K,), lambda i: (jnp.minimum(i + 1, n_steps - 1),),
                               memory_space=pltpu.SMEM),
                  pl.BlockSpec(memory_space=pl.ANY),
                  pl.BlockSpec((SUBLANES, tm), lambda i: (i, 0)),
                  pl.BlockSpec((tm, D_MODEL), lambda i: (i, 0)),
                  pl.BlockSpec((None, 1, D_MODEL), lambda i: (i // nt, 0, 0)),
                  pl.BlockSpec((1, D_MODEL), lambda i: (0, 0))],
        out_specs=pl.BlockSpec((tm, D_MODEL), lambda i: (i, 0)),
        out_shape=jax.ShapeDtypeStruct((n_tok, D_MODEL), F32),
        scratch_shapes=[pltpu.VMEM((2, TOP_K * tm * ROW_LINES, LANES), F32), pltpu.SemaphoreType.DMA((2,))],
        compiler_params=_cparams(("arbitrary",)),
        name="combine",
    )(dest_flat, dest_flat, ys, gates, x1, gf, nw)


def kernel(x, c, ctx, c_ctx, ada_w, ada_b, mix_pre_norm, mix_post_norm, w_in, w_out, ssd_conv_w, ssd_conv_b,
           ssd_dt_bias, ssd_a_log, ssd_d, ssd_norm, hg_lb, hg_norm, ffn_pre_norm, ffn_post_norm, router_w,
           router_b, moe_w1, moe_b1, moe_w2, moe_b2):
    bsz, t_len, d = x.shape
    assert ada_w.shape[0] == 1 and d == D_MODEL and bsz <= SUBLANES - 1
    n_tok = bsz * t_len

    cc = jnp.zeros((SUBLANES, d), F32).at[:bsz].set(c).at[bsz].set(c_ctx)
    mod = ada_call(cc, ada_w[0], ada_b[0][None, :])
    sh_m, sc_m, g_m, sh_f, sc_f, g_f = [m[:bsz, None, :] for m in jnp.split(mod, 6, axis=-1)]
    csh_m, csc_m = [jnp.broadcast_to(m[bsz][None, None, :], (bsz, 1, d)) for m in jnp.split(mod, 6, axis=-1)[:2]]

    w = w_in[0]
    wa = jnp.concatenate([w[:, :SSD_INNER + SSD_XBC], w[:, SSD_COLS + 4 * HG_INNER:]], axis=1).astype(BF16)
    wdt = jnp.zeros((d, LANES), F32).at[:, :2 * SSD_HEADS].set(w[:, SSD_INNER + SSD_XBC:SSD_COLS]).astype(BF16)
    wb = w[:, SSD_COLS:SSD_COLS + 4 * HG_INNER].astype(BF16)
    dtb = jnp.zeros((1, LANES), F32).at[0, :2 * SSD_HEADS].set(ssd_dt_bias[0].reshape(-1))
    nw = mix_pre_norm[0][None, :]
    cw, cb = ssd_conv_w[0], ssd_conv_b[0][None, :]
    neg_a = -jnp.exp(ssd_a_log[0].astype(F32)).reshape(1, 2 * SSD_HEADS)
    dskip = jnp.repeat(ssd_d[0], SSD_HEAD_DIM)[None, :]
    lbp = hg_lb.astype(F32).reshape(2, 2 * HG_INNER)

    _, cxbc, _, cdt, cdtT = inproj_a_call(ctx, csh_m, csc_m, nw, wa, wdt, cw, cb, dtb, tm=ctx.shape[1])
    cu = inproj_b_call(ctx, csh_m, csc_m, nw, wb, col_major=False)
    z_ssd = jnp.zeros((bsz, N_PAIRS, SSD_STATE, LANES), F32)
    z_hg = jnp.zeros((bsz, HG_HEADS, HG_DK, HG_DK), F32)
    _, _, ssf, ssb = ssd_scan_call(cxbc, cdt, cdtT, z_ssd, z_ssd, neg_a, neg_a.reshape(-1, 1), dskip,
                                   need_out=False)
    _, _, hsf, hsb = hg_scan_call(cu, lbp, z_hg, z_hg, row_major_out=False, need_out=False)

    zg, xbc, hgate, dt, dtT = inproj_a_call(x, sh_m, sc_m, nw, wa, wdt, cw, cb, dtb, tm=1024)
    ub = inproj_b_call(x, sh_m, sc_m, nw, wb, col_major=True)
    sof, sob, _, _ = ssd_scan_call(xbc, dt, dtT, ssf, ssb, neg_a, neg_a.reshape(-1, 1), dskip)
    hof, hob, _, _ = hg_scan_call(ub, lbp, hsf, hsb, row_major_out=True)

    rw = jnp.zeros((d, LANES), F32).at[:, :N_EXPERTS].set(router_w[0])
    rwh = rw.astype(BF16)
    rwl = (rw - rwh.astype(F32)).astype(BF16)
    rb = jnp.zeros((1, LANES), F32).at[0, :N_EXPERTS].set(router_b[0])
    x1, hx, idx, rank, gates, cnt = post_call(
        x, sof, sob, zg, hof, hob, hgate, ssd_norm[0][None, :], hg_norm[0][None, :], w_out[0].astype(BF16),
        mix_post_norm[0][None, :], g_m, ffn_pre_norm[0][None, :], sh_f, sc_f, rwh, rwl, rb)

    r = MOE_ROWS
    n_blocks = (n_tok * TOP_K + N_EXPERTS * (r - 1) + r - 1) // r
    n_tiles = n_tok // TOK_TILE
    counts = cnt[:, 0].astype(jnp.int32)
    padded = (counts + r - 1) // r * r
    pend = jnp.cumsum(padded)
    pstart = pend - padded
    idx3 = idx.reshape(n_tiles, SUBLANES, TOK_TILE)[:, :TOP_K, :]
    rank3 = rank.reshape(n_tiles, SUBLANES, TOK_TILE)[:, :TOP_K, :]
    dest = rank3
    for e in range(N_EXPERTS):
        dest = dest + jnp.where(idx3 == e, pstart[e], 0)
    dest = dest.reshape(-1)
    starts = jnp.arange(n_blocks, dtype=jnp.int32) * r
    block_e = jnp.minimum(jnp.sum((pend[None, :] <= starts[:, None]).astype(jnp.int32), axis=1), N_EXPERTS - 1)
    n_used = (pend[-1:] // r).astype(jnp.int32)
    run_end = pend[block_e] // r
    next_e = jnp.where(run_end < n_used[0], block_e[jnp.minimum(run_end, n_blocks - 1)], -1).astype(jnp.int32)

    zstart = jnp.concatenate([pstart + counts, pend[-1:]])
    xs = dispatch_call(zstart, padded - counts, dest, hx, n_blocks * r)
    b1p = jnp.concatenate([moe_b1[0][:, 0::2], moe_b1[0][:, 1::2]], axis=-1)[:, None, :]
    ys = experts_call(block_e, n_used, next_e, xs, moe_w1[0], b1p, moe_w2[0], moe_b2[0][:, None, :])
    out = combine_call(dest, ys, gates, x1, g_f, ffn_post_norm[0][None, :], bsz)
    return out.reshape(bsz, t_len, d)
```
